```python
import math
import jax, jax.numpy as jnp
from jax import lax
import numpy as np

D_MODEL = 1024
BATCH = 2
SEQ = 8192
DEPTH = 2
DEC_BATCH = 128
DEC_SEQ = 8
PAST_LEN = 16384
PAGE_SIZE = 128

ML_HEADS = 4
ML_DK = 128
ML_DV = 128
SW_HEADS = 8
SW_KV_HEADS = 2
SW_GQ = SW_HEADS // SW_KV_HEADS
SW_HD = 64
WINDOW = 128
ROPE_THETA = 10000.0
GLA_HEADS = 4
GLA_DK = 64
GLA_DV = 128
GLA_RANK = 16
GLA_TAU = 16.0
S5_WIDTH = 512
S5_GROUP = 16
S5_GROUPS = S5_WIDTH // S5_GROUP
S5_STATE = 64
D_FF = 2816
CONV_W = 3
CHUNK = 64
EPS = 1e-6

EVEN_SPLITS = (ML_HEADS * ML_DK, ML_HEADS * ML_DK, ML_HEADS * ML_DV, ML_HEADS * ML_DV, 2 * ML_HEADS,
               SW_HEADS * SW_HD, SW_KV_HEADS * SW_HD, SW_KV_HEADS * SW_HD)
EVEN_COLS = 2 * ML_HEADS * ML_DK + 2 * ML_HEADS * ML_DV + 2 * ML_HEADS + SW_HEADS * SW_HD + 2 * SW_KV_HEADS * SW_HD
EVEN_MIX = ML_HEADS * ML_DV + SW_HEADS * SW_HD
ODD_SPLITS = (GLA_HEADS * GLA_DK, GLA_HEADS * GLA_DK, GLA_HEADS * GLA_DV, GLA_HEADS * GLA_DV, GLA_RANK, S5_WIDTH)
ODD_COLS = 2 * GLA_HEADS * GLA_DK + 2 * GLA_HEADS * GLA_DV + GLA_RANK + S5_WIDTH
ODD_MIX = GLA_HEADS * GLA_DV + S5_WIDTH

kernel_name = 'hybrid_mlstm_swa_gla_s5_decode_step'


def _rmsnorm(x, g):
    xf = x.astype(jnp.float32)
    y = xf * lax.rsqrt(jnp.mean(xf * xf, axis=-1, keepdims=True) + EPS) * g.astype(jnp.float32)
    return y.astype(x.dtype)


def _split(z, sizes):
    return jnp.split(z, [int(c) for c in np.cumsum(sizes)[:-1]], axis=-1)


def _chunk_len(t):
    return CHUNK if t % CHUNK == 0 else t


def _to_chunks(a, L):
    B, T = a.shape[:2]
    return a.reshape((B, T // L, L) + a.shape[2:]).swapaxes(0, 1)


def _from_chunks(a):
    nc, B, L = a.shape[:3]
    return a.swapaxes(0, 1).reshape((B, nc * L) + a.shape[3:])


def _rope(x, pos):
    half = x.shape[-1] // 2
    inv = ROPE_THETA ** (-jnp.arange(half, dtype=jnp.float32) / half)
    ang = pos.astype(jnp.float32)[:, None] * inv[None, :]
    cos = jnp.cos(ang)[None, :, None, :]
    sin = jnp.sin(ang)[None, :, None, :]
    xf = x.astype(jnp.float32)
    x1, x2 = xf[..., :half], xf[..., half:]
    return jnp.concatenate([x1 * cos - x2 * sin, x2 * cos + x1 * sin], axis=-1).astype(x.dtype)


def _sink_probs(s, sink):
    sk = sink[..., None]
    m = jnp.maximum(s.max(axis=-1, keepdims=True), sk)
    e = jnp.exp(s - m)
    return e / (e.sum(axis=-1, keepdims=True) + jnp.exp(sk - m))


def _swa_prompt(q, k, v, sinks):
    B, T = q.shape[:2]
    nb = T // WINDOW
    qb = q.reshape(B, nb, WINDOW, SW_KV_HEADS, SW_GQ, SW_HD)

    def band(a):
        blocks = a.reshape(B, nb, WINDOW, SW_KV_HEADS, SW_HD)
        prev = jnp.concatenate([jnp.zeros_like(blocks[:, :1]), blocks[:, :-1]], axis=1)
        return jnp.concatenate([prev, blocks], axis=2)

    kb, vb = band(k), band(v)
    s = jnp.einsum('bnqkgd,bnskd->bnkgqs', qb, kb).astype(jnp.float32) * SW_HD ** -0.5
    i = jnp.arange(WINDOW)[:, None]
    j = jnp.arange(2 * WINDOW)[None, :]
    diff = WINDOW + i - j
    band_ok = (diff >= 0) & (diff < WINDOW)
    blk = jnp.arange(nb)[:, None, None]
    valid = band_ok[None] & ((blk > 0) | (j[None] >= WINDOW))
    s = jnp.where(valid[None, :, None, None], s, -jnp.inf)
    p = _sink_probs(s, sinks.astype(jnp.float32).reshape(SW_KV_HEADS, SW_GQ)[None, None, :, :, None])
    o = jnp.einsum('bnkgqs,bnskd->bnqkgd', p.astype(vb.dtype), vb)
    return o.reshape(B, T, SW_HEADS * SW_HD)


def _swa_sample(q, k, v, sinks, kbuf, vbuf):
    B, T = q.shape[:2]
    kk = jnp.concatenate([kbuf.astype(k.dtype), k], axis=1)
    vv = jnp.concatenate([vbuf.astype(v.dtype), v], axis=1)
    qg = q.reshape(B, T, SW_KV_HEADS, SW_GQ, SW_HD)
    s = jnp.einsum('btkgd,bskd->bkgts', qg, kk).astype(jnp.float32) * SW_HD ** -0.5
    diff = WINDOW + jnp.arange(T)[:, None] - jnp.arange(WINDOW + T)[None, :]
    valid = (diff >= 0) & (diff < WINDOW)
    s = jnp.where(valid, s, -jnp.inf)
    p = _sink_probs(s, sinks.astype(jnp.float32).reshape(SW_KV_HEADS, SW_GQ)[None, :, :, None])
    o = jnp.einsum('bkgts,bskd->btkgd', p.astype(vv.dtype), vv)
    return o.reshape(B, T, SW_HEADS * SW_HD), kk[:, -WINDOW:], vv[:, -WINDOW:]


def _mlstm(q, k, v, ig, lf, C0, n0, m0):
    B, T = q.shape[:2]
    L = _chunk_len(T)
    causal = jnp.tril(jnp.ones((L, L), dtype=bool))

    def step(carry, inp):
        C, n, m = carry
        qc, kc, vc, ic, fc = inp
        b = jnp.cumsum(fc, axis=1)
        d = b[:, :, None, :] - b[:, None, :, :] + ic[:, None, :, :]
        d = jnp.where(causal[None, :, :, None], d, -jnp.inf)
        inter = b + m[:, None, :]
        m_t = jnp.maximum(inter, d.max(axis=2))
        w = jnp.exp(d - m_t[:, :, None, :])
        a_inter = jnp.exp(inter - m_t)
        s = jnp.einsum('bthd,bshd->btsh', qc, kc) * w
        num = jnp.einsum('btsh,bshv->bthv', s, vc) + a_inter[..., None] * jnp.einsum('bthd,bhdv->bthv', qc, C)
        den = s.sum(axis=2) + a_inter * jnp.einsum('bthd,bhd->bth', qc, n)
        h = num / jnp.maximum(jnp.abs(den), jnp.exp(-m_t))[..., None]
        m_new = m_t[:, -1]
        w_end = jnp.exp(b[:, -1:] - b + ic - m_new[:, None])
        decay = jnp.exp(b[:, -1] + m - m_new)
        C = decay[..., None, None] * C + jnp.einsum('bshd,bshv->bhdv', kc * w_end[..., None], vc)
        n = decay[..., None] * n + jnp.einsum('bsh,bshd->bhd', w_end, kc)
        return (C, n, m_new), h

    xs = tuple(_to_chunks(a, L) for a in (q, k, v, ig, lf))
    (C, n, m), h = lax.scan(step, (C0, n0, m0), xs)
    return _from_chunks(h), C, n, m


def _gla(q, k, v, lg, S0):
    B, T = q.shape[:2]
    L = _chunk_len(T)
    causal = jnp.tril(jnp.ones((L, L), dtype=bool))

    def step(S, inp):
        qc, kc, vc, gc = inp
        G = jnp.cumsum(gc, axis=1)
        rel = G[:, :, None] - G[:, None, :]
        rel = jnp.where(causal[None, :, :, None, None], rel, -jnp.inf)
        att = jnp.einsum('bthd,bshd,btshd->btsh', qc, kc, jnp.exp(rel))
        o = jnp.einsum('btsh,bshv->bthv', att, vc) + jnp.einsum('bthd,bhdv->bthv', qc * jnp.exp(G), S)
        G_end = G[:, -1]
        S = jnp.exp(G_end)[..., None] * S + jnp.einsum('bshd,bshv->bhdv', kc * jnp.exp(G_end[:, None] - G), vc)
        return S, o

    xs = tuple(_to_chunks(a, L) for a in (q, k, v, lg))
    S, o = lax.scan(step, S0, xs)
    return _from_chunks(o), S


def _complex_affine_combine(e1, e2):
    a1r, a1i, b1r, b1i = e1
    a2r, a2i, b2r, b2i = e2
    return (a1r * a2r - a1i * a2i, a1r * a2i + a1i * a2r,
            a2r * b1r - a2i * b1i + b2r, a2r * b1i + a2i * b1r + b2i)


def _s5(u, W, x0_re, x0_im):
    f32 = jnp.float32
    lam_re = W['s5_a_re'].astype(f32)
    lam_im = W['s5_a_im'].astype(f32)
    dt = jnp.exp(W['s5_log_dt'].astype(f32))[:, None]
    mag = jnp.exp(lam_re * dt)
    ang = lam_im * dt
    ab_re = mag * jnp.cos(ang)
    ab_im = mag * jnp.sin(ang)
    den = lam_re * lam_re + lam_im * lam_im
    co_re = ((ab_re - 1.0) * lam_re + ab_im * lam_im) / den
    co_im = (ab_im * lam_re - (ab_re - 1.0) * lam_im) / den
    b_re = W['s5_b_re'].astype(f32)
    b_im = W['s5_b_im'].astype(f32)
    bb_re = co_re[..., None] * b_re - co_im[..., None] * b_im
    bb_im = co_re[..., None] * b_im + co_im[..., None] * b_re
    bu_re = jnp.einsum('btgh,gph->btgp', u, bb_re)
    bu_im = jnp.einsum('btgh,gph->btgp', u, bb_im)
    x0_re = x0_re.astype(f32)
    x0_im = x0_im.astype(f32)
    bu_re = bu_re.at[:, 0].add(ab_re * x0_re - ab_im * x0_im)
    bu_im = bu_im.at[:, 0].add(ab_re * x0_im + ab_im * x0_re)
    a_re = jnp.broadcast_to(ab_re, bu_re.shape)
    a_im = jnp.broadcast_to(ab_im, bu_im.shape)
    _, _, xr, xi = lax.associative_scan(_complex_affine_combine, (a_re, a_im, bu_re, bu_im), axis=1)
    y = (jnp.einsum('btgp,ghp->btgh', xr, W['s5_c_re'].astype(f32))
         - jnp.einsum('btgp,ghp->btgh', xi, W['s5_c_im'].astype(f32))
         + W['s5_d'].astype(f32) * u)
    return y, xr[:, -1], xi[:, -1]


def _mixer_even(h, pos, W, C0, n0, m0, kbuf, vbuf):
    B, T, _ = h.shape
    f32 = jnp.float32
    z = h @ W['w_in_even']
    q_m, k_m, v_m, o_m, g_m, q_a, k_a, v_a = _split(z, EVEN_SPLITS)
    q_m = q_m.reshape(B, T, ML_HEADS, ML_DK).astype(f32)
    k_m = k_m.reshape(B, T, ML_HEADS, ML_DK).astype(f32) * ML_DK ** -0.5
    v_m = v_m.reshape(B, T, ML_HEADS, ML_DV).astype(f32)
    gates = g_m.astype(f32) + W['b_mlstm_gates'].astype(f32)
    ig = gates[..., :ML_HEADS]
    lf = jax.nn.log_sigmoid(gates[..., ML_HEADS:])
    h_m, C, n, m = _mlstm(q_m, k_m, v_m, ig, lf, C0.astype(f32), n0.astype(f32), m0.astype(f32))
    h_m = h_m.reshape(B, T, ML_HEADS * ML_DV) * jax.nn.sigmoid(o_m.astype(f32))
    q_a = _rope(q_a.reshape(B, T, SW_HEADS, SW_HD), pos)
    k_a = _rope(k_a.reshape(B, T, SW_KV_HEADS, SW_HD), pos)
    v_a = v_a.reshape(B, T, SW_KV_HEADS, SW_HD)
    if kbuf is None:
        h_a = _swa_prompt(q_a, k_a, v_a, W['sw_sinks'])
        kb, vb = k_a[:, -WINDOW:], v_a[:, -WINDOW:]
    else:
        h_a, kb, vb = _swa_sample(q_a, k_a, v_a, W['sw_sinks'], kbuf, vbuf)
    mixed = jnp.concatenate([h_m.astype(h.dtype), h_a.astype(h.dtype)], axis=-1)
    return mixed @ W['w_out_even'], C, n, m, kb, vb


def _mixer_odd(h, W, S0, x0_re, x0_im):
    B, T, _ = h.shape
    f32 = jnp.float32
    z = h @ W['w_in_odd']
    q, k, v, r, glr, u = _split(z, ODD_SPLITS)
    q = q.reshape(B, T, GLA_HEADS, GLA_DK).astype(f32) * GLA_DK ** -0.5
    k = k.reshape(B, T, GLA_HEADS, GLA_DK).astype(f32)
    v = v.reshape(B, T, GLA_HEADS, GLA_DV).astype(f32)
    lg = jax.nn.log_sigmoid(glr.astype(f32) @ W['w_gla_gate_up'].astype(f32) + W['b_gla_gate'].astype(f32)) / GLA_TAU
    o, S = _gla(q, k, v, lg.reshape(B, T, GLA_HEADS, GLA_DK), S0.astype(f32))
    o = o * lax.rsqrt(jnp.mean(o * o, axis=-1, keepdims=True) + EPS)
    o = o.reshape(B, T, GLA_HEADS * GLA_DV) * W['g_gla_norm'].astype(f32) * jax.nn.silu(r.astype(f32))
    y, xr, xi = _s5(u.reshape(B, T, S5_GROUPS, S5_GROUP).astype(f32), W, x0_re, x0_im)
    zz = jax.nn.gelu(y.reshape(B, T, S5_WIDTH))
    s5_out = zz * jax.nn.sigmoid(zz @ W['w_s5_glu'].astype(f32) + W['b_s5_glu'].astype(f32))
    mixed = jnp.concatenate([o, s5_out], axis=-1).astype(h.dtype)
    return mixed @ W['w_out_odd'], S, xr, xi


def _conv_ffn(h, w_up, conv_w, conv_b, w_down, buf):
    T = h.shape[1]
    u = h @ w_up
    ext = jnp.concatenate([buf.astype(u.dtype), u], axis=1)
    c = conv_b
    for j in range(CONV_W):
        c = c + ext[:, j:j + T] * conv_w[j]
    a, g = jnp.split(c, 2, axis=-1)
    return (jax.nn.silu(g) * a) @ w_down, ext[:, T:]


def _trunk(x, pos, st, W):
    h = x
    new = {}
    conv = []
    for layer in range(DEPTH):
        hn = _rmsnorm(h, W['norm_mix'][layer])
        if layer % 2 == 0:
            y, new['ml_C'], new['ml_n'], new['ml_m'], new['kbuf'], new['vbuf'] = _mixer_even(
                hn, pos, W, st['ml_C'], st['ml_n'], st['ml_m'], st['kbuf'], st['vbuf'])
        else:
            y, new['gla'], new['s5_re'], new['s5_im'] = _mixer_odd(hn, W, st['gla'], st['s5_re'], st['s5_im'])
        h = h + y
        f, cb = _conv_ffn(_rmsnorm(h, W['norm_ffn'][layer]), W['w_ffn_up'][layer], W['ffn_conv_w'][layer],
                          W['ffn_conv_b'][layer], W['w_ffn_down'][layer], st['conv'][layer])
        h = h + f
        conv.append(cb)
    new['conv'] = jnp.stack(conv)
    return _rmsnorm(h, W['norm_final']), new


def setup_inputs(seed: int = 0) -> dict:
    key = jax.random.key(seed)
    ks = jax.random.split(key, 40)
    f32 = jnp.float32

    def nrm(i, shape, scale):
        return jax.random.normal(ks[i], shape, f32) * scale

    n_idx = jnp.arange(S5_STATE, dtype=f32)
    return {
        'x_prompt': nrm(0, (BATCH, SEQ, D_MODEL), 1.0),
        'x_sample': nrm(1, (DEC_BATCH, DEC_SEQ, D_MODEL), 1.0),
        'state_mlstm_C': nrm(2, (DEC_BATCH, ML_HEADS, ML_DK, ML_DV), 0.5),
        'state_mlstm_n': nrm(3, (DEC_BATCH, ML_HEADS, ML_DK), 0.5),
        'state_mlstm_m': 2.0 + nrm(4, (DEC_BATCH, ML_HEADS), 0.5),
        'cache_swa_k': nrm(5, (DEC_BATCH, WINDOW, SW_KV_HEADS, SW_HD), 1.0),
        'cache_swa_v': nrm(6, (DEC_BATCH, WINDOW, SW_KV_HEADS, SW_HD), 1.0),
        'state_gla': nrm(7, (DEC_BATCH, GLA_HEADS, GLA_DK, GLA_DV), 0.5),
        'state_s5_re': nrm(8, (DEC_BATCH, S5_GROUPS, S5_STATE), 0.5),
        'state_s5_im': nrm(9, (DEC_BATCH, S5_GROUPS, S5_STATE), 0.5),
        'state_ffn_conv': nrm(10, (DEPTH, DEC_BATCH, CONV_W - 1, 2 * D_FF), 1.0),
        'norm_mix': 1.0 + nrm(11, (DEPTH, D_MODEL), 0.02),
        'norm_ffn': 1.0 + nrm(12, (DEPTH, D_MODEL), 0.02),
        'norm_final': 1.0 + nrm(13, (D_MODEL,), 0.02),
        'w_in_even': nrm(14, (D_MODEL, EVEN_COLS), D_MODEL ** -0.5),
        'b_mlstm_gates': jnp.concatenate([nrm(15, (ML_HEADS,), 0.1), 3.0 + nrm(16, (ML_HEADS,), 0.5)]),
        'sw_sinks': nrm(17, (SW_HEADS,), 0.5),
        'w_out_even': nrm(18, (EVEN_MIX, D_MODEL), EVEN_MIX ** -0.5),
        'w_in_odd': nrm(19, (D_MODEL, ODD_COLS), D_MODEL ** -0.5),
        'w_gla_gate_up': nrm(20, (GLA_RANK, GLA_HEADS * GLA_DK), GLA_RANK ** -0.5),
        'b_gla_gate': nrm(21, (GLA_HEADS * GLA_DK,), 0.1),
        'g_gla_norm': 1.0 + nrm(22, (GLA_HEADS * GLA_DV,), 0.02),
        's5_a_re': -0.5 + nrm(23, (S5_GROUPS, S5_STATE), 0.01),
        's5_a_im': math.pi * n_idx[None, :] + nrm(24, (S5_GROUPS, S5_STATE), 0.01),
        's5_log_dt': jax.random.uniform(ks[25], (S5_GROUPS,), f32, math.log(1e-3), math.log(1e-1)),
        's5_b_re': nrm(26, (S5_GROUPS, S5_STATE, S5_GROUP), 1.0),
        's5_b_im': nrm(27, (S5_GROUPS, S5_STATE, S5_GROUP), 1.0),
        's5_c_re': nrm(28, (S5_GROUPS, S5_GROUP, S5_STATE), S5_STATE ** -0.5),
        's5_c_im': nrm(29, (S5_GROUPS, S5_GROUP, S5_STATE), S5_STATE ** -0.5),
        's5_d': nrm(30, (S5_GROUPS, S5_GROUP), 1.0),
        'w_s5_glu': nrm(31, (S5_WIDTH, S5_WIDTH), S5_WIDTH ** -0.5),
        'b_s5_glu': nrm(32, (S5_WIDTH,), 0.02),
        'w_out_odd': nrm(33, (ODD_MIX, D_MODEL), ODD_MIX ** -0.5),
        'w_ffn_up': nrm(34, (DEPTH, D_MODEL, 2 * D_FF), D_MODEL ** -0.5),
        'ffn_conv_w': nrm(35, (DEPTH, CONV_W, 2 * D_FF), CONV_W ** -0.5),
        'ffn_conv_b': nrm(36, (DEPTH, 2 * D_FF), 0.02),
        'w_ffn_down': nrm(37, (DEPTH, D_FF, D_MODEL), D_FF ** -0.5),
    }


def reference(x_prompt, x_sample, state_mlstm_C, state_mlstm_n, state_mlstm_m, cache_swa_k, cache_swa_v,
              state_gla, state_s5_re, state_s5_im, state_ffn_conv, norm_mix, norm_ffn, norm_final,
              w_in_even, b_mlstm_gates, sw_sinks, w_out_even, w_in_odd, w_gla_gate_up, b_gla_gate, g_gla_norm,
              s5_a_re, s5_a_im, s5_log_dt, s5_b_re, s5_b_im, s5_c_re, s5_c_im, s5_d, w_s5_glu, b_s5_glu,
              w_out_odd, w_ffn_up, ffn_conv_w, ffn_conv_b, w_ffn_down):
    f32 = jnp.float32
    W = {'norm_mix': norm_mix, 'norm_ffn': norm_ffn, 'norm_final': norm_final,
         'w_in_even': w_in_even, 'b_mlstm_gates': b_mlstm_gates, 'sw_sinks': sw_sinks, 'w_out_even': w_out_even,
         'w_in_odd': w_in_odd, 'w_gla_gate_up': w_gla_gate_up, 'b_gla_gate': b_gla_gate, 'g_gla_norm': g_gla_norm,
         's5_a_re': s5_a_re, 's5_a_im': s5_a_im, 's5_log_dt': s5_log_dt, 's5_b_re': s5_b_re, 's5_b_im': s5_b_im,
         's5_c_re': s5_c_re, 's5_c_im': s5_c_im, 's5_d': s5_d, 'w_s5_glu': w_s5_glu, 'b_s5_glu': b_s5_glu,
         'w_out_odd': w_out_odd, 'w_ffn_up': w_ffn_up, 'ffn_conv_w': ffn_conv_w, 'ffn_conv_b': ffn_conv_b,
         'w_ffn_down': w_ffn_down}
    Bp, Tp = x_prompt.shape[:2]
    st_prompt = {'ml_C': jnp.zeros((Bp, ML_HEADS, ML_DK, ML_DV), f32),
                 'ml_n': jnp.zeros((Bp, ML_HEADS, ML_DK), f32),
                 'ml_m': jnp.zeros((Bp, ML_HEADS), f32),
                 'kbuf': None, 'vbuf': None,
                 'gla': jnp.zeros((Bp, GLA_HEADS, GLA_DK, GLA_DV), f32),
                 's5_re': jnp.zeros((Bp, S5_GROUPS, S5_STATE), f32),
                 's5_im': jnp.zeros((Bp, S5_GROUPS, S5_STATE), f32),
                 'conv': jnp.zeros((DEPTH, Bp, CONV_W - 1, 2 * D_FF), x_prompt.dtype)}
    st_sample = {'ml_C': state_mlstm_C, 'ml_n': state_mlstm_n, 'ml_m': state_mlstm_m,
                 'kbuf': cache_swa_k, 'vbuf': cache_swa_v, 'gla': state_gla,
                 's5_re': state_s5_re, 's5_im': state_s5_im, 'conv': state_ffn_conv}
    y_prompt, np_ = _trunk(x_prompt, jnp.arange(Tp), st_prompt, W)
    y_sample, ns_ = _trunk(x_sample, PAST_LEN + jnp.arange(x_sample.shape[1]), st_sample, W)
    return (y_prompt, y_sample,
            np_['ml_C'], ns_['ml_C'], np_['ml_n'], ns_['ml_n'], np_['ml_m'], ns_['ml_m'],
            np_['kbuf'], ns_['kbuf'], np_['vbuf'], ns_['vbuf'], np_['gla'], ns_['gla'],
            np_['s5_re'], ns_['s5_re'], np_['s5_im'], ns_['s5_im'], np_['conv'], ns_['conv'])
```

```python
import functools
import math

import numpy as np
import jax
import jax.numpy as jnp
from jax import lax
from jax.experimental import pallas as pl
from jax.experimental.pallas import tpu as pltpu

F32 = jnp.float32
BF16 = jnp.bfloat16
HI = lax.Precision.HIGHEST
NT = (((1,), (1,)), ((), ()))
TN = (((0,), (0,)), ((), ()))

D_MODEL = 1024
ML_HEADS, ML_DK, ML_DV = 4, 128, 128
SW_HEADS, SW_KV_HEADS, SW_HD, WINDOW = 8, 2, 64, 128
SW_GQ = SW_HEADS // SW_KV_HEADS
ROPE_THETA = 10000.0
GLA_HEADS, GLA_DK, GLA_DV, GLA_RANK, GLA_TAU = 4, 64, 128, 16, 16.0
S5_WIDTH, S5_GROUP, S5_STATE = 512, 16, 64
S5_GROUPS = S5_WIDTH // S5_GROUP
S5_LANES = S5_GROUPS * S5_STATE
D_FF = 2816
CONV_W = 3
CHUNK = 64
EPS = 1e-6

LANE = 128
SUBLANE = 8
VMEM_LIMIT = 56 * 1024 * 1024

EVEN_PAD = 2944
ODD_PAD = 2176


def _params(sem):
    return pltpu.CompilerParams(dimension_semantics=sem, vmem_limit_bytes=VMEM_LIMIT)


def _norm_matmul_kernel(x_ref, g_ref, w_ref, o_ref):
    x = x_ref[...]
    ms = jnp.mean(x * x, axis=-1, keepdims=True)
    hn = (x * lax.rsqrt(ms + EPS) * g_ref[...]).astype(BF16)
    o_ref[...] = jnp.dot(hn, w_ref[...], preferred_element_type=F32)


def _norm_matmul(x, g, w, tm):
    n, d = x.shape
    c = w.shape[1]
    return pl.pallas_call(
        _norm_matmul_kernel,
        grid=(n // tm,),
        in_specs=[pl.BlockSpec((tm, d), lambda i: (i, 0)),
                  pl.BlockSpec((1, d), lambda i: (0, 0)),
                  pl.BlockSpec((d, c), lambda i: (0, 0))],
        out_specs=pl.BlockSpec((tm, c), lambda i: (i, 0)),
        out_shape=jax.ShapeDtypeStruct((n, c), F32),
        compiler_params=_params(("parallel",)),
        name="norm_matmul",
    )(x, g.reshape(1, d), w)


def _proj_res_kernel(a1_ref, a2_ref, w1_ref, w2_ref, r_ref, o_ref):
    acc = jnp.dot(a1_ref[...], w1_ref[...], preferred_element_type=F32)
    acc = acc + jnp.dot(a2_ref[...], w2_ref[...], preferred_element_type=F32)
    o_ref[...] = r_ref[...] + acc


def _proj_res(a1, a2, w1, w2, res, tm):
    n, k1 = a1.shape
    k2 = a2.shape[1]
    d = res.shape[1]
    return pl.pallas_call(
        _proj_res_kernel,
        grid=(n // tm,),
        in_specs=[pl.BlockSpec((tm, k1), lambda i: (i, 0)),
                  pl.BlockSpec((tm, k2), lambda i: (i, 0)),
                  pl.BlockSpec((k1, d), lambda i: (0, 0)),
                  pl.BlockSpec((k2, d), lambda i: (0, 0)),
                  pl.BlockSpec((tm, d), lambda i: (i, 0))],
        out_specs=pl.BlockSpec((tm, d), lambda i: (i, 0)),
        out_shape=jax.ShapeDtypeStruct((n, d), F32),
        compiler_params=_params(("parallel",)),
        name="proj_res",
    )(a1, a2, w1, w2, res)


def _mlstm_kernel(q_ref, k_ref, v_ref, og_ref, gz_ref, bias_ref, tri_ref, sel_ref,
                  c0_ref, n0_ref, m0_ref,
                  h_ref, co_ref, no_ref, mo_ref,
                  c_s, n_s, m_s, *, L):
    c = pl.program_id(1)

    @pl.when(c == 0)
    def _():
        c_s[...] = c0_ref[0]
        n_s[...] = n0_ref[0]
        m_s[...] = m0_ref[0]

    gates = gz_ref[0] + bias_ref[...]
    lf = jax.nn.log_sigmoid(gates)
    bcum = jnp.dot(tri_ref[...], lf, precision=HI, preferred_element_type=F32)
    rows_b = lax.dot_general(sel_ref[...], bcum, NT, precision=HI, preferred_element_type=F32)
    rows_g = lax.dot_general(sel_ref[...], gates, NT, precision=HI, preferred_element_type=F32)
    ti = lax.broadcasted_iota(jnp.int32, (L, L), 0)
    si = lax.broadcasted_iota(jnp.int32, (L, L), 1)
    causal = si <= ti
    scale = ML_DK ** -0.5

    for h in range(ML_HEADS):
        hs = slice(h * ML_DK, (h + 1) * ML_DK)
        q = q_ref[0, :, hs]
        k = k_ref[0, :, hs] * scale
        v = v_ref[0, :, hs]
        b_col = bcum[:, ML_HEADS + h:ML_HEADS + h + 1]
        i_col = gates[:, h:h + 1]
        b_row = rows_b[ML_HEADS + h:ML_HEADS + h + 1, :]
        i_row = rows_g[h:h + 1, :]
        m_prev = m_s[:, h:h + 1]
        d = jnp.where(causal, b_col - b_row + i_row, -jnp.inf)
        inter = b_col + m_prev
        m_t = jnp.maximum(inter, jnp.max(d, axis=1, keepdims=True))
        w = jnp.exp(d - m_t)
        a_inter = jnp.exp(inter - m_t)
        qb = q.astype(BF16)
        kb = k.astype(BF16)
        vb = v.astype(BF16)
        s = lax.dot_general(qb, kb, NT, preferred_element_type=F32) * w
        ch = c_s[h]
        nh = n_s[h:h + 1, :]
        num = (jnp.dot(s.astype(BF16), vb, preferred_element_type=F32)
               + a_inter * jnp.dot(qb, ch.astype(BF16), preferred_element_type=F32))
        den = jnp.sum(s, axis=1, keepdims=True) + a_inter * jnp.sum(q * nh, axis=1, keepdims=True)
        hh = num / jnp.maximum(jnp.abs(den), jnp.exp(-m_t))
        hh = hh * jax.nn.sigmoid(og_ref[0, :, hs])
        h_ref[0, :, hs] = hh.astype(h_ref.dtype)
        m_new = m_t[L - 1:L, :]
        b_end = b_col[L - 1:L, :]
        w_end = jnp.exp(b_end - b_col + i_col - m_new)
        decay = jnp.exp(b_end + m_prev - m_new)
        kw = k * w_end
        c_s[h] = decay * ch + lax.dot_general(kw.astype(BF16), vb, TN, preferred_element_type=F32)
        n_s[h:h + 1, :] = decay * nh + jnp.sum(kw, axis=0, keepdims=True)
        m_s[:, h:h + 1] = m_new

    @pl.when(c == pl.num_programs(1) - 1)
    def _():
        co_ref[0] = c_s[...]
        no_ref[0] = n_s[...]
        mo_ref[0] = m_s[...]


def _mlstm(z3, bias, c0, n0, m0, L):
    b, t, _ = z3.shape
    nc = t // L
    hw = ML_HEADS * ML_DK
    tri = jnp.asarray(np.tril(np.ones((L, L), np.float32)))
    sel = jnp.asarray(np.eye(SUBLANE, LANE, dtype=np.float32))
    gate_blk = (EVEN_PAD - LANE) // LANE
    col = lambda j: (lambda bi, ci: (bi, ci, j))
    const2 = lambda bi, ci: (0, 0)
    return pl.pallas_call(
        functools.partial(_mlstm_kernel, L=L),
        grid=(b, nc),
        in_specs=[pl.BlockSpec((1, L, hw), col(0)),
                  pl.BlockSpec((1, L, hw), col(1)),
                  pl.BlockSpec((1, L, hw), col(2)),
                  pl.BlockSpec((1, L, hw), col(3)),
                  pl.BlockSpec((1, L, LANE), col(gate_blk)),
                  pl.BlockSpec((1, LANE), const2),
                  pl.BlockSpec((L, L), const2),
                  pl.BlockSpec((SUBLANE, LANE), const2),
                  pl.BlockSpec((1, ML_HEADS, ML_DK, ML_DV), lambda bi, ci: (bi, 0, 0, 0)),
                  pl.BlockSpec((1, ML_HEADS, ML_DK), lambda bi, ci: (bi, 0, 0)),
                  pl.BlockSpec((1, 1, ML_HEADS), lambda bi, ci: (bi, 0, 0))],
        out_specs=[pl.BlockSpec((1, L, hw), col(0)),
                   pl.BlockSpec((1, ML_HEADS, ML_DK, ML_DV), lambda bi, ci: (bi, 0, 0, 0)),
                   pl.BlockSpec((1, ML_HEADS, ML_DK), lambda bi, ci: (bi, 0, 0)),
                   pl.BlockSpec((1, 1, ML_HEADS), lambda bi, ci: (bi, 0, 0))],
        out_shape=[jax.ShapeDtypeStruct((b, t, hw), BF16),
                   jax.ShapeDtypeStruct((b, ML_HEADS, ML_DK, ML_DV), F32),
                   jax.ShapeDtypeStruct((b, ML_HEADS, ML_DK), F32),
                   jax.ShapeDtypeStruct((b, 1, ML_HEADS), F32)],
        scratch_shapes=[pltpu.VMEM((ML_HEADS, ML_DK, ML_DV), F32),
                        pltpu.VMEM((ML_HEADS, ML_DK), F32),
                        pltpu.VMEM((1, ML_HEADS), F32)],
        compiler_params=_params(("parallel", "arbitrary")),
        name="mlstm",
    )(z3, z3, z3, z3, z3, bias, tri, sel, c0, n0, m0.reshape(b, 1, ML_HEADS))


def _rope(x, cos, sin_signed, width):
    half = SW_HD // 2
    lane = lax.broadcasted_iota(jnp.int32, x.shape, 1)
    first = (lane % SW_HD) < half
    partner = jnp.where(first, pltpu.roll(x, width - half, axis=1), pltpu.roll(x, half, axis=1))
    return x * cos + partner * sin_signed


def _sink_attend(q_h, kk_h, vv_h, valid, sink):
    s = lax.dot_general(q_h.astype(BF16), kk_h.astype(BF16), NT, preferred_element_type=F32) * (SW_HD ** -0.5)
    s = jnp.where(valid, s, -jnp.inf)
    m = jnp.maximum(jnp.max(s, axis=1, keepdims=True), sink)
    e = jnp.exp(s - m)
    p = e / (jnp.sum(e, axis=1, keepdims=True) + jnp.exp(sink - m))
    return jnp.dot(p.astype(BF16), vv_h.astype(BF16), preferred_element_type=F32)


def _swa_prompt_kernel(q_ref, kc_ref, kp_ref, vc_ref, vp_ref, cosc_ref, sinc_ref, cosp_ref, sinp_ref, sink_ref,
                       h_ref, ko_ref):
    n = pl.program_id(1)
    qw = SW_HEADS * SW_HD
    kw = SW_KV_HEADS * SW_HD
    cosc = cosc_ref[...]
    sinc = sinc_ref[...]
    q = _rope(q_ref[0], jnp.concatenate([cosc] * (qw // LANE), axis=1),
              jnp.concatenate([sinc] * (qw // LANE), axis=1), qw)
    k_cur = _rope(kc_ref[0], cosc, sinc, kw)
    k_prev = _rope(kp_ref[0], cosp_ref[...], sinp_ref[...], kw)
    kk = jnp.concatenate([k_prev, k_cur], axis=0)
    vv = jnp.concatenate([vp_ref[0], vc_ref[0]], axis=0)
    i = lax.broadcasted_iota(jnp.int32, (WINDOW, 2 * WINDOW), 0)
    j = lax.broadcasted_iota(jnp.int32, (WINDOW, 2 * WINDOW), 1)
    diff = WINDOW + i - j
    valid = (diff >= 0) & (diff < WINDOW) & ((n > 0) | (j >= WINDOW))
    for h in range(SW_HEADS):
        kv = h // SW_GQ
        o = _sink_attend(q[:, h * SW_HD:(h + 1) * SW_HD], kk[:, kv * SW_HD:(kv + 1) * SW_HD],
                         vv[:, kv * SW_HD:(kv + 1) * SW_HD], valid, sink_ref[:, h:h + 1])
        h_ref[0, :, h * SW_HD:(h + 1) * SW_HD] = o.astype(h_ref.dtype)

    @pl.when(n == pl.num_programs(1) - 1)
    def _():
        ko_ref[0] = k_cur


def _swa_prompt(z3, cos, sin_signed, sinks):
    b, t, _ = z3.shape
    nb = t // WINDOW
    qw = SW_HEADS * SW_HD
    kw = SW_KV_HEADS * SW_HD
    qblk, kblk, vblk = 2048 // qw, 2560 // kw, 2688 // kw
    cur = lambda j: (lambda bi, ni: (bi, ni, j))
    prev = lambda j: (lambda bi, ni: (bi, jnp.maximum(ni - 1, 0), j))
    return pl.pallas_call(
        _swa_prompt_kernel,
        grid=(b, nb),
        in_specs=[pl.BlockSpec((1, WINDOW, qw), cur(qblk)),
                  pl.BlockSpec((1, WINDOW, kw), cur(kblk)),
                  pl.BlockSpec((1, WINDOW, kw), prev(kblk)),
                  pl.BlockSpec((1, WINDOW, kw), cur(vblk)),
                  pl.BlockSpec((1, WINDOW, kw), prev(vblk)),
                  pl.BlockSpec((WINDOW, LANE), lambda bi, ni: (ni, 0)),
                  pl.BlockSpec((WINDOW, LANE), lambda bi, ni: (ni, 0)),
                  pl.BlockSpec((WINDOW, LANE), lambda bi, ni: (jnp.maximum(ni - 1, 0), 0)),
                  pl.BlockSpec((WINDOW, LANE), lambda bi, ni: (jnp.maximum(ni - 1, 0), 0)),
                  pl.BlockSpec((1, SW_HEADS), lambda bi, ni: (0, 0))],
        out_specs=[pl.BlockSpec((1, WINDOW, qw), cur(0)),
                   pl.BlockSpec((1, WINDOW, kw), lambda bi, ni: (bi, 0, 0))],
        out_shape=[jax.ShapeDtypeStruct((b, t, qw), BF16),
                   jax.ShapeDtypeStruct((b, WINDOW, kw), F32)],
        compiler_params=_params(("parallel", "arbitrary")),
        name="swa_prompt",
    )(z3, z3, z3, z3, z3, cos, sin_signed, cos, sin_signed, sinks.reshape(1, SW_HEADS))


def _swa_sample_kernel(q_ref, k_ref, v_ref, kbuf_ref, vbuf_ref, cos_ref, sin_ref, sink_ref,
                       h_ref, ko_ref, vo_ref, *, T):
    qw = SW_HEADS * SW_HD
    kw = SW_KV_HEADS * SW_HD
    cos = cos_ref[...]
    sin = sin_ref[...]
    q = _rope(q_ref[0], jnp.concatenate([cos] * (qw // LANE), axis=1),
              jnp.concatenate([sin] * (qw // LANE), axis=1), qw)
    k_new = _rope(k_ref[0], cos, sin, kw)
    kk = jnp.concatenate([kbuf_ref[0], k_new], axis=0)
    vv = jnp.concatenate([vbuf_ref[0], v_ref[0]], axis=0)
    i = lax.broadcasted_iota(jnp.int32, (T, WINDOW + T), 0)
    j = lax.broadcasted_iota(jnp.int32, (T, WINDOW + T), 1)
    diff = WINDOW + i - j
    valid = (diff >= 0) & (diff < WINDOW)
    for h in range(SW_HEADS):
        kv = h // SW_GQ
        o = _sink_attend(q[:, h * SW_HD:(h + 1) * SW_HD], kk[:, kv * SW_HD:(kv + 1) * SW_HD],
                         vv[:, kv * SW_HD:(kv + 1) * SW_HD], valid, sink_ref[:, h:h + 1])
        h_ref[0, :, h * SW_HD:(h + 1) * SW_HD] = o.astype(h_ref.dtype)
    ko_ref[0] = kk[T:, :]
    vo_ref[0] = vv[T:, :]


def _swa_sample(z3, kbuf, vbuf, cos, sin_signed, sinks):
    b, t, _ = z3.shape
    qw = SW_HEADS * SW_HD
    kw = SW_KV_HEADS * SW_HD
    qblk, kblk, vblk = 2048 // qw, 2560 // kw, 2688 // kw
    col = lambda j: (lambda bi: (bi, 0, j))
    const2 = lambda bi: (0, 0)
    return pl.pallas_call(
        functools.partial(_swa_sample_kernel, T=t),
        grid=(b,),
        in_specs=[pl.BlockSpec((1, t, qw), col(qblk)),
                  pl.BlockSpec((1, t, kw), col(kblk)),
                  pl.BlockSpec((1, t, kw), col(vblk)),
                  pl.BlockSpec((1, WINDOW, kw), col(0)),
                  pl.BlockSpec((1, WINDOW, kw), col(0)),
                  pl.BlockSpec((t, LANE), const2),
                  pl.BlockSpec((t, LANE), const2),
                  pl.BlockSpec((1, SW_HEADS), const2)],
        out_specs=[pl.BlockSpec((1, t, qw), col(0)),
                   pl.BlockSpec((1, WINDOW, kw), col(0)),
                   pl.BlockSpec((1, WINDOW, kw), col(0))],
        out_shape=[jax.ShapeDtypeStruct((b, t, qw), BF16),
                   jax.ShapeDtypeStruct((b, WINDOW, kw), F32),
                   jax.ShapeDtypeStruct((b, WINDOW, kw), F32)],
        compiler_params=_params(("parallel",)),
        name="swa_sample",
    )(z3, z3, z3, kbuf, vbuf, cos, sin_signed, sinks.reshape(1, SW_HEADS))


def _gla_tables(L):
    nlev = int(math.log2(L))
    assert 2 ** nlev == L
    mstack = np.zeros((2 * nlev * L, L), np.float32)
    neg = np.zeros((2 * nlev * L, 1), np.float32)
    masks = np.zeros((nlev + 1, L, L), np.float32)
    masks[0] = np.eye(L)
    for l in range(1, nlev + 1):
        n = 2 ** l
        qa = 2 * (l - 1) * L
        ka = qa + L
        for t in range(L):
            blk, pos = divmod(t, n)
            m = blk * n + n // 2 - 1
            if pos >= n // 2:
                mstack[qa + t, m + 1:t + 1] = 1.0
                neg[ka + t, 0] = -1e30
                masks[l, t, blk * n:blk * n + n // 2] = 1.0
            else:
                mstack[ka + t, t + 1:m + 1] = 1.0
                neg[qa + t, 0] = -1e30
    return nlev, jnp.asarray(mstack), jnp.asarray(neg), jnp.asarray(masks)


def _gla_kernel(q_ref, k_ref, v_ref, r_ref, glr_ref, wg_ref, bg_ref, gn_ref, tri_ref, mstack_ref, neg_ref,
                masks_ref, eye_ref, s0_ref,
                o_ref, so_ref, s_s, *, L, nlev):
    c = pl.program_id(1)

    @pl.when(c == 0)
    def _():
        s_s[...] = s0_ref[0]

    q = q_ref[0] * (GLA_DK ** -0.5)
    k = k_ref[0]
    pre = jnp.dot(glr_ref[0], wg_ref[...], precision=HI, preferred_element_type=F32) + bg_ref[...]
    lg = jax.nn.log_sigmoid(pre) * (1.0 / GLA_TAU)
    G = jnp.dot(tri_ref[...], lg, precision=HI, preferred_element_type=F32)
    ab = jnp.dot(mstack_ref[...], lg, precision=HI, preferred_element_type=F32) + neg_ref[...]
    E = jnp.exp(ab)
    qg = q * jnp.exp(G)
    qls = [(q * E[2 * l * L:(2 * l + 1) * L]).astype(BF16) for l in range(nlev)]
    kls = [(k * E[(2 * l + 1) * L:(2 * l + 2) * L]).astype(BF16) for l in range(nlev)]
    qb = q.astype(BF16)
    kb = k.astype(BF16)

    for h in range(GLA_HEADS):
        ds = slice(h * GLA_DK, (h + 1) * GLA_DK)
        vs = slice(h * GLA_DV, (h + 1) * GLA_DV)
        att = jnp.where(masks_ref[0] > 0.5, lax.dot_general(qb[:, ds], kb[:, ds], NT, preferred_element_type=F32), 0.0)
        for l in range(nlev):
            part = lax.dot_general(qls[l][:, ds], kls[l][:, ds], NT, preferred_element_type=F32)
            att = att + jnp.where(masks_ref[l + 1] > 0.5, part, 0.0)
        vb = v_ref[0, :, vs].astype(BF16)
        sh = s_s[h]
        o = (jnp.dot(att.astype(BF16), vb, preferred_element_type=F32)
             + jnp.dot(qg[:, ds].astype(BF16), sh.astype(BF16), preferred_element_type=F32))
        o = o * lax.rsqrt(jnp.mean(o * o, axis=-1, keepdims=True) + EPS)
        o = o * gn_ref[:, vs] * jax.nn.silu(r_ref[0, :, vs])
        o_ref[0, :, vs] = o.astype(o_ref.dtype)
        g_h = G[:, ds]
        g_end = g_h[L - 1:L, :]
        k_end = (k[:, ds] * jnp.exp(g_end - g_h)).astype(BF16)
        g_end_b = lax.dot_general(eye_ref[...], jnp.broadcast_to(g_end, (GLA_DV, GLA_DK)), NT,
                                  precision=HI, preferred_element_type=F32)
        s_s[h] = jnp.exp(g_end_b) * sh + lax.dot_general(k_end, vb, TN, preferred_element_type=F32)

    @pl.when(c == pl.num_programs(1) - 1)
    def _():
        so_ref[0] = s_s[...]


def _gla(z3, wg, bg, gn, s0, L):
    b, t, _ = z3.shape
    nc = t // L
    nlev, mstack, neg, masks = _gla_tables(L)
    tri = jnp.asarray(np.tril(np.ones((L, L), np.float32)))
    eye = jnp.asarray(np.eye(GLA_DK, dtype=np.float32))
    qk_w = GLA_HEADS * GLA_DK
    v_w = GLA_HEADS * GLA_DV
    col = lambda j: (lambda bi, ci: (bi, ci, j))
    const2 = lambda bi, ci: (0, 0)
    const3 = lambda bi, ci: (0, 0, 0)
    return pl.pallas_call(
        functools.partial(_gla_kernel, L=L, nlev=nlev),
        grid=(b, nc),
        in_specs=[pl.BlockSpec((1, L, qk_w), col(0)),
                  pl.BlockSpec((1, L, qk_w), col(1)),
                  pl.BlockSpec((1, L, v_w), col(1)),
                  pl.BlockSpec((1, L, v_w), col(2)),
                  pl.BlockSpec((1, L, LANE), col(2048 // LANE)),
                  pl.BlockSpec((LANE, qk_w), const2),
                  pl.BlockSpec((1, qk_w), const2),
                  pl.BlockSpec((1, v_w), const2),
                  pl.BlockSpec((L, L), const2),
                  pl.BlockSpec(mstack.shape, const2),
                  pl.BlockSpec(neg.shape, const2),
                  pl.BlockSpec(masks.shape, const3),
                  pl.BlockSpec((GLA_DK, GLA_DK), const2),
                  pl.BlockSpec((1, GLA_HEADS, GLA_DK, GLA_DV), lambda bi, ci: (bi, 0, 0, 0))],
        out_specs=[pl.BlockSpec((1, L, v_w), col(0)),
                   pl.BlockSpec((1, GLA_HEADS, GLA_DK, GLA_DV), lambda bi, ci: (bi, 0, 0, 0))],
        out_shape=[jax.ShapeDtypeStruct((b, t, v_w), BF16),
                   jax.ShapeDtypeStruct((b, GLA_HEADS, GLA_DK, GLA_DV), F32)],
        scratch_shapes=[pltpu.VMEM((GLA_HEADS, GLA_DK, GLA_DV), F32)],
        compiler_params=_params(("parallel", "arbitrary")),
        name="gla",
    )(z3, z3, z3, z3, z3, wg, bg, gn, tri, mstack, neg, masks, eye, s0)


def _s5_kernel(u_ref, bbre_ref, bbim_ref, cre_ref, cim_ref, d_ref, wglu_ref, bglu_ref, pre_ref, pim_ref,
               x0re_ref, x0im_ref,
               o_ref, xore_ref, xoim_ref,
               xr_s, xi_s, cr_s, ci_s, *, rows, sequential):
    c = pl.program_id(1)
    groups = rows // SUBLANE
    u = u_ref[0]
    ub = u.astype(BF16)
    xr = jnp.dot(ub, bbre_ref[...], preferred_element_type=F32)
    xi = jnp.dot(ub, bbim_ref[...], preferred_element_type=F32)
    pre = pre_ref[...]
    pim = pim_ref[...]
    rowmod = lax.broadcasted_iota(jnp.int32, (rows, 1), 0) % SUBLANE
    for sh in (1, 2, 4):
        ar = pre[sh - 1:sh, :]
        ai = pim[sh - 1:sh, :]
        keep = rowmod >= sh
        sr = jnp.where(keep, pltpu.roll(xr, sh, axis=0), 0.0)
        si = jnp.where(keep, pltpu.roll(xi, sh, axis=0), 0.0)
        xr, xi = xr + ar * sr - ai * si, xi + ar * si + ai * sr
    xr_s[...] = xr
    xi_s[...] = xi

    if sequential:
        @pl.when(c == 0)
        def _():
            cr_s[...] = x0re_ref[0]
            ci_s[...] = x0im_ref[0]

        def body(j, carry):
            cr, ci = carry
            r0 = pl.multiple_of(j * SUBLANE, SUBLANE)
            nr = xr_s[pl.ds(r0, SUBLANE), :] + pre * cr - pim * ci
            ni = xi_s[pl.ds(r0, SUBLANE), :] + pre * ci + pim * cr
            xr_s[pl.ds(r0, SUBLANE), :] = nr
            xi_s[pl.ds(r0, SUBLANE), :] = ni
            return nr[SUBLANE - 1:SUBLANE, :], ni[SUBLANE - 1:SUBLANE, :]

        cr, ci = lax.fori_loop(0, groups, body, (cr_s[...], ci_s[...]))
        cr_s[...] = cr
        ci_s[...] = ci

        @pl.when(c == pl.num_programs(1) - 1)
        def _():
            xore_ref[0] = cr
            xoim_ref[0] = ci
    else:
        def body(j, carry):
            r0 = pl.multiple_of(j * SUBLANE, SUBLANE)
            cr = x0re_ref[0, pl.ds(j, 1), :]
            ci = x0im_ref[0, pl.ds(j, 1), :]
            nr = xr_s[pl.ds(r0, SUBLANE), :] + pre * cr - pim * ci
            ni = xi_s[pl.ds(r0, SUBLANE), :] + pre * ci + pim * cr
            xr_s[pl.ds(r0, SUBLANE), :] = nr
            xi_s[pl.ds(r0, SUBLANE), :] = ni
            xore_ref[0, pl.ds(j, 1), :] = nr[SUBLANE - 1:SUBLANE, :]
            xoim_ref[0, pl.ds(j, 1), :] = ni[SUBLANE - 1:SUBLANE, :]
            return carry

        lax.fori_loop(0, groups, body, 0)

    y = (jnp.dot(xr_s[...].astype(BF16), cre_ref[...], preferred_element_type=F32)
         - jnp.dot(xi_s[...].astype(BF16), cim_ref[...], preferred_element_type=F32)
         + d_ref[...] * u)
    zz = jax.nn.gelu(y)
    gate = jnp.dot(zz.astype(BF16), wglu_ref[...], preferred_element_type=F32) + bglu_ref[...]
    o_ref[0] = (zz * jax.nn.sigmoid(gate)).astype(o_ref.dtype)


def _s5(z3, tabs, x0re, x0im, rows, sequential):
    b, t, _ = z3.shape
    nblk = t // rows
    groups = rows // SUBLANE
    ublk = 1536 // S5_WIDTH
    const2 = lambda bi, ci: (0, 0)
    if sequential:
        st_spec = pl.BlockSpec((1, 1, S5_LANES), lambda bi, ci: (bi, 0, 0))
        st_shape = jax.ShapeDtypeStruct((b, 1, S5_LANES), F32)
    else:
        st_spec = pl.BlockSpec((1, groups, S5_LANES), lambda bi, ci: (bi, ci, 0))
        st_shape = jax.ShapeDtypeStruct(x0re.shape, F32)
    return pl.pallas_call(
        functools.partial(_s5_kernel, rows=rows, sequential=sequential),
        grid=(b, nblk),
        in_specs=[pl.BlockSpec((1, rows, S5_WIDTH), lambda bi, ci: (bi, ci, ublk)),
                  pl.BlockSpec((S5_WIDTH, S5_LANES), const2),
                  pl.BlockSpec((S5_WIDTH, S5_LANES), const2),
                  pl.BlockSpec((S5_LANES, S5_WIDTH), const2),
                  pl.BlockSpec((S5_LANES, S5_WIDTH), const2),
                  pl.BlockSpec((1, S5_WIDTH), const2),
                  pl.BlockSpec((S5_WIDTH, S5_WIDTH), const2),
                  pl.BlockSpec((1, S5_WIDTH), const2),
                  pl.BlockSpec((SUBLANE, S5_LANES), const2),
                  pl.BlockSpec((SUBLANE, S5_LANES), const2),
                  st_spec, st_spec],
        out_specs=[pl.BlockSpec((1, rows, S5_WIDTH), lambda bi, ci: (bi, ci, 0)),
                   st_spec, st_spec],
        out_shape=[jax.ShapeDtypeStruct((b, t, S5_WIDTH), BF16), st_shape, st_shape],
        scratch_shapes=[pltpu.VMEM((rows, S5_LANES), F32),
                        pltpu.VMEM((rows, S5_LANES), F32),
                        pltpu.VMEM((1, S5_LANES), F32),
                        pltpu.VMEM((1, S5_LANES), F32)],
        compiler_params=_params(("parallel", "arbitrary")),
        name="s5",
    )(z3, tabs['bbre'], tabs['bbim'], tabs['cre'], tabs['cim'], tabs['d'], tabs['wglu'], tabs['bglu'],
      tabs['pre'], tabs['pim'], x0re, x0im)


def _s5_tables(a_re, a_im, log_dt, b_re, b_im, c_re, c_im, d, w_glu, b_glu):
    lam_re = a_re.astype(F32)
    lam_im = a_im.astype(F32)
    dt = jnp.exp(log_dt.astype(F32))[:, None]
    mag = jnp.exp(lam_re * dt)
    ang = lam_im * dt
    ab_re = mag * jnp.cos(ang)
    ab_im = mag * jnp.sin(ang)
    den = lam_re * lam_re + lam_im * lam_im
    co_re = ((ab_re - 1.0) * lam_re + ab_im * lam_im) / den
    co_im = (ab_im * lam_re - (ab_re - 1.0) * lam_im) / den
    b_re = b_re.astype(F32)
    b_im = b_im.astype(F32)
    bb_re = co_re[..., None] * b_re - co_im[..., None] * b_im
    bb_im = co_re[..., None] * b_im + co_im[..., None] * b_re
    eye = jnp.eye(S5_GROUPS, dtype=F32)
    blockdiag_in = lambda bb: jnp.einsum('gph,gk->ghkp', bb, eye).reshape(S5_WIDTH, S5_LANES)
    blockdiag_out = lambda cc: jnp.einsum('ghp,gk->gpkh', cc.astype(F32), eye).reshape(S5_LANES, S5_WIDTH)
    ar = ab_re.reshape(1, S5_LANES)
    ai = ab_im.reshape(1, S5_LANES)
    pr, pi = [ar], [ai]
    for _ in range(SUBLANE - 1):
        pr, pi = pr + [pr[-1] * ar - pi[-1] * ai], pi + [pr[-1] * ai + pi[-1] * ar]
    return {
        'bbre': blockdiag_in(bb_re).astype(BF16), 'bbim': blockdiag_in(bb_im).astype(BF16),
        'cre': blockdiag_out(c_re).astype(BF16), 'cim': blockdiag_out(c_im).astype(BF16),
        'd': d.astype(F32).reshape(1, S5_WIDTH),
        'wglu': w_glu.astype(BF16), 'bglu': b_glu.astype(F32).reshape(1, S5_WIDTH),
        'pre': jnp.concatenate(pr, axis=0), 'pim': jnp.concatenate(pi, axis=0),
    }


def _rmsnorm_val(x, g):
    return x * lax.rsqrt(jnp.mean(x * x, axis=-1, keepdims=True) + EPS) * g


def _ffn_seq_kernel(ua_ref, ug_ref, ha_ref, hg_ref, sa_ref, sg_ref, cwa_ref, cwg_ref, cba_ref, cbg_ref,
                    wd_ref, r_ref, gf_ref, o_ref, exta, extg, *, tm, final_norm):
    i = pl.program_id(1)

    def conv(u_ref, halo_ref, st_ref, cw_ref, cb_ref, ext):
        ext[pl.ds(SUBLANE, tm), :] = u_ref[0]
        ext[pl.ds(0, SUBLANE), :] = halo_ref[0]

        @pl.when(i == 0)
        def _():
            ext[pl.ds(SUBLANE - (CONV_W - 1), CONV_W - 1), :] = st_ref[0]

        acc = cb_ref[...]
        for j in range(CONV_W):
            acc = acc + ext[pl.ds(SUBLANE - (CONV_W - 1) + j, tm), :] * cw_ref[j:j + 1, :]
        return acc

    ca = conv(ua_ref, ha_ref, sa_ref, cwa_ref, cba_ref, exta)
    cg = conv(ug_ref, hg_ref, sg_ref, cwg_ref, cbg_ref, extg)
    act = (jax.nn.silu(cg) * ca).astype(BF16)
    out = r_ref[0] + jnp.dot(act, wd_ref[...], preferred_element_type=F32)
    if final_norm:
        out = _rmsnorm_val(out, gf_ref[...])
    o_ref[0] = out


def _ffn_seq(u3, state, cw, cb, wd, res3, gfinal, tm, final_norm):
    b, t, _ = u3.shape
    d = res3.shape[-1]
    hb = tm // SUBLANE
    main = lambda j: (lambda bi, i: (bi, i, j))
    halo = lambda j: (lambda bi, i: (bi, jnp.maximum(i * hb - 1, 0), j))
    return pl.pallas_call(
        functools.partial(_ffn_seq_kernel, tm=tm, final_norm=final_norm),
        grid=(b, t // tm),
        in_specs=[pl.BlockSpec((1, tm, D_FF), main(0)),
                  pl.BlockSpec((1, tm, D_FF), main(1)),
                  pl.BlockSpec((1, SUBLANE, D_FF), halo(0)),
                  pl.BlockSpec((1, SUBLANE, D_FF), halo(1)),
                  pl.BlockSpec((1, CONV_W - 1, D_FF), lambda bi, i: (bi, 0, 0)),
                  pl.BlockSpec((1, CONV_W - 1, D_FF), lambda bi, i: (bi, 0, 1)),
                  pl.BlockSpec((CONV_W, D_FF), lambda bi, i: (0, 0)),
                  pl.BlockSpec((CONV_W, D_FF), lambda bi, i: (0, 1)),
                  pl.BlockSpec((1, D_FF), lambda bi, i: (0, 0)),
                  pl.BlockSpec((1, D_FF), lambda bi, i: (0, 1)),
                  pl.BlockSpec((D_FF, d), lambda bi, i: (0, 0)),
                  pl.BlockSpec((1, tm, d), lambda bi, i: (bi, i, 0)),
                  pl.BlockSpec((1, d), lambda bi, i: (0, 0))],
        out_specs=pl.BlockSpec((1, tm, d), lambda bi, i: (bi, i, 0)),
        out_shape=jax.ShapeDtypeStruct((b, t, d), F32),
        scratch_shapes=[pltpu.VMEM((tm + SUBLANE, D_FF), F32),
                        pltpu.VMEM((tm + SUBLANE, D_FF), F32)],
        compiler_params=_params(("parallel", "arbitrary")),
        name="ffn_seq",
    )(u3, u3, u3, u3, state, state, cw, cw, cb, cb, wd, res3, gfinal)


def _ffn_tm_kernel(ua_ref, ug_ref, sa_ref, sg_ref, cwa_ref, cwg_ref, cba_ref, cbg_ref, wd_ref, r_ref, gf_ref,
                   o_ref, acc, *, T, nb, final_norm):
    j = pl.program_id(0)

    @pl.when(j == 0)
    def _():
        acc[...] = jnp.zeros_like(acc)

    def conv_rows(u_ref, st_ref, cw_ref, cb_ref):
        rows = [st_ref[s] for s in range(CONV_W - 1)] + [u_ref[t] for t in range(T)]
        out = []
        for t in range(T):
            a = cb_ref[...]
            for jj in range(CONV_W):
                a = a + rows[t + jj] * cw_ref[jj:jj + 1, :]
            out.append(a)
        return out

    ca = conv_rows(ua_ref, sa_ref, cwa_ref, cba_ref)
    cg = conv_rows(ug_ref, sg_ref, cwg_ref, cbg_ref)
    for t in range(T):
        act = (jax.nn.silu(cg[t]) * ca[t]).astype(BF16)
        acc[pl.ds(t * nb, nb), :] += jnp.dot(act, wd_ref[...], preferred_element_type=F32)

    @pl.when(j == pl.num_programs(0) - 1)
    def _():
        out = r_ref[...] + acc[...]
        if final_norm:
            out = _rmsnorm_val(out, gf_ref[...])
        o_ref[...] = out


def _ffn_tm(u_tm, st_tm, cw, cb, wd, res_tm, gfinal, tf, final_norm):
    T, nb, _ = u_tm.shape
    d = res_tm.shape[-1]
    nj = D_FF // tf
    return pl.pallas_call(
        functools.partial(_ffn_tm_kernel, T=T, nb=nb, final_norm=final_norm),
        grid=(nj,),
        in_specs=[pl.BlockSpec((T, nb, tf), lambda j: (0, 0, j)),
                  pl.BlockSpec((T, nb, tf), lambda j: (0, 0, j + nj)),
                  pl.BlockSpec((CONV_W - 1, nb, tf), lambda j: (0, 0, j)),
                  pl.BlockSpec((CONV_W - 1, nb, tf), lambda j: (0, 0, j + nj)),
                  pl.BlockSpec((CONV_W, tf), lambda j: (0, j)),
                  pl.BlockSpec((CONV_W, tf), lambda j: (0, j + nj)),
                  pl.BlockSpec((1, tf), lambda j: (0, j)),
                  pl.BlockSpec((1, tf), lambda j: (0, j + nj)),
                  pl.BlockSpec((tf, d), lambda j: (j, 0)),
                  pl.BlockSpec((T * nb, d), lambda j: (0, 0)),
                  pl.BlockSpec((1, d), lambda j: (0, 0))],
        out_specs=pl.BlockSpec((T * nb, d), lambda j: (0, 0)),
        out_shape=jax.ShapeDtypeStruct((T * nb, d), F32),
        scratch_shapes=[pltpu.VMEM((T * nb, d), F32)],
        compiler_params=_params(("arbitrary",)),
        name="ffn_tm",
    )(u_tm, u_tm, st_tm, st_tm, cw, cw, cb, cb, wd, res_tm, gfinal)


def _rope_tables(pos):
    half = SW_HD // 2
    inv = ROPE_THETA ** (-jnp.arange(half, dtype=F32) / half)
    ang = pos.astype(F32)[:, None] * inv[None, :]
    cos = jnp.cos(ang)
    sin = jnp.sin(ang)
    reps = LANE // SW_HD
    cos_t = jnp.tile(jnp.concatenate([cos, cos], axis=1), (1, reps))
    sin_t = jnp.tile(jnp.concatenate([-sin, sin], axis=1), (1, reps))
    return cos_t, sin_t


def _prep_weights(W):
    P = {}
    we = W['w_in_even']
    parts = jnp.split(we, [int(c) for c in np.cumsum((512, 512, 512, 512, 8, 512, 128))], axis=1)
    q_m, k_m, v_m, o_m, g_m, q_a, k_a, v_a = parts
    pad = jnp.zeros((D_MODEL, EVEN_PAD - 2816 - 8), we.dtype)
    P['w_in_even'] = jnp.concatenate([q_m, k_m, v_m, o_m, q_a, k_a, v_a, g_m, pad], axis=1).astype(BF16)
    P['b_gates'] = jnp.pad(W['b_mlstm_gates'].astype(F32), (0, LANE - 2 * ML_HEADS)).reshape(1, LANE)
    wo = W['w_in_odd']
    q, k, v, r, glr, u = jnp.split(wo, [int(c) for c in np.cumsum((256, 256, 512, 512, 16))], axis=1)
    pad = jnp.zeros((D_MODEL, ODD_PAD - 2048 - GLA_RANK), wo.dtype)
    P['w_in_odd'] = jnp.concatenate([q, k, v, r, u, glr, pad], axis=1).astype(BF16)
    P['w_gate_up'] = jnp.pad(W['w_gla_gate_up'].astype(F32), ((0, LANE - GLA_RANK), (0, 0)))
    P['b_gate'] = W['b_gla_gate'].astype(F32).reshape(1, -1)
    P['g_gla'] = W['g_gla_norm'].astype(F32).reshape(1, -1)
    hm = ML_HEADS * ML_DV
    P['w_out_even'] = (W['w_out_even'][:hm].astype(BF16), W['w_out_even'][hm:].astype(BF16))
    hg = GLA_HEADS * GLA_DV
    P['w_out_odd'] = (W['w_out_odd'][:hg].astype(BF16), W['w_out_odd'][hg:].astype(BF16))
    P['w_ffn_up'] = W['w_ffn_up'].astype(BF16)
    P['w_ffn_down'] = W['w_ffn_down'].astype(BF16)
    P['s5'] = _s5_tables(W['s5_a_re'], W['s5_a_im'], W['s5_log_dt'], W['s5_b_re'], W['s5_b_im'],
                         W['s5_c_re'], W['s5_c_im'], W['s5_d'], W['w_s5_glu'], W['b_s5_glu'])
    return P


def _trunk(x, pos, st, W, P, is_prompt):
    b, t, d = x.shape
    n = b * t
    L = CHUNK if t % CHUNK == 0 else t
    tm = 512 if n % 512 == 0 else n
    tm_up = 256 if n % 256 == 0 else n
    cos, sin_signed = _rope_tables(pos)
    gfinal = W['norm_final'].astype(F32).reshape(1, d)
    new = {}
    conv_out = []
    h = x.reshape(n, d)

    z = _norm_matmul(h, W['norm_mix'][0], P['w_in_even'], tm)
    z3 = z.reshape(b, t, EVEN_PAD)
    h_m, new['ml_C'], new['ml_n'], m_new = _mlstm(z3, P['b_gates'], st['ml_C'], st['ml_n'], st['ml_m'], L)
    new['ml_m'] = m_new.reshape(b, ML_HEADS)
    if is_prompt:
        h_a, kb = _swa_prompt(z3, cos, sin_signed, W['sw_sinks'].astype(F32))
        vb = z3[:, t - WINDOW:, 2688:2816]
    else:
        h_a, kb, vb = _swa_sample(z3, st['kbuf'].reshape(b, WINDOW, -1), st['vbuf'].reshape(b, WINDOW, -1),
                                  cos, sin_signed, W['sw_sinks'].astype(F32))
    new['kbuf'] = kb.reshape(b, WINDOW, SW_KV_HEADS, SW_HD)
    new['vbuf'] = vb.reshape(b, WINDOW, SW_KV_HEADS, SW_HD)
    h = _proj_res(h_m.reshape(n, -1), h_a.reshape(n, -1), *P['w_out_even'], h, tm)
    h, cb = _ffn(h, 0, st, W, P, b, t, gfinal, tm_up, is_prompt, final_norm=False)
    conv_out.append(cb)

    z = _norm_matmul(h, W['norm_mix'][1], P['w_in_odd'], tm)
    z3 = z.reshape(b, t, ODD_PAD)
    o_g, new['gla'] = _gla(z3, P['w_gate_up'], P['b_gate'], P['g_gla'], st['gla'], L)
    x0re = st['s5_re'].astype(F32).reshape(b, 1, S5_LANES)
    x0im = st['s5_im'].astype(F32).reshape(b, 1, S5_LANES)
    if is_prompt:
        o_s, xre, xim = _s5(z3, P['s5'], x0re, x0im, rows=256 if t % 256 == 0 else t, sequential=True)
    else:
        o_s, xre, xim = _s5(z3.reshape(1, n, ODD_PAD), P['s5'], x0re.reshape(1, b, S5_LANES),
                            x0im.reshape(1, b, S5_LANES), rows=256 if n % 256 == 0 else n, sequential=False)
    new['s5_re'] = xre.reshape(b, S5_GROUPS, S5_STATE)
    new['s5_im'] = xim.reshape(b, S5_GROUPS, S5_STATE)
    h = _proj_res(o_g.reshape(n, -1), o_s.reshape(n, -1), *P['w_out_odd'], h, tm)
    h, cb = _ffn(h, 1, st, W, P, b, t, gfinal, tm_up, is_prompt, final_norm=True)
    conv_out.append(cb)
    new['conv'] = jnp.stack(conv_out)
    return h.reshape(b, t, d), new


def _ffn(h, layer, st, W, P, b, t, gfinal, tm_up, is_prompt, final_norm):
    n, d = h.shape
    u = _norm_matmul(h, W['norm_ffn'][layer], P['w_ffn_up'][layer], tm_up)
    u3 = u.reshape(b, t, 2 * D_FF)
    cw = W['ffn_conv_w'][layer].astype(F32)
    cb = W['ffn_conv_b'][layer].astype(F32).reshape(1, 2 * D_FF)
    wd = P['w_ffn_down'][layer]
    state = st['conv'][layer].astype(F32)
    if is_prompt:
        out = _ffn_seq(u3, state, cw, cb, wd, h.reshape(b, t, d), gfinal, 256, final_norm).reshape(n, d)
    else:
        u_tm = jnp.swapaxes(u3, 0, 1)
        st_tm = jnp.swapaxes(state, 0, 1)
        res_tm = jnp.swapaxes(h.reshape(b, t, d), 0, 1).reshape(n, d)
        out_tm = _ffn_tm(u_tm, st_tm, cw, cb, wd, res_tm, gfinal, 256, final_norm)
        out = jnp.swapaxes(out_tm.reshape(t, b, d), 0, 1).reshape(n, d)
    return out, u3[:, t - (CONV_W - 1):, :]


def kernel(x_prompt, x_sample, state_mlstm_C, state_mlstm_n, state_mlstm_m, cache_swa_k, cache_swa_v,
           state_gla, state_s5_re, state_s5_im, state_ffn_conv, norm_mix, norm_ffn, norm_final,
           w_in_even, b_mlstm_gates, sw_sinks, w_out_even, w_in_odd, w_gla_gate_up, b_gla_gate, g_gla_norm,
           s5_a_re, s5_a_im, s5_log_dt, s5_b_re, s5_b_im, s5_c_re, s5_c_im, s5_d, w_s5_glu, b_s5_glu,
           w_out_odd, w_ffn_up, ffn_conv_w, ffn_conv_b, w_ffn_down):
    W = {'norm_mix': norm_mix.astype(F32), 'norm_ffn': norm_ffn.astype(F32), 'norm_final': norm_final,
         'w_in_even': w_in_even, 'b_mlstm_gates': b_mlstm_gates, 'sw_sinks': sw_sinks, 'w_out_even': w_out_even,
         'w_in_odd': w_in_odd, 'w_gla_gate_up': w_gla_gate_up, 'b_gla_gate': b_gla_gate, 'g_gla_norm': g_gla_norm,
         's5_a_re': s5_a_re, 's5_a_im': s5_a_im, 's5_log_dt': s5_log_dt, 's5_b_re': s5_b_re, 's5_b_im': s5_b_im,
         's5_c_re': s5_c_re, 's5_c_im': s5_c_im, 's5_d': s5_d, 'w_s5_glu': w_s5_glu, 'b_s5_glu': b_s5_glu,
         'w_out_odd': w_out_odd, 'w_ffn_up': w_ffn_up, 'ffn_conv_w': ffn_conv_w, 'ffn_conv_b': ffn_conv_b,
         'w_ffn_down': w_ffn_down}
    P = _prep_weights(W)
    bp, tp = x_prompt.shape[:2]
    st_prompt = {'ml_C': jnp.zeros((bp, ML_HEADS, ML_DK, ML_DV), F32),
                 'ml_n': jnp.zeros((bp, ML_HEADS, ML_DK), F32),
                 'ml_m': jnp.zeros((bp, ML_HEADS), F32),
                 'gla': jnp.zeros((bp, GLA_HEADS, GLA_DK, GLA_DV), F32),
                 's5_re': jnp.zeros((bp, S5_GROUPS, S5_STATE), F32),
                 's5_im': jnp.zeros((bp, S5_GROUPS, S5_STATE), F32),
                 'conv': jnp.zeros((2, bp, CONV_W - 1, 2 * D_FF), F32)}
    st_sample = {'ml_C': state_mlstm_C.astype(F32), 'ml_n': state_mlstm_n.astype(F32),
                 'ml_m': state_mlstm_m.astype(F32), 'kbuf': cache_swa_k.astype(F32),
                 'vbuf': cache_swa_v.astype(F32), 'gla': state_gla.astype(F32),
                 's5_re': state_s5_re, 's5_im': state_s5_im, 'conv': state_ffn_conv}
    past_len = 16384
    y_p, np_ = _trunk(x_prompt.astype(F32), jnp.arange(tp), st_prompt, W, P, True)
    y_s, ns_ = _trunk(x_sample.astype(F32), past_len + jnp.arange(x_sample.shape[1]), st_sample, W, P, False)
    return (y_p, y_s,
            np_['ml_C'], ns_['ml_C'], np_['ml_n'], ns_['ml_n'], np_['ml_m'], ns_['ml_m'],
            np_['kbuf'], ns_['kbuf'], np_['vbuf'], ns_['vbuf'], np_['gla'], ns_['gla'],
            np_['s5_re'], ns_['s5_re'], np_['s5_im'], ns_['s5_im'], np_['conv'], ns_['conv'])
```

```python
import functools
import math

import numpy as np
import jax
import jax.numpy as jnp
from jax import lax
from jax.experimental import pallas as pl
from jax.experimental.pallas import tpu as pltpu

F32 = jnp.float32
BF16 = jnp.bfloat16
HI = lax.Precision.HIGHEST
NT = (((1,), (1,)), ((), ()))
TN = (((0,), (0,)), ((), ()))

D_MODEL = 1024
ML_HEADS, ML_DK, ML_DV = 4, 128, 128
SW_HEADS, SW_KV_HEADS, SW_HD, WINDOW = 8, 2, 64, 128
SW_GQ = SW_HEADS // SW_KV_HEADS
ROPE_THETA = 10000.0
GLA_HEADS, GLA_DK, GLA_DV, GLA_RANK, GLA_TAU = 4, 64, 128, 16, 16.0
S5_WIDTH, S5_GROUP, S5_STATE = 512, 16, 64
S5_GROUPS = S5_WIDTH // S5_GROUP
S5_LANES = S5_GROUPS * S5_STATE
D_FF = 2816
FFN_TF = 1408
CONV_W = 3
CHUNK = 64
EPS = 1e-6

LANE = 128
SUBLANE = 8
VMEM_LIMIT = 56 * 1024 * 1024

EVEN_QA, EVEN_KD, EVEN_VD, EVEN_PAD = 2048, 2560, 2816, 3200
ODD_PAD = 2176


def _params(sem):
    return pltpu.CompilerParams(dimension_semantics=sem, vmem_limit_bytes=VMEM_LIMIT)


def _norm_matmul_kernel(x_ref, g_ref, w_ref, o_ref):
    x = x_ref[...]
    ms = jnp.mean(x * x, axis=-1, keepdims=True)
    hn = (x * lax.rsqrt(ms + EPS) * g_ref[...]).astype(BF16)
    o_ref[...] = jnp.dot(hn, w_ref[...], preferred_element_type=F32)


def _norm_matmul(x, g, w, tm):
    n, d = x.shape
    c = w.shape[1]
    return pl.pallas_call(
        _norm_matmul_kernel,
        grid=(n // tm,),
        in_specs=[pl.BlockSpec((tm, d), lambda i: (i, 0)),
                  pl.BlockSpec((1, d), lambda i: (0, 0)),
                  pl.BlockSpec((d, c), lambda i: (0, 0))],
        out_specs=pl.BlockSpec((tm, c), lambda i: (i, 0)),
        out_shape=jax.ShapeDtypeStruct((n, c), F32),
        compiler_params=_params(("parallel",)),
        name="norm_matmul",
    )(x, g.reshape(1, d), w)


def _proj_res_kernel(a1_ref, a2_ref, w1_ref, w2_ref, r_ref, o_ref):
    acc = jnp.dot(a1_ref[...], w1_ref[...], preferred_element_type=F32)
    acc = acc + jnp.dot(a2_ref[...], w2_ref[...], preferred_element_type=F32)
    o_ref[...] = r_ref[...] + acc


def _proj_res(a1, a2, w1, w2, res, tm):
    n, k1 = a1.shape
    k2 = a2.shape[1]
    d = res.shape[1]
    return pl.pallas_call(
        _proj_res_kernel,
        grid=(n // tm,),
        in_specs=[pl.BlockSpec((tm, k1), lambda i: (i, 0)),
                  pl.BlockSpec((tm, k2), lambda i: (i, 0)),
                  pl.BlockSpec((k1, d), lambda i: (0, 0)),
                  pl.BlockSpec((k2, d), lambda i: (0, 0)),
                  pl.BlockSpec((tm, d), lambda i: (i, 0))],
        out_specs=pl.BlockSpec((tm, d), lambda i: (i, 0)),
        out_shape=jax.ShapeDtypeStruct((n, d), F32),
        compiler_params=_params(("parallel",)),
        name="proj_res",
    )(a1, a2, w1, w2, res)


def _dot_sel(m_bf16, x, dims=None):
    hi = x.astype(BF16)
    r1 = x - hi.astype(F32)
    mid = r1.astype(BF16)
    lo = (r1 - mid.astype(F32)).astype(BF16)
    if dims is None:
        f = lambda p: jnp.dot(m_bf16, p, preferred_element_type=F32)
    else:
        f = lambda p: lax.dot_general(m_bf16, p, dims, preferred_element_type=F32)
    return f(hi) + f(mid) + f(lo)


def _mlstm_kernel(q_ref, k_ref, v_ref, og_ref, gz_ref, bias_ref, tri_ref, sel_ref,
                  c0_ref, n0_ref, m0_ref,
                  h_ref, co_ref, no_ref, mo_ref,
                  c_s, n_s, m_s, *, L, Bb):
    c = pl.program_id(1)

    @pl.when(c == 0)
    def _():
        c_s[...] = c0_ref[...]
        n_s[...] = n0_ref[...]
        m_s[...] = m0_ref[...]

    ti = lax.broadcasted_iota(jnp.int32, (L, L), 0)
    si = lax.broadcasted_iota(jnp.int32, (L, L), 1)
    causal = si <= ti
    for bi in range(Bb):
        _mlstm_one(bi, q_ref, k_ref, v_ref, og_ref, gz_ref, bias_ref, tri_ref, sel_ref, h_ref,
                   c_s, n_s, m_s, causal, L)

    @pl.when(c == pl.num_programs(1) - 1)
    def _():
        co_ref[...] = c_s[...]
        no_ref[...] = n_s[...]
        mo_ref[...] = m_s[...]


def _mlstm_one(bi, q_ref, k_ref, v_ref, og_ref, gz_ref, bias_ref, tri_ref, sel_ref, h_ref,
               c_s, n_s, m_s, causal, L):
    gates = gz_ref[bi] + bias_ref[...]
    lf = jax.nn.log_sigmoid(gates)
    bcum = _dot_sel(tri_ref[...], lf)
    rows_b = _dot_sel(sel_ref[...], bcum, NT)
    rows_g = _dot_sel(sel_ref[...], gates, NT)
    scale = ML_DK ** -0.5

    for h in range(ML_HEADS):
        hs = slice(h * ML_DK, (h + 1) * ML_DK)
        q = q_ref[bi, :, hs]
        k = k_ref[bi, :, hs] * scale
        v = v_ref[bi, :, hs]
        b_col = bcum[:, ML_HEADS + h:ML_HEADS + h + 1]
        i_col = gates[:, h:h + 1]
        b_row = rows_b[ML_HEADS + h:ML_HEADS + h + 1, :]
        i_row = rows_g[h:h + 1, :]
        m_prev = m_s[bi, :, h:h + 1]
        d = jnp.where(causal, b_col - b_row + i_row, -jnp.inf)
        inter = b_col + m_prev
        m_t = jnp.maximum(inter, jnp.max(d, axis=1, keepdims=True))
        w = jnp.exp(d - m_t)
        a_inter = jnp.exp(inter - m_t)
        qb = q.astype(BF16)
        kb = k.astype(BF16)
        vb = v.astype(BF16)
        s = lax.dot_general(qb, kb, NT, preferred_element_type=F32) * w
        ch = c_s[bi, h]
        nh = n_s[bi, h:h + 1, :]
        num = (jnp.dot(s.astype(BF16), vb, preferred_element_type=F32)
               + a_inter * jnp.dot(qb, ch.astype(BF16), preferred_element_type=F32))
        den = jnp.sum(s, axis=1, keepdims=True) + a_inter * jnp.sum(q * nh, axis=1, keepdims=True)
        hh = num / jnp.maximum(jnp.abs(den), jnp.exp(-m_t))
        hh = hh * jax.nn.sigmoid(og_ref[bi, :, hs])
        h_ref[bi, :, hs] = hh.astype(h_ref.dtype)
        m_new = m_t[L - 1:L, :]
        b_end = b_col[L - 1:L, :]
        w_end = jnp.exp(b_end - b_col + i_col - m_new)
        decay = jnp.exp(b_end + m_prev - m_new)
        kw = k * w_end
        c_s[bi, h] = decay * ch + lax.dot_general(kw.astype(BF16), vb, TN, preferred_element_type=F32)
        n_s[bi, h:h + 1, :] = decay * nh + jnp.sum(kw, axis=0, keepdims=True)
        m_s[bi, :, h:h + 1] = m_new


def _mlstm(z3, bias, c0, n0, m0, L, Bb):
    b, t, _ = z3.shape
    nc = t // L
    hw = ML_HEADS * ML_DK
    tri = jnp.asarray(np.tril(np.ones((L, L), np.float32)), BF16)
    sel = jnp.asarray(np.eye(SUBLANE, LANE, dtype=np.float32), BF16)
    gate_blk = (EVEN_PAD - LANE) // LANE
    col = lambda j: (lambda bi, ci: (bi, ci, j))
    const2 = lambda bi, ci: (0, 0)
    return pl.pallas_call(
        functools.partial(_mlstm_kernel, L=L, Bb=Bb),
        grid=(b // Bb, nc),
        in_specs=[pl.BlockSpec((Bb, L, hw), col(0)),
                  pl.BlockSpec((Bb, L, hw), col(1)),
                  pl.BlockSpec((Bb, L, hw), col(2)),
                  pl.BlockSpec((Bb, L, hw), col(3)),
                  pl.BlockSpec((Bb, L, LANE), col(gate_blk)),
                  pl.BlockSpec((1, LANE), const2),
                  pl.BlockSpec((L, L), const2),
                  pl.BlockSpec((SUBLANE, LANE), const2),
                  pl.BlockSpec((Bb, ML_HEADS, ML_DK, ML_DV), lambda bi, ci: (bi, 0, 0, 0)),
                  pl.BlockSpec((Bb, ML_HEADS, ML_DK), lambda bi, ci: (bi, 0, 0)),
                  pl.BlockSpec((Bb, 1, ML_HEADS), lambda bi, ci: (bi, 0, 0))],
        out_specs=[pl.BlockSpec((Bb, L, hw), col(0)),
                   pl.BlockSpec((Bb, ML_HEADS, ML_DK, ML_DV), lambda bi, ci: (bi, 0, 0, 0)),
                   pl.BlockSpec((Bb, ML_HEADS, ML_DK), lambda bi, ci: (bi, 0, 0)),
                   pl.BlockSpec((Bb, 1, ML_HEADS), lambda bi, ci: (bi, 0, 0))],
        out_shape=[jax.ShapeDtypeStruct((b, t, hw), BF16),
                   jax.ShapeDtypeStruct((b, ML_HEADS, ML_DK, ML_DV), F32),
                   jax.ShapeDtypeStruct((b, ML_HEADS, ML_DK), F32),
                   jax.ShapeDtypeStruct((b, 1, ML_HEADS), F32)],
        scratch_shapes=[pltpu.VMEM((Bb, ML_HEADS, ML_DK, ML_DV), F32),
                        pltpu.VMEM((Bb, ML_HEADS, ML_DK), F32),
                        pltpu.VMEM((Bb, 1, ML_HEADS), F32)],
        compiler_params=_params(("parallel", "arbitrary")),
        name="mlstm",
    )(z3, z3, z3, z3, z3, bias, tri, sel, c0, n0, m0.reshape(b, 1, ML_HEADS))


def _rope(x, cos, sin_signed, width):
    half = SW_HD // 2
    lane = lax.broadcasted_iota(jnp.int32, x.shape, 1)
    first = (lane % SW_HD) < half
    partner = jnp.where(first, pltpu.roll(x, width - half, axis=1), pltpu.roll(x, half, axis=1))
    return x * cos + partner * sin_signed


def _undup(xd):
    lane = lax.broadcasted_iota(jnp.int32, xd.shape[:-1] + (LANE,), xd.ndim - 1)
    return jnp.where(lane < SW_HD, xd[..., :LANE], xd[..., LANE:])


def _swa_prompt_kernel(sink_ref, q_ref, kc_ref, kp_ref, vc_ref, vp_ref, cosc_ref, sinc_ref, cosp_ref, sinp_ref,
                       h_ref, ko_ref):
    n = pl.program_id(1)
    W2 = 2 * WINDOW
    cosc, sinc = cosc_ref[...], sinc_ref[...]
    cosp, sinp = cosp_ref[...], sinp_ref[...]
    q = _rope(q_ref[0], jnp.concatenate([cosc] * 4, axis=1), jnp.concatenate([sinc] * 4, axis=1), 4 * LANE)
    k_cur = _rope(kc_ref[0], jnp.concatenate([cosc] * 2, axis=1), jnp.concatenate([sinc] * 2, axis=1), 2 * LANE)
    k_prev = _rope(kp_ref[0], jnp.concatenate([cosp] * 2, axis=1), jnp.concatenate([sinp] * 2, axis=1), 2 * LANE)
    kd = jnp.concatenate([k_prev, k_cur], axis=0)
    vd = jnp.concatenate([vp_ref[0], vc_ref[0]], axis=0)
    i = lax.broadcasted_iota(jnp.int32, (WINDOW, W2), 0)
    j = lax.broadcasted_iota(jnp.int32, (WINDOW, W2), 1)
    diff = WINDOW + i - j
    valid = (diff >= 0) & (diff < WINDOW) & ((n > 0) | (j >= WINDOW))
    row = lax.broadcasted_iota(jnp.int32, (2 * W2, LANE), 0)
    lane = lax.broadcasted_iota(jnp.int32, (2 * W2, LANE), 1)
    bd = (row < W2) == (lane < SW_HD)
    ones_bd = bd.astype(BF16)
    low = lax.broadcasted_iota(jnp.int32, (WINDOW, LANE), 1) < SW_HD
    for kv in range(SW_KV_HEADS):
        kblk = kd[:, kv * LANE:(kv + 1) * LANE]
        vblk = vd[:, kv * LANE:(kv + 1) * LANE]
        kbd = jnp.where(bd, jnp.concatenate([kblk, kblk], axis=0), 0.0).astype(BF16)
        vbd = jnp.where(bd, jnp.concatenate([vblk, vblk], axis=0), 0.0).astype(BF16)
        for pair in range(SW_GQ // 2):
            h0 = kv * SW_GQ + 2 * pair
            q2 = q[:, h0 * SW_HD:(h0 + 2) * SW_HD].astype(BF16)
            s = lax.dot_general(q2, kbd, NT, preferred_element_type=F32) * (SW_HD ** -0.5)
            es, sink_terms = [], []
            for hh in range(2):
                sh = jnp.where(valid, s[:, hh * W2:(hh + 1) * W2], -jnp.inf)
                sink = sink_ref[h0 + hh]
                m = jnp.maximum(jnp.max(sh, axis=1, keepdims=True), sink)
                es.append(jnp.exp(sh - m).astype(BF16))
                sink_terms.append(jnp.exp(sink - m))
            e = jnp.concatenate(es, axis=1)
            num = jnp.dot(e, vbd, preferred_element_type=F32)
            den = jnp.dot(e, ones_bd, preferred_element_type=F32) + jnp.where(low, sink_terms[0], sink_terms[1])
            h_ref[0, :, h0 * SW_HD:(h0 + 2) * SW_HD] = (num / den).astype(h_ref.dtype)

    @pl.when(n == pl.num_programs(1) - 1)
    def _():
        ko_ref[0] = _undup(k_cur)


def _swa_prompt(z3, cos, sin_signed, sinks):
    b, t, _ = z3.shape
    nb = t // WINDOW
    qw = SW_HEADS * SW_HD
    kw = SW_KV_HEADS * SW_HD
    dw = 2 * kw
    qblk, kblk, vblk = EVEN_QA // qw, EVEN_KD // dw, EVEN_VD // dw
    cur = lambda j: (lambda bi, ni: (bi, ni, j))
    prev = lambda j: (lambda bi, ni: (bi, jnp.maximum(ni - 1, 0), j))
    return pl.pallas_call(
        _swa_prompt_kernel,
        grid=(b, nb),
        in_specs=[pl.BlockSpec(memory_space=pltpu.SMEM),
                  pl.BlockSpec((1, WINDOW, qw), cur(qblk)),
                  pl.BlockSpec((1, WINDOW, dw), cur(kblk)),
                  pl.BlockSpec((1, WINDOW, dw), prev(kblk)),
                  pl.BlockSpec((1, WINDOW, dw), cur(vblk)),
                  pl.BlockSpec((1, WINDOW, dw), prev(vblk)),
                  pl.BlockSpec((WINDOW, LANE), lambda bi, ni: (ni, 0)),
                  pl.BlockSpec((WINDOW, LANE), lambda bi, ni: (ni, 0)),
                  pl.BlockSpec((WINDOW, LANE), lambda bi, ni: (jnp.maximum(ni - 1, 0), 0)),
                  pl.BlockSpec((WINDOW, LANE), lambda bi, ni: (jnp.maximum(ni - 1, 0), 0))],
        out_specs=[pl.BlockSpec((1, WINDOW, qw), cur(0)),
                   pl.BlockSpec((1, WINDOW, kw), lambda bi, ni: (bi, 0, 0))],
        out_shape=[jax.ShapeDtypeStruct((b, t, qw), BF16),
                   jax.ShapeDtypeStruct((b, WINDOW, kw), F32)],
        compiler_params=_params(("parallel", "arbitrary")),
        name="swa_prompt",
    )(sinks, z3, z3, z3, z3, z3, cos, sin_signed, cos, sin_signed)


def _swa_sample_kernel(q_ref, k_ref, v_ref, kbuf_ref, vbuf_ref, cos_ref, sin_ref, sink_ref,
                       h_ref, ko_ref, vo_ref, *, T, Bb):
    qw = SW_HEADS * SW_HD
    kw = SW_KV_HEADS * SW_HD
    cos = cos_ref[...]
    sin = sin_ref[...]
    q = _rope(q_ref[...].reshape(Bb * T, qw), jnp.concatenate([cos] * (qw // LANE), axis=1),
              jnp.concatenate([sin] * (qw // LANE), axis=1), qw).reshape(Bb, T, qw)
    k_new = _rope(_undup(k_ref[...]).reshape(Bb * T, kw), cos, sin, kw).reshape(Bb, T, kw)
    kk = jnp.concatenate([kbuf_ref[...], k_new], axis=1)
    vv = jnp.concatenate([vbuf_ref[...], _undup(v_ref[...])], axis=1)
    ko_ref[...] = kk[:, T:, :]
    vo_ref[...] = vv[:, T:, :]
    rows = SW_GQ * T
    i = lax.broadcasted_iota(jnp.int32, (rows, WINDOW + T), 0) % T
    j = lax.broadcasted_iota(jnp.int32, (rows, WINDOW + T), 1)
    diff = WINDOW + i - j
    valid = (diff >= 0) & (diff < WINDOW)
    for kv in range(SW_KV_HEADS):
        qs = jnp.concatenate([q[:, :, (kv * SW_GQ + g) * SW_HD:(kv * SW_GQ + g + 1) * SW_HD]
                              for g in range(SW_GQ)], axis=1).astype(BF16)
        kh = kk[:, :, kv * SW_HD:(kv + 1) * SW_HD].astype(BF16)
        vh = vv[:, :, kv * SW_HD:(kv + 1) * SW_HD].astype(BF16)
        s = jnp.einsum('bqd,bkd->bqk', qs, kh, preferred_element_type=F32) * (SW_HD ** -0.5)
        s = jnp.where(valid, s, -jnp.inf)
        sink = sink_ref[kv * rows:(kv + 1) * rows, :]
        m = jnp.maximum(jnp.max(s, axis=-1, keepdims=True), sink)
        e = jnp.exp(s - m)
        p = e / (jnp.sum(e, axis=-1, keepdims=True) + jnp.exp(sink - m))
        o = jnp.einsum('bqk,bkd->bqd', p.astype(BF16), vh, preferred_element_type=F32)
        for g in range(SW_GQ):
            hh = kv * SW_GQ + g
            h_ref[:, :, hh * SW_HD:(hh + 1) * SW_HD] = o[:, g * T:(g + 1) * T, :].astype(h_ref.dtype)


def _swa_sample(z3, kbuf, vbuf, cos, sin_signed, sinks, Bb):
    b, t, _ = z3.shape
    qw = SW_HEADS * SW_HD
    kw = SW_KV_HEADS * SW_HD
    dw = 2 * kw
    qblk, kblk, vblk = EVEN_QA // qw, EVEN_KD // dw, EVEN_VD // dw
    col = lambda j: (lambda bi: (bi, 0, j))
    const2 = lambda bi: (0, 0)
    sink_col = jnp.repeat(sinks, t).reshape(SW_HEADS * t, 1)
    return pl.pallas_call(
        functools.partial(_swa_sample_kernel, T=t, Bb=Bb),
        grid=(b // Bb,),
        in_specs=[pl.BlockSpec((Bb, t, qw), col(qblk)),
                  pl.BlockSpec((Bb, t, dw), col(kblk)),
                  pl.BlockSpec((Bb, t, dw), col(vblk)),
                  pl.BlockSpec((Bb, WINDOW, kw), col(0)),
                  pl.BlockSpec((Bb, WINDOW, kw), col(0)),
                  pl.BlockSpec((Bb * t, LANE), const2),
                  pl.BlockSpec((Bb * t, LANE), const2),
                  pl.BlockSpec((SW_HEADS * t, 1), const2)],
        out_specs=[pl.BlockSpec((Bb, t, qw), col(0)),
                   pl.BlockSpec((Bb, WINDOW, kw), col(0)),
                   pl.BlockSpec((Bb, WINDOW, kw), col(0))],
        out_shape=[jax.ShapeDtypeStruct((b, t, qw), BF16),
                   jax.ShapeDtypeStruct((b, WINDOW, kw), F32),
                   jax.ShapeDtypeStruct((b, WINDOW, kw), F32)],
        compiler_params=_params(("parallel",)),
        name="swa_sample",
    )(z3, z3, z3, kbuf, vbuf, jnp.tile(cos, (Bb, 1)), jnp.tile(sin_signed, (Bb, 1)), sink_col)


def _gla_tables(L):
    nlev = int(math.log2(L))
    assert 2 ** nlev == L
    mstack = np.zeros((2 * nlev * L, L), np.float32)
    neg = np.zeros((2 * nlev * L, 1), np.float32)
    masks = np.zeros((nlev + 1, L, L), np.float32)
    masks[0] = np.eye(L)
    for l in range(1, nlev + 1):
        n = 2 ** l
        qa = 2 * (l - 1) * L
        ka = qa + L
        for t in range(L):
            blk, pos = divmod(t, n)
            m = blk * n + n // 2 - 1
            if pos >= n // 2:
                mstack[qa + t, m + 1:t + 1] = 1.0
                neg[ka + t, 0] = -1e30
                masks[l, t, blk * n:blk * n + n // 2] = 1.0
            else:
                mstack[ka + t, t + 1:m + 1] = 1.0
                neg[qa + t, 0] = -1e30
    masks = np.tile(masks, (1, 1, GLA_HEADS))
    return nlev, jnp.asarray(mstack, BF16), jnp.asarray(neg), jnp.asarray(masks)


def _gla_consts(L):
    kw = GLA_HEADS * GLA_DK
    vw = GLA_HEADS * GLA_DV
    hk = np.zeros((GLA_HEADS * L, kw), np.float32)
    for h in range(GLA_HEADS):
        hk[h * L:(h + 1) * L, h * GLA_DK:(h + 1) * GLA_DK] = 1.0
    hs = np.zeros((vw, kw), np.float32)
    for h in range(GLA_HEADS):
        hs[h * GLA_DV:(h + 1) * GLA_DV, h * GLA_DK:(h + 1) * GLA_DK] = 1.0
    return (jnp.asarray(hk, BF16), jnp.asarray(hs), jnp.asarray(np.eye(GLA_DV), BF16), jnp.asarray(np.eye(kw), BF16))


def _gla_kernel(q_ref, k_ref, v_ref, r_ref, glr_ref, wg_ref, bg_ref, gn_ref, tri_ref, mstack_ref, neg_ref,
                masks_ref, hk_ref, hs_ref, eyev_ref, eyek_ref, s0_ref,
                o_ref, so_ref, st_s, *, L, nlev, Bb):
    c = pl.program_id(1)
    H, DK, DV = GLA_HEADS, GLA_DK, GLA_DV

    @pl.when(c == 0)
    def _():
        for bi in range(Bb):
            for h in range(H):
                pieces = ([jnp.zeros((h * DK, DV), F32)] if h else []) + [s0_ref[bi, h]]
                pieces += [jnp.zeros(((H - 1 - h) * DK, DV), F32)] if h < H - 1 else []
                padded = jnp.concatenate(pieces, axis=0)
                st_s[bi, h * DV:(h + 1) * DV, :] = _dot_sel(eyev_ref[...], padded, NT)

    for bi in range(Bb):
        _gla_one(bi, q_ref, k_ref, v_ref, r_ref, glr_ref, wg_ref, bg_ref, gn_ref, tri_ref, mstack_ref, neg_ref,
                 masks_ref, hk_ref, hs_ref, o_ref, st_s, L, nlev)

    @pl.when(c == pl.num_programs(1) - 1)
    def _():
        for bi in range(Bb):
            for h in range(H):
                blk = st_s[bi, h * DV:(h + 1) * DV, :]
                so_ref[bi, h] = _dot_sel(eyek_ref[h * DK:(h + 1) * DK, :], blk, NT)


def _gla_one(bi, q_ref, k_ref, v_ref, r_ref, glr_ref, wg_ref, bg_ref, gn_ref, tri_ref, mstack_ref, neg_ref,
             masks_ref, hk_ref, hs_ref, o_ref, st_s, L, nlev):
    H, DK, DV = GLA_HEADS, GLA_DK, GLA_DV
    packed_rows = L % (2 * SUBLANE) == 0
    q = q_ref[bi] * (DK ** -0.5)
    k = k_ref[bi]
    pre = jnp.dot(glr_ref[bi], wg_ref[...], precision=HI, preferred_element_type=F32) + bg_ref[...]
    lg = jax.nn.log_sigmoid(pre) * (1.0 / GLA_TAU)
    G = _dot_sel(tri_ref[...], lg)
    E = jnp.exp(_dot_sel(mstack_ref[...], lg) + neg_ref[...])
    hk = hk_ref[...]

    att = None
    for l in range(nlev + 1):
        if l == 0:
            ql, kl = q, k
        else:
            ql = q * E[2 * (l - 1) * L:(2 * l - 1) * L]
            kl = k * E[(2 * l - 1) * L:2 * l * L]
        if packed_rows:
            kst = jnp.concatenate([kl.astype(BF16)] * H, axis=0) * hk
        else:
            kst = (jnp.concatenate([kl] * H, axis=0) * hk.astype(F32)).astype(BF16)
        part = lax.dot_general(ql.astype(BF16), kst, NT, preferred_element_type=F32)
        part = jnp.where(masks_ref[l] > 0.5, part, 0.0)
        att = part if att is None else att + part

    v = v_ref[bi]
    vsrc = v.astype(BF16) if packed_rows else v
    zero = jnp.zeros((L, DV), vsrc.dtype)
    vbd = jnp.concatenate(
        [jnp.concatenate([zero] * h + [vsrc[:, h * DV:(h + 1) * DV]] + [zero] * (H - 1 - h), axis=1)
         for h in range(H)], axis=0).astype(BF16)
    stb = st_s[bi]
    o = (jnp.dot(att.astype(BF16), vbd, preferred_element_type=F32)
         + lax.dot_general((q * jnp.exp(G)).astype(BF16), stb.astype(BF16), NT, preferred_element_type=F32))
    for h in range(H):
        vs = slice(h * DV, (h + 1) * DV)
        oh = o[:, vs]
        oh = oh * lax.rsqrt(jnp.mean(oh * oh, axis=-1, keepdims=True) + EPS)
        oh = oh * gn_ref[:, vs] * jax.nn.silu(r_ref[bi, :, vs])
        o_ref[bi, :, vs] = oh.astype(o_ref.dtype)
    g_end = G[L - 1:L, :]
    k_end = (k * jnp.exp(g_end - G)).astype(BF16)
    upd = lax.dot_general(v.astype(BF16), k_end, TN, preferred_element_type=F32)
    st_s[bi] = stb * jnp.exp(g_end) + jnp.where(hs_ref[...] > 0.5, upd, 0.0)


def _gla(z3, wg, bg, gn, s0, L, Bb):
    b, t, _ = z3.shape
    nc = t // L
    nlev, mstack, neg, masks = _gla_tables(L)
    hk, hs, eyev, eyek = _gla_consts(L)
    tri = jnp.asarray(np.tril(np.ones((L, L), np.float32)), BF16)
    qk_w = GLA_HEADS * GLA_DK
    v_w = GLA_HEADS * GLA_DV
    col = lambda j: (lambda bi, ci: (bi, ci, j))
    const2 = lambda bi, ci: (0, 0)
    const3 = lambda bi, ci: (0, 0, 0)
    return pl.pallas_call(
        functools.partial(_gla_kernel, L=L, nlev=nlev, Bb=Bb),
        grid=(b // Bb, nc),
        in_specs=[pl.BlockSpec((Bb, L, qk_w), col(0)),
                  pl.BlockSpec((Bb, L, qk_w), col(1)),
                  pl.BlockSpec((Bb, L, v_w), col(1)),
                  pl.BlockSpec((Bb, L, v_w), col(2)),
                  pl.BlockSpec((Bb, L, LANE), col(2048 // LANE)),
                  pl.BlockSpec((LANE, qk_w), const2),
                  pl.BlockSpec((1, qk_w), const2),
                  pl.BlockSpec((1, v_w), const2),
                  pl.BlockSpec((L, L), const2),
                  pl.BlockSpec(mstack.shape, const2),
                  pl.BlockSpec(neg.shape, const2),
                  pl.BlockSpec(masks.shape, const3),
                  pl.BlockSpec(hk.shape, const2),
                  pl.BlockSpec(hs.shape, const2),
                  pl.BlockSpec(eyev.shape, const2),
                  pl.BlockSpec(eyek.shape, const2),
                  pl.BlockSpec((Bb, GLA_HEADS, GLA_DK, GLA_DV), lambda bi, ci: (bi, 0, 0, 0))],
        out_specs=[pl.BlockSpec((Bb, L, v_w), col(0)),
                   pl.BlockSpec((Bb, GLA_HEADS, GLA_DK, GLA_DV), lambda bi, ci: (bi, 0, 0, 0))],
        out_shape=[jax.ShapeDtypeStruct((b, t, v_w), BF16),
                   jax.ShapeDtypeStruct((b, GLA_HEADS, GLA_DK, GLA_DV), F32)],
        scratch_shapes=[pltpu.VMEM((Bb, v_w, qk_w), F32)],
        compiler_params=_params(("parallel", "arbitrary")),
        name="gla",
    )(z3, z3, z3, z3, z3, wg, bg, gn, tri, mstack, neg, masks, hk, hs, eyev, eyek, s0)


def _s5_kernel(u_ref, bbre_ref, bbim_ref, cre_ref, cim_ref, d_ref, wglu_ref, bglu_ref, pre_ref, pim_ref,
               x0re_ref, x0im_ref,
               o_ref, xore_ref, xoim_ref,
               xr_s, xi_s, cr_s, ci_s, *, rows, sequential):
    c = pl.program_id(1)
    groups = rows // SUBLANE
    u = u_ref[0]
    ub = u.astype(BF16)
    xr = jnp.dot(ub, bbre_ref[...], preferred_element_type=F32)
    xi = jnp.dot(ub, bbim_ref[...], preferred_element_type=F32)
    pre = pre_ref[...]
    pim = pim_ref[...]
    rowmod = lax.broadcasted_iota(jnp.int32, (rows, 1), 0) % SUBLANE
    for sh in (1, 2, 4):
        ar = pre[sh - 1:sh, :]
        ai = pim[sh - 1:sh, :]
        keep = rowmod >= sh
        sr = jnp.where(keep, pltpu.roll(xr, sh, axis=0), 0.0)
        si = jnp.where(keep, pltpu.roll(xi, sh, axis=0), 0.0)
        xr, xi = xr + ar * sr - ai * si, xi + ar * si + ai * sr
    xr_s[...] = xr
    xi_s[...] = xi

    if sequential:
        @pl.when(c == 0)
        def _():
            cr_s[...] = x0re_ref[0]
            ci_s[...] = x0im_ref[0]

        def body(j, carry):
            cr, ci = carry
            r0 = pl.multiple_of(j * SUBLANE, SUBLANE)
            nr = xr_s[pl.ds(r0, SUBLANE), :] + pre * cr - pim * ci
            ni = xi_s[pl.ds(r0, SUBLANE), :] + pre * ci + pim * cr
            xr_s[pl.ds(r0, SUBLANE), :] = nr
            xi_s[pl.ds(r0, SUBLANE), :] = ni
            return nr[SUBLANE - 1:SUBLANE, :], ni[SUBLANE - 1:SUBLANE, :]

        cr, ci = lax.fori_loop(0, groups, body, (cr_s[...], ci_s[...]))
        cr_s[...] = cr
        ci_s[...] = ci

        @pl.when(c == pl.num_programs(1) - 1)
        def _():
            xore_ref[0] = cr
            xoim_ref[0] = ci
    else:
        def body(j, carry):
            r0 = pl.multiple_of(j * SUBLANE, SUBLANE)
            cr = x0re_ref[0, pl.ds(j, 1), :]
            ci = x0im_ref[0, pl.ds(j, 1), :]
            nr = xr_s[pl.ds(r0, SUBLANE), :] + pre * cr - pim * ci
            ni = xi_s[pl.ds(r0, SUBLANE), :] + pre * ci + pim * cr
            xr_s[pl.ds(r0, SUBLANE), :] = nr
            xi_s[pl.ds(r0, SUBLANE), :] = ni
            xore_ref[0, pl.ds(j, 1), :] = nr[SUBLANE - 1:SUBLANE, :]
            xoim_ref[0, pl.ds(j, 1), :] = ni[SUBLANE - 1:SUBLANE, :]
            return carry

        lax.fori_loop(0, groups, body, 0)

    y = (jnp.dot(xr_s[...].astype(BF16), cre_ref[...], preferred_element_type=F32)
         - jnp.dot(xi_s[...].astype(BF16), cim_ref[...], preferred_element_type=F32)
         + d_ref[...] * u)
    zz = jax.nn.gelu(y)
    gate = jnp.dot(zz.astype(BF16), wglu_ref[...], preferred_element_type=F32) + bglu_ref[...]
    o_ref[0] = (zz * jax.nn.sigmoid(gate)).astype(o_ref.dtype)


def _s5(z3, tabs, x0re, x0im, rows, sequential):
    b, t, _ = z3.shape
    nblk = t // rows
    groups = rows // SUBLANE
    ublk = 1536 // S5_WIDTH
    const2 = lambda bi, ci: (0, 0)
    if sequential:
        st_spec = pl.BlockSpec((1, 1, S5_LANES), lambda bi, ci: (bi, 0, 0))
        st_shape = jax.ShapeDtypeStruct((b, 1, S5_LANES), F32)
    else:
        st_spec = pl.BlockSpec((1, groups, S5_LANES), lambda bi, ci: (bi, ci, 0))
        st_shape = jax.ShapeDtypeStruct(x0re.shape, F32)
    return pl.pallas_call(
        functools.partial(_s5_kernel, rows=rows, sequential=sequential),
        grid=(b, nblk),
        in_specs=[pl.BlockSpec((1, rows, S5_WIDTH), lambda bi, ci: (bi, ci, ublk)),
                  pl.BlockSpec((S5_WIDTH, S5_LANES), const2),
                  pl.BlockSpec((S5_WIDTH, S5_LANES), const2),
                  pl.BlockSpec((S5_LANES, S5_WIDTH), const2),
                  pl.BlockSpec((S5_LANES, S5_WIDTH), const2),
                  pl.BlockSpec((1, S5_WIDTH), const2),
                  pl.BlockSpec((S5_WIDTH, S5_WIDTH), const2),
                  pl.BlockSpec((1, S5_WIDTH), const2),
                  pl.BlockSpec((SUBLANE, S5_LANES), const2),
                  pl.BlockSpec((SUBLANE, S5_LANES), const2),
                  st_spec, st_spec],
        out_specs=[pl.BlockSpec((1, rows, S5_WIDTH), lambda bi, ci: (bi, ci, 0)),
                   st_spec, st_spec],
        out_shape=[jax.ShapeDtypeStruct((b, t, S5_WIDTH), BF16), st_shape, st_shape],
        scratch_shapes=[pltpu.VMEM((rows, S5_LANES), F32),
                        pltpu.VMEM((rows, S5_LANES), F32),
                        pltpu.VMEM((1, S5_LANES), F32),
                        pltpu.VMEM((1, S5_LANES), F32)],
        compiler_params=_params(("parallel", "arbitrary")),
        name="s5",
    )(z3, tabs['bbre'], tabs['bbim'], tabs['cre'], tabs['cim'], tabs['d'], tabs['wglu'], tabs['bglu'],
      tabs['pre'], tabs['pim'], x0re, x0im)


def _s5_tables(a_re, a_im, log_dt, b_re, b_im, c_re, c_im, d, w_glu, b_glu):
    lam_re = a_re.astype(F32)
    lam_im = a_im.astype(F32)
    dt = jnp.exp(log_dt.astype(F32))[:, None]
    mag = jnp.exp(lam_re * dt)
    ang = lam_im * dt
    ab_re = mag * jnp.cos(ang)
    ab_im = mag * jnp.sin(ang)
    den = lam_re * lam_re + lam_im * lam_im
    co_re = ((ab_re - 1.0) * lam_re + ab_im * lam_im) / den
    co_im = (ab_im * lam_re - (ab_re - 1.0) * lam_im) / den
    b_re = b_re.astype(F32)
    b_im = b_im.astype(F32)
    bb_re = co_re[..., None] * b_re - co_im[..., None] * b_im
    bb_im = co_re[..., None] * b_im + co_im[..., None] * b_re
    eye = jnp.eye(S5_GROUPS, dtype=F32)
    blockdiag_in = lambda bb: jnp.einsum('gph,gk->ghkp', bb, eye).reshape(S5_WIDTH, S5_LANES)
    blockdiag_out = lambda cc: jnp.einsum('ghp,gk->gpkh', cc.astype(F32), eye).reshape(S5_LANES, S5_WIDTH)
    ar = ab_re.reshape(1, S5_LANES)
    ai = ab_im.reshape(1, S5_LANES)
    pr, pi = [ar], [ai]
    for _ in range(SUBLANE - 1):
        pr, pi = pr + [pr[-1] * ar - pi[-1] * ai], pi + [pr[-1] * ai + pi[-1] * ar]
    return {
        'bbre': blockdiag_in(bb_re).astype(BF16), 'bbim': blockdiag_in(bb_im).astype(BF16),
        'cre': blockdiag_out(c_re).astype(BF16), 'cim': blockdiag_out(c_im).astype(BF16),
        'd': d.astype(F32).reshape(1, S5_WIDTH),
        'wglu': w_glu.astype(BF16), 'bglu': b_glu.astype(F32).reshape(1, S5_WIDTH),
        'pre': jnp.concatenate(pr, axis=0), 'pim': jnp.concatenate(pi, axis=0),
    }


def _rmsnorm_val(x, g):
    return x * lax.rsqrt(jnp.mean(x * x, axis=-1, keepdims=True) + EPS) * g


def _ffn_kernel(x_ref, g_ref, wup_ref, cw_ref, cb_ref, wd_ref, st_ref, gf_ref, o_ref, so_ref, halo_s,
                *, tm, tf, long_seq, final_norm):
    i = pl.program_id(1)
    n = tm // SUBLANE
    d = x_ref.shape[-1]
    x = x_ref[0]
    hn = _rmsnorm_val(x, g_ref[...])
    if long_seq:
        lead = n

        @pl.when(i == 0)
        def _():
            halo_s[...] = st_ref[0]
    else:
        lead = SUBLANE
    hnp = jnp.swapaxes(hn.reshape(tm // lead, lead, d), 0, 1).reshape(tm, d).astype(BF16)
    inner = tm // lead
    sub = lax.broadcasted_iota(jnp.int32, (SUBLANE, tf), 0)

    acc = jnp.zeros((tm, d), F32)
    for c in range(D_FF // tf):
        conv = []
        for half in range(2):
            c0 = half * D_FF + c * tf
            u = jnp.dot(hnp, wup_ref[:, c0:c0 + tf], preferred_element_type=F32).reshape(lead, inner, tf)
            if long_seq:
                um1 = jnp.where(sub == 0, halo_s[1:2, c0:c0 + tf], pltpu.roll(u[lead - 1], 1, axis=0))
                um2 = jnp.where(sub == 0, halo_s[0:1, c0:c0 + tf], pltpu.roll(u[lead - 2], 1, axis=0))
                halo_s[0:1, c0:c0 + tf] = u[lead - 2][SUBLANE - 1:SUBLANE]
                halo_s[1:2, c0:c0 + tf] = u[lead - 1][SUBLANE - 1:SUBLANE]
            else:
                um2 = st_ref[0, :, c0:c0 + tf]
                um1 = st_ref[0, :, 2 * D_FF + c0:2 * D_FF + c0 + tf]
                so_ref[0, :, c0:c0 + tf] = u[lead - 2]
                so_ref[0, :, 2 * D_FF + c0:2 * D_FF + c0 + tf] = u[lead - 1]
            ext = jnp.concatenate([um2[None], um1[None], u], axis=0)
            cv = cb_ref[:, c0:c0 + tf]
            for j in range(CONV_W):
                cv = cv + ext[j:j + lead] * cw_ref[j:j + 1, c0:c0 + tf]
            conv.append(cv)
        act = (jax.nn.silu(conv[1]) * conv[0]).reshape(tm, tf).astype(BF16)
        acc = acc + jnp.dot(act, wd_ref[c * tf:(c + 1) * tf, :], preferred_element_type=F32)

    out = x + jnp.swapaxes(acc.reshape(lead, inner, d), 0, 1).reshape(tm, d)
    if final_norm:
        out = _rmsnorm_val(out, gf_ref[...])
    o_ref[0] = out

    if long_seq:
        @pl.when(i == pl.num_programs(1) - 1)
        def _():
            so_ref[0] = halo_s[...]


def _ffn(h3, g, wup, cw, cb, wd, state, gfinal, tm, tf, long_seq, final_norm):
    b, t, d = h3.shape
    nt = t // tm
    resident = dict(pipeline_mode=pl.Buffered(1))
    if long_seq:
        st_spec = pl.BlockSpec((1, CONV_W - 1, 2 * D_FF), lambda bi, i: (bi, 0, 0))
    else:
        st_spec = pl.BlockSpec((1, tm // SUBLANE, (CONV_W - 1) * 2 * D_FF), lambda bi, i: (bi, i, 0))
    return pl.pallas_call(
        functools.partial(_ffn_kernel, tm=tm, tf=tf, long_seq=long_seq, final_norm=final_norm),
        grid=(b, nt),
        in_specs=[pl.BlockSpec((1, tm, d), lambda bi, i: (bi, i, 0)),
                  pl.BlockSpec((1, d), lambda bi, i: (0, 0)),
                  pl.BlockSpec((d, 2 * D_FF), lambda bi, i: (0, 0), **resident),
                  pl.BlockSpec((CONV_W, 2 * D_FF), lambda bi, i: (0, 0)),
                  pl.BlockSpec((1, 2 * D_FF), lambda bi, i: (0, 0)),
                  pl.BlockSpec((D_FF, d), lambda bi, i: (0, 0), **resident),
                  st_spec,
                  pl.BlockSpec((1, d), lambda bi, i: (0, 0))],
        out_specs=[pl.BlockSpec((1, tm, d), lambda bi, i: (bi, i, 0)), st_spec],
        out_shape=[jax.ShapeDtypeStruct((b, t, d), F32), jax.ShapeDtypeStruct(state.shape, F32)],
        scratch_shapes=[pltpu.VMEM((CONV_W - 1, 2 * D_FF), F32)],
        compiler_params=_params(("parallel", "arbitrary")),
        name="ffn",
    )(h3, g.reshape(1, d), wup, cw, cb, wd, state, gfinal)


def _rope_tables(pos):
    half = SW_HD // 2
    inv = ROPE_THETA ** (-jnp.arange(half, dtype=F32) / half)
    ang = pos.astype(F32)[:, None] * inv[None, :]
    cos = jnp.cos(ang)
    sin = jnp.sin(ang)
    reps = LANE // SW_HD
    cos_t = jnp.tile(jnp.concatenate([cos, cos], axis=1), (1, reps))
    sin_t = jnp.tile(jnp.concatenate([-sin, sin], axis=1), (1, reps))
    return cos_t, sin_t


def _prep_weights(W):
    P = {}
    we = W['w_in_even']
    parts = jnp.split(we, [int(c) for c in np.cumsum((512, 512, 512, 512, 8, 512, 128))], axis=1)
    q_m, k_m, v_m, o_m, g_m, q_a, k_a, v_a = parts
    dup = lambda w: jnp.concatenate([w[:, :SW_HD], w[:, :SW_HD], w[:, SW_HD:], w[:, SW_HD:]], axis=1)
    pad = jnp.zeros((D_MODEL, LANE - 2 * ML_HEADS), we.dtype)
    P['w_in_even'] = jnp.concatenate([q_m, k_m, v_m, o_m, q_a, dup(k_a), dup(v_a), g_m, pad], axis=1).astype(BF16)
    assert P['w_in_even'].shape[1] == EVEN_PAD
    P['b_gates'] = jnp.pad(W['b_mlstm_gates'].astype(F32), (0, LANE - 2 * ML_HEADS)).reshape(1, LANE)
    wo = W['w_in_odd']
    q, k, v, r, glr, u = jnp.split(wo, [int(c) for c in np.cumsum((256, 256, 512, 512, 16))], axis=1)
    pad = jnp.zeros((D_MODEL, ODD_PAD - 2048 - GLA_RANK), wo.dtype)
    P['w_in_odd'] = jnp.concatenate([q, k, v, r, u, glr, pad], axis=1).astype(BF16)
    P['w_gate_up'] = jnp.pad(W['w_gla_gate_up'].astype(F32), ((0, LANE - GLA_RANK), (0, 0)))
    P['b_gate'] = W['b_gla_gate'].astype(F32).reshape(1, -1)
    P['g_gla'] = W['g_gla_norm'].astype(F32).reshape(1, -1)
    hm = ML_HEADS * ML_DV
    P['w_out_even'] = (W['w_out_even'][:hm].astype(BF16), W['w_out_even'][hm:].astype(BF16))
    hg = GLA_HEADS * GLA_DV
    P['w_out_odd'] = (W['w_out_odd'][:hg].astype(BF16), W['w_out_odd'][hg:].astype(BF16))
    P['w_ffn_up'] = W['w_ffn_up'].astype(BF16)
    P['w_ffn_down'] = W['w_ffn_down'].astype(BF16)
    P['s5'] = _s5_tables(W['s5_a_re'], W['s5_a_im'], W['s5_log_dt'], W['s5_b_re'], W['s5_b_im'],
                         W['s5_c_re'], W['s5_c_im'], W['s5_d'], W['w_s5_glu'], W['b_s5_glu'])
    return P


def _trunk(x, pos, st, W, P, is_prompt):
    b, t, d = x.shape
    n = b * t
    L = CHUNK if t % CHUNK == 0 else t
    tm = 512 if n % 512 == 0 else n
    cos, sin_signed = _rope_tables(pos)
    gfinal = W['norm_final'].astype(F32).reshape(1, d)
    new = {}
    conv_out = []
    h = x.reshape(n, d)

    z = _norm_matmul(h, W['norm_mix'][0], P['w_in_even'], tm)
    z3 = z.reshape(b, t, EVEN_PAD)
    h_m, new['ml_C'], new['ml_n'], m_new = _mlstm(z3, P['b_gates'], st['ml_C'], st['ml_n'], st['ml_m'], L,
                                                  Bb=2 if is_prompt else 8)
    new['ml_m'] = m_new.reshape(b, ML_HEADS)
    if is_prompt:
        h_a, kb = _swa_prompt(z3, cos, sin_signed, W['sw_sinks'].astype(F32))
        vb = jnp.concatenate([z3[:, t - WINDOW:, EVEN_VD:EVEN_VD + SW_HD],
                              z3[:, t - WINDOW:, EVEN_VD + LANE:EVEN_VD + LANE + SW_HD]], axis=-1)
    else:
        h_a, kb, vb = _swa_sample(z3, st['kbuf'].reshape(b, WINDOW, -1), st['vbuf'].reshape(b, WINDOW, -1),
                                  cos, sin_signed, W['sw_sinks'].astype(F32), Bb=8)
    new['kbuf'] = kb.reshape(b, WINDOW, SW_KV_HEADS, SW_HD)
    new['vbuf'] = vb.reshape(b, WINDOW, SW_KV_HEADS, SW_HD)
    h = _proj_res(h_m.reshape(n, -1), h_a.reshape(n, -1), *P['w_out_even'], h, tm)
    h, cb = _ffn_layer(h, 0, st, W, P, b, t, gfinal, is_prompt, final_norm=False)
    conv_out.append(cb)

    z = _norm_matmul(h, W['norm_mix'][1], P['w_in_odd'], tm)
    z3 = z.reshape(b, t, ODD_PAD)
    o_g, new['gla'] = _gla(z3, P['w_gate_up'], P['b_gate'], P['g_gla'], st['gla'], L, Bb=2 if is_prompt else 4)
    x0re = st['s5_re'].astype(F32).reshape(b, 1, S5_LANES)
    x0im = st['s5_im'].astype(F32).reshape(b, 1, S5_LANES)
    if is_prompt:
        o_s, xre, xim = _s5(z3, P['s5'], x0re, x0im, rows=256 if t % 256 == 0 else t, sequential=True)
    else:
        o_s, xre, xim = _s5(z3.reshape(1, n, ODD_PAD), P['s5'], x0re.reshape(1, b, S5_LANES),
                            x0im.reshape(1, b, S5_LANES), rows=256 if n % 256 == 0 else n, sequential=False)
    new['s5_re'] = xre.reshape(b, S5_GROUPS, S5_STATE)
    new['s5_im'] = xim.reshape(b, S5_GROUPS, S5_STATE)
    h = _proj_res(o_g.reshape(n, -1), o_s.reshape(n, -1), *P['w_out_odd'], h, tm)
    h, cb = _ffn_layer(h, 1, st, W, P, b, t, gfinal, is_prompt, final_norm=True)
    conv_out.append(cb)
    new['conv'] = jnp.stack(conv_out)
    return h.reshape(b, t, d), new


def _ffn_layer(h, layer, st, W, P, b, t, gfinal, is_prompt, final_norm):
    n, d = h.shape
    cw = W['ffn_conv_w'][layer].astype(F32)
    cb = W['ffn_conv_b'][layer].astype(F32).reshape(1, 2 * D_FF)
    state = st['conv'][layer].astype(F32)
    tm = 256 if (t if is_prompt else n) % 256 == 0 else (t if is_prompt else n)
    args = (W['norm_ffn'][layer], P['w_ffn_up'][layer], cw, cb, P['w_ffn_down'][layer])
    if is_prompt:
        out, new_state = _ffn(h.reshape(b, t, d), *args, state, gfinal, tm, FFN_TF, True, final_norm)
    else:
        assert t == SUBLANE
        out, new_state = _ffn(h.reshape(1, n, d), *args, state.reshape(1, b, -1), gfinal, tm, FFN_TF, False,
                              final_norm)
    return out.reshape(n, d), new_state.reshape(b, CONV_W - 1, 2 * D_FF)


def kernel(x_prompt, x_sample, state_mlstm_C, state_mlstm_n, state_mlstm_m, cache_swa_k, cache_swa_v,
           state_gla, state_s5_re, state_s5_im, state_ffn_conv, norm_mix, norm_ffn, norm_final,
           w_in_even, b_mlstm_gates, sw_sinks, w_out_even, w_in_odd, w_gla_gate_up, b_gla_gate, g_gla_norm,
           s5_a_re, s5_a_im, s5_log_dt, s5_b_re, s5_b_im, s5_c_re, s5_c_im, s5_d, w_s5_glu, b_s5_glu,
           w_out_odd, w_ffn_up, ffn_conv_w, ffn_conv_b, w_ffn_down):
    W = {'norm_mix': norm_mix.astype(F32), 'norm_ffn': norm_ffn.astype(F32), 'norm_final': norm_final,
         'w_in_even': w_in_even, 'b_mlstm_gates': b_mlstm_gates, 'sw_sinks': sw_sinks, 'w_out_even': w_out_even,
         'w_in_odd': w_in_odd, 'w_gla_gate_up': w_gla_gate_up, 'b_gla_gate': b_gla_gate, 'g_gla_norm': g_gla_norm,
         's5_a_re': s5_a_re, 's5_a_im': s5_a_im, 's5_log_dt': s5_log_dt, 's5_b_re': s5_b_re, 's5_b_im': s5_b_im,
         's5_c_re': s5_c_re, 's5_c_im': s5_c_im, 's5_d': s5_d, 'w_s5_glu': w_s5_glu, 'b_s5_glu': b_s5_glu,
         'w_out_odd': w_out_odd, 'w_ffn_up': w_ffn_up, 'ffn_conv_w': ffn_conv_w, 'ffn_conv_b': ffn_conv_b,
         'w_ffn_down': w_ffn_down}
    P = _prep_weights(W)
    bp, tp = x_prompt.shape[:2]
    st_prompt = {'ml_C': jnp.zeros((bp, ML_HEADS, ML_DK, ML_DV), F32),
                 'ml_n': jnp.zeros((bp, ML_HEADS, ML_DK), F32),
                 'ml_m': jnp.zeros((bp, ML_HEADS), F32),
                 'gla': jnp.zeros((bp, GLA_HEADS, GLA_DK, GLA_DV), F32),
                 's5_re': jnp.zeros((bp, S5_GROUPS, S5_STATE), F32),
                 's5_im': jnp.zeros((bp, S5_GROUPS, S5_STATE), F32),
                 'conv': jnp.zeros((2, bp, CONV_W - 1, 2 * D_FF), F32)}
    st_sample = {'ml_C': state_mlstm_C.astype(F32), 'ml_n': state_mlstm_n.astype(F32),
                 'ml_m': state_mlstm_m.astype(F32), 'kbuf': cache_swa_k.astype(F32),
                 'vbuf': cache_swa_v.astype(F32), 'gla': state_gla.astype(F32),
                 's5_re': state_s5_re, 's5_im': state_s5_im, 'conv': state_ffn_conv}
    past_len = 16384
    y_p, np_ = _trunk(x_prompt.astype(F32), jnp.arange(tp), st_prompt, W, P, True)
    y_s, ns_ = _trunk(x_sample.astype(F32), past_len + jnp.arange(x_sample.shape[1]), st_sample, W, P, False)
    return (y_p, y_s,
            np_['ml_C'], ns_['ml_C'], np_['ml_n'], ns_['ml_n'], np_['ml_m'], ns_['ml_m'],
            np_['kbuf'], ns_['kbuf'], np_['vbuf'], ns_['vbuf'], np_['gla'], ns_['gla'],
            np_['s5_re'], ns_['s5_re'], np_['s5_im'], ns_['s5_im'], np_['conv'], ns_['conv'])
```

```python
import functools
import math

import numpy as np
import jax
import jax.numpy as jnp
from jax import lax
from jax.experimental import pallas as pl
from jax.experimental.pallas import tpu as pltpu

F32 = jnp.float32
BF16 = jnp.bfloat16
HI = lax.Precision.HIGHEST
NT = (((1,), (1,)), ((), ()))
TN = (((0,), (0,)), ((), ()))

D_MODEL = 1024
ML_HEADS, ML_DK, ML_DV = 4, 128, 128
SW_HEADS, SW_KV_HEADS, SW_HD, WINDOW = 8, 2, 64, 128
SW_GQ = SW_HEADS // SW_KV_HEADS
ROPE_THETA = 10000.0
GLA_HEADS, GLA_DK, GLA_DV, GLA_RANK, GLA_TAU = 4, 64, 128, 16, 16.0
S5_WIDTH, S5_GROUP, S5_STATE = 512, 16, 64
S5_GROUPS = S5_WIDTH // S5_GROUP
S5_LANES = S5_GROUPS * S5_STATE
D_FF = 2816
FFN_TF = 1408
CONV_W = 3
CHUNK = 64
EPS = 1e-6

LANE = 128
SUBLANE = 8
VMEM_LIMIT = 56 * 1024 * 1024

EVEN_QA, EVEN_KD, EVEN_VD, EVEN_PAD = 2048, 2560, 2816, 3200
ODD_PAD = 2176


def _params(sem):
    return pltpu.CompilerParams(dimension_semantics=sem, vmem_limit_bytes=VMEM_LIMIT)


def _norm_matmul_kernel(x_ref, g_ref, w_ref, o_ref):
    x = x_ref[...]
    ms = jnp.mean(x * x, axis=-1, keepdims=True)
    hn = (x * lax.rsqrt(ms + EPS) * g_ref[...]).astype(BF16)
    o_ref[...] = jnp.dot(hn, w_ref[...], preferred_element_type=F32)


def _norm_matmul(x, g, w, tm):
    n, d = x.shape
    c = w.shape[1]
    return pl.pallas_call(
        _norm_matmul_kernel,
        grid=(n // tm,),
        in_specs=[pl.BlockSpec((tm, d), lambda i: (i, 0)),
                  pl.BlockSpec((1, d), lambda i: (0, 0)),
                  pl.BlockSpec((d, c), lambda i: (0, 0))],
        out_specs=pl.BlockSpec((tm, c), lambda i: (i, 0)),
        out_shape=jax.ShapeDtypeStruct((n, c), F32),
        compiler_params=_params(("parallel",)),
        name="norm_matmul",
    )(x, g.reshape(1, d), w)


def _proj_res_kernel(a1_ref, a2_ref, w1_ref, w2_ref, r_ref, o_ref):
    acc = jnp.dot(a1_ref[...], w1_ref[...], preferred_element_type=F32)
    acc = acc + jnp.dot(a2_ref[...], w2_ref[...], preferred_element_type=F32)
    o_ref[...] = r_ref[...] + acc


def _proj_res(a1, a2, w1, w2, res, tm):
    n, k1 = a1.shape
    k2 = a2.shape[1]
    d = res.shape[1]
    return pl.pallas_call(
        _proj_res_kernel,
        grid=(n // tm,),
        in_specs=[pl.BlockSpec((tm, k1), lambda i: (i, 0)),
                  pl.BlockSpec((tm, k2), lambda i: (i, 0)),
                  pl.BlockSpec((k1, d), lambda i: (0, 0)),
                  pl.BlockSpec((k2, d), lambda i: (0, 0)),
                  pl.BlockSpec((tm, d), lambda i: (i, 0))],
        out_specs=pl.BlockSpec((tm, d), lambda i: (i, 0)),
        out_shape=jax.ShapeDtypeStruct((n, d), F32),
        compiler_params=_params(("parallel",)),
        name="proj_res",
    )(a1, a2, w1, w2, res)


def _pieces(x, n):
    out, r = [], x
    for _ in range(n):
        p = r.astype(BF16).astype(F32)
        out.append(p)
        r = r - p
    return out


def _dot_sel(m_bf16, x, dims=None, pieces=3):
    ps = _pieces(x, pieces)
    axis = 1 if dims is None else 0
    w = x.shape[axis]
    if w % LANE:
        f = (lambda p: jnp.dot(m_bf16, p, preferred_element_type=F32)) if dims is None else (
            lambda p: lax.dot_general(m_bf16, p, dims, preferred_element_type=F32))
        return sum(f(p.astype(BF16)) for p in ps)
    stacked = jnp.concatenate(ps, axis=axis).astype(BF16)
    if dims is None:
        r = jnp.dot(m_bf16, stacked, preferred_element_type=F32)
    else:
        r = lax.dot_general(m_bf16, stacked, dims, preferred_element_type=F32)
    return sum(r[:, i * w:(i + 1) * w] for i in range(pieces))


def _sel_right(x, e_bf16, pieces=3):
    rows = x.shape[0]
    r = jnp.dot(jnp.concatenate(_pieces(x, pieces), axis=0).astype(BF16), e_bf16, preferred_element_type=F32)
    return sum(r[i * rows:(i + 1) * rows] for i in range(pieces))


def _mlstm_kernel(q_ref, k_ref, v_ref, og_ref, gz_ref, bias_ref, tri_ref, ones_ref, expc_ref, expv_ref, eyet_ref,
                  seg_ref, segv_ref, hmk_ref, c0_ref, n0_ref, m0_ref,
                  h_ref, co_ref, no_ref, mo_ref,
                  c_s, n_s, m_s, *, L, Bb):
    c = pl.program_id(1)

    @pl.when(c == 0)
    def _():
        c_s[...] = c0_ref[...]
        n_s[...] = n0_ref[...]
        m_s[...] = m0_ref[...]

    for bi in range(Bb):
        _mlstm_one(bi, q_ref, k_ref, v_ref, og_ref, gz_ref, bias_ref, tri_ref, ones_ref, expc_ref, expv_ref,
                   eyet_ref, seg_ref, segv_ref, hmk_ref, h_ref, c_s, n_s, m_s, L)

    @pl.when(c == pl.num_programs(1) - 1)
    def _():
        co_ref[...] = c_s[...]
        no_ref[...] = n_s[...]
        mo_ref[...] = m_s[...]


def _mlstm_one(bi, q_ref, k_ref, v_ref, og_ref, gz_ref, bias_ref, tri_ref, ones_ref, expc_ref, expv_ref,
               eyet_ref, seg_ref, segv_ref, hmk_ref, h_ref, c_s, n_s, m_s, L):
    H, DK = ML_HEADS, ML_DK
    HL = H * L
    lane = lax.broadcasted_iota(jnp.int32, (L, LANE), 1)
    rowi = lax.broadcasted_iota(jnp.int32, (L, LANE), 0)
    keep = lane < H
    heads_only = lambda x: jnp.where(keep[:x.shape[0]], x, 0.0)

    gates = gz_ref[bi] + bias_ref[...]
    lf = jax.nn.log_sigmoid(gates)
    b = pltpu.roll(_dot_sel(tri_ref[...], lf), LANE - H, axis=1)
    vv = gates - b
    cm = vv
    sh = 1
    while sh < L:
        cm = jnp.maximum(cm, jnp.where(rowi >= sh, pltpu.roll(cm, sh, axis=0), -jnp.inf))
        sh *= 2
    m_prev = m_s[bi]
    mt = b + jnp.maximum(m_prev, cm)
    a_inter = jnp.exp(b + m_prev - mt)

    expv = expv_ref[...]
    m_new = mt[L - 1:L, :]
    b_end = b[L - 1:L, :]
    w_end = heads_only(jnp.exp(b_end - b + gates - m_new))
    decay = heads_only(jnp.exp(b_end + m_prev - m_new))
    spread_k = _sel_right(jnp.concatenate([heads_only(b - mt), heads_only(vv)], axis=0), expc_ref[...])
    uc = spread_k[:L]
    vr = _dot_sel(ones_ref[...], spread_k[L:] * eyet_ref[...])
    spread_v = _sel_right(jnp.concatenate([heads_only(a_inter), w_end, jnp.broadcast_to(decay, (SUBLANE, LANE))],
                                          axis=0), expv, pieces=2)
    ac, wc, dc = spread_v[:L], spread_v[L:2 * L], spread_v[2 * L:2 * L + 1]
    ti = lax.broadcasted_iota(jnp.int32, (L, HL), 0)
    si = lax.broadcasted_iota(jnp.int32, (L, HL), 1) % L
    w = jnp.exp(jnp.where(si <= ti, uc + vr, -jnp.inf))

    q = q_ref[bi]
    ks = k_ref[bi] * (DK ** -0.5)
    v = v_ref[bi]
    qb = q.astype(BF16)
    vb = v.astype(BF16)
    if L % (2 * SUBLANE) == 0:
        kst = jnp.concatenate([ks.astype(BF16)] * H, axis=0) * hmk_ref[...]
        vsrc = vb
    else:
        kst = (jnp.concatenate([ks] * H, axis=0) * hmk_ref[...].astype(F32)).astype(BF16)
        vsrc = v
    zero = jnp.zeros((L, DK), vsrc.dtype)
    vbd = jnp.concatenate(
        [jnp.concatenate([zero] * h + [vsrc[:, h * DK:(h + 1) * DK]] + [zero] * (H - 1 - h), axis=1)
         for h in range(H)], axis=0).astype(BF16)
    s = lax.dot_general(qb, kst, NT, preferred_element_type=F32) * w
    qc = jnp.concatenate([jnp.dot(qb[:, h * DK:(h + 1) * DK], c_s[bi, h].astype(BF16), preferred_element_type=F32)
                          for h in range(H)], axis=1)
    num = jnp.dot(s.astype(BF16), vbd, preferred_element_type=F32) + ac * qc
    n_row = n_s[bi]
    den = (_sel_right(s, seg_ref[...], pieces=2)
           + a_inter * _sel_right(q * n_row, segv_ref[...], pieces=2))
    rden = heads_only(1.0 / jnp.maximum(jnp.abs(den), jnp.exp(-mt)))
    hh = num * _sel_right(rden, expv, pieces=2) * jax.nn.sigmoid(og_ref[bi])
    h_ref[bi] = hh.astype(h_ref.dtype)

    kw = ks * wc
    kwb = kw.astype(BF16)
    for h in range(H):
        hs = slice(h * DK, (h + 1) * DK)
        c_s[bi, h] = dc[:, hs] * c_s[bi, h] + lax.dot_general(kwb[:, hs], vb[:, hs], TN, preferred_element_type=F32)
    n_s[bi] = dc * n_row + jnp.sum(kw, axis=0, keepdims=True)
    m_s[bi] = heads_only(m_new)


def _mlstm_consts(L):
    H, DK = ML_HEADS, ML_DK
    expc = np.zeros((LANE, H * L), np.float32)
    expv = np.zeros((LANE, H * DK), np.float32)
    seg = np.zeros((H * L, LANE), np.float32)
    segv = np.zeros((H * DK, LANE), np.float32)
    hmk = np.zeros((H * L, H * DK), np.float32)
    for h in range(H):
        expc[h, h * L:(h + 1) * L] = 1.0
        expv[h, h * DK:(h + 1) * DK] = 1.0
        seg[h * L:(h + 1) * L, h] = 1.0
        segv[h * DK:(h + 1) * DK, h] = 1.0
        hmk[h * L:(h + 1) * L, h * DK:(h + 1) * DK] = 1.0
    eyet = np.tile(np.eye(L, dtype=np.float32), (1, H))
    bf = lambda a: jnp.asarray(a, BF16)
    return (bf(np.tril(np.ones((L, L), np.float32))), bf(np.ones((L, L), np.float32)), bf(expc), bf(expv),
            jnp.asarray(eyet), bf(seg), bf(segv), bf(hmk))


def _mlstm(z3, bias, c0, n0, m0, L, Bb):
    b, t, _ = z3.shape
    nc = t // L
    hw = ML_HEADS * ML_DK
    consts = _mlstm_consts(L)
    gate_blk = (EVEN_PAD - LANE) // LANE
    col = lambda j: (lambda bi, ci: (bi, ci, j))
    const2 = lambda bi, ci: (0, 0)
    state3 = lambda bi, ci: (bi, 0, 0)
    m0p = jnp.pad(m0, ((0, 0), (0, LANE - ML_HEADS))).reshape(b, 1, LANE)
    h, c_new, n_new, m_new = pl.pallas_call(
        functools.partial(_mlstm_kernel, L=L, Bb=Bb),
        grid=(b // Bb, nc),
        in_specs=[pl.BlockSpec((Bb, L, hw), col(0)),
                  pl.BlockSpec((Bb, L, hw), col(1)),
                  pl.BlockSpec((Bb, L, hw), col(2)),
                  pl.BlockSpec((Bb, L, hw), col(3)),
                  pl.BlockSpec((Bb, L, LANE), col(gate_blk)),
                  pl.BlockSpec((1, LANE), const2)]
                 + [pl.BlockSpec(a.shape, const2) for a in consts]
                 + [pl.BlockSpec((Bb, ML_HEADS, ML_DK, ML_DV), lambda bi, ci: (bi, 0, 0, 0)),
                    pl.BlockSpec((Bb, 1, hw), state3),
                    pl.BlockSpec((Bb, 1, LANE), state3)],
        out_specs=[pl.BlockSpec((Bb, L, hw), col(0)),
                   pl.BlockSpec((Bb, ML_HEADS, ML_DK, ML_DV), lambda bi, ci: (bi, 0, 0, 0)),
                   pl.BlockSpec((Bb, 1, hw), state3),
                   pl.BlockSpec((Bb, 1, LANE), state3)],
        out_shape=[jax.ShapeDtypeStruct((b, t, hw), BF16),
                   jax.ShapeDtypeStruct((b, ML_HEADS, ML_DK, ML_DV), F32),
                   jax.ShapeDtypeStruct((b, 1, hw), F32),
                   jax.ShapeDtypeStruct((b, 1, LANE), F32)],
        scratch_shapes=[pltpu.VMEM((Bb, ML_HEADS, ML_DK, ML_DV), F32),
                        pltpu.VMEM((Bb, 1, hw), F32),
                        pltpu.VMEM((Bb, 1, LANE), F32)],
        compiler_params=_params(("parallel", "arbitrary")),
        name="mlstm",
    )(z3, z3, z3, z3, z3, bias, *consts, c0, n0.reshape(b, 1, hw), m0p)
    return h, c_new, n_new.reshape(b, ML_HEADS, ML_DK), m_new[:, 0, :ML_HEADS]


def _rope(x, cos, sin_signed, width):
    half = SW_HD // 2
    lane = lax.broadcasted_iota(jnp.int32, x.shape, 1)
    first = (lane % SW_HD) < half
    partner = jnp.where(first, pltpu.roll(x, width - half, axis=1), pltpu.roll(x, half, axis=1))
    return x * cos + partner * sin_signed


def _undup(xd):
    lane = lax.broadcasted_iota(jnp.int32, xd.shape[:-1] + (LANE,), xd.ndim - 1)
    return jnp.where(lane < SW_HD, xd[..., :LANE], xd[..., LANE:])


def _swa_prompt_kernel(sink_ref, q_ref, kc_ref, kp_ref, vc_ref, vp_ref, cosc_ref, sinc_ref, cosp_ref, sinp_ref,
                       h_ref, ko_ref):
    n = pl.program_id(1)
    W2 = 2 * WINDOW
    cosc, sinc = cosc_ref[...], sinc_ref[...]
    cosp, sinp = cosp_ref[...], sinp_ref[...]
    q = _rope(q_ref[0], jnp.concatenate([cosc] * 4, axis=1), jnp.concatenate([sinc] * 4, axis=1), 4 * LANE)
    k_cur = _rope(kc_ref[0], jnp.concatenate([cosc] * 2, axis=1), jnp.concatenate([sinc] * 2, axis=1), 2 * LANE)
    k_prev = _rope(kp_ref[0], jnp.concatenate([cosp] * 2, axis=1), jnp.concatenate([sinp] * 2, axis=1), 2 * LANE)
    kd = jnp.concatenate([k_prev, k_cur], axis=0)
    vd = jnp.concatenate([vp_ref[0], vc_ref[0]], axis=0)
    i = lax.broadcasted_iota(jnp.int32, (WINDOW, W2), 0)
    j = lax.broadcasted_iota(jnp.int32, (WINDOW, W2), 1)
    diff = WINDOW + i - j
    valid = (diff >= 0) & (diff < WINDOW) & ((n > 0) | (j >= WINDOW))
    row = lax.broadcasted_iota(jnp.int32, (2 * W2, LANE), 0)
    lane = lax.broadcasted_iota(jnp.int32, (2 * W2, LANE), 1)
    bd = (row < W2) == (lane < SW_HD)
    ones_bd = bd.astype(BF16)
    low = lax.broadcasted_iota(jnp.int32, (WINDOW, LANE), 1) < SW_HD
    for kv in range(SW_KV_HEADS):
        kblk = kd[:, kv * LANE:(kv + 1) * LANE]
        vblk = vd[:, kv * LANE:(kv + 1) * LANE]
        kbd = jnp.where(bd, jnp.concatenate([kblk, kblk], axis=0), 0.0).astype(BF16)
        vbd = jnp.where(bd, jnp.concatenate([vblk, vblk], axis=0), 0.0).astype(BF16)
        for pair in range(SW_GQ // 2):
            h0 = kv * SW_GQ + 2 * pair
            q2 = q[:, h0 * SW_HD:(h0 + 2) * SW_HD].astype(BF16)
            s = lax.dot_general(q2, kbd, NT, preferred_element_type=F32) * (SW_HD ** -0.5)
            es, sink_terms = [], []
            for hh in range(2):
                sh = jnp.where(valid, s[:, hh * W2:(hh + 1) * W2], -jnp.inf)
                sink = sink_ref[h0 + hh]
                m = jnp.maximum(jnp.max(sh, axis=1, keepdims=True), sink)
                es.append(jnp.exp(sh - m).astype(BF16))
                sink_terms.append(jnp.exp(sink - m))
            e = jnp.concatenate(es, axis=1)
            num = jnp.dot(e, vbd, preferred_element_type=F32)
            den = jnp.dot(e, ones_bd, preferred_element_type=F32) + jnp.where(low, sink_terms[0], sink_terms[1])
            h_ref[0, :, h0 * SW_HD:(h0 + 2) * SW_HD] = (num / den).astype(h_ref.dtype)

    @pl.when(n == pl.num_programs(1) - 1)
    def _():
        ko_ref[0] = _undup(k_cur)


def _swa_prompt(z3, cos, sin_signed, sinks):
    b, t, _ = z3.shape
    nb = t // WINDOW
    qw = SW_HEADS * SW_HD
    kw = SW_KV_HEADS * SW_HD
    dw = 2 * kw
    qblk, kblk, vblk = EVEN_QA // qw, EVEN_KD // dw, EVEN_VD // dw
    cur = lambda j: (lambda bi, ni: (bi, ni, j))
    prev = lambda j: (lambda bi, ni: (bi, jnp.maximum(ni - 1, 0), j))
    return pl.pallas_call(
        _swa_prompt_kernel,
        grid=(b, nb),
        in_specs=[pl.BlockSpec(memory_space=pltpu.SMEM),
                  pl.BlockSpec((1, WINDOW, qw), cur(qblk)),
                  pl.BlockSpec((1, WINDOW, dw), cur(kblk)),
                  pl.BlockSpec((1, WINDOW, dw), prev(kblk)),
                  pl.BlockSpec((1, WINDOW, dw), cur(vblk)),
                  pl.BlockSpec((1, WINDOW, dw), prev(vblk)),
                  pl.BlockSpec((WINDOW, LANE), lambda bi, ni: (ni, 0)),
                  pl.BlockSpec((WINDOW, LANE), lambda bi, ni: (ni, 0)),
                  pl.BlockSpec((WINDOW, LANE), lambda bi, ni: (jnp.maximum(ni - 1, 0), 0)),
                  pl.BlockSpec((WINDOW, LANE), lambda bi, ni: (jnp.maximum(ni - 1, 0), 0))],
        out_specs=[pl.BlockSpec((1, WINDOW, qw), cur(0)),
                   pl.BlockSpec((1, WINDOW, kw), lambda bi, ni: (bi, 0, 0))],
        out_shape=[jax.ShapeDtypeStruct((b, t, qw), BF16),
                   jax.ShapeDtypeStruct((b, WINDOW, kw), F32)],
        compiler_params=_params(("parallel", "arbitrary")),
        name="swa_prompt",
    )(sinks, z3, z3, z3, z3, z3, cos, sin_signed, cos, sin_signed)


def _swa_sample_kernel(q_ref, k_ref, v_ref, kbuf_ref, vbuf_ref, cos_ref, sin_ref, sink_ref,
                       h_ref, ko_ref, vo_ref, *, T, Bb):
    qw = SW_HEADS * SW_HD
    kw = SW_KV_HEADS * SW_HD
    cos = cos_ref[...]
    sin = sin_ref[...]
    q = _rope(q_ref[...].reshape(Bb * T, qw), jnp.concatenate([cos] * (qw // LANE), axis=1),
              jnp.concatenate([sin] * (qw // LANE), axis=1), qw).reshape(Bb, T, qw)
    k_new = _rope(_undup(k_ref[...]).reshape(Bb * T, kw), cos, sin, kw).reshape(Bb, T, kw)
    kk = jnp.concatenate([kbuf_ref[...], k_new], axis=1)
    vv = jnp.concatenate([vbuf_ref[...], _undup(v_ref[...])], axis=1)
    ko_ref[...] = kk[:, T:, :]
    vo_ref[...] = vv[:, T:, :]
    rows = SW_GQ * T
    i = lax.broadcasted_iota(jnp.int32, (rows, WINDOW + T), 0) % T
    j = lax.broadcasted_iota(jnp.int32, (rows, WINDOW + T), 1)
    diff = WINDOW + i - j
    valid = (diff >= 0) & (diff < WINDOW)
    for kv in range(SW_KV_HEADS):
        qs = jnp.concatenate([q[:, :, (kv * SW_GQ + g) * SW_HD:(kv * SW_GQ + g + 1) * SW_HD]
                              for g in range(SW_GQ)], axis=1).astype(BF16)
        kh = kk[:, :, kv * SW_HD:(kv + 1) * SW_HD].astype(BF16)
        vh = vv[:, :, kv * SW_HD:(kv + 1) * SW_HD].astype(BF16)
        s = jnp.einsum('bqd,bkd->bqk', qs, kh, preferred_element_type=F32) * (SW_HD ** -0.5)
        s = jnp.where(valid, s, -jnp.inf)
        sink = sink_ref[kv * rows:(kv + 1) * rows, :]
        m = jnp.maximum(jnp.max(s, axis=-1, keepdims=True), sink)
        e = jnp.exp(s - m)
        p = e / (jnp.sum(e, axis=-1, keepdims=True) + jnp.exp(sink - m))
        o = jnp.einsum('bqk,bkd->bqd', p.astype(BF16), vh, preferred_element_type=F32)
        for g in range(SW_GQ):
            hh = kv * SW_GQ + g
            h_ref[:, :, hh * SW_HD:(hh + 1) * SW_HD] = o[:, g * T:(g + 1) * T, :].astype(h_ref.dtype)


def _swa_sample(z3, kbuf, vbuf, cos, sin_signed, sinks, Bb):
    b, t, _ = z3.shape
    qw = SW_HEADS * SW_HD
    kw = SW_KV_HEADS * SW_HD
    dw = 2 * kw
    qblk, kblk, vblk = EVEN_QA // qw, EVEN_KD // dw, EVEN_VD // dw
    col = lambda j: (lambda bi: (bi, 0, j))
    const2 = lambda bi: (0, 0)
    sink_col = jnp.repeat(sinks, t).reshape(SW_HEADS * t, 1)
    return pl.pallas_call(
        functools.partial(_swa_sample_kernel, T=t, Bb=Bb),
        grid=(b // Bb,),
        in_specs=[pl.BlockSpec((Bb, t, qw), col(qblk)),
                  pl.BlockSpec((Bb, t, dw), col(kblk)),
                  pl.BlockSpec((Bb, t, dw), col(vblk)),
                  pl.BlockSpec((Bb, WINDOW, kw), col(0)),
                  pl.BlockSpec((Bb, WINDOW, kw), col(0)),
                  pl.BlockSpec((Bb * t, LANE), const2),
                  pl.BlockSpec((Bb * t, LANE), const2),
                  pl.BlockSpec((SW_HEADS * t, 1), const2)],
        out_specs=[pl.BlockSpec((Bb, t, qw), col(0)),
                   pl.BlockSpec((Bb, WINDOW, kw), col(0)),
                   pl.BlockSpec((Bb, WINDOW, kw), col(0))],
        out_shape=[jax.ShapeDtypeStruct((b, t, qw), BF16),
                   jax.ShapeDtypeStruct((b, WINDOW, kw), F32),
                   jax.ShapeDtypeStruct((b, WINDOW, kw), F32)],
        compiler_params=_params(("parallel",)),
        name="swa_sample",
    )(z3, z3, z3, kbuf, vbuf, jnp.tile(cos, (Bb, 1)), jnp.tile(sin_signed, (Bb, 1)), sink_col)


def _gla_tables(L):
    nlev = int(math.log2(L))
    assert 2 ** nlev == L
    mstack = np.zeros((nlev * L, L), np.float32)
    masks = np.zeros((nlev + 1, L, L), np.float32)
    masks[0] = np.eye(L)
    for l in range(1, nlev + 1):
        n = 2 ** l
        for t in range(L):
            blk, pos = divmod(t, n)
            m = blk * n + n // 2 - 1
            if pos >= n // 2:
                mstack[(l - 1) * L + t, m + 1:t + 1] = 1.0
                masks[l, t, blk * n:blk * n + n // 2] = 1.0
            else:
                mstack[(l - 1) * L + t, t + 1:m + 1] = 1.0
    masks = np.tile(masks, (1, 1, GLA_HEADS))
    return nlev, jnp.asarray(mstack, BF16), jnp.asarray(masks)


def _gla_consts(L):
    kw = GLA_HEADS * GLA_DK
    vw = GLA_HEADS * GLA_DV
    hk = np.zeros((GLA_HEADS * L, kw), np.float32)
    for h in range(GLA_HEADS):
        hk[h * L:(h + 1) * L, h * GLA_DK:(h + 1) * GLA_DK] = 1.0
    hs = np.zeros((vw, kw), np.float32)
    for h in range(GLA_HEADS):
        hs[h * GLA_DV:(h + 1) * GLA_DV, h * GLA_DK:(h + 1) * GLA_DK] = 1.0
    return (jnp.asarray(hk, BF16), jnp.asarray(hs), jnp.asarray(np.eye(GLA_DV), BF16), jnp.asarray(np.eye(kw), BF16))


def _gla_kernel(q_ref, k_ref, v_ref, r_ref, glr_ref, wg_ref, bg_ref, gn_ref, tri_ref, mstack_ref,
                masks_ref, hk_ref, hs_ref, eyev_ref, eyek_ref, s0_ref,
                o_ref, so_ref, st_s, *, L, nlev, Bb):
    c = pl.program_id(1)
    H, DK, DV = GLA_HEADS, GLA_DK, GLA_DV

    @pl.when(c == 0)
    def _():
        for bi in range(Bb):
            for h in range(H):
                pieces = ([jnp.zeros((h * DK, DV), F32)] if h else []) + [s0_ref[bi, h]]
                pieces += [jnp.zeros(((H - 1 - h) * DK, DV), F32)] if h < H - 1 else []
                padded = jnp.concatenate(pieces, axis=0)
                st_s[bi, h * DV:(h + 1) * DV, :] = _dot_sel(eyev_ref[...], padded, NT)

    for bi in range(Bb):
        _gla_one(bi, q_ref, k_ref, v_ref, r_ref, glr_ref, wg_ref, bg_ref, gn_ref, tri_ref, mstack_ref,
                 masks_ref, hk_ref, hs_ref, o_ref, st_s, L, nlev)

    @pl.when(c == pl.num_programs(1) - 1)
    def _():
        for bi in range(Bb):
            for h in range(H):
                blk = st_s[bi, h * DV:(h + 1) * DV, :]
                so_ref[bi, h] = _dot_sel(eyek_ref[h * DK:(h + 1) * DK, :], blk, NT)


def _gla_one(bi, q_ref, k_ref, v_ref, r_ref, glr_ref, wg_ref, bg_ref, gn_ref, tri_ref, mstack_ref,
             masks_ref, hk_ref, hs_ref, o_ref, st_s, L, nlev):
    H, DK, DV = GLA_HEADS, GLA_DK, GLA_DV
    packed_rows = L % (2 * SUBLANE) == 0
    q = q_ref[bi] * (DK ** -0.5)
    k = k_ref[bi]
    kwid = H * DK
    g_hi, g_lo = _pieces(glr_ref[bi], 2)
    pre2 = jnp.dot(g_hi.astype(BF16), wg_ref[...], preferred_element_type=F32)
    pre = (pre2[:, :kwid] + pre2[:, kwid:] + jnp.dot(g_lo.astype(BF16), wg_ref[:, :kwid], preferred_element_type=F32)
           + bg_ref[...])
    lg = jax.nn.log_sigmoid(pre) * (1.0 / GLA_TAU)
    G = _dot_sel(tri_ref[...], lg)
    E = jnp.exp(_dot_sel(mstack_ref[...], lg, pieces=2))
    hk = hk_ref[...]
    tok = lax.broadcasted_iota(jnp.int32, (L, kwid), 0)

    att = None
    for l in range(nlev + 1):
        if l == 0:
            ql, kl = q, k
        else:
            e_l = E[(l - 1) * L:l * L]
            upper = ((tok >> (l - 1)) & 1) == 1
            ql = jnp.where(upper, q * e_l, 0.0)
            kl = jnp.where(upper, 0.0, k * e_l)
        if packed_rows:
            kst = jnp.concatenate([kl.astype(BF16)] * H, axis=0) * hk
        else:
            kst = (jnp.concatenate([kl] * H, axis=0) * hk.astype(F32)).astype(BF16)
        part = lax.dot_general(ql.astype(BF16), kst, NT, preferred_element_type=F32)
        part = jnp.where(masks_ref[l] > 0.5, part, 0.0)
        att = part if att is None else att + part

    v = v_ref[bi]
    vsrc = v.astype(BF16) if packed_rows else v
    zero = jnp.zeros((L, DV), vsrc.dtype)
    vbd = jnp.concatenate(
        [jnp.concatenate([zero] * h + [vsrc[:, h * DV:(h + 1) * DV]] + [zero] * (H - 1 - h), axis=1)
         for h in range(H)], axis=0).astype(BF16)
    stb = st_s[bi]
    o = (jnp.dot(att.astype(BF16), vbd, preferred_element_type=F32)
         + lax.dot_general((q * jnp.exp(G)).astype(BF16), stb.astype(BF16), NT, preferred_element_type=F32))
    for h in range(H):
        vs = slice(h * DV, (h + 1) * DV)
        oh = o[:, vs]
        oh = oh * lax.rsqrt(jnp.mean(oh * oh, axis=-1, keepdims=True) + EPS)
        oh = oh * gn_ref[:, vs] * jax.nn.silu(r_ref[bi, :, vs])
        o_ref[bi, :, vs] = oh.astype(o_ref.dtype)
    g_end = G[L - 1:L, :]
    k_end = (k * jnp.exp(g_end - G)).astype(BF16)
    upd = lax.dot_general(v.astype(BF16), k_end, TN, preferred_element_type=F32)
    st_s[bi] = stb * jnp.exp(g_end) + jnp.where(hs_ref[...] > 0.5, upd, 0.0)


def _gla(z3, wg, bg, gn, s0, L, Bb):
    b, t, _ = z3.shape
    nc = t // L
    nlev, mstack, masks = _gla_tables(L)
    hk, hs, eyev, eyek = _gla_consts(L)
    tri = jnp.asarray(np.tril(np.ones((L, L), np.float32)), BF16)
    qk_w = GLA_HEADS * GLA_DK
    v_w = GLA_HEADS * GLA_DV
    col = lambda j: (lambda bi, ci: (bi, ci, j))
    const2 = lambda bi, ci: (0, 0)
    const3 = lambda bi, ci: (0, 0, 0)
    return pl.pallas_call(
        functools.partial(_gla_kernel, L=L, nlev=nlev, Bb=Bb),
        grid=(b // Bb, nc),
        in_specs=[pl.BlockSpec((Bb, L, qk_w), col(0)),
                  pl.BlockSpec((Bb, L, qk_w), col(1)),
                  pl.BlockSpec((Bb, L, v_w), col(1)),
                  pl.BlockSpec((Bb, L, v_w), col(2)),
                  pl.BlockSpec((Bb, L, LANE), col(2048 // LANE)),
                  pl.BlockSpec((LANE, 2 * qk_w), const2),
                  pl.BlockSpec((1, qk_w), const2),
                  pl.BlockSpec((1, v_w), const2),
                  pl.BlockSpec((L, L), const2),
                  pl.BlockSpec(mstack.shape, const2),
                  pl.BlockSpec(masks.shape, const3),
                  pl.BlockSpec(hk.shape, const2),
                  pl.BlockSpec(hs.shape, const2),
                  pl.BlockSpec(eyev.shape, const2),
                  pl.BlockSpec(eyek.shape, const2),
                  pl.BlockSpec((Bb, GLA_HEADS, GLA_DK, GLA_DV), lambda bi, ci: (bi, 0, 0, 0))],
        out_specs=[pl.BlockSpec((Bb, L, v_w), col(0)),
                   pl.BlockSpec((Bb, GLA_HEADS, GLA_DK, GLA_DV), lambda bi, ci: (bi, 0, 0, 0))],
        out_shape=[jax.ShapeDtypeStruct((b, t, v_w), BF16),
                   jax.ShapeDtypeStruct((b, GLA_HEADS, GLA_DK, GLA_DV), F32)],
        scratch_shapes=[pltpu.VMEM((Bb, v_w, qk_w), F32)],
        compiler_params=_params(("parallel", "arbitrary")),
        name="gla",
    )(z3, z3, z3, z3, z3, wg, bg, gn, tri, mstack, masks, hk, hs, eyev, eyek, s0)


def _s5_kernel(u_ref, bbre_ref, bbim_ref, cre_ref, cim_ref, d_ref, wglu_ref, bglu_ref, pre_ref, pim_ref,
               x0re_ref, x0im_ref,
               o_ref, xore_ref, xoim_ref,
               xr_s, xi_s, cr_s, ci_s, *, rows, sequential):
    c = pl.program_id(1)
    lead = rows // SUBLANE if sequential else SUBLANE
    inner = rows // lead
    cw = S5_LANES // 4 if inner == SUBLANE else LANE
    u = jnp.swapaxes(u_ref[0].reshape(inner, lead, S5_WIDTH), 0, 1).reshape(rows, S5_WIDTH)
    ub = u.astype(BF16)
    hw, hl = S5_WIDTH // 2, S5_LANES // 2
    for half in range(2):
        uh = ub[:, half * hw:(half + 1) * hw]
        ls = slice(half * hl, (half + 1) * hl)
        xr_s[:, :, ls] = jnp.dot(uh, bbre_ref[half * hw:(half + 1) * hw, ls],
                                 preferred_element_type=F32).reshape(lead, inner, hl)
        xi_s[:, :, ls] = jnp.dot(uh, bbim_ref[half * hw:(half + 1) * hw, ls],
                                 preferred_element_type=F32).reshape(lead, inner, hl)

    for c0 in range(0, S5_LANES, cw):
        ar = jnp.broadcast_to(pre_ref[0, :, c0:c0 + cw], (inner, cw))
        ai = jnp.broadcast_to(pim_ref[0, :, c0:c0 + cw], (inner, cw))

        def body(j, carry, c0=c0, ar=ar, ai=ai):
            xr, xi = carry
            nr = ar * xr - ai * xi + xr_s[j, :, c0:c0 + cw]
            ni = ar * xi + ai * xr + xi_s[j, :, c0:c0 + cw]
            xr_s[j, :, c0:c0 + cw] = nr
            xi_s[j, :, c0:c0 + cw] = ni
            return nr, ni

        if sequential:
            init = (jnp.zeros((inner, cw), F32), jnp.zeros((inner, cw), F32))
        else:
            init = (x0re_ref[0, :, c0:c0 + cw], x0im_ref[0, :, c0:c0 + cw])
        lax.fori_loop(0, lead, body, init, unroll=4)

    if sequential:
        @pl.when(c == 0)
        def _():
            cr_s[...] = x0re_ref[0]
            ci_s[...] = x0im_ref[0]

        end_r = xr_s[lead - 1]
        end_i = xi_s[lead - 1]
        anr = pre_ref[lead - 1]
        ani = pim_ref[lead - 1]
        ent_r, ent_i = [cr_s[...]], [ci_s[...]]
        for s in range(SUBLANE):
            pr, pi = ent_r[-1], ent_i[-1]
            ent_r.append(end_r[s:s + 1] + anr * pr - ani * pi)
            ent_i.append(end_i[s:s + 1] + anr * pi + ani * pr)
        cr_s[...] = ent_r[SUBLANE]
        ci_s[...] = ent_i[SUBLANE]
        er = jnp.concatenate(ent_r[:SUBLANE], axis=0)[None]
        ei = jnp.concatenate(ent_i[:SUBLANE], axis=0)[None]
        pr3, pi3 = pre_ref[...], pim_ref[...]
        xr = xr_s[...] + pr3 * er - pi3 * ei
        xi = xi_s[...] + pr3 * ei + pi3 * er

        @pl.when(c == pl.num_programs(1) - 1)
        def _():
            xore_ref[0] = cr_s[...]
            xoim_ref[0] = ci_s[...]
    else:
        xr = xr_s[...]
        xi = xi_s[...]
        xore_ref[0] = xr[lead - 1]
        xoim_ref[0] = xi[lead - 1]

    xrb = xr.reshape(rows, S5_LANES).astype(BF16)
    xib = xi.reshape(rows, S5_LANES).astype(BF16)
    y = jnp.concatenate(
        [jnp.dot(xrb[:, h * hl:(h + 1) * hl], cre_ref[h * hl:(h + 1) * hl, h * hw:(h + 1) * hw],
                 preferred_element_type=F32)
         - jnp.dot(xib[:, h * hl:(h + 1) * hl], cim_ref[h * hl:(h + 1) * hl, h * hw:(h + 1) * hw],
                   preferred_element_type=F32) for h in range(2)], axis=1) + d_ref[...] * u
    zz = jax.nn.gelu(y)
    gate = jnp.dot(zz.astype(BF16), wglu_ref[...], preferred_element_type=F32) + bglu_ref[...]
    out = zz * jax.nn.sigmoid(gate)
    o_ref[0] = jnp.swapaxes(out.reshape(lead, inner, S5_WIDTH), 0, 1).reshape(rows, S5_WIDTH).astype(o_ref.dtype)


def _s5(z3, tabs, x0re, x0im, rows, sequential):
    b, t, _ = z3.shape
    nblk = t // rows
    groups = rows // SUBLANE
    ublk = 1536 // S5_WIDTH
    const2 = lambda bi, ci: (0, 0)
    const3 = lambda bi, ci: (0, 0, 0)
    lead = groups if sequential else SUBLANE
    kk = jnp.arange(1, lead + 1, dtype=F32).reshape(lead, 1, 1)
    mag = jnp.exp(kk * tabs['la_re'][None])
    p_re = mag * jnp.cos(kk * tabs['la_im'][None])
    p_im = mag * jnp.sin(kk * tabs['la_im'][None])
    if sequential:
        st_spec = pl.BlockSpec((1, 1, S5_LANES), lambda bi, ci: (bi, 0, 0))
        st_shape = jax.ShapeDtypeStruct((b, 1, S5_LANES), F32)
    else:
        st_spec = pl.BlockSpec((1, groups, S5_LANES), lambda bi, ci: (bi, ci, 0))
        st_shape = jax.ShapeDtypeStruct(x0re.shape, F32)
    return pl.pallas_call(
        functools.partial(_s5_kernel, rows=rows, sequential=sequential),
        grid=(b, nblk),
        in_specs=[pl.BlockSpec((1, rows, S5_WIDTH), lambda bi, ci: (bi, ci, ublk)),
                  pl.BlockSpec((S5_WIDTH, S5_LANES), const2),
                  pl.BlockSpec((S5_WIDTH, S5_LANES), const2),
                  pl.BlockSpec((S5_LANES, S5_WIDTH), const2),
                  pl.BlockSpec((S5_LANES, S5_WIDTH), const2),
                  pl.BlockSpec((1, S5_WIDTH), const2),
                  pl.BlockSpec((S5_WIDTH, S5_WIDTH), const2),
                  pl.BlockSpec((1, S5_WIDTH), const2),
                  pl.BlockSpec((lead, 1, S5_LANES), const3),
                  pl.BlockSpec((lead, 1, S5_LANES), const3),
                  st_spec, st_spec],
        out_specs=[pl.BlockSpec((1, rows, S5_WIDTH), lambda bi, ci: (bi, ci, 0)),
                   st_spec, st_spec],
        out_shape=[jax.ShapeDtypeStruct((b, t, S5_WIDTH), BF16), st_shape, st_shape],
        scratch_shapes=[pltpu.VMEM((lead, rows // lead, S5_LANES), F32),
                        pltpu.VMEM((lead, rows // lead, S5_LANES), F32),
                        pltpu.VMEM((1, S5_LANES), F32),
                        pltpu.VMEM((1, S5_LANES), F32)],
        compiler_params=_params(("parallel", "arbitrary")),
        name="s5",
    )(z3, tabs['bbre'], tabs['bbim'], tabs['cre'], tabs['cim'], tabs['d'], tabs['wglu'], tabs['bglu'],
      p_re, p_im, x0re, x0im)


def _s5_tables(a_re, a_im, log_dt, b_re, b_im, c_re, c_im, d, w_glu, b_glu):
    lam_re = a_re.astype(F32)
    lam_im = a_im.astype(F32)
    dt = jnp.exp(log_dt.astype(F32))[:, None]
    mag = jnp.exp(lam_re * dt)
    ang = lam_im * dt
    ab_re = mag * jnp.cos(ang)
    ab_im = mag * jnp.sin(ang)
    den = lam_re * lam_re + lam_im * lam_im
    co_re = ((ab_re - 1.0) * lam_re + ab_im * lam_im) / den
    co_im = (ab_im * lam_re - (ab_re - 1.0) * lam_im) / den
    b_re = b_re.astype(F32)
    b_im = b_im.astype(F32)
    bb_re = co_re[..., None] * b_re - co_im[..., None] * b_im
    bb_im = co_re[..., None] * b_im + co_im[..., None] * b_re
    eye = jnp.eye(S5_GROUPS, dtype=F32)
    blockdiag_in = lambda bb: jnp.einsum('gph,gk->ghkp', bb, eye).reshape(S5_WIDTH, S5_LANES)
    blockdiag_out = lambda cc: jnp.einsum('ghp,gk->gpkh', cc.astype(F32), eye).reshape(S5_LANES, S5_WIDTH)
    return {
        'bbre': blockdiag_in(bb_re).astype(BF16), 'bbim': blockdiag_in(bb_im).astype(BF16),
        'cre': blockdiag_out(c_re).astype(BF16), 'cim': blockdiag_out(c_im).astype(BF16),
        'd': d.astype(F32).reshape(1, S5_WIDTH),
        'wglu': w_glu.astype(BF16), 'bglu': b_glu.astype(F32).reshape(1, S5_WIDTH),
        'la_re': (lam_re * dt).reshape(1, S5_LANES), 'la_im': ang.reshape(1, S5_LANES),
    }


def _rmsnorm_val(x, g):
    return x * lax.rsqrt(jnp.mean(x * x, axis=-1, keepdims=True) + EPS) * g


def _ffn_kernel(x_ref, g_ref, wup_ref, cw_ref, cb_ref, wd_ref, st_ref, gf_ref, o_ref, so_ref, halo_s,
                *, tm, tf, long_seq, final_norm):
    i = pl.program_id(1)
    n = tm // SUBLANE
    d = x_ref.shape[-1]
    x = x_ref[0]
    hn = _rmsnorm_val(x, g_ref[...])
    if long_seq:
        lead = n

        @pl.when(i == 0)
        def _():
            halo_s[...] = st_ref[0]
    else:
        lead = SUBLANE
    hnp = jnp.swapaxes(hn.reshape(tm // lead, lead, d), 0, 1).reshape(tm, d).astype(BF16)
    inner = tm // lead
    sub = lax.broadcasted_iota(jnp.int32, (SUBLANE, tf), 0)

    acc = jnp.zeros((tm, d), F32)
    for c in range(D_FF // tf):
        conv = []
        for half in range(2):
            c0 = half * D_FF + c * tf
            u = jnp.dot(hnp, wup_ref[:, c0:c0 + tf], preferred_element_type=F32).reshape(lead, inner, tf)
            if long_seq:
                um1 = jnp.where(sub == 0, halo_s[1:2, c0:c0 + tf], pltpu.roll(u[lead - 1], 1, axis=0))
                um2 = jnp.where(sub == 0, halo_s[0:1, c0:c0 + tf], pltpu.roll(u[lead - 2], 1, axis=0))
                halo_s[0:1, c0:c0 + tf] = u[lead - 2][SUBLANE - 1:SUBLANE]
                halo_s[1:2, c0:c0 + tf] = u[lead - 1][SUBLANE - 1:SUBLANE]
            else:
                um2 = st_ref[0, :, c0:c0 + tf]
                um1 = st_ref[0, :, 2 * D_FF + c0:2 * D_FF + c0 + tf]
                so_ref[0, :, c0:c0 + tf] = u[lead - 2]
                so_ref[0, :, 2 * D_FF + c0:2 * D_FF + c0 + tf] = u[lead - 1]
            ext = jnp.concatenate([um2[None], um1[None], u], axis=0)
            cv = cb_ref[:, c0:c0 + tf]
            for j in range(CONV_W):
                cv = cv + ext[j:j + lead] * cw_ref[j:j + 1, c0:c0 + tf]
            conv.append(cv)
        act = (jax.nn.silu(conv[1]) * conv[0]).reshape(tm, tf).astype(BF16)
        acc = acc + jnp.dot(act, wd_ref[c * tf:(c + 1) * tf, :], preferred_element_type=F32)

    out = x + jnp.swapaxes(acc.reshape(lead, inner, d), 0, 1).reshape(tm, d)
    if final_norm:
        out = _rmsnorm_val(out, gf_ref[...])
    o_ref[0] = out

    if long_seq:
        @pl.when(i == pl.num_programs(1) - 1)
        def _():
            so_ref[0] = halo_s[...]


def _ffn(h3, g, wup, cw, cb, wd, state, gfinal, tm, tf, long_seq, final_norm):
    b, t, d = h3.shape
    nt = t // tm
    resident = dict(pipeline_mode=pl.Buffered(1))
    if long_seq:
        st_spec = pl.BlockSpec((1, CONV_W - 1, 2 * D_FF), lambda bi, i: (bi, 0, 0))
    else:
        st_spec = pl.BlockSpec((1, tm // SUBLANE, (CONV_W - 1) * 2 * D_FF), lambda bi, i: (bi, i, 0))
    return pl.pallas_call(
        functools.partial(_ffn_kernel, tm=tm, tf=tf, long_seq=long_seq, final_norm=final_norm),
        grid=(b, nt),
        in_specs=[pl.BlockSpec((1, tm, d), lambda bi, i: (bi, i, 0)),
                  pl.BlockSpec((1, d), lambda bi, i: (0, 0)),
                  pl.BlockSpec((d, 2 * D_FF), lambda bi, i: (0, 0), **resident),
                  pl.BlockSpec((CONV_W, 2 * D_FF), lambda bi, i: (0, 0)),
                  pl.BlockSpec((1, 2 * D_FF), lambda bi, i: (0, 0)),
                  pl.BlockSpec((D_FF, d), lambda bi, i: (0, 0), **resident),
                  st_spec,
                  pl.BlockSpec((1, d), lambda bi, i: (0, 0))],
        out_specs=[pl.BlockSpec((1, tm, d), lambda bi, i: (bi, i, 0)), st_spec],
        out_shape=[jax.ShapeDtypeStruct((b, t, d), F32), jax.ShapeDtypeStruct(state.shape, F32)],
        scratch_shapes=[pltpu.VMEM((CONV_W - 1, 2 * D_FF), F32)],
        compiler_params=_params(("parallel", "arbitrary")),
        name="ffn",
    )(h3, g.reshape(1, d), wup, cw, cb, wd, state, gfinal)


def _rope_tables(pos):
    half = SW_HD // 2
    inv = ROPE_THETA ** (-jnp.arange(half, dtype=F32) / half)
    ang = pos.astype(F32)[:, None] * inv[None, :]
    cos = jnp.cos(ang)
    sin = jnp.sin(ang)
    reps = LANE // SW_HD
    cos_t = jnp.tile(jnp.concatenate([cos, cos], axis=1), (1, reps))
    sin_t = jnp.tile(jnp.concatenate([-sin, sin], axis=1), (1, reps))
    return cos_t, sin_t


def _prep_weights(W):
    P = {}
    we = W['w_in_even']
    parts = jnp.split(we, [int(c) for c in np.cumsum((512, 512, 512, 512, 8, 512, 128))], axis=1)
    q_m, k_m, v_m, o_m, g_m, q_a, k_a, v_a = parts
    dup = lambda w: jnp.concatenate([w[:, :SW_HD], w[:, :SW_HD], w[:, SW_HD:], w[:, SW_HD:]], axis=1)
    pad = jnp.zeros((D_MODEL, LANE - 2 * ML_HEADS), we.dtype)
    P['w_in_even'] = jnp.concatenate([q_m, k_m, v_m, o_m, q_a, dup(k_a), dup(v_a), g_m, pad], axis=1).astype(BF16)
    assert P['w_in_even'].shape[1] == EVEN_PAD
    P['b_gates'] = jnp.pad(W['b_mlstm_gates'].astype(F32), (0, LANE - 2 * ML_HEADS)).reshape(1, LANE)
    wo = W['w_in_odd']
    q, k, v, r, glr, u = jnp.split(wo, [int(c) for c in np.cumsum((256, 256, 512, 512, 16))], axis=1)
    pad = jnp.zeros((D_MODEL, ODD_PAD - 2048 - GLA_RANK), wo.dtype)
    P['w_in_odd'] = jnp.concatenate([q, k, v, r, u, glr, pad], axis=1).astype(BF16)
    wg = jnp.pad(W['w_gla_gate_up'].astype(F32), ((0, LANE - GLA_RANK), (0, 0)))
    wg_hi = wg.astype(BF16)
    P['w_gate_up'] = jnp.concatenate([wg_hi, (wg - wg_hi.astype(F32)).astype(BF16)], axis=1)
    P['b_gate'] = W['b_gla_gate'].astype(F32).reshape(1, -1)
    P['g_gla'] = W['g_gla_norm'].astype(F32).reshape(1, -1)
    hm = ML_HEADS * ML_DV
    P['w_out_even'] = (W['w_out_even'][:hm].astype(BF16), W['w_out_even'][hm:].astype(BF16))
    hg = GLA_HEADS * GLA_DV
    P['w_out_odd'] = (W['w_out_odd'][:hg].astype(BF16), W['w_out_odd'][hg:].astype(BF16))
    P['w_ffn_up'] = W['w_ffn_up'].astype(BF16)
    P['w_ffn_down'] = W['w_ffn_down'].astype(BF16)
    P['s5'] = _s5_tables(W['s5_a_re'], W['s5_a_im'], W['s5_log_dt'], W['s5_b_re'], W['s5_b_im'],
                         W['s5_c_re'], W['s5_c_im'], W['s5_d'], W['w_s5_glu'], W['b_s5_glu'])
    return P


def _trunk(x, pos, st, W, P, is_prompt):
    b, t, d = x.shape
    n = b * t
    L = CHUNK if t % CHUNK == 0 else t
    tm = 512 if n % 512 == 0 else n
    cos, sin_signed = _rope_tables(pos)
    gfinal = W['norm_final'].astype(F32).reshape(1, d)
    new = {}
    conv_out = []
    h = x.reshape(n, d)

    z = _norm_matmul(h, W['norm_mix'][0], P['w_in_even'], tm)
    z3 = z.reshape(b, t, EVEN_PAD)
    h_m, new['ml_C'], new['ml_n'], m_new = _mlstm(z3, P['b_gates'], st['ml_C'], st['ml_n'], st['ml_m'], L,
                                                  Bb=2 if is_prompt else 8)
    new['ml_m'] = m_new.reshape(b, ML_HEADS)
    if is_prompt:
        h_a, kb = _swa_prompt(z3, cos, sin_signed, W['sw_sinks'].astype(F32))
        vb = jnp.concatenate([z3[:, t - WINDOW:, EVEN_VD:EVEN_VD + SW_HD],
                              z3[:, t - WINDOW:, EVEN_VD + LANE:EVEN_VD + LANE + SW_HD]], axis=-1)
    else:
        h_a, kb, vb = _swa_sample(z3, st['kbuf'].reshape(b, WINDOW, -1), st['vbuf'].reshape(b, WINDOW, -1),
                                  cos, sin_signed, W['sw_sinks'].astype(F32), Bb=8)
    new['kbuf'] = kb.reshape(b, WINDOW, SW_KV_HEADS, SW_HD)
    new['vbuf'] = vb.reshape(b, WINDOW, SW_KV_HEADS, SW_HD)
    h = _proj_res(h_m.reshape(n, -1), h_a.reshape(n, -1), *P['w_out_even'], h, tm)
    h, cb = _ffn_layer(h, 0, st, W, P, b, t, gfinal, is_prompt, final_norm=False)
    conv_out.append(cb)

    z = _norm_matmul(h, W['norm_mix'][1], P['w_in_odd'], tm)
    z3 = z.reshape(b, t, ODD_PAD)
    o_g, new['gla'] = _gla(z3, P['w_gate_up'], P['b_gate'], P['g_gla'], st['gla'], L, Bb=2 if is_prompt else 4)
    x0re = st['s5_re'].astype(F32).reshape(b, 1, S5_LANES)
    x0im = st['s5_im'].astype(F32).reshape(b, 1, S5_LANES)
    if is_prompt:
        o_s, xre, xim = _s5(z3, P['s5'], x0re, x0im, rows=256 if t % 256 == 0 else t, sequential=True)
    else:
        o_s, xre, xim = _s5(z3.reshape(1, n, ODD_PAD), P['s5'], x0re.reshape(1, b, S5_LANES),
                            x0im.reshape(1, b, S5_LANES), rows=256 if n % 256 == 0 else n, sequential=False)
    new['s5_re'] = xre.reshape(b, S5_GROUPS, S5_STATE)
    new['s5_im'] = xim.reshape(b, S5_GROUPS, S5_STATE)
    h = _proj_res(o_g.reshape(n, -1), o_s.reshape(n, -1), *P['w_out_odd'], h, tm)
    h, cb = _ffn_layer(h, 1, st, W, P, b, t, gfinal, is_prompt, final_norm=True)
    conv_out.append(cb)
    new['conv'] = jnp.stack(conv_out)
    return h.reshape(b, t, d), new


def _ffn_layer(h, layer, st, W, P, b, t, gfinal, is_prompt, final_norm):
    n, d = h.shape
    cw = W['ffn_conv_w'][layer].astype(F32)
    cb = W['ffn_conv_b'][layer].astype(F32).reshape(1, 2 * D_FF)
    state = st['conv'][layer].astype(F32)
    rows = t if is_prompt else n
    tm = next((c for c in (512, 256) if rows % c == 0), rows)
    args = (W['norm_ffn'][layer], P['w_ffn_up'][layer], cw, cb, P['w_ffn_down'][layer])
    if is_prompt:
        out, new_state = _ffn(h.reshape(b, t, d), *args, state, gfinal, tm, FFN_TF, True, final_norm)
    else:
        assert t == SUBLANE
        out, new_state = _ffn(h.reshape(1, n, d), *args, state.reshape(1, b, -1), gfinal, tm, FFN_TF, False,
                              final_norm)
    return out.reshape(n, d), new_state.reshape(b, CONV_W - 1, 2 * D_FF)


def kernel(x_prompt, x_sample, state_mlstm_C, state_mlstm_n, state_mlstm_m, cache_swa_k, cache_swa_v,
           state_gla, state_s5_re, state_s5_im, state_ffn_conv, norm_mix, norm_ffn, norm_final,
           w_in_even, b_mlstm_gates, sw_sinks, w_out_even, w_in_odd, w_gla_gate_up, b_gla_gate, g_gla_norm,
           s5_a_re, s5_a_im, s5_log_dt, s5_b_re, s5_b_im, s5_c_re, s5_c_im, s5_d, w_s5_glu, b_s5_glu,
           w_out_odd, w_ffn_up, ffn_conv_w, ffn_conv_b, w_ffn_down):
    W = {'norm_mix': norm_mix.astype(F32), 'norm_ffn': norm_ffn.astype(F32), 'norm_final': norm_final,
         'w_in_even': w_in_even, 'b_mlstm_gates': b_mlstm_gates, 'sw_sinks': sw_sinks, 'w_out_even': w_out_even,
         'w_in_odd': w_in_odd, 'w_gla_gate_up': w_gla_gate_up, 'b_gla_gate': b_gla_gate, 'g_gla_norm': g_gla_norm,
         's5_a_re': s5_a_re, 's5_a_im': s5_a_im, 's5_log_dt': s5_log_dt, 's5_b_re': s5_b_re, 's5_b_im': s5_b_im,
         's5_c_re': s5_c_re, 's5_c_im': s5_c_im, 's5_d': s5_d, 'w_s5_glu': w_s5_glu, 'b_s5_glu': b_s5_glu,
         'w_out_odd': w_out_odd, 'w_ffn_up': w_ffn_up, 'ffn_conv_w': ffn_conv_w, 'ffn_conv_b': ffn_conv_b,
         'w_ffn_down': w_ffn_down}
    P = _prep_weights(W)
    bp, tp = x_prompt.shape[:2]
    st_prompt = {'ml_C': jnp.zeros((bp, ML_HEADS, ML_DK, ML_DV), F32),
                 'ml_n': jnp.zeros((bp, ML_HEADS, ML_DK), F32),
                 'ml_m': jnp.zeros((bp, ML_HEADS), F32),
                 'gla': jnp.zeros((bp, GLA_HEADS, GLA_DK, GLA_DV), F32),
                 's5_re': jnp.zeros((bp, S5_GROUPS, S5_STATE), F32),
                 's5_im': jnp.zeros((bp, S5_GROUPS, S5_STATE), F32),
                 'conv': jnp.zeros((2, bp, CONV_W - 1, 2 * D_FF), F32)}
    st_sample = {'ml_C': state_mlstm_C.astype(F32), 'ml_n': state_mlstm_n.astype(F32),
                 'ml_m': state_mlstm_m.astype(F32), 'kbuf': cache_swa_k.astype(F32),
                 'vbuf': cache_swa_v.astype(F32), 'gla': state_gla.astype(F32),
                 's5_re': state_s5_re, 's5_im': state_s5_im, 'conv': state_ffn_conv}
    past_len = 16384
    y_p, np_ = _trunk(x_prompt.astype(F32), jnp.arange(tp), st_prompt, W, P, True)
    y_s, ns_ = _trunk(x_sample.astype(F32), past_len + jnp.arange(x_sample.shape[1]), st_sample, W, P, False)
    return (y_p, y_s,
            np_['ml_C'], ns_['ml_C'], np_['ml_n'], ns_['ml_n'], np_['ml_m'], ns_['ml_m'],
            np_['kbuf'], ns_['kbuf'], np_['vbuf'], ns_['vbuf'], np_['gla'], ns_['gla'],
            np_['s5_re'], ns_['s5_re'], np_['s5_im'], ns_['s5_im'], np_['conv'], ns_['conv'])
```

```python
import functools
import math

import numpy as np
import jax
import jax.numpy as jnp
from jax import lax
from jax.experimental import pallas as pl
from jax.experimental.pallas import tpu as pltpu

F32 = jnp.float32
BF16 = jnp.bfloat16
HI = lax.Precision.HIGHEST
NT = (((1,), (1,)), ((), ()))
TN = (((0,), (0,)), ((), ()))

D_MODEL = 1024
ML_HEADS, ML_DK, ML_DV = 4, 128, 128
SW_HEADS, SW_KV_HEADS, SW_HD, WINDOW = 8, 2, 64, 128
SW_GQ = SW_HEADS // SW_KV_HEADS
ROPE_THETA = 10000.0
GLA_HEADS, GLA_DK, GLA_DV, GLA_RANK, GLA_TAU = 4, 64, 128, 16, 16.0
S5_WIDTH, S5_GROUP, S5_STATE = 512, 16, 64
S5_GROUPS = S5_WIDTH // S5_GROUP
S5_LANES = S5_GROUPS * S5_STATE
D_FF = 2816
FFN_TF = 1408
CONV_W = 3
CHUNK = 64
EPS = 1e-6

LANE = 128
SUBLANE = 8
VMEM_LIMIT = 56 * 1024 * 1024

EVEN_QA, EVEN_KD, EVEN_VD, EVEN_PAD = 2048, 2560, 2816, 3200
ODD_PAD = 2176


def _params(sem):
    return pltpu.CompilerParams(dimension_semantics=sem, vmem_limit_bytes=VMEM_LIMIT)


def _norm_matmul_kernel(x_ref, g_ref, w_ref, o_ref):
    x = x_ref[...]
    ms = jnp.mean(x * x, axis=-1, keepdims=True)
    hn = (x * lax.rsqrt(ms + EPS) * g_ref[...]).astype(BF16)
    o_ref[...] = jnp.dot(hn, w_ref[...], preferred_element_type=F32)


def _norm_matmul(x, g, w, tm):
    n, d = x.shape
    c = w.shape[1]
    return pl.pallas_call(
        _norm_matmul_kernel,
        grid=(n // tm,),
        in_specs=[pl.BlockSpec((tm, d), lambda i: (i, 0)),
                  pl.BlockSpec((1, d), lambda i: (0, 0)),
                  pl.BlockSpec((d, c), lambda i: (0, 0))],
        out_specs=pl.BlockSpec((tm, c), lambda i: (i, 0)),
        out_shape=jax.ShapeDtypeStruct((n, c), F32),
        compiler_params=_params(("parallel",)),
        name="norm_matmul",
    )(x, g.reshape(1, d), w)


def _pieces(x, n):
    out, r = [], x
    for _ in range(n):
        p = r.astype(BF16).astype(F32)
        out.append(p)
        r = r - p
    return out


def _dot_sel(m_bf16, x, dims=None, pieces=3):
    ps = _pieces(x, pieces)
    axis = 1 if dims is None else 0
    w = x.shape[axis]
    if w % LANE:
        f = (lambda p: jnp.dot(m_bf16, p, preferred_element_type=F32)) if dims is None else (
            lambda p: lax.dot_general(m_bf16, p, dims, preferred_element_type=F32))
        return sum(f(p.astype(BF16)) for p in ps)
    stacked = jnp.concatenate(ps, axis=axis).astype(BF16)
    if dims is None:
        r = jnp.dot(m_bf16, stacked, preferred_element_type=F32)
    else:
        r = lax.dot_general(m_bf16, stacked, dims, preferred_element_type=F32)
    return sum(r[:, i * w:(i + 1) * w] for i in range(pieces))


def _sel_right(x, e_bf16, pieces=3):
    rows = x.shape[0]
    r = jnp.dot(jnp.concatenate(_pieces(x, pieces), axis=0).astype(BF16), e_bf16, preferred_element_type=F32)
    return sum(r[i * rows:(i + 1) * rows] for i in range(pieces))


def _mlstm_kernel(q_ref, k_ref, v_ref, og_ref, gz_ref, bias_ref, tri_ref, ones_ref, expc_ref, expv_ref, eyet_ref,
                  seg_ref, segv_ref, hmk_ref, c0_ref, n0_ref, m0_ref,
                  h_ref, co_ref, no_ref, mo_ref,
                  c_s, n_s, m_s, *, L, Bb, nsub):
    c = pl.program_id(1)

    @pl.when(c == 0)
    def _():
        c_s[...] = c0_ref[...]
        n_s[...] = n0_ref[...]
        m_s[...] = m0_ref[...]

    for sub in range(nsub):
        for bi in range(Bb):
            _mlstm_one(bi, sub * L, q_ref, k_ref, v_ref, og_ref, gz_ref, bias_ref, tri_ref, ones_ref, expc_ref,
                       expv_ref, eyet_ref, seg_ref, segv_ref, hmk_ref, h_ref, c_s, n_s, m_s, L)

    @pl.when(c == pl.num_programs(1) - 1)
    def _():
        co_ref[...] = c_s[...]
        no_ref[...] = n_s[...]
        mo_ref[...] = m_s[...]


def _mlstm_one(bi, r0, q_ref, k_ref, v_ref, og_ref, gz_ref, bias_ref, tri_ref, ones_ref, expc_ref, expv_ref,
               eyet_ref, seg_ref, segv_ref, hmk_ref, h_ref, c_s, n_s, m_s, L):
    H, DK = ML_HEADS, ML_DK
    HL = H * L
    lane = lax.broadcasted_iota(jnp.int32, (L, LANE), 1)
    rowi = lax.broadcasted_iota(jnp.int32, (L, LANE), 0)
    keep = lane < H
    heads_only = lambda x: jnp.where(keep[:x.shape[0]], x, 0.0)

    rw = pl.ds(r0, L)
    gates = gz_ref[bi, rw] + bias_ref[...]
    lf = jax.nn.log_sigmoid(gates)
    b = pltpu.roll(_dot_sel(tri_ref[...], lf), LANE - H, axis=1)
    vv = gates - b
    cm = vv
    sh = 1
    while sh < L:
        cm = jnp.maximum(cm, jnp.where(rowi >= sh, pltpu.roll(cm, sh, axis=0), -jnp.inf))
        sh *= 2
    m_prev = m_s[bi]
    mt = b + jnp.maximum(m_prev, cm)
    a_inter = jnp.exp(b + m_prev - mt)

    expv = expv_ref[...]
    m_new = mt[L - 1:L, :]
    b_end = b[L - 1:L, :]
    w_end = heads_only(jnp.exp(b_end - b + gates - m_new))
    decay = heads_only(jnp.exp(b_end + m_prev - m_new))
    spread_k = _sel_right(jnp.concatenate([heads_only(b - mt), heads_only(vv)], axis=0), expc_ref[...])
    uc = spread_k[:L]
    vr = _dot_sel(ones_ref[...], spread_k[L:] * eyet_ref[...])
    spread_v = _sel_right(jnp.concatenate([heads_only(a_inter), w_end, jnp.broadcast_to(decay, (SUBLANE, LANE))],
                                          axis=0), expv, pieces=2)
    ac, wc, dc = spread_v[:L], spread_v[L:2 * L], spread_v[2 * L:2 * L + 1]
    ti = lax.broadcasted_iota(jnp.int32, (L, HL), 0)
    si = lax.broadcasted_iota(jnp.int32, (L, HL), 1) % L
    w = jnp.exp(jnp.where(si <= ti, uc + vr, -jnp.inf))

    q = q_ref[bi, rw]
    ks = k_ref[bi, rw] * (DK ** -0.5)
    v = v_ref[bi, rw]
    qb = q.astype(BF16)
    vb = v.astype(BF16)
    if L % (2 * SUBLANE) == 0:
        kst = jnp.concatenate([ks.astype(BF16)] * H, axis=0) * hmk_ref[...]
        vsrc = vb
    else:
        kst = (jnp.concatenate([ks] * H, axis=0) * hmk_ref[...].astype(F32)).astype(BF16)
        vsrc = v
    zero = jnp.zeros((L, DK), vsrc.dtype)
    vbd = jnp.concatenate(
        [jnp.concatenate([zero] * h + [vsrc[:, h * DK:(h + 1) * DK]] + [zero] * (H - 1 - h), axis=1)
         for h in range(H)], axis=0).astype(BF16)
    s = lax.dot_general(qb, kst, NT, preferred_element_type=F32) * w
    qc = jnp.concatenate([jnp.dot(qb[:, h * DK:(h + 1) * DK], c_s[bi, h].astype(BF16), preferred_element_type=F32)
                          for h in range(H)], axis=1)
    num = jnp.dot(s.astype(BF16), vbd, preferred_element_type=F32) + ac * qc
    n_row = n_s[bi]
    den = (_sel_right(s, seg_ref[...], pieces=2)
           + a_inter * _sel_right(q * n_row, segv_ref[...], pieces=2))
    rden = heads_only(1.0 / jnp.maximum(jnp.abs(den), jnp.exp(-mt)))
    hh = num * _sel_right(rden, expv, pieces=2) * jax.nn.sigmoid(og_ref[bi, rw])
    h_ref[bi, rw] = hh.astype(h_ref.dtype)

    kw = ks * wc
    kwb = kw.astype(BF16)
    for h in range(H):
        hs = slice(h * DK, (h + 1) * DK)
        c_s[bi, h] = dc[:, hs] * c_s[bi, h] + lax.dot_general(kwb[:, hs], vb[:, hs], TN, preferred_element_type=F32)
    n_s[bi] = dc * n_row + jnp.sum(kw, axis=0, keepdims=True)
    m_s[bi] = heads_only(m_new)


def _mlstm_consts(L):
    H, DK = ML_HEADS, ML_DK
    expc = np.zeros((LANE, H * L), np.float32)
    expv = np.zeros((LANE, H * DK), np.float32)
    seg = np.zeros((H * L, LANE), np.float32)
    segv = np.zeros((H * DK, LANE), np.float32)
    hmk = np.zeros((H * L, H * DK), np.float32)
    for h in range(H):
        expc[h, h * L:(h + 1) * L] = 1.0
        expv[h, h * DK:(h + 1) * DK] = 1.0
        seg[h * L:(h + 1) * L, h] = 1.0
        segv[h * DK:(h + 1) * DK, h] = 1.0
        hmk[h * L:(h + 1) * L, h * DK:(h + 1) * DK] = 1.0
    eyet = np.tile(np.eye(L, dtype=np.float32), (1, H))
    bf = lambda a: jnp.asarray(a, BF16)
    return (bf(np.tril(np.ones((L, L), np.float32))), bf(np.ones((L, L), np.float32)), bf(expc), bf(expv),
            jnp.asarray(eyet), bf(seg), bf(segv), bf(hmk))


def _mlstm(z3, bias, c0, n0, m0, L, Bb, nsub):
    b, t, _ = z3.shape
    nc = t // (L * nsub)
    hw = ML_HEADS * ML_DK
    consts = _mlstm_consts(L)
    gate_blk = (EVEN_PAD - LANE) // LANE
    col = lambda j: (lambda bi, ci: (bi, ci, j))
    const2 = lambda bi, ci: (0, 0)
    state3 = lambda bi, ci: (bi, 0, 0)
    m0p = jnp.pad(m0, ((0, 0), (0, LANE - ML_HEADS))).reshape(b, 1, LANE)
    h, c_new, n_new, m_new = pl.pallas_call(
        functools.partial(_mlstm_kernel, L=L, Bb=Bb, nsub=nsub),
        grid=(b // Bb, nc),
        in_specs=[pl.BlockSpec((Bb, nsub * L, hw), col(0)),
                  pl.BlockSpec((Bb, nsub * L, hw), col(1)),
                  pl.BlockSpec((Bb, nsub * L, hw), col(2)),
                  pl.BlockSpec((Bb, nsub * L, hw), col(3)),
                  pl.BlockSpec((Bb, nsub * L, LANE), col(gate_blk)),
                  pl.BlockSpec((1, LANE), const2)]
                 + [pl.BlockSpec(a.shape, const2) for a in consts]
                 + [pl.BlockSpec((Bb, ML_HEADS, ML_DK, ML_DV), lambda bi, ci: (bi, 0, 0, 0)),
                    pl.BlockSpec((Bb, 1, hw), state3),
                    pl.BlockSpec((Bb, 1, LANE), state3)],
        out_specs=[pl.BlockSpec((Bb, nsub * L, hw), col(0)),
                   pl.BlockSpec((Bb, ML_HEADS, ML_DK, ML_DV), lambda bi, ci: (bi, 0, 0, 0)),
                   pl.BlockSpec((Bb, 1, hw), state3),
                   pl.BlockSpec((Bb, 1, LANE), state3)],
        out_shape=[jax.ShapeDtypeStruct((b, t, hw), BF16),
                   jax.ShapeDtypeStruct((b, ML_HEADS, ML_DK, ML_DV), F32),
                   jax.ShapeDtypeStruct((b, 1, hw), F32),
                   jax.ShapeDtypeStruct((b, 1, LANE), F32)],
        scratch_shapes=[pltpu.VMEM((Bb, ML_HEADS, ML_DK, ML_DV), F32),
                        pltpu.VMEM((Bb, 1, hw), F32),
                        pltpu.VMEM((Bb, 1, LANE), F32)],
        compiler_params=_params(("parallel", "arbitrary")),
        name="mlstm",
    )(z3, z3, z3, z3, z3, bias, *consts, c0, n0.reshape(b, 1, hw), m0p)
    return h, c_new, n_new.reshape(b, ML_HEADS, ML_DK), m_new[:, 0, :ML_HEADS]


def _rope(x, cos, sin_signed, width):
    half = SW_HD // 2
    lane = lax.broadcasted_iota(jnp.int32, x.shape, 1)
    first = (lane % SW_HD) < half
    partner = jnp.where(first, pltpu.roll(x, width - half, axis=1), pltpu.roll(x, half, axis=1))
    return x * cos + partner * sin_signed


def _undup(xd):
    lane = lax.broadcasted_iota(jnp.int32, xd.shape[:-1] + (LANE,), xd.ndim - 1)
    return jnp.where(lane < SW_HD, xd[..., :LANE], xd[..., LANE:])


def _swa_prompt_kernel(sink_ref, q_ref, kc_ref, kp_ref, vc_ref, vp_ref, cosc_ref, sinc_ref, cosp_ref, sinp_ref,
                       h_ref, ko_ref):
    n = pl.program_id(1)
    W2 = 2 * WINDOW
    cosc, sinc = cosc_ref[...], sinc_ref[...]
    cosp, sinp = cosp_ref[...], sinp_ref[...]
    q = _rope(q_ref[0], jnp.concatenate([cosc] * 4, axis=1), jnp.concatenate([sinc] * 4, axis=1), 4 * LANE)
    k_cur = _rope(kc_ref[0], jnp.concatenate([cosc] * 2, axis=1), jnp.concatenate([sinc] * 2, axis=1), 2 * LANE)
    k_prev = _rope(kp_ref[0], jnp.concatenate([cosp] * 2, axis=1), jnp.concatenate([sinp] * 2, axis=1), 2 * LANE)
    kd = jnp.concatenate([k_prev, k_cur], axis=0)
    vd = jnp.concatenate([vp_ref[0], vc_ref[0]], axis=0)
    i = lax.broadcasted_iota(jnp.int32, (WINDOW, W2), 0)
    j = lax.broadcasted_iota(jnp.int32, (WINDOW, W2), 1)
    diff = WINDOW + i - j
    valid = (diff >= 0) & (diff < WINDOW) & ((n > 0) | (j >= WINDOW))
    row = lax.broadcasted_iota(jnp.int32, (2 * W2, LANE), 0)
    lane = lax.broadcasted_iota(jnp.int32, (2 * W2, LANE), 1)
    bd = (row < W2) == (lane < SW_HD)
    ones_bd = bd.astype(BF16)
    npair = SW_GQ // 2
    rows = npair * WINDOW
    valid = jnp.concatenate([valid] * npair, axis=0)
    low = lax.broadcasted_iota(jnp.int32, (rows, LANE), 1) < SW_HD
    prow = lax.broadcasted_iota(jnp.int32, (rows, 1), 0) // WINDOW
    for kv in range(SW_KV_HEADS):
        kblk = kd[:, kv * LANE:(kv + 1) * LANE]
        vblk = vd[:, kv * LANE:(kv + 1) * LANE]
        kbd = jnp.where(bd, jnp.concatenate([kblk, kblk], axis=0), 0.0).astype(BF16)
        vbd = jnp.where(bd, jnp.concatenate([vblk, vblk], axis=0), 0.0).astype(BF16)
        vo = jnp.concatenate([vbd, ones_bd], axis=1)
        h0 = kv * SW_GQ
        qs = jnp.concatenate([q[:, (h0 + 2 * p) * SW_HD:(h0 + 2 * p + 2) * SW_HD] for p in range(npair)],
                             axis=0).astype(BF16)
        s = lax.dot_general(qs, kbd, NT, preferred_element_type=F32) * (SW_HD ** -0.5)
        es, sink_terms = [], []
        for hh in range(2):
            sh = jnp.where(valid, s[:, hh * W2:(hh + 1) * W2], -jnp.inf)
            sink = sink_ref[h0 + hh]
            for p in range(1, npair):
                sink = jnp.where(prow == p, sink_ref[h0 + 2 * p + hh], sink)
            m = jnp.maximum(jnp.max(sh, axis=1, keepdims=True), sink)
            es.append(jnp.exp(sh - m).astype(BF16))
            sink_terms.append(jnp.exp(sink - m))
        r = jnp.dot(jnp.concatenate(es, axis=1), vo, preferred_element_type=F32)
        o = r[:, :LANE] / (r[:, LANE:] + jnp.where(low, sink_terms[0], sink_terms[1]))
        for p in range(npair):
            h_ref[0, :, (h0 + 2 * p) * SW_HD:(h0 + 2 * p + 2) * SW_HD] = (
                o[p * WINDOW:(p + 1) * WINDOW].astype(h_ref.dtype))

    @pl.when(n == pl.num_programs(1) - 1)
    def _():
        ko_ref[0] = _undup(k_cur)


def _swa_prompt(z3, cos, sin_signed, sinks):
    b, t, _ = z3.shape
    nb = t // WINDOW
    qw = SW_HEADS * SW_HD
    kw = SW_KV_HEADS * SW_HD
    dw = 2 * kw
    qblk, kblk, vblk = EVEN_QA // qw, EVEN_KD // dw, EVEN_VD // dw
    cur = lambda j: (lambda bi, ni: (bi, ni, j))
    prev = lambda j: (lambda bi, ni: (bi, jnp.maximum(ni - 1, 0), j))
    return pl.pallas_call(
        _swa_prompt_kernel,
        grid=(b, nb),
        in_specs=[pl.BlockSpec(memory_space=pltpu.SMEM),
                  pl.BlockSpec((1, WINDOW, qw), cur(qblk)),
                  pl.BlockSpec((1, WINDOW, dw), cur(kblk)),
                  pl.BlockSpec((1, WINDOW, dw), prev(kblk)),
                  pl.BlockSpec((1, WINDOW, dw), cur(vblk)),
                  pl.BlockSpec((1, WINDOW, dw), prev(vblk)),
                  pl.BlockSpec((WINDOW, LANE), lambda bi, ni: (ni, 0)),
                  pl.BlockSpec((WINDOW, LANE), lambda bi, ni: (ni, 0)),
                  pl.BlockSpec((WINDOW, LANE), lambda bi, ni: (jnp.maximum(ni - 1, 0), 0)),
                  pl.BlockSpec((WINDOW, LANE), lambda bi, ni: (jnp.maximum(ni - 1, 0), 0))],
        out_specs=[pl.BlockSpec((1, WINDOW, qw), cur(0)),
                   pl.BlockSpec((1, WINDOW, kw), lambda bi, ni: (bi, 0, 0))],
        out_shape=[jax.ShapeDtypeStruct((b, t, qw), BF16),
                   jax.ShapeDtypeStruct((b, WINDOW, kw), F32)],
        compiler_params=_params(("parallel", "arbitrary")),
        name="swa_prompt",
    )(sinks, z3, z3, z3, z3, z3, cos, sin_signed, cos, sin_signed)


def _swa_sample_kernel(q_ref, k_ref, v_ref, kbuf_ref, vbuf_ref, cos_ref, sin_ref, sink_ref,
                       h_ref, ko_ref, vo_ref, *, T, Bb):
    qw = SW_HEADS * SW_HD
    kw = SW_KV_HEADS * SW_HD
    cos = cos_ref[...]
    sin = sin_ref[...]
    q = _rope(q_ref[...].reshape(Bb * T, qw), jnp.concatenate([cos] * (qw // LANE), axis=1),
              jnp.concatenate([sin] * (qw // LANE), axis=1), qw).reshape(Bb, T, qw)
    k_new = _rope(_undup(k_ref[...]).reshape(Bb * T, kw), cos, sin, kw).reshape(Bb, T, kw)
    kk = jnp.concatenate([kbuf_ref[...], k_new], axis=1)
    vv = jnp.concatenate([vbuf_ref[...], _undup(v_ref[...])], axis=1)
    ko_ref[...] = kk[:, T:, :]
    vo_ref[...] = vv[:, T:, :]
    rows = SW_GQ * T
    i = lax.broadcasted_iota(jnp.int32, (rows, WINDOW + T), 0) % T
    j = lax.broadcasted_iota(jnp.int32, (rows, WINDOW + T), 1)
    diff = WINDOW + i - j
    valid = (diff >= 0) & (diff < WINDOW)
    for kv in range(SW_KV_HEADS):
        qs = jnp.concatenate([q[:, :, (kv * SW_GQ + g) * SW_HD:(kv * SW_GQ + g + 1) * SW_HD]
                              for g in range(SW_GQ)], axis=1).astype(BF16)
        kh = kk[:, :, kv * SW_HD:(kv + 1) * SW_HD].astype(BF16)
        vh = vv[:, :, kv * SW_HD:(kv + 1) * SW_HD].astype(BF16)
        s = jnp.einsum('bqd,bkd->bqk', qs, kh, preferred_element_type=F32) * (SW_HD ** -0.5)
        s = jnp.where(valid, s, -jnp.inf)
        sink = sink_ref[kv * rows:(kv + 1) * rows, :]
        m = jnp.maximum(jnp.max(s, axis=-1, keepdims=True), sink)
        e = jnp.exp(s - m)
        p = e / (jnp.sum(e, axis=-1, keepdims=True) + jnp.exp(sink - m))
        o = jnp.einsum('bqk,bkd->bqd', p.astype(BF16), vh, preferred_element_type=F32)
        for g in range(SW_GQ):
            hh = kv * SW_GQ + g
            h_ref[:, :, hh * SW_HD:(hh + 1) * SW_HD] = o[:, g * T:(g + 1) * T, :].astype(h_ref.dtype)


def _swa_sample(z3, kbuf, vbuf, cos, sin_signed, sinks, Bb):
    b, t, _ = z3.shape
    qw = SW_HEADS * SW_HD
    kw = SW_KV_HEADS * SW_HD
    dw = 2 * kw
    qblk, kblk, vblk = EVEN_QA // qw, EVEN_KD // dw, EVEN_VD // dw
    col = lambda j: (lambda bi: (bi, 0, j))
    const2 = lambda bi: (0, 0)
    sink_col = jnp.repeat(sinks, t).reshape(SW_HEADS * t, 1)
    return pl.pallas_call(
        functools.partial(_swa_sample_kernel, T=t, Bb=Bb),
        grid=(b // Bb,),
        in_specs=[pl.BlockSpec((Bb, t, qw), col(qblk)),
                  pl.BlockSpec((Bb, t, dw), col(kblk)),
                  pl.BlockSpec((Bb, t, dw), col(vblk)),
                  pl.BlockSpec((Bb, WINDOW, kw), col(0)),
                  pl.BlockSpec((Bb, WINDOW, kw), col(0)),
                  pl.BlockSpec((Bb * t, LANE), const2),
                  pl.BlockSpec((Bb * t, LANE), const2),
                  pl.BlockSpec((SW_HEADS * t, 1), const2)],
        out_specs=[pl.BlockSpec((Bb, t, qw), col(0)),
                   pl.BlockSpec((Bb, WINDOW, kw), col(0)),
                   pl.BlockSpec((Bb, WINDOW, kw), col(0))],
        out_shape=[jax.ShapeDtypeStruct((b, t, qw), BF16),
                   jax.ShapeDtypeStruct((b, WINDOW, kw), F32),
                   jax.ShapeDtypeStruct((b, WINDOW, kw), F32)],
        compiler_params=_params(("parallel",)),
        name="swa_sample",
    )(z3, z3, z3, kbuf, vbuf, jnp.tile(cos, (Bb, 1)), jnp.tile(sin_signed, (Bb, 1)), sink_col)


def _gla_tables(L):
    nlev = int(math.log2(L))
    assert 2 ** nlev == L
    mstack = np.zeros((nlev * L, L), np.float32)
    masks = np.zeros((nlev + 1, L, L), np.float32)
    masks[0] = np.eye(L)
    for l in range(1, nlev + 1):
        n = 2 ** l
        for t in range(L):
            blk, pos = divmod(t, n)
            m = blk * n + n // 2 - 1
            if pos >= n // 2:
                mstack[(l - 1) * L + t, m + 1:t + 1] = 1.0
                masks[l, t, blk * n:blk * n + n // 2] = 1.0
            else:
                mstack[(l - 1) * L + t, t + 1:m + 1] = 1.0
    masks = np.tile(masks, (1, 1, GLA_HEADS))
    return nlev, jnp.asarray(mstack, BF16), jnp.asarray(masks)


def _gla_consts(L):
    kw = GLA_HEADS * GLA_DK
    vw = GLA_HEADS * GLA_DV
    hk = np.zeros((GLA_HEADS * L, kw), np.float32)
    for h in range(GLA_HEADS):
        hk[h * L:(h + 1) * L, h * GLA_DK:(h + 1) * GLA_DK] = 1.0
    hs = np.zeros((vw, kw), np.float32)
    for h in range(GLA_HEADS):
        hs[h * GLA_DV:(h + 1) * GLA_DV, h * GLA_DK:(h + 1) * GLA_DK] = 1.0
    return (jnp.asarray(hk, BF16), jnp.asarray(hs), jnp.asarray(np.eye(GLA_DV), BF16), jnp.asarray(np.eye(kw), BF16))


def _gla_kernel(q_ref, k_ref, v_ref, r_ref, glr_ref, wg_ref, bg_ref, gn_ref, tri_ref, mstack_ref,
                masks_ref, hk_ref, hs_ref, eyev_ref, eyek_ref, s0_ref,
                o_ref, so_ref, st_s, *, L, nlev, Bb, nsub):
    c = pl.program_id(1)
    H, DK, DV = GLA_HEADS, GLA_DK, GLA_DV

    @pl.when(c == 0)
    def _():
        for bi in range(Bb):
            for h in range(H):
                pieces = ([jnp.zeros((h * DK, DV), F32)] if h else []) + [s0_ref[bi, h]]
                pieces += [jnp.zeros(((H - 1 - h) * DK, DV), F32)] if h < H - 1 else []
                padded = jnp.concatenate(pieces, axis=0)
                st_s[bi, h * DV:(h + 1) * DV, :] = _dot_sel(eyev_ref[...], padded, NT)

    for sub in range(nsub):
        for bi in range(Bb):
            _gla_one(bi, sub * L, q_ref, k_ref, v_ref, r_ref, glr_ref, wg_ref, bg_ref, gn_ref, tri_ref, mstack_ref,
                     masks_ref, hk_ref, hs_ref, o_ref, st_s, L, nlev)

    @pl.when(c == pl.num_programs(1) - 1)
    def _():
        for bi in range(Bb):
            for h in range(H):
                blk = st_s[bi, h * DV:(h + 1) * DV, :]
                so_ref[bi, h] = _dot_sel(eyek_ref[h * DK:(h + 1) * DK, :], blk, NT)


def _gla_one(bi, r0, q_ref, k_ref, v_ref, r_ref, glr_ref, wg_ref, bg_ref, gn_ref, tri_ref, mstack_ref,
             masks_ref, hk_ref, hs_ref, o_ref, st_s, L, nlev):
    H, DK, DV = GLA_HEADS, GLA_DK, GLA_DV
    packed_rows = L % (2 * SUBLANE) == 0
    rw = pl.ds(r0, L)
    q = q_ref[bi, rw] * (DK ** -0.5)
    k = k_ref[bi, rw]
    kwid = H * DK
    g_hi, g_lo = _pieces(glr_ref[bi, rw], 2)
    pre2 = jnp.dot(g_hi.astype(BF16), wg_ref[...], preferred_element_type=F32)
    pre = (pre2[:, :kwid] + pre2[:, kwid:] + jnp.dot(g_lo.astype(BF16), wg_ref[:, :kwid], preferred_element_type=F32)
           + bg_ref[...])
    lg = jax.nn.log_sigmoid(pre) * (1.0 / GLA_TAU)
    G = _dot_sel(tri_ref[...], lg)
    E = jnp.exp(_dot_sel(mstack_ref[...], lg, pieces=2))
    hk = hk_ref[...]
    tok = lax.broadcasted_iota(jnp.int32, (L, kwid), 0)

    att = None
    for l in range(nlev + 1):
        if l == 0:
            ql, kl = q, k
        else:
            e_l = E[(l - 1) * L:l * L]
            upper = ((tok >> (l - 1)) & 1) == 1
            ql = jnp.where(upper, q * e_l, 0.0)
            kl = jnp.where(upper, 0.0, k * e_l)
        if packed_rows:
            kst = jnp.concatenate([kl.astype(BF16)] * H, axis=0) * hk
        else:
            kst = (jnp.concatenate([kl] * H, axis=0) * hk.astype(F32)).astype(BF16)
        part = lax.dot_general(ql.astype(BF16), kst, NT, preferred_element_type=F32)
        part = jnp.where(masks_ref[l] > 0.5, part, 0.0)
        att = part if att is None else att + part

    v = v_ref[bi, rw]
    vsrc = v.astype(BF16) if packed_rows else v
    zero = jnp.zeros((L, DV), vsrc.dtype)
    vbd = jnp.concatenate(
        [jnp.concatenate([zero] * h + [vsrc[:, h * DV:(h + 1) * DV]] + [zero] * (H - 1 - h), axis=1)
         for h in range(H)], axis=0).astype(BF16)
    stb = st_s[bi]
    o = (jnp.dot(att.astype(BF16), vbd, preferred_element_type=F32)
         + lax.dot_general((q * jnp.exp(G)).astype(BF16), stb.astype(BF16), NT, preferred_element_type=F32))
    for h in range(H):
        vs = slice(h * DV, (h + 1) * DV)
        oh = o[:, vs]
        oh = oh * lax.rsqrt(jnp.mean(oh * oh, axis=-1, keepdims=True) + EPS)
        oh = oh * gn_ref[:, vs] * jax.nn.silu(r_ref[bi, rw, vs])
        o_ref[bi, rw, vs] = oh.astype(o_ref.dtype)
    g_end = G[L - 1:L, :]
    k_end = (k * jnp.exp(g_end - G)).astype(BF16)
    upd = lax.dot_general(v.astype(BF16), k_end, TN, preferred_element_type=F32)
    st_s[bi] = stb * jnp.exp(g_end) + jnp.where(hs_ref[...] > 0.5, upd, 0.0)


def _gla(z3, wg, bg, gn, s0, L, Bb, nsub):
    b, t, _ = z3.shape
    nc = t // (L * nsub)
    nlev, mstack, masks = _gla_tables(L)
    hk, hs, eyev, eyek = _gla_consts(L)
    tri = jnp.asarray(np.tril(np.ones((L, L), np.float32)), BF16)
    qk_w = GLA_HEADS * GLA_DK
    v_w = GLA_HEADS * GLA_DV
    col = lambda j: (lambda bi, ci: (bi, ci, j))
    const2 = lambda bi, ci: (0, 0)
    const3 = lambda bi, ci: (0, 0, 0)
    return pl.pallas_call(
        functools.partial(_gla_kernel, L=L, nlev=nlev, Bb=Bb, nsub=nsub),
        grid=(b // Bb, nc),
        in_specs=[pl.BlockSpec((Bb, nsub * L, qk_w), col(0)),
                  pl.BlockSpec((Bb, nsub * L, qk_w), col(1)),
                  pl.BlockSpec((Bb, nsub * L, v_w), col(1)),
                  pl.BlockSpec((Bb, nsub * L, v_w), col(2)),
                  pl.BlockSpec((Bb, nsub * L, LANE), col(2048 // LANE)),
                  pl.BlockSpec((LANE, 2 * qk_w), const2),
                  pl.BlockSpec((1, qk_w), const2),
                  pl.BlockSpec((1, v_w), const2),
                  pl.BlockSpec((L, L), const2),
                  pl.BlockSpec(mstack.shape, const2),
                  pl.BlockSpec(masks.shape, const3),
                  pl.BlockSpec(hk.shape, const2),
                  pl.BlockSpec(hs.shape, const2),
                  pl.BlockSpec(eyev.shape, const2),
                  pl.BlockSpec(eyek.shape, const2),
                  pl.BlockSpec((Bb, GLA_HEADS, GLA_DK, GLA_DV), lambda bi, ci: (bi, 0, 0, 0))],
        out_specs=[pl.BlockSpec((Bb, nsub * L, v_w), col(0)),
                   pl.BlockSpec((Bb, GLA_HEADS, GLA_DK, GLA_DV), lambda bi, ci: (bi, 0, 0, 0))],
        out_shape=[jax.ShapeDtypeStruct((b, t, v_w), BF16),
                   jax.ShapeDtypeStruct((b, GLA_HEADS, GLA_DK, GLA_DV), F32)],
        scratch_shapes=[pltpu.VMEM((Bb, v_w, qk_w), F32)],
        compiler_params=_params(("parallel", "arbitrary")),
        name="gla",
    )(z3, z3, z3, z3, z3, wg, bg, gn, tri, mstack, masks, hk, hs, eyev, eyek, s0)


def _s5_kernel(u_ref, bbre_ref, bbim_ref, cre_ref, cim_ref, d_ref, wglu_ref, bglu_ref, pre_ref, pim_ref,
               x0re_ref, x0im_ref,
               o_ref, xore_ref, xoim_ref,
               xr_s, xi_s, cr_s, ci_s, *, rows, sequential):
    c = pl.program_id(1)
    lead = rows // SUBLANE if sequential else SUBLANE
    inner = rows // lead
    cw = S5_LANES // 4 if inner == SUBLANE else LANE
    u = jnp.swapaxes(u_ref[0].reshape(inner, lead, S5_WIDTH), 0, 1).reshape(rows, S5_WIDTH)
    ub = u.astype(BF16)
    hw, hl = S5_WIDTH // 2, S5_LANES // 2
    for half in range(2):
        uh = ub[:, half * hw:(half + 1) * hw]
        ls = slice(half * hl, (half + 1) * hl)
        xr_s[:, :, ls] = jnp.dot(uh, bbre_ref[half * hw:(half + 1) * hw, ls],
                                 preferred_element_type=F32).reshape(lead, inner, hl)
        xi_s[:, :, ls] = jnp.dot(uh, bbim_ref[half * hw:(half + 1) * hw, ls],
                                 preferred_element_type=F32).reshape(lead, inner, hl)

    for c0 in range(0, S5_LANES, cw):
        ar = jnp.broadcast_to(pre_ref[0, :, c0:c0 + cw], (inner, cw))
        ai = jnp.broadcast_to(pim_ref[0, :, c0:c0 + cw], (inner, cw))

        def body(j, carry, c0=c0, ar=ar, ai=ai):
            xr, xi = carry
            nr = ar * xr - ai * xi + xr_s[j, :, c0:c0 + cw]
            ni = ar * xi + ai * xr + xi_s[j, :, c0:c0 + cw]
            xr_s[j, :, c0:c0 + cw] = nr
            xi_s[j, :, c0:c0 + cw] = ni
            return nr, ni

        if sequential:
            init = (jnp.zeros((inner, cw), F32), jnp.zeros((inner, cw), F32))
        else:
            init = (x0re_ref[0, :, c0:c0 + cw], x0im_ref[0, :, c0:c0 + cw])
        lax.fori_loop(0, lead, body, init, unroll=4)

    if sequential:
        @pl.when(c == 0)
        def _():
            cr_s[...] = x0re_ref[0]
            ci_s[...] = x0im_ref[0]

        end_r = xr_s[lead - 1]
        end_i = xi_s[lead - 1]
        anr = pre_ref[lead - 1]
        ani = pim_ref[lead - 1]
        ent_r, ent_i = [cr_s[...]], [ci_s[...]]
        for s in range(SUBLANE):
            pr, pi = ent_r[-1], ent_i[-1]
            ent_r.append(end_r[s:s + 1] + anr * pr - ani * pi)
            ent_i.append(end_i[s:s + 1] + anr * pi + ani * pr)
        cr_s[...] = ent_r[SUBLANE]
        ci_s[...] = ent_i[SUBLANE]
        er = jnp.concatenate(ent_r[:SUBLANE], axis=0)[None]
        ei = jnp.concatenate(ent_i[:SUBLANE], axis=0)[None]
        pr3, pi3 = pre_ref[...], pim_ref[...]
        xr = xr_s[...] + pr3 * er - pi3 * ei
        xi = xi_s[...] + pr3 * ei + pi3 * er

        @pl.when(c == pl.num_programs(1) - 1)
        def _():
            xore_ref[0] = cr_s[...]
            xoim_ref[0] = ci_s[...]
    else:
        xr = xr_s[...]
        xi = xi_s[...]
        xore_ref[0] = xr[lead - 1]
        xoim_ref[0] = xi[lead - 1]

    xrb = xr.reshape(rows, S5_LANES).astype(BF16)
    xib = xi.reshape(rows, S5_LANES).astype(BF16)
    y = jnp.concatenate(
        [jnp.dot(xrb[:, h * hl:(h + 1) * hl], cre_ref[h * hl:(h + 1) * hl, h * hw:(h + 1) * hw],
                 preferred_element_type=F32)
         - jnp.dot(xib[:, h * hl:(h + 1) * hl], cim_ref[h * hl:(h + 1) * hl, h * hw:(h + 1) * hw],
                   preferred_element_type=F32) for h in range(2)], axis=1) + d_ref[...] * u
    zz = jax.nn.gelu(y)
    gate = jnp.dot(zz.astype(BF16), wglu_ref[...], preferred_element_type=F32) + bglu_ref[...]
    out = zz * jax.nn.sigmoid(gate)
    o_ref[0] = jnp.swapaxes(out.reshape(lead, inner, S5_WIDTH), 0, 1).reshape(rows, S5_WIDTH).astype(o_ref.dtype)


def _s5(z3, tabs, x0re, x0im, rows, sequential):
    b, t, _ = z3.shape
    nblk = t // rows
    groups = rows // SUBLANE
    ublk = 1536 // S5_WIDTH
    const2 = lambda bi, ci: (0, 0)
    const3 = lambda bi, ci: (0, 0, 0)
    lead = groups if sequential else SUBLANE
    kk = jnp.arange(1, lead + 1, dtype=F32).reshape(lead, 1, 1)
    mag = jnp.exp(kk * tabs['la_re'][None])
    p_re = mag * jnp.cos(kk * tabs['la_im'][None])
    p_im = mag * jnp.sin(kk * tabs['la_im'][None])
    if sequential:
        st_spec = pl.BlockSpec((1, 1, S5_LANES), lambda bi, ci: (bi, 0, 0))
        st_shape = jax.ShapeDtypeStruct((b, 1, S5_LANES), F32)
    else:
        st_spec = pl.BlockSpec((1, groups, S5_LANES), lambda bi, ci: (bi, ci, 0))
        st_shape = jax.ShapeDtypeStruct(x0re.shape, F32)
    return pl.pallas_call(
        functools.partial(_s5_kernel, rows=rows, sequential=sequential),
        grid=(b, nblk),
        in_specs=[pl.BlockSpec((1, rows, S5_WIDTH), lambda bi, ci: (bi, ci, ublk)),
                  pl.BlockSpec((S5_WIDTH, S5_LANES), const2),
                  pl.BlockSpec((S5_WIDTH, S5_LANES), const2),
                  pl.BlockSpec((S5_LANES, S5_WIDTH), const2),
                  pl.BlockSpec((S5_LANES, S5_WIDTH), const2),
                  pl.BlockSpec((1, S5_WIDTH), const2),
                  pl.BlockSpec((S5_WIDTH, S5_WIDTH), const2),
                  pl.BlockSpec((1, S5_WIDTH), const2),
                  pl.BlockSpec((lead, 1, S5_LANES), const3),
                  pl.BlockSpec((lead, 1, S5_LANES), const3),
                  st_spec, st_spec],
        out_specs=[pl.BlockSpec((1, rows, S5_WIDTH), lambda bi, ci: (bi, ci, 0)),
                   st_spec, st_spec],
        out_shape=[jax.ShapeDtypeStruct((b, t, S5_WIDTH), BF16), st_shape, st_shape],
        scratch_shapes=[pltpu.VMEM((lead, rows // lead, S5_LANES), F32),
                        pltpu.VMEM((lead, rows // lead, S5_LANES), F32),
                        pltpu.VMEM((1, S5_LANES), F32),
                        pltpu.VMEM((1, S5_LANES), F32)],
        compiler_params=_params(("parallel", "arbitrary")),
        name="s5",
    )(z3, tabs['bbre'], tabs['bbim'], tabs['cre'], tabs['cim'], tabs['d'], tabs['wglu'], tabs['bglu'],
      p_re, p_im, x0re, x0im)


def _s5_tables(a_re, a_im, log_dt, b_re, b_im, c_re, c_im, d, w_glu, b_glu):
    lam_re = a_re.astype(F32)
    lam_im = a_im.astype(F32)
    dt = jnp.exp(log_dt.astype(F32))[:, None]
    mag = jnp.exp(lam_re * dt)
    ang = lam_im * dt
    ab_re = mag * jnp.cos(ang)
    ab_im = mag * jnp.sin(ang)
    den = lam_re * lam_re + lam_im * lam_im
    co_re = ((ab_re - 1.0) * lam_re + ab_im * lam_im) / den
    co_im = (ab_im * lam_re - (ab_re - 1.0) * lam_im) / den
    b_re = b_re.astype(F32)
    b_im = b_im.astype(F32)
    bb_re = co_re[..., None] * b_re - co_im[..., None] * b_im
    bb_im = co_re[..., None] * b_im + co_im[..., None] * b_re
    eye = jnp.eye(S5_GROUPS, dtype=F32)
    blockdiag_in = lambda bb: jnp.einsum('gph,gk->ghkp', bb, eye).reshape(S5_WIDTH, S5_LANES)
    blockdiag_out = lambda cc: jnp.einsum('ghp,gk->gpkh', cc.astype(F32), eye).reshape(S5_LANES, S5_WIDTH)
    return {
        'bbre': blockdiag_in(bb_re).astype(BF16), 'bbim': blockdiag_in(bb_im).astype(BF16),
        'cre': blockdiag_out(c_re).astype(BF16), 'cim': blockdiag_out(c_im).astype(BF16),
        'd': d.astype(F32).reshape(1, S5_WIDTH),
        'wglu': w_glu.astype(BF16), 'bglu': b_glu.astype(F32).reshape(1, S5_WIDTH),
        'la_re': (lam_re * dt).reshape(1, S5_LANES), 'la_im': ang.reshape(1, S5_LANES),
    }


def _rmsnorm_val(x, g):
    return x * lax.rsqrt(jnp.mean(x * x, axis=-1, keepdims=True) + EPS) * g


def _ffn_kernel(r_ref, a1_ref, a2_ref, w1_ref, w2_ref, g_ref, wup_ref, cw_ref, cb_ref, wd_ref, st_ref, gf_ref,
                o_ref, so_ref, halo_s, *, tm, tf, long_seq, final_norm):
    i = pl.program_id(1)
    n = tm // SUBLANE
    d = r_ref.shape[-1]
    x = (r_ref[0] + jnp.dot(a1_ref[0], w1_ref[...], preferred_element_type=F32)
         + jnp.dot(a2_ref[0], w2_ref[...], preferred_element_type=F32))
    hn = _rmsnorm_val(x, g_ref[...])
    if long_seq:
        lead = n

        @pl.when(i == 0)
        def _():
            halo_s[...] = st_ref[0]
    else:
        lead = SUBLANE
    hnp = jnp.swapaxes(hn.reshape(tm // lead, lead, d), 0, 1).reshape(tm, d).astype(BF16)
    inner = tm // lead
    sub = lax.broadcasted_iota(jnp.int32, (SUBLANE, tf), 0)

    acc = jnp.zeros((tm, d), F32)
    for c in range(D_FF // tf):
        conv = []
        for half in range(2):
            c0 = half * D_FF + c * tf
            u = jnp.dot(hnp, wup_ref[:, c0:c0 + tf], preferred_element_type=F32).reshape(lead, inner, tf)
            if long_seq:
                um1 = jnp.where(sub == 0, halo_s[1:2, c0:c0 + tf], pltpu.roll(u[lead - 1], 1, axis=0))
                um2 = jnp.where(sub == 0, halo_s[0:1, c0:c0 + tf], pltpu.roll(u[lead - 2], 1, axis=0))
                halo_s[0:1, c0:c0 + tf] = u[lead - 2][SUBLANE - 1:SUBLANE]
                halo_s[1:2, c0:c0 + tf] = u[lead - 1][SUBLANE - 1:SUBLANE]
            else:
                um2 = st_ref[0, :, 0, c0:c0 + tf]
                um1 = st_ref[0, :, 1, c0:c0 + tf]
                so_ref[:, 0, c0:c0 + tf] = u[lead - 2]
                so_ref[:, 1, c0:c0 + tf] = u[lead - 1]
            ext = jnp.concatenate([um2[None], um1[None], u], axis=0)
            cv = cb_ref[:, c0:c0 + tf]
            for j in range(CONV_W):
                cv = cv + ext[j:j + lead] * cw_ref[j:j + 1, c0:c0 + tf]
            conv.append(cv)
        act = (jax.nn.silu(conv[1]) * conv[0]).reshape(tm, tf).astype(BF16)
        acc = acc + jnp.dot(act, wd_ref[c * tf:(c + 1) * tf, :], preferred_element_type=F32)

    out = x + jnp.swapaxes(acc.reshape(lead, inner, d), 0, 1).reshape(tm, d)
    if final_norm:
        out = _rmsnorm_val(out, gf_ref[...])
    o_ref[0] = out

    if long_seq:
        @pl.when(i == pl.num_programs(1) - 1)
        def _():
            so_ref[0] = halo_s[...]


def _ffn(res3, a1, a2, w1, w2, g, wup, cw, cb, wd, state, layer, gfinal, tm, tf, long_seq, final_norm):
    b, t, d = res3.shape
    nt = t // tm
    resident = dict(pipeline_mode=pl.Buffered(1))
    const2 = lambda bi, i: (0, 0)
    row_blk = lambda w: pl.BlockSpec((1, tm, w), lambda bi, i: (bi, i, 0))
    if long_seq:
        st_in = st_out = pl.BlockSpec((1, CONV_W - 1, 2 * D_FF), lambda bi, i: (bi, 0, 0))
        st_shape = state.shape
    else:
        nseq = tm // SUBLANE
        st_in = pl.BlockSpec((1, nseq, CONV_W - 1, 2 * D_FF), lambda bi, i: (layer, i, 0, 0))
        st_out = pl.BlockSpec((nseq, CONV_W - 1, 2 * D_FF), lambda bi, i: (i, 0, 0))
        st_shape = state.shape[1:]
    return pl.pallas_call(
        functools.partial(_ffn_kernel, tm=tm, tf=tf, long_seq=long_seq, final_norm=final_norm),
        grid=(b, nt),
        in_specs=[row_blk(d), row_blk(a1.shape[-1]), row_blk(a2.shape[-1]),
                  pl.BlockSpec(w1.shape, const2, **resident),
                  pl.BlockSpec(w2.shape, const2, **resident),
                  pl.BlockSpec((1, d), const2),
                  pl.BlockSpec((d, 2 * D_FF), const2, **resident),
                  pl.BlockSpec((CONV_W, 2 * D_FF), const2),
                  pl.BlockSpec((1, 2 * D_FF), const2),
                  pl.BlockSpec((D_FF, d), const2, **resident),
                  st_in,
                  pl.BlockSpec((1, d), const2)],
        out_specs=[row_blk(d), st_out],
        out_shape=[jax.ShapeDtypeStruct((b, t, d), F32), jax.ShapeDtypeStruct(st_shape, F32)],
        scratch_shapes=[pltpu.VMEM((CONV_W - 1, 2 * D_FF), F32)],
        compiler_params=_params(("parallel", "arbitrary")),
        name="ffn",
    )(res3, a1, a2, w1, w2, g.reshape(1, d), wup, cw, cb, wd, state, gfinal)


def _rope_tables(pos):
    half = SW_HD // 2
    inv = ROPE_THETA ** (-jnp.arange(half, dtype=F32) / half)
    ang = pos.astype(F32)[:, None] * inv[None, :]
    cos = jnp.cos(ang)
    sin = jnp.sin(ang)
    reps = LANE // SW_HD
    cos_t = jnp.tile(jnp.concatenate([cos, cos], axis=1), (1, reps))
    sin_t = jnp.tile(jnp.concatenate([-sin, sin], axis=1), (1, reps))
    return cos_t, sin_t


def _prep_weights(W):
    P = {}
    we = W['w_in_even']
    parts = jnp.split(we, [int(c) for c in np.cumsum((512, 512, 512, 512, 8, 512, 128))], axis=1)
    q_m, k_m, v_m, o_m, g_m, q_a, k_a, v_a = parts
    dup = lambda w: jnp.concatenate([w[:, :SW_HD], w[:, :SW_HD], w[:, SW_HD:], w[:, SW_HD:]], axis=1)
    pad = jnp.zeros((D_MODEL, LANE - 2 * ML_HEADS), we.dtype)
    P['w_in_even'] = jnp.concatenate([q_m, k_m, v_m, o_m, q_a, dup(k_a), dup(v_a), g_m, pad], axis=1).astype(BF16)
    assert P['w_in_even'].shape[1] == EVEN_PAD
    P['b_gates'] = jnp.pad(W['b_mlstm_gates'].astype(F32), (0, LANE - 2 * ML_HEADS)).reshape(1, LANE)
    wo = W['w_in_odd']
    q, k, v, r, glr, u = jnp.split(wo, [int(c) for c in np.cumsum((256, 256, 512, 512, 16))], axis=1)
    pad = jnp.zeros((D_MODEL, ODD_PAD - 2048 - GLA_RANK), wo.dtype)
    P['w_in_odd'] = jnp.concatenate([q, k, v, r, u, glr, pad], axis=1).astype(BF16)
    wg = jnp.pad(W['w_gla_gate_up'].astype(F32), ((0, LANE - GLA_RANK), (0, 0)))
    wg_hi = wg.astype(BF16)
    P['w_gate_up'] = jnp.concatenate([wg_hi, (wg - wg_hi.astype(F32)).astype(BF16)], axis=1)
    P['b_gate'] = W['b_gla_gate'].astype(F32).reshape(1, -1)
    P['g_gla'] = W['g_gla_norm'].astype(F32).reshape(1, -1)
    hm = ML_HEADS * ML_DV
    P['w_out_even'] = (W['w_out_even'][:hm].astype(BF16), W['w_out_even'][hm:].astype(BF16))
    hg = GLA_HEADS * GLA_DV
    P['w_out_odd'] = (W['w_out_odd'][:hg].astype(BF16), W['w_out_odd'][hg:].astype(BF16))
    P['w_ffn_up'] = W['w_ffn_up'].astype(BF16)
    P['w_ffn_down'] = W['w_ffn_down'].astype(BF16)
    P['s5'] = _s5_tables(W['s5_a_re'], W['s5_a_im'], W['s5_log_dt'], W['s5_b_re'], W['s5_b_im'],
                         W['s5_c_re'], W['s5_c_im'], W['s5_d'], W['w_s5_glu'], W['b_s5_glu'])
    return P


def _trunk(x, pos, st, W, P, is_prompt):
    b, t, d = x.shape
    n = b * t
    L = CHUNK if t % CHUNK == 0 else t
    nsub = 2 if t % (2 * L) == 0 else 1
    tm = 512 if n % 512 == 0 else n
    cos, sin_signed = _rope_tables(pos)
    gfinal = W['norm_final'].astype(F32).reshape(1, d)
    new = {}
    conv_out = []
    h = x.reshape(n, d)

    z = _norm_matmul(h, W['norm_mix'][0], P['w_in_even'], tm)
    z3 = z.reshape(b, t, EVEN_PAD)
    h_m, new['ml_C'], new['ml_n'], m_new = _mlstm(z3, P['b_gates'], st['ml_C'], st['ml_n'], st['ml_m'], L,
                                                  Bb=2 if is_prompt else 8, nsub=nsub)
    new['ml_m'] = m_new.reshape(b, ML_HEADS)
    if is_prompt:
        h_a, kb = _swa_prompt(z3, cos, sin_signed, W['sw_sinks'].astype(F32))
        vb = jnp.concatenate([z3[:, t - WINDOW:, EVEN_VD:EVEN_VD + SW_HD],
                              z3[:, t - WINDOW:, EVEN_VD + LANE:EVEN_VD + LANE + SW_HD]], axis=-1)
    else:
        h_a, kb, vb = _swa_sample(z3, st['kbuf'].reshape(b, WINDOW, -1), st['vbuf'].reshape(b, WINDOW, -1),
                                  cos, sin_signed, W['sw_sinks'].astype(F32), Bb=8)
    new['kbuf'] = kb.reshape(b, WINDOW, SW_KV_HEADS, SW_HD)
    new['vbuf'] = vb.reshape(b, WINDOW, SW_KV_HEADS, SW_HD)
    h, cb = _ffn_layer(h, h_m, h_a, P['w_out_even'], 0, st, W, P, b, t, gfinal, is_prompt, final_norm=False)
    conv_out.append(cb)

    z = _norm_matmul(h, W['norm_mix'][1], P['w_in_odd'], tm)
    z3 = z.reshape(b, t, ODD_PAD)
    o_g, new['gla'] = _gla(z3, P['w_gate_up'], P['b_gate'], P['g_gla'], st['gla'], L, Bb=2 if is_prompt else 4,
                           nsub=nsub)
    x0re = st['s5_re'].astype(F32).reshape(b, 1, S5_LANES)
    x0im = st['s5_im'].astype(F32).reshape(b, 1, S5_LANES)
    if is_prompt:
        o_s, xre, xim = _s5(z3, P['s5'], x0re, x0im, rows=256 if t % 256 == 0 else t, sequential=True)
    else:
        o_s, xre, xim = _s5(z3.reshape(1, n, ODD_PAD), P['s5'], x0re.reshape(1, b, S5_LANES),
                            x0im.reshape(1, b, S5_LANES), rows=256 if n % 256 == 0 else n, sequential=False)
    new['s5_re'] = xre.reshape(b, S5_GROUPS, S5_STATE)
    new['s5_im'] = xim.reshape(b, S5_GROUPS, S5_STATE)
    h, cb = _ffn_layer(h, o_g, o_s, P['w_out_odd'], 1, st, W, P, b, t, gfinal, is_prompt, final_norm=True)
    conv_out.append(cb)
    new['conv'] = jnp.stack(conv_out)
    return h.reshape(b, t, d), new


def _ffn_layer(h, a1, a2, w_out, layer, st, W, P, b, t, gfinal, is_prompt, final_norm):
    n, d = h.shape
    cw = W['ffn_conv_w'][layer].astype(F32)
    cb = W['ffn_conv_b'][layer].astype(F32).reshape(1, 2 * D_FF)
    rows = t if is_prompt else n
    tm = next((c for c in ((512, 256) if is_prompt else (128,)) if rows % c == 0), rows)
    args = (*w_out, W['norm_ffn'][layer], P['w_ffn_up'][layer], cw, cb, P['w_ffn_down'][layer])
    if is_prompt:
        shape3 = lambda a: a.reshape(b, t, -1)
        out, new_state = _ffn(shape3(h), shape3(a1), shape3(a2), *args, st['conv'][layer], layer, gfinal, tm,
                              FFN_TF, True, final_norm)
    else:
        assert t == SUBLANE
        shape3 = lambda a: a.reshape(1, n, -1)
        out, new_state = _ffn(shape3(h), shape3(a1), shape3(a2), *args, st['conv'], layer, gfinal, tm, FFN_TF, False,
                              final_norm)
    return out.reshape(n, d), new_state


def kernel(x_prompt, x_sample, state_mlstm_C, state_mlstm_n, state_mlstm_m, cache_swa_k, cache_swa_v,
           state_gla, state_s5_re, state_s5_im, state_ffn_conv, norm_mix, norm_ffn, norm_final,
           w_in_even, b_mlstm_gates, sw_sinks, w_out_even, w_in_odd, w_gla_gate_up, b_gla_gate, g_gla_norm,
           s5_a_re, s5_a_im, s5_log_dt, s5_b_re, s5_b_im, s5_c_re, s5_c_im, s5_d, w_s5_glu, b_s5_glu,
           w_out_odd, w_ffn_up, ffn_conv_w, ffn_conv_b, w_ffn_down):
    W = {'norm_mix': norm_mix.astype(F32), 'norm_ffn': norm_ffn.astype(F32), 'norm_final': norm_final,
         'w_in_even': w_in_even, 'b_mlstm_gates': b_mlstm_gates, 'sw_sinks': sw_sinks, 'w_out_even': w_out_even,
         'w_in_odd': w_in_odd, 'w_gla_gate_up': w_gla_gate_up, 'b_gla_gate': b_gla_gate, 'g_gla_norm': g_gla_norm,
         's5_a_re': s5_a_re, 's5_a_im': s5_a_im, 's5_log_dt': s5_log_dt, 's5_b_re': s5_b_re, 's5_b_im': s5_b_im,
         's5_c_re': s5_c_re, 's5_c_im': s5_c_im, 's5_d': s5_d, 'w_s5_glu': w_s5_glu, 'b_s5_glu': b_s5_glu,
         'w_out_odd': w_out_odd, 'w_ffn_up': w_ffn_up, 'ffn_conv_w': ffn_conv_w, 'ffn_conv_b': ffn_conv_b,
         'w_ffn_down': w_ffn_down}
    P = _prep_weights(W)
    bp, tp = x_prompt.shape[:2]
    st_prompt = {'ml_C': jnp.zeros((bp, ML_HEADS, ML_DK, ML_DV), F32),
                 'ml_n': jnp.zeros((bp, ML_HEADS, ML_DK), F32),
                 'ml_m': jnp.zeros((bp, ML_HEADS), F32),
                 'gla': jnp.zeros((bp, GLA_HEADS, GLA_DK, GLA_DV), F32),
                 's5_re': jnp.zeros((bp, S5_GROUPS, S5_STATE), F32),
                 's5_im': jnp.zeros((bp, S5_GROUPS, S5_STATE), F32),
                 'conv': jnp.zeros((2, bp, CONV_W - 1, 2 * D_FF), F32)}
    st_sample = {'ml_C': state_mlstm_C.astype(F32), 'ml_n': state_mlstm_n.astype(F32),
                 'ml_m': state_mlstm_m.astype(F32), 'kbuf': cache_swa_k.astype(F32),
                 'vbuf': cache_swa_v.astype(F32), 'gla': state_gla.astype(F32),
                 's5_re': state_s5_re, 's5_im': state_s5_im, 'conv': state_ffn_conv}
    past_len = 16384
    y_p, np_ = _trunk(x_prompt.astype(F32), jnp.arange(tp), st_prompt, W, P, True)
    y_s, ns_ = _trunk(x_sample.astype(F32), past_len + jnp.arange(x_sample.shape[1]), st_sample, W, P, False)
    return (y_p, y_s,
            np_['ml_C'], ns_['ml_C'], np_['ml_n'], ns_['ml_n'], np_['ml_m'], ns_['ml_m'],
            np_['kbuf'], ns_['kbuf'], np_['vbuf'], ns_['vbuf'], np_['gla'], ns_['gla'],
            np_['s5_re'], ns_['s5_re'], np_['s5_im'], ns_['s5_im'], np_['conv'], ns_['conv'])
```

```python
import functools
import math

import numpy as np
import jax
import jax.numpy as jnp
from jax import lax
from jax.experimental import pallas as pl
from jax.experimental.pallas import tpu as pltpu

F32 = jnp.float32
BF16 = jnp.bfloat16
HI = lax.Precision.HIGHEST
NT = (((1,), (1,)), ((), ()))
TN = (((0,), (0,)), ((), ()))

D_MODEL = 1024
ML_HEADS, ML_DK, ML_DV = 4, 128, 128
SW_HEADS, SW_KV_HEADS, SW_HD, WINDOW = 8, 2, 64, 128
SW_GQ = SW_HEADS // SW_KV_HEADS
ROPE_THETA = 10000.0
GLA_HEADS, GLA_DK, GLA_DV, GLA_RANK, GLA_TAU = 4, 64, 128, 16, 16.0
S5_WIDTH, S5_GROUP, S5_STATE = 512, 16, 64
S5_GROUPS = S5_WIDTH // S5_GROUP
S5_LANES = S5_GROUPS * S5_STATE
D_FF = 2816
FFN_TF = 1408
CONV_W = 3
CHUNK = 64
EPS = 1e-6

LANE = 128
SUBLANE = 8
VMEM_LIMIT = 56 * 1024 * 1024

EVEN_QA, EVEN_KD, EVEN_VD, EVEN_PAD = 2048, 2560, 2816, 3200
ODD_PAD = 2176


def _params(sem):
    return pltpu.CompilerParams(dimension_semantics=sem, vmem_limit_bytes=VMEM_LIMIT)


def _norm_matmul_kernel(x_ref, g_ref, w_ref, o_ref):
    x = x_ref[...]
    ms = jnp.mean(x * x, axis=-1, keepdims=True)
    hn = (x * lax.rsqrt(ms + EPS) * g_ref[...]).astype(BF16)
    o_ref[...] = jnp.dot(hn, w_ref[...], preferred_element_type=F32)


def _norm_matmul(x, g, w, tm):
    n, d = x.shape
    c = w.shape[1]
    return pl.pallas_call(
        _norm_matmul_kernel,
        grid=(n // tm,),
        in_specs=[pl.BlockSpec((tm, d), lambda i: (i, 0)),
                  pl.BlockSpec((1, d), lambda i: (0, 0)),
                  pl.BlockSpec((d, c), lambda i: (0, 0))],
        out_specs=pl.BlockSpec((tm, c), lambda i: (i, 0)),
        out_shape=jax.ShapeDtypeStruct((n, c), F32),
        compiler_params=_params(("parallel",)),
        name="norm_matmul",
    )(x, g.reshape(1, d), w)


def _pieces(x, n):
    out, r = [], x
    for _ in range(n):
        p = r.astype(BF16).astype(F32)
        out.append(p)
        r = r - p
    return out


def _dot_sel(m_bf16, x, dims=None, pieces=3):
    ps = _pieces(x, pieces)
    axis = 1 if dims is None else 0
    w = x.shape[axis]
    if w % LANE:
        f = (lambda p: jnp.dot(m_bf16, p, preferred_element_type=F32)) if dims is None else (
            lambda p: lax.dot_general(m_bf16, p, dims, preferred_element_type=F32))
        return sum(f(p.astype(BF16)) for p in ps)
    stacked = jnp.concatenate(ps, axis=axis).astype(BF16)
    if dims is None:
        r = jnp.dot(m_bf16, stacked, preferred_element_type=F32)
    else:
        r = lax.dot_general(m_bf16, stacked, dims, preferred_element_type=F32)
    return sum(r[:, i * w:(i + 1) * w] for i in range(pieces))


def _sel_right(x, e_bf16, pieces=3):
    rows = x.shape[0]
    r = jnp.dot(jnp.concatenate(_pieces(x, pieces), axis=0).astype(BF16), e_bf16, preferred_element_type=F32)
    return sum(r[i * rows:(i + 1) * rows] for i in range(pieces))


def _mlstm_kernel(q_ref, k_ref, v_ref, og_ref, gz_ref, bias_ref, tri_ref, ones_ref, expc_ref, expv_ref, eyet_ref,
                  seg_ref, segv_ref, hmk_ref, c0_ref, n0_ref, m0_ref,
                  h_ref, co_ref, no_ref, mo_ref,
                  c_s, n_s, m_s, *, L, Bb, nsub):
    c = pl.program_id(1)

    @pl.when(c == 0)
    def _():
        c_s[...] = c0_ref[...]
        n_s[...] = n0_ref[...]
        m_s[...] = m0_ref[...]

    for sub in range(nsub):
        for bi in range(Bb):
            _mlstm_one(bi, sub * L, q_ref, k_ref, v_ref, og_ref, gz_ref, bias_ref, tri_ref, ones_ref, expc_ref,
                       expv_ref, eyet_ref, seg_ref, segv_ref, hmk_ref, h_ref, c_s, n_s, m_s, L)

    @pl.when(c == pl.num_programs(1) - 1)
    def _():
        co_ref[...] = c_s[...]
        no_ref[...] = n_s[...]
        mo_ref[...] = m_s[...]


def _mlstm_one(bi, r0, q_ref, k_ref, v_ref, og_ref, gz_ref, bias_ref, tri_ref, ones_ref, expc_ref, expv_ref,
               eyet_ref, seg_ref, segv_ref, hmk_ref, h_ref, c_s, n_s, m_s, L):
    H, DK = ML_HEADS, ML_DK
    HL = H * L
    lane = lax.broadcasted_iota(jnp.int32, (L, LANE), 1)
    rowi = lax.broadcasted_iota(jnp.int32, (L, LANE), 0)
    keep = lane < H
    heads_only = lambda x: jnp.where(keep[:x.shape[0]], x, 0.0)

    rw = pl.ds(r0, L)
    gates = gz_ref[bi, rw] + bias_ref[...]
    lf = jax.nn.log_sigmoid(gates)
    b = pltpu.roll(_dot_sel(tri_ref[...], lf), LANE - H, axis=1)
    vv = gates - b
    cm = vv
    sh = 1
    while sh < L:
        cm = jnp.maximum(cm, jnp.where(rowi >= sh, pltpu.roll(cm, sh, axis=0), -jnp.inf))
        sh *= 2
    m_prev = m_s[bi]
    mt = b + jnp.maximum(m_prev, cm)
    a_inter = jnp.exp(b + m_prev - mt)

    expv = expv_ref[...]
    m_new = mt[L - 1:L, :]
    b_end = b[L - 1:L, :]
    w_end = heads_only(jnp.exp(b_end - b + gates - m_new))
    decay = heads_only(jnp.exp(b_end + m_prev - m_new))
    spread_k = _sel_right(jnp.concatenate([heads_only(b - mt), heads_only(vv)], axis=0), expc_ref[...])
    uc = spread_k[:L]
    vr = _dot_sel(ones_ref[...], spread_k[L:] * eyet_ref[...])
    spread_v = _sel_right(jnp.concatenate([heads_only(a_inter), w_end, jnp.broadcast_to(decay, (SUBLANE, LANE))],
                                          axis=0), expv, pieces=2)
    ac, wc, dc = spread_v[:L], spread_v[L:2 * L], spread_v[2 * L:2 * L + 1]
    ti = lax.broadcasted_iota(jnp.int32, (L, HL), 0)
    si = lax.broadcasted_iota(jnp.int32, (L, HL), 1) % L
    w = jnp.exp(jnp.where(si <= ti, uc + vr, -jnp.inf))

    q = q_ref[bi, rw]
    ks = k_ref[bi, rw] * (DK ** -0.5)
    v = v_ref[bi, rw]
    qb = q.astype(BF16)
    vb = v.astype(BF16)
    if L % (2 * SUBLANE) == 0:
        kst = jnp.concatenate([ks.astype(BF16)] * H, axis=0) * hmk_ref[...]
        vsrc = vb
    else:
        kst = (jnp.concatenate([ks] * H, axis=0) * hmk_ref[...].astype(F32)).astype(BF16)
        vsrc = v
    zero = jnp.zeros((L, DK), vsrc.dtype)
    vbd = jnp.concatenate(
        [jnp.concatenate([zero] * h + [vsrc[:, h * DK:(h + 1) * DK]] + [zero] * (H - 1 - h), axis=1)
         for h in range(H)], axis=0).astype(BF16)
    s = lax.dot_general(qb, kst, NT, preferred_element_type=F32) * w
    qc = jnp.concatenate([jnp.dot(qb[:, h * DK:(h + 1) * DK], c_s[bi, h].astype(BF16), preferred_element_type=F32)
                          for h in range(H)], axis=1)
    num = jnp.dot(s.astype(BF16), vbd, preferred_element_type=F32) + ac * qc
    n_row = n_s[bi]
    den = (_sel_right(s, seg_ref[...], pieces=2)
           + a_inter * _sel_right(q * n_row, segv_ref[...], pieces=2))
    rden = heads_only(1.0 / jnp.maximum(jnp.abs(den), jnp.exp(-mt)))
    hh = num * _sel_right(rden, expv, pieces=2) * jax.nn.sigmoid(og_ref[bi, rw])
    h_ref[bi, rw] = hh.astype(h_ref.dtype)

    kw = ks * wc
    kwb = kw.astype(BF16)
    for h in range(H):
        hs = slice(h * DK, (h + 1) * DK)
        c_s[bi, h] = dc[:, hs] * c_s[bi, h] + lax.dot_general(kwb[:, hs], vb[:, hs], TN, preferred_element_type=F32)
    n_s[bi] = dc * n_row + jnp.sum(kw, axis=0, keepdims=True)
    m_s[bi] = heads_only(m_new)


def _mlstm_consts(L):
    H, DK = ML_HEADS, ML_DK
    expc = np.zeros((LANE, H * L), np.float32)
    expv = np.zeros((LANE, H * DK), np.float32)
    seg = np.zeros((H * L, LANE), np.float32)
    segv = np.zeros((H * DK, LANE), np.float32)
    hmk = np.zeros((H * L, H * DK), np.float32)
    for h in range(H):
        expc[h, h * L:(h + 1) * L] = 1.0
        expv[h, h * DK:(h + 1) * DK] = 1.0
        seg[h * L:(h + 1) * L, h] = 1.0
        segv[h * DK:(h + 1) * DK, h] = 1.0
        hmk[h * L:(h + 1) * L, h * DK:(h + 1) * DK] = 1.0
    eyet = np.tile(np.eye(L, dtype=np.float32), (1, H))
    bf = lambda a: jnp.asarray(a, BF16)
    return (bf(np.tril(np.ones((L, L), np.float32))), bf(np.ones((L, L), np.float32)), bf(expc), bf(expv),
            jnp.asarray(eyet), bf(seg), bf(segv), bf(hmk))


def _mlstm(z3, bias, c0, n0, m0, L, Bb, nsub):
    b, t, _ = z3.shape
    nc = t // (L * nsub)
    hw = ML_HEADS * ML_DK
    consts = _mlstm_consts(L)
    gate_blk = (EVEN_PAD - LANE) // LANE
    col = lambda j: (lambda bi, ci: (bi, ci, j))
    const2 = lambda bi, ci: (0, 0)
    state3 = lambda bi, ci: (bi, 0, 0)
    m0p = jnp.pad(m0, ((0, 0), (0, LANE - ML_HEADS))).reshape(b, 1, LANE)
    h, c_new, n_new, m_new = pl.pallas_call(
        functools.partial(_mlstm_kernel, L=L, Bb=Bb, nsub=nsub),
        grid=(b // Bb, nc),
        in_specs=[pl.BlockSpec((Bb, nsub * L, hw), col(0)),
                  pl.BlockSpec((Bb, nsub * L, hw), col(1)),
                  pl.BlockSpec((Bb, nsub * L, hw), col(2)),
                  pl.BlockSpec((Bb, nsub * L, hw), col(3)),
                  pl.BlockSpec((Bb, nsub * L, LANE), col(gate_blk)),
                  pl.BlockSpec((1, LANE), const2)]
                 + [pl.BlockSpec(a.shape, const2) for a in consts]
                 + [pl.BlockSpec((Bb, ML_HEADS, ML_DK, ML_DV), lambda bi, ci: (bi, 0, 0, 0)),
                    pl.BlockSpec((Bb, 1, hw), state3),
                    pl.BlockSpec((Bb, 1, LANE), state3)],
        out_specs=[pl.BlockSpec((Bb, nsub * L, hw), col(0)),
                   pl.BlockSpec((Bb, ML_HEADS, ML_DK, ML_DV), lambda bi, ci: (bi, 0, 0, 0)),
                   pl.BlockSpec((Bb, 1, hw), state3),
                   pl.BlockSpec((Bb, 1, LANE), state3)],
        out_shape=[jax.ShapeDtypeStruct((b, t, hw), BF16),
                   jax.ShapeDtypeStruct((b, ML_HEADS, ML_DK, ML_DV), F32),
                   jax.ShapeDtypeStruct((b, 1, hw), F32),
                   jax.ShapeDtypeStruct((b, 1, LANE), F32)],
        scratch_shapes=[pltpu.VMEM((Bb, ML_HEADS, ML_DK, ML_DV), F32),
                        pltpu.VMEM((Bb, 1, hw), F32),
                        pltpu.VMEM((Bb, 1, LANE), F32)],
        compiler_params=_params(("parallel", "arbitrary")),
        name="mlstm",
    )(z3, z3, z3, z3, z3, bias, *consts, c0, n0.reshape(b, 1, hw), m0p)
    return h, c_new, n_new.reshape(b, ML_HEADS, ML_DK), m_new[:, 0, :ML_HEADS]


def _rope(x, cos, sin_signed, width):
    half = SW_HD // 2
    lane = lax.broadcasted_iota(jnp.int32, x.shape, 1)
    first = (lane % SW_HD) < half
    partner = jnp.where(first, pltpu.roll(x, width - half, axis=1), pltpu.roll(x, half, axis=1))
    return x * cos + partner * sin_signed


def _undup(xd):
    lane = lax.broadcasted_iota(jnp.int32, xd.shape[:-1] + (LANE,), xd.ndim - 1)
    return jnp.where(lane < SW_HD, xd[..., :LANE], xd[..., LANE:])


def _swa_prompt_kernel(sink_ref, q_ref, kc_ref, kp_ref, vc_ref, vp_ref, cosc_ref, sinc_ref, cosp_ref, sinp_ref,
                       h_ref, ko_ref):
    n = pl.program_id(1)
    W2 = 2 * WINDOW
    cosc, sinc = cosc_ref[...], sinc_ref[...]
    cosp, sinp = cosp_ref[...], sinp_ref[...]
    i = lax.broadcasted_iota(jnp.int32, (WINDOW, W2), 0)
    j = lax.broadcasted_iota(jnp.int32, (WINDOW, W2), 1)
    diff = WINDOW + i - j
    valid = (diff >= 0) & (diff < WINDOW) & ((n > 0) | (j >= WINDOW))
    row = lax.broadcasted_iota(jnp.int32, (2 * W2, LANE), 0)
    lane = lax.broadcasted_iota(jnp.int32, (2 * W2, LANE), 1)
    bd = (row < W2) == (lane < SW_HD)
    ones_bd = bd.astype(BF16)
    npair = SW_GQ // 2
    rows = npair * WINDOW
    valid = jnp.concatenate([valid] * npair, axis=0)
    low = lax.broadcasted_iota(jnp.int32, (rows, LANE), 1) < SW_HD
    prow = lax.broadcasted_iota(jnp.int32, (rows, 1), 0) // WINDOW
    cos4, sin4 = jnp.concatenate([cosc] * 4, axis=1), jnp.concatenate([sinc] * 4, axis=1)
    for bi in range(q_ref.shape[0]):
        q = _rope(q_ref[bi], cos4, sin4, 4 * LANE)
        k_cur = _rope(kc_ref[bi], cos4[:, :2 * LANE], sin4[:, :2 * LANE], 2 * LANE)
        k_prev = _rope(kp_ref[bi], jnp.concatenate([cosp] * 2, axis=1), jnp.concatenate([sinp] * 2, axis=1), 2 * LANE)
        kd = jnp.concatenate([k_prev, k_cur], axis=0)
        vd = jnp.concatenate([vp_ref[bi], vc_ref[bi]], axis=0)
        for kv in range(SW_KV_HEADS):
            kblk = kd[:, kv * LANE:(kv + 1) * LANE]
            vblk = vd[:, kv * LANE:(kv + 1) * LANE]
            kbd = jnp.where(bd, jnp.concatenate([kblk, kblk], axis=0), 0.0).astype(BF16)
            vbd = jnp.where(bd, jnp.concatenate([vblk, vblk], axis=0), 0.0).astype(BF16)
            vo = jnp.concatenate([vbd, ones_bd], axis=1)
            h0 = kv * SW_GQ
            qs = jnp.concatenate([q[:, (h0 + 2 * p) * SW_HD:(h0 + 2 * p + 2) * SW_HD] for p in range(npair)],
                                 axis=0).astype(BF16)
            s = lax.dot_general(qs, kbd, NT, preferred_element_type=F32) * (SW_HD ** -0.5)
            es, sink_terms = [], []
            for hh in range(2):
                sh = jnp.where(valid, s[:, hh * W2:(hh + 1) * W2], -jnp.inf)
                sink = sink_ref[h0 + hh]
                for p in range(1, npair):
                    sink = jnp.where(prow == p, sink_ref[h0 + 2 * p + hh], sink)
                m = jnp.maximum(jnp.max(sh, axis=1, keepdims=True), sink)
                es.append(jnp.exp(sh - m).astype(BF16))
                sink_terms.append(jnp.exp(sink - m))
            r = jnp.dot(jnp.concatenate(es, axis=1), vo, preferred_element_type=F32)
            o = r[:, :LANE] / (r[:, LANE:] + jnp.where(low, sink_terms[0], sink_terms[1]))
            for p in range(npair):
                h_ref[bi, :, (h0 + 2 * p) * SW_HD:(h0 + 2 * p + 2) * SW_HD] = (
                    o[p * WINDOW:(p + 1) * WINDOW].astype(h_ref.dtype))

        @pl.when(n == pl.num_programs(1) - 1)
        def _(bi=bi, k_cur=k_cur):
            ko_ref[bi] = _undup(k_cur)


def _swa_prompt(z3, cos, sin_signed, sinks, Bb):
    b, t, _ = z3.shape
    nb = t // WINDOW
    qw = SW_HEADS * SW_HD
    kw = SW_KV_HEADS * SW_HD
    dw = 2 * kw
    qblk, kblk, vblk = EVEN_QA // qw, EVEN_KD // dw, EVEN_VD // dw
    cur = lambda j: (lambda bi, ni: (bi, ni, j))
    prev = lambda j: (lambda bi, ni: (bi, jnp.maximum(ni - 1, 0), j))
    return pl.pallas_call(
        _swa_prompt_kernel,
        grid=(b // Bb, nb),
        in_specs=[pl.BlockSpec(memory_space=pltpu.SMEM),
                  pl.BlockSpec((Bb, WINDOW, qw), cur(qblk)),
                  pl.BlockSpec((Bb, WINDOW, dw), cur(kblk)),
                  pl.BlockSpec((Bb, WINDOW, dw), prev(kblk)),
                  pl.BlockSpec((Bb, WINDOW, dw), cur(vblk)),
                  pl.BlockSpec((Bb, WINDOW, dw), prev(vblk)),
                  pl.BlockSpec((WINDOW, LANE), lambda bi, ni: (ni, 0)),
                  pl.BlockSpec((WINDOW, LANE), lambda bi, ni: (ni, 0)),
                  pl.BlockSpec((WINDOW, LANE), lambda bi, ni: (jnp.maximum(ni - 1, 0), 0)),
                  pl.BlockSpec((WINDOW, LANE), lambda bi, ni: (jnp.maximum(ni - 1, 0), 0))],
        out_specs=[pl.BlockSpec((Bb, WINDOW, qw), cur(0)),
                   pl.BlockSpec((Bb, WINDOW, kw), lambda bi, ni: (bi, 0, 0))],
        out_shape=[jax.ShapeDtypeStruct((b, t, qw), BF16),
                   jax.ShapeDtypeStruct((b, WINDOW, kw), F32)],
        compiler_params=_params(("parallel", "arbitrary")),
        name="swa_prompt",
    )(sinks, z3, z3, z3, z3, z3, cos, sin_signed, cos, sin_signed)


def _swa_sample_kernel(q_ref, k_ref, v_ref, kbuf_ref, vbuf_ref, cos_ref, sin_ref, sink_ref,
                       h_ref, ko_ref, vo_ref, *, T, Bb):
    qw = SW_HEADS * SW_HD
    kw = SW_KV_HEADS * SW_HD
    cos = cos_ref[...]
    sin = sin_ref[...]
    q = _rope(q_ref[...].reshape(Bb * T, qw), jnp.concatenate([cos] * (qw // LANE), axis=1),
              jnp.concatenate([sin] * (qw // LANE), axis=1), qw).reshape(Bb, T, qw)
    k_new = _rope(_undup(k_ref[...]).reshape(Bb * T, kw), cos, sin, kw).reshape(Bb, T, kw)
    kk = jnp.concatenate([kbuf_ref[...], k_new], axis=1)
    vv = jnp.concatenate([vbuf_ref[...], _undup(v_ref[...])], axis=1)
    ko_ref[...] = kk[:, T:, :]
    vo_ref[...] = vv[:, T:, :]
    rows = SW_GQ * T
    i = lax.broadcasted_iota(jnp.int32, (rows, WINDOW + T), 0) % T
    j = lax.broadcasted_iota(jnp.int32, (rows, WINDOW + T), 1)
    diff = WINDOW + i - j
    valid = (diff >= 0) & (diff < WINDOW)
    for kv in range(SW_KV_HEADS):
        qs = jnp.concatenate([q[:, :, (kv * SW_GQ + g) * SW_HD:(kv * SW_GQ + g + 1) * SW_HD]
                              for g in range(SW_GQ)], axis=1).astype(BF16)
        kh = kk[:, :, kv * SW_HD:(kv + 1) * SW_HD].astype(BF16)
        vh = vv[:, :, kv * SW_HD:(kv + 1) * SW_HD].astype(BF16)
        s = jnp.einsum('bqd,bkd->bqk', qs, kh, preferred_element_type=F32) * (SW_HD ** -0.5)
        s = jnp.where(valid, s, -jnp.inf)
        sink = sink_ref[kv * rows:(kv + 1) * rows, :]
        m = jnp.maximum(jnp.max(s, axis=-1, keepdims=True), sink)
        e = jnp.exp(s - m)
        p = e / (jnp.sum(e, axis=-1, keepdims=True) + jnp.exp(sink - m))
        o = jnp.einsum('bqk,bkd->bqd', p.astype(BF16), vh, preferred_element_type=F32)
        for g in range(SW_GQ):
            hh = kv * SW_GQ + g
            h_ref[:, :, hh * SW_HD:(hh + 1) * SW_HD] = o[:, g * T:(g + 1) * T, :].astype(h_ref.dtype)


def _swa_sample(z3, kbuf, vbuf, cos, sin_signed, sinks, Bb):
    b, t, _ = z3.shape
    qw = SW_HEADS * SW_HD
    kw = SW_KV_HEADS * SW_HD
    dw = 2 * kw
    qblk, kblk, vblk = EVEN_QA // qw, EVEN_KD // dw, EVEN_VD // dw
    col = lambda j: (lambda bi: (bi, 0, j))
    const2 = lambda bi: (0, 0)
    sink_col = jnp.repeat(sinks, t).reshape(SW_HEADS * t, 1)
    return pl.pallas_call(
        functools.partial(_swa_sample_kernel, T=t, Bb=Bb),
        grid=(b // Bb,),
        in_specs=[pl.BlockSpec((Bb, t, qw), col(qblk)),
                  pl.BlockSpec((Bb, t, dw), col(kblk)),
                  pl.BlockSpec((Bb, t, dw), col(vblk)),
                  pl.BlockSpec((Bb, WINDOW, kw), col(0)),
                  pl.BlockSpec((Bb, WINDOW, kw), col(0)),
                  pl.BlockSpec((Bb * t, LANE), const2),
                  pl.BlockSpec((Bb * t, LANE), const2),
                  pl.BlockSpec((SW_HEADS * t, 1), const2)],
        out_specs=[pl.BlockSpec((Bb, t, qw), col(0)),
                   pl.BlockSpec((Bb, WINDOW, kw), col(0)),
                   pl.BlockSpec((Bb, WINDOW, kw), col(0))],
        out_shape=[jax.ShapeDtypeStruct((b, t, qw), BF16),
                   jax.ShapeDtypeStruct((b, WINDOW, kw), F32),
                   jax.ShapeDtypeStruct((b, WINDOW, kw), F32)],
        compiler_params=_params(("parallel",)),
        name="swa_sample",
    )(z3, z3, z3, kbuf, vbuf, jnp.tile(cos, (Bb, 1)), jnp.tile(sin_signed, (Bb, 1)), sink_col)


def _gla_tables(L):
    nlev = int(math.log2(L))
    assert 2 ** nlev == L
    mstack = np.zeros((nlev * L, L), np.float32)
    masks = np.zeros((nlev + 1, L, L), np.float32)
    masks[0] = np.eye(L)
    for l in range(1, nlev + 1):
        n = 2 ** l
        for t in range(L):
            blk, pos = divmod(t, n)
            m = blk * n + n // 2 - 1
            if pos >= n // 2:
                mstack[(l - 1) * L + t, m + 1:t + 1] = 1.0
                masks[l, t, blk * n:blk * n + n // 2] = 1.0
            else:
                mstack[(l - 1) * L + t, t + 1:m + 1] = 1.0
    masks = np.tile(masks, (1, 1, GLA_HEADS))
    return nlev, jnp.asarray(mstack, BF16), jnp.asarray(masks)


def _gla_consts(L):
    kw = GLA_HEADS * GLA_DK
    vw = GLA_HEADS * GLA_DV
    hk = np.zeros((GLA_HEADS * L, kw), np.float32)
    for h in range(GLA_HEADS):
        hk[h * L:(h + 1) * L, h * GLA_DK:(h + 1) * GLA_DK] = 1.0
    hs = np.zeros((vw, kw), np.float32)
    for h in range(GLA_HEADS):
        hs[h * GLA_DV:(h + 1) * GLA_DV, h * GLA_DK:(h + 1) * GLA_DK] = 1.0
    return (jnp.asarray(hk, BF16), jnp.asarray(hs), jnp.asarray(np.eye(GLA_DV), BF16), jnp.asarray(np.eye(kw), BF16))


def _gla_kernel(q_ref, k_ref, v_ref, r_ref, glr_ref, wg_ref, bg_ref, gn_ref, tri_ref, mstack_ref,
                masks_ref, hk_ref, hs_ref, eyev_ref, eyek_ref, s0_ref,
                o_ref, so_ref, st_s, *, L, nlev, Bb, nsub):
    c = pl.program_id(1)
    H, DK, DV = GLA_HEADS, GLA_DK, GLA_DV

    @pl.when(c == 0)
    def _():
        for bi in range(Bb):
            for h in range(H):
                pieces = ([jnp.zeros((h * DK, DV), F32)] if h else []) + [s0_ref[bi, h]]
                pieces += [jnp.zeros(((H - 1 - h) * DK, DV), F32)] if h < H - 1 else []
                padded = jnp.concatenate(pieces, axis=0)
                st_s[bi, h * DV:(h + 1) * DV, :] = _dot_sel(eyev_ref[...], padded, NT)

    for sub in range(nsub):
        for bi in range(Bb):
            _gla_one(bi, sub * L, q_ref, k_ref, v_ref, r_ref, glr_ref, wg_ref, bg_ref, gn_ref, tri_ref, mstack_ref,
                     masks_ref, hk_ref, hs_ref, o_ref, st_s, L, nlev)

    @pl.when(c == pl.num_programs(1) - 1)
    def _():
        for bi in range(Bb):
            for h in range(H):
                blk = st_s[bi, h * DV:(h + 1) * DV, :]
                so_ref[bi, h] = _dot_sel(eyek_ref[h * DK:(h + 1) * DK, :], blk, NT)


def _gla_one(bi, r0, q_ref, k_ref, v_ref, r_ref, glr_ref, wg_ref, bg_ref, gn_ref, tri_ref, mstack_ref,
             masks_ref, hk_ref, hs_ref, o_ref, st_s, L, nlev):
    H, DK, DV = GLA_HEADS, GLA_DK, GLA_DV
    packed_rows = L % (2 * SUBLANE) == 0
    rw = pl.ds(r0, L)
    q = q_ref[bi, rw] * (DK ** -0.5)
    k = k_ref[bi, rw]
    kwid = H * DK
    g_hi, g_lo = _pieces(glr_ref[bi, rw], 2)
    pre2 = jnp.dot(g_hi.astype(BF16), wg_ref[...], preferred_element_type=F32)
    pre = (pre2[:, :kwid] + pre2[:, kwid:] + jnp.dot(g_lo.astype(BF16), wg_ref[:, :kwid], preferred_element_type=F32)
           + bg_ref[...])
    lg = jax.nn.log_sigmoid(pre) * (1.0 / GLA_TAU)
    G = _dot_sel(tri_ref[...], lg)
    E = jnp.exp(_dot_sel(mstack_ref[...], lg, pieces=2))
    hk = hk_ref[...]
    tok = lax.broadcasted_iota(jnp.int32, (L, kwid), 0)

    att = None
    for l in range(nlev + 1):
        if l == 0:
            ql, kl = q, k
        else:
            e_l = E[(l - 1) * L:l * L]
            upper = ((tok >> (l - 1)) & 1) == 1
            ql = jnp.where(upper, q * e_l, 0.0)
            kl = jnp.where(upper, 0.0, k * e_l)
        if packed_rows:
            kst = jnp.concatenate([kl.astype(BF16)] * H, axis=0) * hk
        else:
            kst = (jnp.concatenate([kl] * H, axis=0) * hk.astype(F32)).astype(BF16)
        part = lax.dot_general(ql.astype(BF16), kst, NT, preferred_element_type=F32)
        part = jnp.where(masks_ref[l] > 0.5, part, 0.0)
        att = part if att is None else att + part

    v = v_ref[bi, rw]
    vsrc = v.astype(BF16) if packed_rows else v
    zero = jnp.zeros((L, DV), vsrc.dtype)
    vbd = jnp.concatenate(
        [jnp.concatenate([zero] * h + [vsrc[:, h * DV:(h + 1) * DV]] + [zero] * (H - 1 - h), axis=1)
         for h in range(H)], axis=0).astype(BF16)
    stb = st_s[bi]
    o = (jnp.dot(att.astype(BF16), vbd, preferred_element_type=F32)
         + lax.dot_general((q * jnp.exp(G)).astype(BF16), stb.astype(BF16), NT, preferred_element_type=F32))
    for h in range(H):
        vs = slice(h * DV, (h + 1) * DV)
        oh = o[:, vs]
        oh = oh * lax.rsqrt(jnp.mean(oh * oh, axis=-1, keepdims=True) + EPS)
        oh = oh * gn_ref[:, vs] * jax.nn.silu(r_ref[bi, rw, vs])
        o_ref[bi, rw, vs] = oh.astype(o_ref.dtype)
    g_end = G[L - 1:L, :]
    k_end = (k * jnp.exp(g_end - G)).astype(BF16)
    upd = lax.dot_general(v.astype(BF16), k_end, TN, preferred_element_type=F32)
    st_s[bi] = stb * jnp.exp(g_end) + jnp.where(hs_ref[...] > 0.5, upd, 0.0)


def _gla(z3, wg, bg, gn, s0, L, Bb, nsub):
    b, t, _ = z3.shape
    nc = t // (L * nsub)
    nlev, mstack, masks = _gla_tables(L)
    hk, hs, eyev, eyek = _gla_consts(L)
    tri = jnp.asarray(np.tril(np.ones((L, L), np.float32)), BF16)
    qk_w = GLA_HEADS * GLA_DK
    v_w = GLA_HEADS * GLA_DV
    col = lambda j: (lambda bi, ci: (bi, ci, j))
    const2 = lambda bi, ci: (0, 0)
    const3 = lambda bi, ci: (0, 0, 0)
    return pl.pallas_call(
        functools.partial(_gla_kernel, L=L, nlev=nlev, Bb=Bb, nsub=nsub),
        grid=(b // Bb, nc),
        in_specs=[pl.BlockSpec((Bb, nsub * L, qk_w), col(0)),
                  pl.BlockSpec((Bb, nsub * L, qk_w), col(1)),
                  pl.BlockSpec((Bb, nsub * L, v_w), col(1)),
                  pl.BlockSpec((Bb, nsub * L, v_w), col(2)),
                  pl.BlockSpec((Bb, nsub * L, LANE), col(2048 // LANE)),
                  pl.BlockSpec((LANE, 2 * qk_w), const2),
                  pl.BlockSpec((1, qk_w), const2),
                  pl.BlockSpec((1, v_w), const2),
                  pl.BlockSpec((L, L), const2),
                  pl.BlockSpec(mstack.shape, const2),
                  pl.BlockSpec(masks.shape, const3),
                  pl.BlockSpec(hk.shape, const2),
                  pl.BlockSpec(hs.shape, const2),
                  pl.BlockSpec(eyev.shape, const2),
                  pl.BlockSpec(eyek.shape, const2),
                  pl.BlockSpec((Bb, GLA_HEADS, GLA_DK, GLA_DV), lambda bi, ci: (bi, 0, 0, 0))],
        out_specs=[pl.BlockSpec((Bb, nsub * L, v_w), col(0)),
                   pl.BlockSpec((Bb, GLA_HEADS, GLA_DK, GLA_DV), lambda bi, ci: (bi, 0, 0, 0))],
        out_shape=[jax.ShapeDtypeStruct((b, t, v_w), BF16),
                   jax.ShapeDtypeStruct((b, GLA_HEADS, GLA_DK, GLA_DV), F32)],
        scratch_shapes=[pltpu.VMEM((Bb, v_w, qk_w), F32)],
        compiler_params=_params(("parallel", "arbitrary")),
        name="gla",
    )(z3, z3, z3, z3, z3, wg, bg, gn, tri, mstack, masks, hk, hs, eyev, eyek, s0)


def _s5_kernel(u_ref, bbre_ref, bbim_ref, cre_ref, cim_ref, d_ref, wglu_ref, bglu_ref, pre_ref, pim_ref,
               x0re_ref, x0im_ref,
               o_ref, xore_ref, xoim_ref,
               xr_s, xi_s, cr_s, ci_s, *, rows, sequential):
    c = pl.program_id(1)
    lead = rows // SUBLANE if sequential else SUBLANE
    inner = rows // lead
    cw = S5_LANES // 4 if inner == SUBLANE else LANE
    u = jnp.swapaxes(u_ref[0].reshape(inner, lead, S5_WIDTH), 0, 1).reshape(rows, S5_WIDTH)
    ub = u.astype(BF16)
    hw, hl = S5_WIDTH // 2, S5_LANES // 2
    for half in range(2):
        uh = ub[:, half * hw:(half + 1) * hw]
        ls = slice(half * hl, (half + 1) * hl)
        xr_s[:, :, ls] = jnp.dot(uh, bbre_ref[half * hw:(half + 1) * hw, ls],
                                 preferred_element_type=F32).reshape(lead, inner, hl)
        xi_s[:, :, ls] = jnp.dot(uh, bbim_ref[half * hw:(half + 1) * hw, ls],
                                 preferred_element_type=F32).reshape(lead, inner, hl)

    for c0 in range(0, S5_LANES, cw):
        ar = jnp.broadcast_to(pre_ref[0, :, c0:c0 + cw], (inner, cw))
        ai = jnp.broadcast_to(pim_ref[0, :, c0:c0 + cw], (inner, cw))

        def body(j, carry, c0=c0, ar=ar, ai=ai):
            xr, xi = carry
            nr = ar * xr - ai * xi + xr_s[j, :, c0:c0 + cw]
            ni = ar * xi + ai * xr + xi_s[j, :, c0:c0 + cw]
            xr_s[j, :, c0:c0 + cw] = nr
            xi_s[j, :, c0:c0 + cw] = ni
            return nr, ni

        if sequential:
            init = (jnp.zeros((inner, cw), F32), jnp.zeros((inner, cw), F32))
        else:
            init = (x0re_ref[0, :, c0:c0 + cw], x0im_ref[0, :, c0:c0 + cw])
        lax.fori_loop(0, lead, body, init, unroll=4)

    if sequential:
        @pl.when(c == 0)
        def _():
            cr_s[...] = x0re_ref[0]
            ci_s[...] = x0im_ref[0]

        end_r = xr_s[lead - 1]
        end_i = xi_s[lead - 1]
        anr = pre_ref[lead - 1]
        ani = pim_ref[lead - 1]
        ent_r, ent_i = [cr_s[...]], [ci_s[...]]
        for s in range(SUBLANE):
            pr, pi = ent_r[-1], ent_i[-1]
            ent_r.append(end_r[s:s + 1] + anr * pr - ani * pi)
            ent_i.append(end_i[s:s + 1] + anr * pi + ani * pr)
        cr_s[...] = ent_r[SUBLANE]
        ci_s[...] = ent_i[SUBLANE]
        er = jnp.concatenate(ent_r[:SUBLANE], axis=0)[None]
        ei = jnp.concatenate(ent_i[:SUBLANE], axis=0)[None]
        pr3, pi3 = pre_ref[...], pim_ref[...]
        xr = xr_s[...] + pr3 * er - pi3 * ei
        xi = xi_s[...] + pr3 * ei + pi3 * er

        @pl.when(c == pl.num_programs(1) - 1)
        def _():
            xore_ref[0] = cr_s[...]
            xoim_ref[0] = ci_s[...]
    else:
        xr = xr_s[...]
        xi = xi_s[...]
        xore_ref[0] = xr[lead - 1]
        xoim_ref[0] = xi[lead - 1]

    xrb = xr.reshape(rows, S5_LANES).astype(BF16)
    xib = xi.reshape(rows, S5_LANES).astype(BF16)
    y = jnp.concatenate(
        [jnp.dot(xrb[:, h * hl:(h + 1) * hl], cre_ref[h * hl:(h + 1) * hl, h * hw:(h + 1) * hw],
                 preferred_element_type=F32)
         - jnp.dot(xib[:, h * hl:(h + 1) * hl], cim_ref[h * hl:(h + 1) * hl, h * hw:(h + 1) * hw],
                   preferred_element_type=F32) for h in range(2)], axis=1) + d_ref[...] * u
    zz = jax.nn.gelu(y)
    gate = jnp.dot(zz.astype(BF16), wglu_ref[...], preferred_element_type=F32) + bglu_ref[...]
    out = zz * jax.nn.sigmoid(gate)
    o_ref[0] = jnp.swapaxes(out.reshape(lead, inner, S5_WIDTH), 0, 1).reshape(rows, S5_WIDTH).astype(o_ref.dtype)


def _s5(z3, tabs, x0re, x0im, rows, sequential):
    b, t, _ = z3.shape
    nblk = t // rows
    groups = rows // SUBLANE
    ublk = 1536 // S5_WIDTH
    const2 = lambda bi, ci: (0, 0)
    const3 = lambda bi, ci: (0, 0, 0)
    lead = groups if sequential else SUBLANE
    kk = jnp.arange(1, lead + 1, dtype=F32).reshape(lead, 1, 1)
    mag = jnp.exp(kk * tabs['la_re'][None])
    p_re = mag * jnp.cos(kk * tabs['la_im'][None])
    p_im = mag * jnp.sin(kk * tabs['la_im'][None])
    if sequential:
        st_spec = pl.BlockSpec((1, 1, S5_LANES), lambda bi, ci: (bi, 0, 0))
        st_shape = jax.ShapeDtypeStruct((b, 1, S5_LANES), F32)
    else:
        st_spec = pl.BlockSpec((1, groups, S5_LANES), lambda bi, ci: (bi, ci, 0))
        st_shape = jax.ShapeDtypeStruct(x0re.shape, F32)
    return pl.pallas_call(
        functools.partial(_s5_kernel, rows=rows, sequential=sequential),
        grid=(b, nblk),
        in_specs=[pl.BlockSpec((1, rows, S5_WIDTH), lambda bi, ci: (bi, ci, ublk)),
                  pl.BlockSpec((S5_WIDTH, S5_LANES), const2),
                  pl.BlockSpec((S5_WIDTH, S5_LANES), const2),
                  pl.BlockSpec((S5_LANES, S5_WIDTH), const2),
                  pl.BlockSpec((S5_LANES, S5_WIDTH), const2),
                  pl.BlockSpec((1, S5_WIDTH), const2),
                  pl.BlockSpec((S5_WIDTH, S5_WIDTH), const2),
                  pl.BlockSpec((1, S5_WIDTH), const2),
                  pl.BlockSpec((lead, 1, S5_LANES), const3),
                  pl.BlockSpec((lead, 1, S5_LANES), const3),
                  st_spec, st_spec],
        out_specs=[pl.BlockSpec((1, rows, S5_WIDTH), lambda bi, ci: (bi, ci, 0)),
                   st_spec, st_spec],
        out_shape=[jax.ShapeDtypeStruct((b, t, S5_WIDTH), BF16), st_shape, st_shape],
        scratch_shapes=[pltpu.VMEM((lead, rows // lead, S5_LANES), F32),
                        pltpu.VMEM((lead, rows // lead, S5_LANES), F32),
                        pltpu.VMEM((1, S5_LANES), F32),
                        pltpu.VMEM((1, S5_LANES), F32)],
        compiler_params=_params(("parallel", "arbitrary")),
        name="s5",
    )(z3, tabs['bbre'], tabs['bbim'], tabs['cre'], tabs['cim'], tabs['d'], tabs['wglu'], tabs['bglu'],
      p_re, p_im, x0re, x0im)


def _s5_tables(a_re, a_im, log_dt, b_re, b_im, c_re, c_im, d, w_glu, b_glu):
    lam_re = a_re.astype(F32)
    lam_im = a_im.astype(F32)
    dt = jnp.exp(log_dt.astype(F32))[:, None]
    mag = jnp.exp(lam_re * dt)
    ang = lam_im * dt
    ab_re = mag * jnp.cos(ang)
    ab_im = mag * jnp.sin(ang)
    den = lam_re * lam_re + lam_im * lam_im
    co_re = ((ab_re - 1.0) * lam_re + ab_im * lam_im) / den
    co_im = (ab_im * lam_re - (ab_re - 1.0) * lam_im) / den
    b_re = b_re.astype(F32)
    b_im = b_im.astype(F32)
    bb_re = co_re[..., None] * b_re - co_im[..., None] * b_im
    bb_im = co_re[..., None] * b_im + co_im[..., None] * b_re
    eye = jnp.eye(S5_GROUPS, dtype=F32)
    blockdiag_in = lambda bb: jnp.einsum('gph,gk->ghkp', bb, eye).reshape(S5_WIDTH, S5_LANES)
    blockdiag_out = lambda cc: jnp.einsum('ghp,gk->gpkh', cc.astype(F32), eye).reshape(S5_LANES, S5_WIDTH)
    return {
        'bbre': blockdiag_in(bb_re).astype(BF16), 'bbim': blockdiag_in(bb_im).astype(BF16),
        'cre': blockdiag_out(c_re).astype(BF16), 'cim': blockdiag_out(c_im).astype(BF16),
        'd': d.astype(F32).reshape(1, S5_WIDTH),
        'wglu': w_glu.astype(BF16), 'bglu': b_glu.astype(F32).reshape(1, S5_WIDTH),
        'la_re': (lam_re * dt).reshape(1, S5_LANES), 'la_im': ang.reshape(1, S5_LANES),
    }


def _rmsnorm_val(x, g):
    return x * lax.rsqrt(jnp.mean(x * x, axis=-1, keepdims=True) + EPS) * g


def _ffn_kernel(r_ref, a1_ref, a2_ref, w1_ref, w2_ref, g_ref, wup_ref, cw_ref, cb_ref, wd_ref, st_ref, gf_ref,
                o_ref, so_ref, halo_s, *, tm, tf, long_seq, final_norm):
    i = pl.program_id(1)
    n = tm // SUBLANE
    d = r_ref.shape[-1]
    x = (r_ref[0] + jnp.dot(a1_ref[0], w1_ref[...], preferred_element_type=F32)
         + jnp.dot(a2_ref[0], w2_ref[...], preferred_element_type=F32))
    hn = _rmsnorm_val(x, g_ref[...])
    if long_seq:
        lead = n

        @pl.when(i == 0)
        def _():
            halo_s[...] = st_ref[0]
    else:
        lead = SUBLANE
    hnp = jnp.swapaxes(hn.reshape(tm // lead, lead, d), 0, 1).reshape(tm, d).astype(BF16)
    inner = tm // lead
    sub = lax.broadcasted_iota(jnp.int32, (SUBLANE, tf), 0)

    acc = jnp.zeros((tm, d), F32)
    for c in range(D_FF // tf):
        conv = []
        for half in range(2):
            c0 = half * D_FF + c * tf
            u = jnp.dot(hnp, wup_ref[:, c0:c0 + tf], preferred_element_type=F32).reshape(lead, inner, tf)
            if long_seq:
                um1 = jnp.where(sub == 0, halo_s[1:2, c0:c0 + tf], pltpu.roll(u[lead - 1], 1, axis=0))
                um2 = jnp.where(sub == 0, halo_s[0:1, c0:c0 + tf], pltpu.roll(u[lead - 2], 1, axis=0))
                halo_s[0:1, c0:c0 + tf] = u[lead - 2][SUBLANE - 1:SUBLANE]
                halo_s[1:2, c0:c0 + tf] = u[lead - 1][SUBLANE - 1:SUBLANE]
            else:
                um2 = st_ref[0, :, c0:c0 + tf]
                um1 = st_ref[1, :, c0:c0 + tf]
                so_ref[0, :, c0:c0 + tf] = u[lead - 2]
                so_ref[1, :, c0:c0 + tf] = u[lead - 1]
            ext = jnp.concatenate([um2[None], um1[None], u], axis=0)
            cv = cb_ref[:, c0:c0 + tf]
            for j in range(CONV_W):
                cv = cv + ext[j:j + lead] * cw_ref[j:j + 1, c0:c0 + tf]
            conv.append(cv)
        act = (jax.nn.silu(conv[1]) * conv[0]).reshape(tm, tf).astype(BF16)
        acc = acc + jnp.dot(act, wd_ref[c * tf:(c + 1) * tf, :], preferred_element_type=F32)

    out = x + jnp.swapaxes(acc.reshape(lead, inner, d), 0, 1).reshape(tm, d)
    if final_norm:
        out = _rmsnorm_val(out, gf_ref[...])
    o_ref[0] = out

    if long_seq:
        @pl.when(i == pl.num_programs(1) - 1)
        def _():
            so_ref[0] = halo_s[...]


def _ffn(res3, a1, a2, w1, w2, g, wup, cw, cb, wd, state, layer, gfinal, tm, tf, long_seq, final_norm):
    b, t, d = res3.shape
    nt = t // tm
    resident = dict(pipeline_mode=pl.Buffered(1))
    const2 = lambda bi, i: (0, 0)
    row_blk = lambda w: pl.BlockSpec((1, tm, w), lambda bi, i: (bi, i, 0))
    per_layer = lambda shape, **kw: pl.BlockSpec((None,) + shape, lambda bi, i: (layer, 0, 0), **kw)
    if long_seq:
        st_in = st_out = pl.BlockSpec((1, CONV_W - 1, 2 * D_FF), lambda bi, i: (bi, 0, 0))
        st_shape = state.shape
    else:
        nseq = tm // SUBLANE
        st_in = pl.BlockSpec((None, CONV_W - 1, nseq, 2 * D_FF), lambda bi, i: (layer, 0, i, 0))
        st_out = pl.BlockSpec((CONV_W - 1, nseq, 2 * D_FF), lambda bi, i: (0, i, 0))
        st_shape = state.shape[1:]
    return pl.pallas_call(
        functools.partial(_ffn_kernel, tm=tm, tf=tf, long_seq=long_seq, final_norm=final_norm),
        grid=(b, nt),
        in_specs=[row_blk(d), row_blk(a1.shape[-1]), row_blk(a2.shape[-1]),
                  pl.BlockSpec(w1.shape, const2, **resident),
                  pl.BlockSpec(w2.shape, const2, **resident),
                  per_layer((1, d)),
                  per_layer((d, 2 * D_FF), **resident),
                  per_layer((CONV_W, 2 * D_FF)),
                  per_layer((1, 2 * D_FF)),
                  per_layer((D_FF, d), **resident),
                  st_in,
                  pl.BlockSpec((1, d), const2)],
        out_specs=[row_blk(d), st_out],
        out_shape=[jax.ShapeDtypeStruct((b, t, d), F32), jax.ShapeDtypeStruct(st_shape, F32)],
        scratch_shapes=[pltpu.VMEM((CONV_W - 1, 2 * D_FF), F32)],
        compiler_params=_params(("parallel", "arbitrary")),
        name="ffn",
    )(res3, a1, a2, w1, w2, g.reshape(-1, 1, d), wup, cw, cb.reshape(-1, 1, 2 * D_FF), wd, state, gfinal)


def _rope_tables(pos):
    half = SW_HD // 2
    inv = ROPE_THETA ** (-jnp.arange(half, dtype=F32) / half)
    ang = pos.astype(F32)[:, None] * inv[None, :]
    cos = jnp.cos(ang)
    sin = jnp.sin(ang)
    reps = LANE // SW_HD
    cos_t = jnp.tile(jnp.concatenate([cos, cos], axis=1), (1, reps))
    sin_t = jnp.tile(jnp.concatenate([-sin, sin], axis=1), (1, reps))
    return cos_t, sin_t


def _prep_weights(W):
    P = {}
    we = W['w_in_even']
    parts = jnp.split(we, [int(c) for c in np.cumsum((512, 512, 512, 512, 8, 512, 128))], axis=1)
    q_m, k_m, v_m, o_m, g_m, q_a, k_a, v_a = parts
    dup = lambda w: jnp.concatenate([w[:, :SW_HD], w[:, :SW_HD], w[:, SW_HD:], w[:, SW_HD:]], axis=1)
    pad = jnp.zeros((D_MODEL, LANE - 2 * ML_HEADS), we.dtype)
    P['w_in_even'] = jnp.concatenate([q_m, k_m, v_m, o_m, q_a, dup(k_a), dup(v_a), g_m, pad], axis=1).astype(BF16)
    assert P['w_in_even'].shape[1] == EVEN_PAD
    P['b_gates'] = jnp.pad(W['b_mlstm_gates'].astype(F32), (0, LANE - 2 * ML_HEADS)).reshape(1, LANE)
    wo = W['w_in_odd']
    q, k, v, r, glr, u = jnp.split(wo, [int(c) for c in np.cumsum((256, 256, 512, 512, 16))], axis=1)
    pad = jnp.zeros((D_MODEL, ODD_PAD - 2048 - GLA_RANK), wo.dtype)
    P['w_in_odd'] = jnp.concatenate([q, k, v, r, u, glr, pad], axis=1).astype(BF16)
    wg = jnp.pad(W['w_gla_gate_up'].astype(F32), ((0, LANE - GLA_RANK), (0, 0)))
    wg_hi = wg.astype(BF16)
    P['w_gate_up'] = jnp.concatenate([wg_hi, (wg - wg_hi.astype(F32)).astype(BF16)], axis=1)
    P['b_gate'] = W['b_gla_gate'].astype(F32).reshape(1, -1)
    P['g_gla'] = W['g_gla_norm'].astype(F32).reshape(1, -1)
    hm = ML_HEADS * ML_DV
    P['w_out_even'] = (W['w_out_even'][:hm].astype(BF16), W['w_out_even'][hm:].astype(BF16))
    hg = GLA_HEADS * GLA_DV
    P['w_out_odd'] = (W['w_out_odd'][:hg].astype(BF16), W['w_out_odd'][hg:].astype(BF16))
    P['w_ffn_up'] = W['w_ffn_up'].astype(BF16)
    P['w_ffn_down'] = W['w_ffn_down'].astype(BF16)
    P['s5'] = _s5_tables(W['s5_a_re'], W['s5_a_im'], W['s5_log_dt'], W['s5_b_re'], W['s5_b_im'],
                         W['s5_c_re'], W['s5_c_im'], W['s5_d'], W['w_s5_glu'], W['b_s5_glu'])
    return P


def _trunk(x, pos, st, W, P, is_prompt):
    b, t, d = x.shape
    n = b * t
    L = CHUNK if t % CHUNK == 0 else t
    nsub = 2 if t % (2 * L) == 0 else 1
    tm = 512 if n % 512 == 0 else n
    cos, sin_signed = _rope_tables(pos)
    gfinal = W['norm_final'].astype(F32).reshape(1, d)
    new = {}
    conv_out = []
    h = x.reshape(n, d)

    z = _norm_matmul(h, W['norm_mix'][0], P['w_in_even'], tm)
    z3 = z.reshape(b, t, EVEN_PAD)
    h_m, new['ml_C'], new['ml_n'], m_new = _mlstm(z3, P['b_gates'], st['ml_C'], st['ml_n'], st['ml_m'], L,
                                                  Bb=2 if is_prompt else 8, nsub=nsub)
    new['ml_m'] = m_new.reshape(b, ML_HEADS)
    if is_prompt:
        h_a, kb = _swa_prompt(z3, cos, sin_signed, W['sw_sinks'].astype(F32), Bb=2 if b % 2 == 0 else 1)
        vb = jnp.concatenate([z3[:, t - WINDOW:, EVEN_VD:EVEN_VD + SW_HD],
                              z3[:, t - WINDOW:, EVEN_VD + LANE:EVEN_VD + LANE + SW_HD]], axis=-1)
    else:
        h_a, kb, vb = _swa_sample(z3, st['kbuf'].reshape(b, WINDOW, -1), st['vbuf'].reshape(b, WINDOW, -1),
                                  cos, sin_signed, W['sw_sinks'].astype(F32), Bb=8)
    new['kbuf'] = kb.reshape(b, WINDOW, SW_KV_HEADS, SW_HD)
    new['vbuf'] = vb.reshape(b, WINDOW, SW_KV_HEADS, SW_HD)
    h, cb = _ffn_layer(h, h_m, h_a, P['w_out_even'], 0, st, W, P, b, t, gfinal, is_prompt, final_norm=False)
    conv_out.append(cb)

    z = _norm_matmul(h, W['norm_mix'][1], P['w_in_odd'], tm)
    z3 = z.reshape(b, t, ODD_PAD)
    o_g, new['gla'] = _gla(z3, P['w_gate_up'], P['b_gate'], P['g_gla'], st['gla'], L, Bb=2 if is_prompt else 4,
                           nsub=nsub)
    x0re = st['s5_re'].astype(F32).reshape(b, 1, S5_LANES)
    x0im = st['s5_im'].astype(F32).reshape(b, 1, S5_LANES)
    if is_prompt:
        o_s, xre, xim = _s5(z3, P['s5'], x0re, x0im, rows=256 if t % 256 == 0 else t, sequential=True)
    else:
        o_s, xre, xim = _s5(z3.reshape(1, n, ODD_PAD), P['s5'], x0re.reshape(1, b, S5_LANES),
                            x0im.reshape(1, b, S5_LANES), rows=256 if n % 256 == 0 else n, sequential=False)
    new['s5_re'] = xre.reshape(b, S5_GROUPS, S5_STATE)
    new['s5_im'] = xim.reshape(b, S5_GROUPS, S5_STATE)
    h, cb = _ffn_layer(h, o_g, o_s, P['w_out_odd'], 1, st, W, P, b, t, gfinal, is_prompt, final_norm=True)
    conv_out.append(cb)
    new['conv'] = jnp.stack(conv_out)
    return h.reshape(b, t, d), new


def _ffn_layer(h, a1, a2, w_out, layer, st, W, P, b, t, gfinal, is_prompt, final_norm):
    n, d = h.shape
    rows = t if is_prompt else n
    tm = next((c for c in (512, 256) if rows % c == 0), rows)
    args = (*w_out, W['norm_ffn'], P['w_ffn_up'], W['ffn_conv_w'].astype(F32), W['ffn_conv_b'].astype(F32),
            P['w_ffn_down'])
    if is_prompt:
        shape3 = lambda a: a.reshape(b, t, -1)
        out, new_state = _ffn(shape3(h), shape3(a1), shape3(a2), *args, st['conv'][layer], layer, gfinal, tm,
                              FFN_TF, True, final_norm)
    else:
        assert t == SUBLANE
        shape3 = lambda a: a.reshape(1, n, -1)
        out, new_state = _ffn(shape3(h), shape3(a1), shape3(a2), *args, st['conv_rows'], layer, gfinal, tm, FFN_TF,
                              False, final_norm)
        new_state = jnp.swapaxes(new_state, 0, 1)
    return out.reshape(n, d), new_state


def kernel(x_prompt, x_sample, state_mlstm_C, state_mlstm_n, state_mlstm_m, cache_swa_k, cache_swa_v,
           state_gla, state_s5_re, state_s5_im, state_ffn_conv, norm_mix, norm_ffn, norm_final,
           w_in_even, b_mlstm_gates, sw_sinks, w_out_even, w_in_odd, w_gla_gate_up, b_gla_gate, g_gla_norm,
           s5_a_re, s5_a_im, s5_log_dt, s5_b_re, s5_b_im, s5_c_re, s5_c_im, s5_d, w_s5_glu, b_s5_glu,
           w_out_odd, w_ffn_up, ffn_conv_w, ffn_conv_b, w_ffn_down):
    W = {'norm_mix': norm_mix.astype(F32), 'norm_ffn': norm_ffn.astype(F32), 'norm_final': norm_final,
         'w_in_even': w_in_even, 'b_mlstm_gates': b_mlstm_gates, 'sw_sinks': sw_sinks, 'w_out_even': w_out_even,
         'w_in_odd': w_in_odd, 'w_gla_gate_up': w_gla_gate_up, 'b_gla_gate': b_gla_gate, 'g_gla_norm': g_gla_norm,
         's5_a_re': s5_a_re, 's5_a_im': s5_a_im, 's5_log_dt': s5_log_dt, 's5_b_re': s5_b_re, 's5_b_im': s5_b_im,
         's5_c_re': s5_c_re, 's5_c_im': s5_c_im, 's5_d': s5_d, 'w_s5_glu': w_s5_glu, 'b_s5_glu': b_s5_glu,
         'w_out_odd': w_out_odd, 'w_ffn_up': w_ffn_up, 'ffn_conv_w': ffn_conv_w, 'ffn_conv_b': ffn_conv_b,
         'w_ffn_down': w_ffn_down}
    P = _prep_weights(W)
    bp, tp = x_prompt.shape[:2]
    st_prompt = {'ml_C': jnp.zeros((bp, ML_HEADS, ML_DK, ML_DV), F32),
                 'ml_n': jnp.zeros((bp, ML_HEADS, ML_DK), F32),
                 'ml_m': jnp.zeros((bp, ML_HEADS), F32),
                 'gla': jnp.zeros((bp, GLA_HEADS, GLA_DK, GLA_DV), F32),
                 's5_re': jnp.zeros((bp, S5_GROUPS, S5_STATE), F32),
                 's5_im': jnp.zeros((bp, S5_GROUPS, S5_STATE), F32),
                 'conv': jnp.zeros((2, bp, CONV_W - 1, 2 * D_FF), F32)}
    st_sample = {'ml_C': state_mlstm_C.astype(F32), 'ml_n': state_mlstm_n.astype(F32),
                 'ml_m': state_mlstm_m.astype(F32), 'kbuf': cache_swa_k.astype(F32),
                 'vbuf': cache_swa_v.astype(F32), 'gla': state_gla.astype(F32),
                 's5_re': state_s5_re, 's5_im': state_s5_im,
                 'conv_rows': jnp.swapaxes(state_ffn_conv.astype(F32), 1, 2)}
    past_len = 16384
    y_p, np_ = _trunk(x_prompt.astype(F32), jnp.arange(tp), st_prompt, W, P, True)
    y_s, ns_ = _trunk(x_sample.astype(F32), past_len + jnp.arange(x_sample.shape[1]), st_sample, W, P, False)
    return (y_p, y_s,
            np_['ml_C'], ns_['ml_C'], np_['ml_n'], ns_['ml_n'], np_['ml_m'], ns_['ml_m'],
            np_['kbuf'], ns_['kbuf'], np_['vbuf'], ns_['vbuf'], np_['gla'], ns_['gla'],
            np_['s5_re'], ns_['s5_re'], np_['s5_im'], ns_['s5_im'], np_['conv'], ns_['conv'])
```

```python
import functools
import math

import numpy as np
import jax
import jax.numpy as jnp
from jax import lax
from jax.experimental import pallas as pl
from jax.experimental.pallas import tpu as pltpu

F32 = jnp.float32
BF16 = jnp.bfloat16
HI = lax.Precision.HIGHEST
NT = (((1,), (1,)), ((), ()))
TN = (((0,), (0,)), ((), ()))

D_MODEL = 1024
ML_HEADS, ML_DK, ML_DV = 4, 128, 128
SW_HEADS, SW_KV_HEADS, SW_HD, WINDOW = 8, 2, 64, 128
SW_GQ = SW_HEADS // SW_KV_HEADS
ROPE_THETA = 10000.0
GLA_HEADS, GLA_DK, GLA_DV, GLA_RANK, GLA_TAU = 4, 64, 128, 16, 16.0
S5_WIDTH, S5_GROUP, S5_STATE = 512, 16, 64
S5_GROUPS = S5_WIDTH // S5_GROUP
S5_LANES = S5_GROUPS * S5_STATE
D_FF = 2816
FFN_TF = 1408
CONV_W = 3
CHUNK = 64
EPS = 1e-6

LANE = 128
SUBLANE = 8
VMEM_LIMIT = 56 * 1024 * 1024

EVEN_QA, EVEN_KD, EVEN_VD, EVEN_PAD = 2048, 2560, 2816, 3200
ODD_PAD = 2176


def _params(sem):
    return pltpu.CompilerParams(dimension_semantics=sem, vmem_limit_bytes=VMEM_LIMIT)


def _norm_matmul_kernel(x_ref, g_ref, w_ref, o_ref):
    x = x_ref[...]
    ms = jnp.mean(x * x, axis=-1, keepdims=True)
    hn = (x * lax.rsqrt(ms + EPS) * g_ref[...]).astype(BF16)
    o_ref[...] = jnp.dot(hn, w_ref[...], preferred_element_type=F32)


def _norm_matmul(x, g, w, tm):
    n, d = x.shape
    c = w.shape[1]
    return pl.pallas_call(
        _norm_matmul_kernel,
        grid=(n // tm,),
        in_specs=[pl.BlockSpec((tm, d), lambda i: (i, 0)),
                  pl.BlockSpec((1, d), lambda i: (0, 0)),
                  pl.BlockSpec((d, c), lambda i: (0, 0))],
        out_specs=pl.BlockSpec((tm, c), lambda i: (i, 0)),
        out_shape=jax.ShapeDtypeStruct((n, c), F32),
        compiler_params=_params(("parallel",)),
        name="norm_matmul",
    )(x, g.reshape(1, d), w)


def _pieces(x, n):
    out, r = [], x
    for _ in range(n):
        p = r.astype(BF16).astype(F32)
        out.append(p)
        r = r - p
    return out


def _dot_sel(m_bf16, x, dims=None, pieces=3):
    ps = _pieces(x, pieces)
    axis = 1 if dims is None else 0
    w = x.shape[axis]
    if w % LANE:
        f = (lambda p: jnp.dot(m_bf16, p, preferred_element_type=F32)) if dims is None else (
            lambda p: lax.dot_general(m_bf16, p, dims, preferred_element_type=F32))
        return sum(f(p.astype(BF16)) for p in ps)
    stacked = jnp.concatenate(ps, axis=axis).astype(BF16)
    if dims is None:
        r = jnp.dot(m_bf16, stacked, preferred_element_type=F32)
    else:
        r = lax.dot_general(m_bf16, stacked, dims, preferred_element_type=F32)
    return sum(r[:, i * w:(i + 1) * w] for i in range(pieces))


def _sel_right(x, e_bf16, pieces=3):
    rows = x.shape[0]
    r = jnp.dot(jnp.concatenate(_pieces(x, pieces), axis=0).astype(BF16), e_bf16, preferred_element_type=F32)
    return sum(r[i * rows:(i + 1) * rows] for i in range(pieces))


def _mlstm_kernel(q_ref, k_ref, v_ref, og_ref, gz_ref, bias_ref, tri_ref, ones_ref, expc_ref, expv_ref, eyet_ref,
                  seg_ref, segv_ref, hmk_ref, c0_ref, n0_ref, m0_ref,
                  h_ref, co_ref, no_ref, mo_ref,
                  c_s, n_s, m_s, *, L, Bb, nsub, G):
    c = pl.program_id(1)

    @pl.when(c == 0)
    def _():
        c_s[...] = c0_ref[...]
        n_s[...] = n0_ref[...]
        m_s[...] = m0_ref[...]

    for sub in range(nsub):
        for bi in range(Bb):
            _mlstm_one(bi, sub * L, q_ref, k_ref, v_ref, og_ref, gz_ref, bias_ref, tri_ref, ones_ref, expc_ref,
                       expv_ref, eyet_ref, seg_ref, segv_ref, hmk_ref, h_ref, c_s, n_s, m_s, L, G)

    @pl.when(c == pl.num_programs(1) - 1)
    def _():
        co_ref[...] = c_s[...]
        no_ref[...] = n_s[...]
        mo_ref[...] = m_s[...]


def _mlstm_one(bi, r0, q_ref, k_ref, v_ref, og_ref, gz_ref, bias_ref, tri_ref, ones_ref, expc_ref, expv_ref,
               eyet_ref, seg_ref, segv_ref, hmk_ref, h_ref, c_s, n_s, m_s, L, G):
    H, DK = ML_HEADS, ML_DK
    HL = H * L
    S = L // G
    seg = lambda a, g: a[g * S:(g + 1) * S]
    per_row = lambda rows: rows[0] if G == 1 else jnp.concatenate(
        [jnp.broadcast_to(r, (S, r.shape[1])) for r in rows], axis=0)
    lane = lax.broadcasted_iota(jnp.int32, (L, LANE), 1)
    rowi = lax.broadcasted_iota(jnp.int32, (L, LANE), 0) % S
    keep = lane < H
    heads_only = lambda x: jnp.where(keep[:x.shape[0]], x, 0.0)

    rw = pl.ds(r0, L)
    gates = gz_ref[bi, rw] + bias_ref[...]
    lf = jax.nn.log_sigmoid(gates)
    b = pltpu.roll(_dot_sel(tri_ref[...], lf), LANE - H, axis=1)
    vv = gates - b
    cm = vv
    sh = 1
    while sh < S:
        cm = jnp.maximum(cm, jnp.where(rowi >= sh, pltpu.roll(cm, sh, axis=0), -jnp.inf))
        sh *= 2
    m_prev_g = [m_s[bi * G + g] for g in range(G)]
    m_prev = per_row(m_prev_g)
    mt = b + jnp.maximum(m_prev, cm)
    a_inter = jnp.exp(b + m_prev - mt)

    expv = expv_ref[...]
    m_new_g = [seg(mt, g)[S - 1:S, :] for g in range(G)]
    b_end_g = [seg(b, g)[S - 1:S, :] for g in range(G)]
    w_end = heads_only(jnp.exp(per_row(b_end_g) - b + gates - per_row(m_new_g)))
    decay_g = [heads_only(jnp.exp(b_end_g[g] + m_prev_g[g] - m_new_g[g])) for g in range(G)]
    decay_rows = jnp.concatenate(decay_g + [jnp.zeros((-G % SUBLANE, LANE), F32)] * (G % SUBLANE != 0), axis=0)
    spread_k = _sel_right(jnp.concatenate([heads_only(b - mt), heads_only(vv)], axis=0), expc_ref[...])
    uc = spread_k[:L]
    vr = _dot_sel(ones_ref[...], spread_k[L:] * eyet_ref[...])
    spread_v = _sel_right(jnp.concatenate([heads_only(a_inter), w_end, decay_rows], axis=0), expv, pieces=2)
    ac, wc, dcs = spread_v[:L], spread_v[L:2 * L], spread_v[2 * L:2 * L + G]
    ti = lax.broadcasted_iota(jnp.int32, (L, HL), 0)
    si = lax.broadcasted_iota(jnp.int32, (L, HL), 1) % L
    causal = (si <= ti) & (si // S == ti // S)
    w = jnp.exp(jnp.where(causal, uc + vr, -jnp.inf))

    q = q_ref[bi, rw]
    ks = k_ref[bi, rw] * (DK ** -0.5)
    v = v_ref[bi, rw]
    qb = q.astype(BF16)
    if L % (2 * SUBLANE) == 0:
        kst = jnp.concatenate([ks.astype(BF16)] * H, axis=0) * hmk_ref[...]
        vsrc = v.astype(BF16)
    else:
        kst = (jnp.concatenate([ks] * H, axis=0) * hmk_ref[...].astype(F32)).astype(BF16)
        vsrc = v
    zero = jnp.zeros((L, DK), vsrc.dtype)
    vbd = jnp.concatenate(
        [jnp.concatenate([zero] * h + [vsrc[:, h * DK:(h + 1) * DK]] + [zero] * (H - 1 - h), axis=1)
         for h in range(H)], axis=0).astype(BF16)
    s = lax.dot_general(qb, kst, NT, preferred_element_type=F32) * w
    qc = jnp.concatenate(
        [jnp.concatenate([jnp.dot(seg(q, g)[:, h * DK:(h + 1) * DK].astype(BF16), c_s[bi * G + g, h].astype(BF16),
                                  preferred_element_type=F32) for h in range(H)], axis=1)
         for g in range(G)], axis=0)
    num = jnp.dot(s.astype(BF16), vbd, preferred_element_type=F32) + ac * qc
    n_rows = per_row([n_s[bi * G + g] for g in range(G)])
    den = (_sel_right(s, seg_ref[...], pieces=2)
           + a_inter * _sel_right(q * n_rows, segv_ref[...], pieces=2))
    rden = heads_only(1.0 / jnp.maximum(jnp.abs(den), jnp.exp(-mt)))
    hh = num * _sel_right(rden, expv, pieces=2) * jax.nn.sigmoid(og_ref[bi, rw])
    h_ref[bi, rw] = hh.astype(h_ref.dtype)

    kw = ks * wc
    for g in range(G):
        e = bi * G + g
        kwb = seg(kw, g).astype(BF16)
        vb = seg(v, g).astype(BF16)
        dc = dcs[g:g + 1]
        for h in range(H):
            hs = slice(h * DK, (h + 1) * DK)
            c_s[e, h] = dc[:, hs] * c_s[e, h] + lax.dot_general(kwb[:, hs], vb[:, hs], TN, preferred_element_type=F32)
        n_s[e] = dc * n_s[e] + jnp.sum(seg(kw, g), axis=0, keepdims=True)
        m_s[e] = heads_only(m_new_g[g])


def _mlstm_consts(S, G):
    L = S * G
    H, DK = ML_HEADS, ML_DK
    expc = np.zeros((LANE, H * L), np.float32)
    expv = np.zeros((LANE, H * DK), np.float32)
    seg = np.zeros((H * L, LANE), np.float32)
    segv = np.zeros((H * DK, LANE), np.float32)
    hmk = np.zeros((H * L, H * DK), np.float32)
    for h in range(H):
        expc[h, h * L:(h + 1) * L] = 1.0
        expv[h, h * DK:(h + 1) * DK] = 1.0
        seg[h * L:(h + 1) * L, h] = 1.0
        segv[h * DK:(h + 1) * DK, h] = 1.0
        hmk[h * L:(h + 1) * L, h * DK:(h + 1) * DK] = 1.0
    eyet = np.tile(np.eye(L, dtype=np.float32), (1, H))
    bf = lambda a: jnp.asarray(a, BF16)
    tri = np.kron(np.eye(G), np.tril(np.ones((S, S)))).astype(np.float32)
    return (bf(tri), bf(np.ones((L, L), np.float32)), bf(expc), bf(expv),
            jnp.asarray(eyet), bf(seg), bf(segv), bf(hmk))


def _mlstm(z3, bias, c0, n0, m0, S, G, Bb, nsub):
    nseq, t_seq, width = z3.shape
    if G > 1:
        assert t_seq == S and nsub == 1
        z3 = z3.reshape(nseq // G, G * S, width)
    b, t, _ = z3.shape
    L = G * S
    nc = t // (L * nsub)
    hw = ML_HEADS * ML_DK
    consts = _mlstm_consts(S, G)
    gate_blk = (EVEN_PAD - LANE) // LANE
    col = lambda j: (lambda bi, ci: (bi, ci, j))
    const2 = lambda bi, ci: (0, 0)
    state3 = lambda bi, ci: (bi, 0, 0)
    m0p = jnp.pad(m0, ((0, 0), (0, LANE - ML_HEADS))).reshape(nseq, 1, LANE)
    Bs = Bb * G
    h, c_new, n_new, m_new = pl.pallas_call(
        functools.partial(_mlstm_kernel, L=L, Bb=Bb, nsub=nsub, G=G),
        grid=(b // Bb, nc),
        in_specs=[pl.BlockSpec((Bb, nsub * L, hw), col(0)),
                  pl.BlockSpec((Bb, nsub * L, hw), col(1)),
                  pl.BlockSpec((Bb, nsub * L, hw), col(2)),
                  pl.BlockSpec((Bb, nsub * L, hw), col(3)),
                  pl.BlockSpec((Bb, nsub * L, LANE), col(gate_blk)),
                  pl.BlockSpec((1, LANE), const2)]
                 + [pl.BlockSpec(a.shape, const2) for a in consts]
                 + [pl.BlockSpec((Bs, ML_HEADS, ML_DK, ML_DV), lambda bi, ci: (bi, 0, 0, 0)),
                    pl.BlockSpec((Bs, 1, hw), state3),
                    pl.BlockSpec((Bs, 1, LANE), state3)],
        out_specs=[pl.BlockSpec((Bb, nsub * L, hw), col(0)),
                   pl.BlockSpec((Bs, ML_HEADS, ML_DK, ML_DV), lambda bi, ci: (bi, 0, 0, 0)),
                   pl.BlockSpec((Bs, 1, hw), state3),
                   pl.BlockSpec((Bs, 1, LANE), state3)],
        out_shape=[jax.ShapeDtypeStruct((b, t, hw), BF16),
                   jax.ShapeDtypeStruct((nseq, ML_HEADS, ML_DK, ML_DV), F32),
                   jax.ShapeDtypeStruct((nseq, 1, hw), F32),
                   jax.ShapeDtypeStruct((nseq, 1, LANE), F32)],
        scratch_shapes=[pltpu.VMEM((Bs, ML_HEADS, ML_DK, ML_DV), F32),
                        pltpu.VMEM((Bs, 1, hw), F32),
                        pltpu.VMEM((Bs, 1, LANE), F32)],
        compiler_params=_params(("parallel", "arbitrary")),
        name="mlstm",
    )(z3, z3, z3, z3, z3, bias, *consts, c0, n0.reshape(nseq, 1, hw), m0p)
    return h.reshape(nseq, t_seq, hw), c_new, n_new.reshape(nseq, ML_HEADS, ML_DK), m_new[:, 0, :ML_HEADS]


def _rope(x, cos, sin_signed, width):
    half = SW_HD // 2
    lane = lax.broadcasted_iota(jnp.int32, x.shape, 1)
    first = (lane % SW_HD) < half
    partner = jnp.where(first, pltpu.roll(x, width - half, axis=1), pltpu.roll(x, half, axis=1))
    return x * cos + partner * sin_signed


def _undup(xd):
    lane = lax.broadcasted_iota(jnp.int32, xd.shape[:-1] + (LANE,), xd.ndim - 1)
    return jnp.where(lane < SW_HD, xd[..., :LANE], xd[..., LANE:])


def _swa_prompt_kernel(sink_ref, q_ref, kc_ref, kp_ref, vc_ref, vp_ref, cosc_ref, sinc_ref, cosp_ref, sinp_ref,
                       h_ref, ko_ref):
    n = pl.program_id(1)
    W2 = 2 * WINDOW
    cosc, sinc = cosc_ref[...], sinc_ref[...]
    cosp, sinp = cosp_ref[...], sinp_ref[...]
    i = lax.broadcasted_iota(jnp.int32, (WINDOW, W2), 0)
    j = lax.broadcasted_iota(jnp.int32, (WINDOW, W2), 1)
    diff = WINDOW + i - j
    valid = (diff >= 0) & (diff < WINDOW) & ((n > 0) | (j >= WINDOW))
    row = lax.broadcasted_iota(jnp.int32, (2 * W2, LANE), 0)
    lane = lax.broadcasted_iota(jnp.int32, (2 * W2, LANE), 1)
    bd = (row < W2) == (lane < SW_HD)
    ones_bd = bd.astype(BF16)
    npair = SW_GQ // 2
    rows = npair * WINDOW
    valid = jnp.concatenate([valid] * npair, axis=0)
    low = lax.broadcasted_iota(jnp.int32, (rows, LANE), 1) < SW_HD
    prow = lax.broadcasted_iota(jnp.int32, (rows, 1), 0) // WINDOW
    cos4, sin4 = jnp.concatenate([cosc] * 4, axis=1), jnp.concatenate([sinc] * 4, axis=1)
    for bi in range(q_ref.shape[0]):
        q = _rope(q_ref[bi], cos4, sin4, 4 * LANE)
        k_cur = _rope(kc_ref[bi], cos4[:, :2 * LANE], sin4[:, :2 * LANE], 2 * LANE)
        k_prev = _rope(kp_ref[bi], jnp.concatenate([cosp] * 2, axis=1), jnp.concatenate([sinp] * 2, axis=1), 2 * LANE)
        kd = jnp.concatenate([k_prev, k_cur], axis=0)
        vd = jnp.concatenate([vp_ref[bi], vc_ref[bi]], axis=0)
        for kv in range(SW_KV_HEADS):
            kblk = kd[:, kv * LANE:(kv + 1) * LANE]
            vblk = vd[:, kv * LANE:(kv + 1) * LANE]
            kbd = jnp.where(bd, jnp.concatenate([kblk, kblk], axis=0), 0.0).astype(BF16)
            vbd = jnp.where(bd, jnp.concatenate([vblk, vblk], axis=0), 0.0).astype(BF16)
            vo = jnp.concatenate([vbd, ones_bd], axis=1)
            h0 = kv * SW_GQ
            qs = jnp.concatenate([q[:, (h0 + 2 * p) * SW_HD:(h0 + 2 * p + 2) * SW_HD] for p in range(npair)],
                                 axis=0).astype(BF16)
            s = lax.dot_general(qs, kbd, NT, preferred_element_type=F32) * (SW_HD ** -0.5)
            es, sink_terms = [], []
            for hh in range(2):
                sh = jnp.where(valid, s[:, hh * W2:(hh + 1) * W2], -jnp.inf)
                sink = sink_ref[h0 + hh]
                for p in range(1, npair):
                    sink = jnp.where(prow == p, sink_ref[h0 + 2 * p + hh], sink)
                m = jnp.maximum(jnp.max(sh, axis=1, keepdims=True), sink)
                es.append(jnp.exp(sh - m).astype(BF16))
                sink_terms.append(jnp.exp(sink - m))
            r = jnp.dot(jnp.concatenate(es, axis=1), vo, preferred_element_type=F32)
            o = r[:, :LANE] / (r[:, LANE:] + jnp.where(low, sink_terms[0], sink_terms[1]))
            for p in range(npair):
                h_ref[bi, :, (h0 + 2 * p) * SW_HD:(h0 + 2 * p + 2) * SW_HD] = (
                    o[p * WINDOW:(p + 1) * WINDOW].astype(h_ref.dtype))

        @pl.when(n == pl.num_programs(1) - 1)
        def _(bi=bi, k_cur=k_cur):
            ko_ref[bi] = _undup(k_cur)


def _swa_prompt(z3, cos, sin_signed, sinks, Bb):
    b, t, _ = z3.shape
    nb = t // WINDOW
    qw = SW_HEADS * SW_HD
    kw = SW_KV_HEADS * SW_HD
    dw = 2 * kw
    qblk, kblk, vblk = EVEN_QA // qw, EVEN_KD // dw, EVEN_VD // dw
    cur = lambda j: (lambda bi, ni: (bi, ni, j))
    prev = lambda j: (lambda bi, ni: (bi, jnp.maximum(ni - 1, 0), j))
    return pl.pallas_call(
        _swa_prompt_kernel,
        grid=(b // Bb, nb),
        in_specs=[pl.BlockSpec(memory_space=pltpu.SMEM),
                  pl.BlockSpec((Bb, WINDOW, qw), cur(qblk)),
                  pl.BlockSpec((Bb, WINDOW, dw), cur(kblk)),
                  pl.BlockSpec((Bb, WINDOW, dw), prev(kblk)),
                  pl.BlockSpec((Bb, WINDOW, dw), cur(vblk)),
                  pl.BlockSpec((Bb, WINDOW, dw), prev(vblk)),
                  pl.BlockSpec((WINDOW, LANE), lambda bi, ni: (ni, 0)),
                  pl.BlockSpec((WINDOW, LANE), lambda bi, ni: (ni, 0)),
                  pl.BlockSpec((WINDOW, LANE), lambda bi, ni: (jnp.maximum(ni - 1, 0), 0)),
                  pl.BlockSpec((WINDOW, LANE), lambda bi, ni: (jnp.maximum(ni - 1, 0), 0))],
        out_specs=[pl.BlockSpec((Bb, WINDOW, qw), cur(0)),
                   pl.BlockSpec((Bb, WINDOW, kw), lambda bi, ni: (bi, 0, 0))],
        out_shape=[jax.ShapeDtypeStruct((b, t, qw), BF16),
                   jax.ShapeDtypeStruct((b, WINDOW, kw), F32)],
        compiler_params=_params(("parallel", "arbitrary")),
        name="swa_prompt",
    )(sinks, z3, z3, z3, z3, z3, cos, sin_signed, cos, sin_signed)


def _swa_sample_kernel(q_ref, k_ref, v_ref, kbuf_ref, vbuf_ref, cos_ref, sin_ref, sink_ref,
                       h_ref, ko_ref, vo_ref, *, T, Bb):
    qw = SW_HEADS * SW_HD
    kw = SW_KV_HEADS * SW_HD
    cos = cos_ref[...]
    sin = sin_ref[...]
    q = _rope(q_ref[...].reshape(Bb * T, qw), jnp.concatenate([cos] * (qw // LANE), axis=1),
              jnp.concatenate([sin] * (qw // LANE), axis=1), qw).reshape(Bb, T, qw)
    k_new = _rope(_undup(k_ref[...]).reshape(Bb * T, kw), cos, sin, kw).reshape(Bb, T, kw)
    kk = jnp.concatenate([kbuf_ref[...], k_new], axis=1)
    vv = jnp.concatenate([vbuf_ref[...], _undup(v_ref[...])], axis=1)
    ko_ref[...] = kk[:, T:, :]
    vo_ref[...] = vv[:, T:, :]
    rows = SW_GQ * T
    i = lax.broadcasted_iota(jnp.int32, (rows, WINDOW + T), 0) % T
    j = lax.broadcasted_iota(jnp.int32, (rows, WINDOW + T), 1)
    diff = WINDOW + i - j
    valid = (diff >= 0) & (diff < WINDOW)
    for kv in range(SW_KV_HEADS):
        qs = jnp.concatenate([q[:, :, (kv * SW_GQ + g) * SW_HD:(kv * SW_GQ + g + 1) * SW_HD]
                              for g in range(SW_GQ)], axis=1).astype(BF16)
        kh = kk[:, :, kv * SW_HD:(kv + 1) * SW_HD].astype(BF16)
        vh = vv[:, :, kv * SW_HD:(kv + 1) * SW_HD].astype(BF16)
        s = jnp.einsum('bqd,bkd->bqk', qs, kh, preferred_element_type=F32) * (SW_HD ** -0.5)
        s = jnp.where(valid, s, -jnp.inf)
        sink = sink_ref[kv * rows:(kv + 1) * rows, :]
        m = jnp.maximum(jnp.max(s, axis=-1, keepdims=True), sink)
        e = jnp.exp(s - m)
        p = e / (jnp.sum(e, axis=-1, keepdims=True) + jnp.exp(sink - m))
        o = jnp.einsum('bqk,bkd->bqd', p.astype(BF16), vh, preferred_element_type=F32)
        for g in range(SW_GQ):
            hh = kv * SW_GQ + g
            h_ref[:, :, hh * SW_HD:(hh + 1) * SW_HD] = o[:, g * T:(g + 1) * T, :].astype(h_ref.dtype)


def _swa_sample(z3, kbuf, vbuf, cos, sin_signed, sinks, Bb):
    b, t, _ = z3.shape
    qw = SW_HEADS * SW_HD
    kw = SW_KV_HEADS * SW_HD
    dw = 2 * kw
    qblk, kblk, vblk = EVEN_QA // qw, EVEN_KD // dw, EVEN_VD // dw
    col = lambda j: (lambda bi: (bi, 0, j))
    const2 = lambda bi: (0, 0)
    sink_col = jnp.repeat(sinks, t).reshape(SW_HEADS * t, 1)
    return pl.pallas_call(
        functools.partial(_swa_sample_kernel, T=t, Bb=Bb),
        grid=(b // Bb,),
        in_specs=[pl.BlockSpec((Bb, t, qw), col(qblk)),
                  pl.BlockSpec((Bb, t, dw), col(kblk)),
                  pl.BlockSpec((Bb, t, dw), col(vblk)),
                  pl.BlockSpec((Bb, WINDOW, kw), col(0)),
                  pl.BlockSpec((Bb, WINDOW, kw), col(0)),
                  pl.BlockSpec((Bb * t, LANE), const2),
                  pl.BlockSpec((Bb * t, LANE), const2),
                  pl.BlockSpec((SW_HEADS * t, 1), const2)],
        out_specs=[pl.BlockSpec((Bb, t, qw), col(0)),
                   pl.BlockSpec((Bb, WINDOW, kw), col(0)),
                   pl.BlockSpec((Bb, WINDOW, kw), col(0))],
        out_shape=[jax.ShapeDtypeStruct((b, t, qw), BF16),
                   jax.ShapeDtypeStruct((b, WINDOW, kw), F32),
                   jax.ShapeDtypeStruct((b, WINDOW, kw), F32)],
        compiler_params=_params(("parallel",)),
        name="swa_sample",
    )(z3, z3, z3, kbuf, vbuf, jnp.tile(cos, (Bb, 1)), jnp.tile(sin_signed, (Bb, 1)), sink_col)


def _gla_tables(S, G):
    nlev = int(math.log2(S))
    assert 2 ** nlev == S
    L = G * S
    mstack = np.zeros((nlev * L, L), np.float32)
    masks = np.zeros((nlev + 1, L, L), np.float32)
    masks[0] = np.eye(L)
    for l in range(1, nlev + 1):
        n = 2 ** l
        for r in range(L):
            blk, pos = divmod(r, n)
            m = blk * n + n // 2 - 1
            if pos >= n // 2:
                mstack[(l - 1) * L + r, m + 1:r + 1] = 1.0
                masks[l, r, blk * n:blk * n + n // 2] = 1.0
            else:
                mstack[(l - 1) * L + r, r + 1:m + 1] = 1.0
    masks = np.tile(masks, (1, 1, GLA_HEADS))
    tri = np.kron(np.eye(G), np.tril(np.ones((S, S)))).astype(np.float32)
    return nlev, jnp.asarray(tri, BF16), jnp.asarray(mstack, BF16), jnp.asarray(masks)


def _gla_consts(L):
    kw = GLA_HEADS * GLA_DK
    vw = GLA_HEADS * GLA_DV
    hk = np.zeros((GLA_HEADS * L, kw), np.float32)
    for h in range(GLA_HEADS):
        hk[h * L:(h + 1) * L, h * GLA_DK:(h + 1) * GLA_DK] = 1.0
    hs = np.zeros((vw, kw), np.float32)
    for h in range(GLA_HEADS):
        hs[h * GLA_DV:(h + 1) * GLA_DV, h * GLA_DK:(h + 1) * GLA_DK] = 1.0
    return (jnp.asarray(hk, BF16), jnp.asarray(hs), jnp.asarray(np.eye(GLA_DV), BF16), jnp.asarray(np.eye(kw), BF16))


def _gla_kernel(q_ref, k_ref, v_ref, r_ref, glr_ref, wg_ref, bg_ref, gn_ref, tri_ref, mstack_ref,
                masks_ref, hk_ref, hs_ref, eyev_ref, eyek_ref, s0_ref,
                o_ref, so_ref, st_s, *, L, nlev, Bb, nsub, G):
    c = pl.program_id(1)
    H, DK, DV = GLA_HEADS, GLA_DK, GLA_DV

    @pl.when(c == 0)
    def _():
        for bi in range(Bb * G):
            for h in range(H):
                pieces = ([jnp.zeros((h * DK, DV), F32)] if h else []) + [s0_ref[bi, h]]
                pieces += [jnp.zeros(((H - 1 - h) * DK, DV), F32)] if h < H - 1 else []
                padded = jnp.concatenate(pieces, axis=0)
                st_s[bi, h * DV:(h + 1) * DV, :] = _dot_sel(eyev_ref[...], padded, NT)

    for sub in range(nsub):
        for bi in range(Bb):
            _gla_one(bi, sub * L, q_ref, k_ref, v_ref, r_ref, glr_ref, wg_ref, bg_ref, gn_ref, tri_ref, mstack_ref,
                     masks_ref, hk_ref, hs_ref, o_ref, st_s, L, nlev, G)

    @pl.when(c == pl.num_programs(1) - 1)
    def _():
        for bi in range(Bb * G):
            for h in range(H):
                blk = st_s[bi, h * DV:(h + 1) * DV, :]
                so_ref[bi, h] = _dot_sel(eyek_ref[h * DK:(h + 1) * DK, :], blk, NT)


def _gla_one(bi, r0, q_ref, k_ref, v_ref, r_ref, glr_ref, wg_ref, bg_ref, gn_ref, tri_ref, mstack_ref,
             masks_ref, hk_ref, hs_ref, o_ref, st_s, L, nlev, G):
    H, DK, DV = GLA_HEADS, GLA_DK, GLA_DV
    packed_rows = L % (2 * SUBLANE) == 0
    rw = pl.ds(r0, L)
    q = q_ref[bi, rw] * (DK ** -0.5)
    k = k_ref[bi, rw]
    kwid = H * DK
    g_hi, g_lo = _pieces(glr_ref[bi, rw], 2)
    pre2 = jnp.dot(g_hi.astype(BF16), wg_ref[...], preferred_element_type=F32)
    pre = (pre2[:, :kwid] + pre2[:, kwid:] + jnp.dot(g_lo.astype(BF16), wg_ref[:, :kwid], preferred_element_type=F32)
           + bg_ref[...])
    lg = jax.nn.log_sigmoid(pre) * (1.0 / GLA_TAU)
    Gc = _dot_sel(tri_ref[...], lg)
    E = jnp.exp(_dot_sel(mstack_ref[...], lg, pieces=2))
    hk = hk_ref[...]
    tok = lax.broadcasted_iota(jnp.int32, (L, kwid), 0)

    att = None
    for l in range(nlev + 1):
        if l == 0:
            ql, kl = q, k
        else:
            e_l = E[(l - 1) * L:l * L]
            upper = ((tok >> (l - 1)) & 1) == 1
            ql = jnp.where(upper, q * e_l, 0.0)
            kl = jnp.where(upper, 0.0, k * e_l)
        if packed_rows:
            kst = jnp.concatenate([kl.astype(BF16)] * H, axis=0) * hk
        else:
            kst = (jnp.concatenate([kl] * H, axis=0) * hk.astype(F32)).astype(BF16)
        part = lax.dot_general(ql.astype(BF16), kst, NT, preferred_element_type=F32)
        part = jnp.where(masks_ref[l] > 0.5, part, 0.0)
        att = part if att is None else att + part

    v = v_ref[bi, rw]
    vsrc = v.astype(BF16) if packed_rows else v
    zero = jnp.zeros((L, DV), vsrc.dtype)
    vbd = jnp.concatenate(
        [jnp.concatenate([zero] * h + [vsrc[:, h * DV:(h + 1) * DV]] + [zero] * (H - 1 - h), axis=1)
         for h in range(H)], axis=0).astype(BF16)
    S = L // G
    seg = lambda a, g: a[g * S:(g + 1) * S]
    qg = q * jnp.exp(Gc)
    o = jnp.dot(att.astype(BF16), vbd, preferred_element_type=F32) + jnp.concatenate(
        [lax.dot_general(seg(qg, g).astype(BF16), st_s[bi * G + g].astype(BF16), NT, preferred_element_type=F32)
         for g in range(G)], axis=0)
    for h in range(H):
        vs = slice(h * DV, (h + 1) * DV)
        oh = o[:, vs]
        oh = oh * lax.rsqrt(jnp.mean(oh * oh, axis=-1, keepdims=True) + EPS)
        oh = oh * gn_ref[:, vs] * jax.nn.silu(r_ref[bi, rw, vs])
        o_ref[bi, rw, vs] = oh.astype(o_ref.dtype)
    for g in range(G):
        g_seg = seg(Gc, g)
        g_end = g_seg[S - 1:S, :]
        k_end = (seg(k, g) * jnp.exp(g_end - g_seg)).astype(BF16)
        upd = lax.dot_general(seg(v, g).astype(BF16), k_end, TN, preferred_element_type=F32)
        st_s[bi * G + g] = st_s[bi * G + g] * jnp.exp(g_end) + jnp.where(hs_ref[...] > 0.5, upd, 0.0)


def _gla(z3, wg, bg, gn, s0, S, G, Bb, nsub):
    nseq, t_seq, width = z3.shape
    if G > 1:
        assert t_seq == S and nsub == 1
        z3 = z3.reshape(nseq // G, G * S, width)
    b, t, _ = z3.shape
    L = G * S
    nc = t // (L * nsub)
    nlev, tri, mstack, masks = _gla_tables(S, G)
    hk, hs, eyev, eyek = _gla_consts(L)
    qk_w = GLA_HEADS * GLA_DK
    v_w = GLA_HEADS * GLA_DV
    col = lambda j: (lambda bi, ci: (bi, ci, j))
    const2 = lambda bi, ci: (0, 0)
    const3 = lambda bi, ci: (0, 0, 0)
    o, s_new = pl.pallas_call(
        functools.partial(_gla_kernel, L=L, nlev=nlev, Bb=Bb, nsub=nsub, G=G),
        grid=(b // Bb, nc),
        in_specs=[pl.BlockSpec((Bb, nsub * L, qk_w), col(0)),
                  pl.BlockSpec((Bb, nsub * L, qk_w), col(1)),
                  pl.BlockSpec((Bb, nsub * L, v_w), col(1)),
                  pl.BlockSpec((Bb, nsub * L, v_w), col(2)),
                  pl.BlockSpec((Bb, nsub * L, LANE), col(2048 // LANE)),
                  pl.BlockSpec((LANE, 2 * qk_w), const2),
                  pl.BlockSpec((1, qk_w), const2),
                  pl.BlockSpec((1, v_w), const2),
                  pl.BlockSpec((L, L), const2),
                  pl.BlockSpec(mstack.shape, const2),
                  pl.BlockSpec(masks.shape, const3),
                  pl.BlockSpec(hk.shape, const2),
                  pl.BlockSpec(hs.shape, const2),
                  pl.BlockSpec(eyev.shape, const2),
                  pl.BlockSpec(eyek.shape, const2),
                  pl.BlockSpec((Bb * G, GLA_HEADS, GLA_DK, GLA_DV), lambda bi, ci: (bi, 0, 0, 0))],
        out_specs=[pl.BlockSpec((Bb, nsub * L, v_w), col(0)),
                   pl.BlockSpec((Bb * G, GLA_HEADS, GLA_DK, GLA_DV), lambda bi, ci: (bi, 0, 0, 0))],
        out_shape=[jax.ShapeDtypeStruct((b, t, v_w), BF16),
                   jax.ShapeDtypeStruct((nseq, GLA_HEADS, GLA_DK, GLA_DV), F32)],
        scratch_shapes=[pltpu.VMEM((Bb * G, v_w, qk_w), F32)],
        compiler_params=_params(("parallel", "arbitrary")),
        name="gla",
    )(z3, z3, z3, z3, z3, wg, bg, gn, tri, mstack, masks, hk, hs, eyev, eyek, s0)
    return o.reshape(nseq, t_seq, v_w), s_new


def _s5_kernel(u_ref, bbre_ref, bbim_ref, cre_ref, cim_ref, d_ref, wglu_ref, bglu_ref, pre_ref, pim_ref,
               x0re_ref, x0im_ref,
               o_ref, xore_ref, xoim_ref,
               xr_s, xi_s, cr_s, ci_s, *, rows, sequential):
    c = pl.program_id(1)
    lead = rows // SUBLANE if sequential else SUBLANE
    inner = rows // lead
    cw = S5_LANES // 4 if inner == SUBLANE else LANE
    u = jnp.swapaxes(u_ref[0].reshape(inner, lead, S5_WIDTH), 0, 1).reshape(rows, S5_WIDTH)
    ub = u.astype(BF16)
    hw, hl = S5_WIDTH // 2, S5_LANES // 2
    for half in range(2):
        uh = ub[:, half * hw:(half + 1) * hw]
        ls = slice(half * hl, (half + 1) * hl)
        xr_s[:, :, ls] = jnp.dot(uh, bbre_ref[half * hw:(half + 1) * hw, ls],
                                 preferred_element_type=F32).reshape(lead, inner, hl)
        xi_s[:, :, ls] = jnp.dot(uh, bbim_ref[half * hw:(half + 1) * hw, ls],
                                 preferred_element_type=F32).reshape(lead, inner, hl)

    for c0 in range(0, S5_LANES, cw):
        ar = jnp.broadcast_to(pre_ref[0, :, c0:c0 + cw], (inner, cw))
        ai = jnp.broadcast_to(pim_ref[0, :, c0:c0 + cw], (inner, cw))

        def body(j, carry, c0=c0, ar=ar, ai=ai):
            xr, xi = carry
            nr = ar * xr - ai * xi + xr_s[j, :, c0:c0 + cw]
            ni = ar * xi + ai * xr + xi_s[j, :, c0:c0 + cw]
            xr_s[j, :, c0:c0 + cw] = nr
            xi_s[j, :, c0:c0 + cw] = ni
            return nr, ni

        if sequential:
            init = (jnp.zeros((inner, cw), F32), jnp.zeros((inner, cw), F32))
        else:
            init = (x0re_ref[0, :, c0:c0 + cw], x0im_ref[0, :, c0:c0 + cw])
        lax.fori_loop(0, lead, body, init, unroll=True)

    if sequential:
        @pl.when(c == 0)
        def _():
            cr_s[...] = x0re_ref[0]
            ci_s[...] = x0im_ref[0]

        end_r = xr_s[lead - 1]
        end_i = xi_s[lead - 1]
        anr = pre_ref[lead - 1]
        ani = pim_ref[lead - 1]
        ent_r, ent_i = [cr_s[...]], [ci_s[...]]
        for s in range(SUBLANE):
            pr, pi = ent_r[-1], ent_i[-1]
            ent_r.append(end_r[s:s + 1] + anr * pr - ani * pi)
            ent_i.append(end_i[s:s + 1] + anr * pi + ani * pr)
        cr_s[...] = ent_r[SUBLANE]
        ci_s[...] = ent_i[SUBLANE]
        er = jnp.concatenate(ent_r[:SUBLANE], axis=0)[None]
        ei = jnp.concatenate(ent_i[:SUBLANE], axis=0)[None]
        pr3, pi3 = pre_ref[...], pim_ref[...]
        xr = xr_s[...] + pr3 * er - pi3 * ei
        xi = xi_s[...] + pr3 * ei + pi3 * er

        @pl.when(c == pl.num_programs(1) - 1)
        def _():
            xore_ref[0] = cr_s[...]
            xoim_ref[0] = ci_s[...]
    else:
        xr = xr_s[...]
        xi = xi_s[...]
        xore_ref[0] = xr[lead - 1]
        xoim_ref[0] = xi[lead - 1]

    xrb = xr.reshape(rows, S5_LANES).astype(BF16)
    xib = xi.reshape(rows, S5_LANES).astype(BF16)
    y = jnp.concatenate(
        [jnp.dot(xrb[:, h * hl:(h + 1) * hl], cre_ref[h * hl:(h + 1) * hl, h * hw:(h + 1) * hw],
                 preferred_element_type=F32)
         - jnp.dot(xib[:, h * hl:(h + 1) * hl], cim_ref[h * hl:(h + 1) * hl, h * hw:(h + 1) * hw],
                   preferred_element_type=F32) for h in range(2)], axis=1) + d_ref[...] * u
    zz = jax.nn.gelu(y)
    gate = jnp.dot(zz.astype(BF16), wglu_ref[...], preferred_element_type=F32) + bglu_ref[...]
    out = zz * jax.nn.sigmoid(gate)
    o_ref[0] = jnp.swapaxes(out.reshape(lead, inner, S5_WIDTH), 0, 1).reshape(rows, S5_WIDTH).astype(o_ref.dtype)


def _s5(z3, tabs, x0re, x0im, rows, sequential):
    b, t, _ = z3.shape
    nblk = t // rows
    groups = rows // SUBLANE
    ublk = 1536 // S5_WIDTH
    const2 = lambda bi, ci: (0, 0)
    const3 = lambda bi, ci: (0, 0, 0)
    lead = groups if sequential else SUBLANE
    kk = jnp.arange(1, lead + 1, dtype=F32).reshape(lead, 1, 1)
    mag = jnp.exp(kk * tabs['la_re'][None])
    p_re = mag * jnp.cos(kk * tabs['la_im'][None])
    p_im = mag * jnp.sin(kk * tabs['la_im'][None])
    if sequential:
        st_spec = pl.BlockSpec((1, 1, S5_LANES), lambda bi, ci: (bi, 0, 0))
        st_shape = jax.ShapeDtypeStruct((b, 1, S5_LANES), F32)
    else:
        st_spec = pl.BlockSpec((1, groups, S5_LANES), lambda bi, ci: (bi, ci, 0))
        st_shape = jax.ShapeDtypeStruct(x0re.shape, F32)
    return pl.pallas_call(
        functools.partial(_s5_kernel, rows=rows, sequential=sequential),
        grid=(b, nblk),
        in_specs=[pl.BlockSpec((1, rows, S5_WIDTH), lambda bi, ci: (bi, ci, ublk)),
                  pl.BlockSpec((S5_WIDTH, S5_LANES), const2),
                  pl.BlockSpec((S5_WIDTH, S5_LANES), const2),
                  pl.BlockSpec((S5_LANES, S5_WIDTH), const2),
                  pl.BlockSpec((S5_LANES, S5_WIDTH), const2),
                  pl.BlockSpec((1, S5_WIDTH), const2),
                  pl.BlockSpec((S5_WIDTH, S5_WIDTH), const2),
                  pl.BlockSpec((1, S5_WIDTH), const2),
                  pl.BlockSpec((lead, 1, S5_LANES), const3),
                  pl.BlockSpec((lead, 1, S5_LANES), const3),
                  st_spec, st_spec],
        out_specs=[pl.BlockSpec((1, rows, S5_WIDTH), lambda bi, ci: (bi, ci, 0)),
                   st_spec, st_spec],
        out_shape=[jax.ShapeDtypeStruct((b, t, S5_WIDTH), BF16), st_shape, st_shape],
        scratch_shapes=[pltpu.VMEM((lead, rows // lead, S5_LANES), F32),
                        pltpu.VMEM((lead, rows // lead, S5_LANES), F32),
                        pltpu.VMEM((1, S5_LANES), F32),
                        pltpu.VMEM((1, S5_LANES), F32)],
        compiler_params=_params(("parallel", "arbitrary")),
        name="s5",
    )(z3, tabs['bbre'], tabs['bbim'], tabs['cre'], tabs['cim'], tabs['d'], tabs['wglu'], tabs['bglu'],
      p_re, p_im, x0re, x0im)


def _s5_tables(a_re, a_im, log_dt, b_re, b_im, c_re, c_im, d, w_glu, b_glu):
    lam_re = a_re.astype(F32)
    lam_im = a_im.astype(F32)
    dt = jnp.exp(log_dt.astype(F32))[:, None]
    mag = jnp.exp(lam_re * dt)
    ang = lam_im * dt
    ab_re = mag * jnp.cos(ang)
    ab_im = mag * jnp.sin(ang)
    den = lam_re * lam_re + lam_im * lam_im
    co_re = ((ab_re - 1.0) * lam_re + ab_im * lam_im) / den
    co_im = (ab_im * lam_re - (ab_re - 1.0) * lam_im) / den
    b_re = b_re.astype(F32)
    b_im = b_im.astype(F32)
    bb_re = co_re[..., None] * b_re - co_im[..., None] * b_im
    bb_im = co_re[..., None] * b_im + co_im[..., None] * b_re
    eye = jnp.eye(S5_GROUPS, dtype=F32)
    blockdiag_in = lambda bb: jnp.einsum('gph,gk->ghkp', bb, eye).reshape(S5_WIDTH, S5_LANES)
    blockdiag_out = lambda cc: jnp.einsum('ghp,gk->gpkh', cc.astype(F32), eye).reshape(S5_LANES, S5_WIDTH)
    return {
        'bbre': blockdiag_in(bb_re).astype(BF16), 'bbim': blockdiag_in(bb_im).astype(BF16),
        'cre': blockdiag_out(c_re).astype(BF16), 'cim': blockdiag_out(c_im).astype(BF16),
        'd': d.astype(F32).reshape(1, S5_WIDTH),
        'wglu': w_glu.astype(BF16), 'bglu': b_glu.astype(F32).reshape(1, S5_WIDTH),
        'la_re': (lam_re * dt).reshape(1, S5_LANES), 'la_im': ang.reshape(1, S5_LANES),
    }


def _rmsnorm_val(x, g):
    return x * lax.rsqrt(jnp.mean(x * x, axis=-1, keepdims=True) + EPS) * g


def _ffn_kernel(r_ref, a1_ref, a2_ref, w1_ref, w2_ref, g_ref, wup_ref, cw_ref, cb_ref, wd_ref, st_ref, gf_ref,
                o_ref, so_ref, halo_s, *, tm, tf, long_seq, final_norm):
    i = pl.program_id(1)
    n = tm // SUBLANE
    d = r_ref.shape[-1]
    x = (r_ref[0] + jnp.dot(a1_ref[0], w1_ref[...], preferred_element_type=F32)
         + jnp.dot(a2_ref[0], w2_ref[...], preferred_element_type=F32))
    hn = _rmsnorm_val(x, g_ref[...])
    if long_seq:
        lead = n

        @pl.when(i == 0)
        def _():
            halo_s[...] = st_ref[0]
    else:
        lead = SUBLANE
    hnp = jnp.swapaxes(hn.reshape(tm // lead, lead, d), 0, 1).reshape(tm, d).astype(BF16)
    inner = tm // lead
    sub = lax.broadcasted_iota(jnp.int32, (SUBLANE, tf), 0)

    acc = jnp.zeros((tm, d), F32)
    for c in range(D_FF // tf):
        conv = []
        for half in range(2):
            c0 = half * D_FF + c * tf
            u = jnp.dot(hnp, wup_ref[:, c0:c0 + tf], preferred_element_type=F32).reshape(lead, inner, tf)
            if long_seq:
                um1 = jnp.where(sub == 0, halo_s[1:2, c0:c0 + tf], pltpu.roll(u[lead - 1], 1, axis=0))
                um2 = jnp.where(sub == 0, halo_s[0:1, c0:c0 + tf], pltpu.roll(u[lead - 2], 1, axis=0))
                halo_s[0:1, c0:c0 + tf] = u[lead - 2][SUBLANE - 1:SUBLANE]
                halo_s[1:2, c0:c0 + tf] = u[lead - 1][SUBLANE - 1:SUBLANE]
            else:
                um2 = st_ref[0, :, c0:c0 + tf]
                um1 = st_ref[1, :, c0:c0 + tf]
                so_ref[0, :, c0:c0 + tf] = u[lead - 2]
                so_ref[1, :, c0:c0 + tf] = u[lead - 1]
            ext = jnp.concatenate([um2[None], um1[None], u], axis=0)
            cv = cb_ref[:, c0:c0 + tf]
            for j in range(CONV_W):
                cv = cv + ext[j:j + lead] * cw_ref[j:j + 1, c0:c0 + tf]
            conv.append(cv)
        act = (jax.nn.silu(conv[1]) * conv[0]).reshape(tm, tf).astype(BF16)
        acc = acc + jnp.dot(act, wd_ref[c * tf:(c + 1) * tf, :], preferred_element_type=F32)

    out = x + jnp.swapaxes(acc.reshape(lead, inner, d), 0, 1).reshape(tm, d)
    if final_norm:
        out = _rmsnorm_val(out, gf_ref[...])
    o_ref[0] = out

    if long_seq:
        @pl.when(i == pl.num_programs(1) - 1)
        def _():
            so_ref[0] = halo_s[...]


def _ffn(res3, a1, a2, w1, w2, g, wup, cw, cb, wd, state, layer, gfinal, tm, tf, long_seq, final_norm):
    b, t, d = res3.shape
    nt = t // tm
    resident = dict(pipeline_mode=pl.Buffered(1))
    const2 = lambda bi, i: (0, 0)
    row_blk = lambda w: pl.BlockSpec((1, tm, w), lambda bi, i: (bi, i, 0))
    per_layer = lambda shape, **kw: pl.BlockSpec((None,) + shape, lambda bi, i: (layer, 0, 0), **kw)
    if long_seq:
        st_in = st_out = pl.BlockSpec((1, CONV_W - 1, 2 * D_FF), lambda bi, i: (bi, 0, 0))
        st_shape = state.shape
    else:
        nseq = tm // SUBLANE
        st_in = pl.BlockSpec((None, CONV_W - 1, nseq, 2 * D_FF), lambda bi, i: (layer, 0, i, 0))
        st_out = pl.BlockSpec((CONV_W - 1, nseq, 2 * D_FF), lambda bi, i: (0, i, 0))
        st_shape = state.shape[1:]
    return pl.pallas_call(
        functools.partial(_ffn_kernel, tm=tm, tf=tf, long_seq=long_seq, final_norm=final_norm),
        grid=(b, nt),
        in_specs=[row_blk(d), row_blk(a1.shape[-1]), row_blk(a2.shape[-1]),
                  pl.BlockSpec(w1.shape, const2, **resident),
                  pl.BlockSpec(w2.shape, const2, **resident),
                  per_layer((1, d)),
                  per_layer((d, 2 * D_FF), **resident),
                  per_layer((CONV_W, 2 * D_FF)),
                  per_layer((1, 2 * D_FF)),
                  per_layer((D_FF, d), **resident),
                  st_in,
                  pl.BlockSpec((1, d), const2)],
        out_specs=[row_blk(d), st_out],
        out_shape=[jax.ShapeDtypeStruct((b, t, d), F32), jax.ShapeDtypeStruct(st_shape, F32)],
        scratch_shapes=[pltpu.VMEM((CONV_W - 1, 2 * D_FF), F32)],
        compiler_params=_params(("parallel", "arbitrary")),
        name="ffn",
    )(res3, a1, a2, w1, w2, g.reshape(-1, 1, d), wup, cw, cb.reshape(-1, 1, 2 * D_FF), wd, state, gfinal)


def _rope_tables(pos):
    half = SW_HD // 2
    inv = ROPE_THETA ** (-jnp.arange(half, dtype=F32) / half)
    ang = pos.astype(F32)[:, None] * inv[None, :]
    cos = jnp.cos(ang)
    sin = jnp.sin(ang)
    reps = LANE // SW_HD
    cos_t = jnp.tile(jnp.concatenate([cos, cos], axis=1), (1, reps))
    sin_t = jnp.tile(jnp.concatenate([-sin, sin], axis=1), (1, reps))
    return cos_t, sin_t


def _prep_weights(W):
    P = {}
    we = W['w_in_even']
    parts = jnp.split(we, [int(c) for c in np.cumsum((512, 512, 512, 512, 8, 512, 128))], axis=1)
    q_m, k_m, v_m, o_m, g_m, q_a, k_a, v_a = parts
    dup = lambda w: jnp.concatenate([w[:, :SW_HD], w[:, :SW_HD], w[:, SW_HD:], w[:, SW_HD:]], axis=1)
    pad = jnp.zeros((D_MODEL, LANE - 2 * ML_HEADS), we.dtype)
    P['w_in_even'] = jnp.concatenate([q_m, k_m, v_m, o_m, q_a, dup(k_a), dup(v_a), g_m, pad], axis=1).astype(BF16)
    assert P['w_in_even'].shape[1] == EVEN_PAD
    P['b_gates'] = jnp.pad(W['b_mlstm_gates'].astype(F32), (0, LANE - 2 * ML_HEADS)).reshape(1, LANE)
    wo = W['w_in_odd']
    q, k, v, r, glr, u = jnp.split(wo, [int(c) for c in np.cumsum((256, 256, 512, 512, 16))], axis=1)
    pad = jnp.zeros((D_MODEL, ODD_PAD - 2048 - GLA_RANK), wo.dtype)
    P['w_in_odd'] = jnp.concatenate([q, k, v, r, u, glr, pad], axis=1).astype(BF16)
    wg = jnp.pad(W['w_gla_gate_up'].astype(F32), ((0, LANE - GLA_RANK), (0, 0)))
    wg_hi = wg.astype(BF16)
    P['w_gate_up'] = jnp.concatenate([wg_hi, (wg - wg_hi.astype(F32)).astype(BF16)], axis=1)
    P['b_gate'] = W['b_gla_gate'].astype(F32).reshape(1, -1)
    P['g_gla'] = W['g_gla_norm'].astype(F32).reshape(1, -1)
    hm = ML_HEADS * ML_DV
    P['w_out_even'] = (W['w_out_even'][:hm].astype(BF16), W['w_out_even'][hm:].astype(BF16))
    hg = GLA_HEADS * GLA_DV
    P['w_out_odd'] = (W['w_out_odd'][:hg].astype(BF16), W['w_out_odd'][hg:].astype(BF16))
    P['w_ffn_up'] = W['w_ffn_up'].astype(BF16)
    P['w_ffn_down'] = W['w_ffn_down'].astype(BF16)
    P['s5'] = _s5_tables(W['s5_a_re'], W['s5_a_im'], W['s5_log_dt'], W['s5_b_re'], W['s5_b_im'],
                         W['s5_c_re'], W['s5_c_im'], W['s5_d'], W['w_s5_glu'], W['b_s5_glu'])
    return P


def _trunk(x, pos, st, W, P, is_prompt):
    b, t, d = x.shape
    n = b * t
    L = CHUNK if t % CHUNK == 0 else t
    nsub = next((c for c in (4, 2) if t % (c * L) == 0), 1)
    tm = 512 if n % 512 == 0 else n
    cos, sin_signed = _rope_tables(pos)
    gfinal = W['norm_final'].astype(F32).reshape(1, d)
    new = {}
    conv_out = []
    h = x.reshape(n, d)

    z = _norm_matmul(h, W['norm_mix'][0], P['w_in_even'], tm)
    z3 = z.reshape(b, t, EVEN_PAD)
    pack = dict(G=1, Bb=2 if b % 2 == 0 else 1, nsub=nsub) if (is_prompt or b % 16) else dict(G=8, Bb=2, nsub=1)
    h_m, new['ml_C'], new['ml_n'], m_new = _mlstm(z3, P['b_gates'], st['ml_C'], st['ml_n'], st['ml_m'], L, **pack)
    new['ml_m'] = m_new.reshape(b, ML_HEADS)
    if is_prompt:
        h_a, kb = _swa_prompt(z3, cos, sin_signed, W['sw_sinks'].astype(F32), Bb=2 if b % 2 == 0 else 1)
        vb = jnp.concatenate([z3[:, t - WINDOW:, EVEN_VD:EVEN_VD + SW_HD],
                              z3[:, t - WINDOW:, EVEN_VD + LANE:EVEN_VD + LANE + SW_HD]], axis=-1)
    else:
        h_a, kb, vb = _swa_sample(z3, st['kbuf'].reshape(b, WINDOW, -1), st['vbuf'].reshape(b, WINDOW, -1),
                                  cos, sin_signed, W['sw_sinks'].astype(F32), Bb=8)
    new['kbuf'] = kb.reshape(b, WINDOW, SW_KV_HEADS, SW_HD)
    new['vbuf'] = vb.reshape(b, WINDOW, SW_KV_HEADS, SW_HD)
    h, cb = _ffn_layer(h, h_m, h_a, P['w_out_even'], 0, st, W, P, b, t, gfinal, is_prompt, final_norm=False)
    conv_out.append(cb)

    z = _norm_matmul(h, W['norm_mix'][1], P['w_in_odd'], tm)
    z3 = z.reshape(b, t, ODD_PAD)
    o_g, new['gla'] = _gla(z3, P['w_gate_up'], P['b_gate'], P['g_gla'], st['gla'], L, **pack)
    x0re = st['s5_re'].astype(F32).reshape(b, 1, S5_LANES)
    x0im = st['s5_im'].astype(F32).reshape(b, 1, S5_LANES)
    if is_prompt:
        o_s, xre, xim = _s5(z3, P['s5'], x0re, x0im, rows=256 if t % 256 == 0 else t, sequential=True)
    else:
        o_s, xre, xim = _s5(z3.reshape(1, n, ODD_PAD), P['s5'], x0re.reshape(1, b, S5_LANES),
                            x0im.reshape(1, b, S5_LANES), rows=256 if n % 256 == 0 else n, sequential=False)
    new['s5_re'] = xre.reshape(b, S5_GROUPS, S5_STATE)
    new['s5_im'] = xim.reshape(b, S5_GROUPS, S5_STATE)
    h, cb = _ffn_layer(h, o_g, o_s, P['w_out_odd'], 1, st, W, P, b, t, gfinal, is_prompt, final_norm=True)
    conv_out.append(cb)
    new['conv'] = jnp.stack(conv_out)
    return h.reshape(b, t, d), new


def _ffn_layer(h, a1, a2, w_out, layer, st, W, P, b, t, gfinal, is_prompt, final_norm):
    n, d = h.shape
    rows = t if is_prompt else n
    tm = next((c for c in (512, 256) if rows % c == 0), rows)
    args = (*w_out, W['norm_ffn'], P['w_ffn_up'], W['ffn_conv_w'].astype(F32), W['ffn_conv_b'].astype(F32),
            P['w_ffn_down'])
    if is_prompt:
        shape3 = lambda a: a.reshape(b, t, -1)
        out, new_state = _ffn(shape3(h), shape3(a1), shape3(a2), *args, st['conv'][layer], layer, gfinal, tm,
                              FFN_TF, True, final_norm)
    else:
        assert t == SUBLANE
        shape3 = lambda a: a.reshape(1, n, -1)
        out, new_state = _ffn(shape3(h), shape3(a1), shape3(a2), *args, st['conv_rows'], layer, gfinal, tm, FFN_TF,
                              False, final_norm)
        new_state = jnp.swapaxes(new_state, 0, 1)
    return out.reshape(n, d), new_state


def kernel(x_prompt, x_sample, state_mlstm_C, state_mlstm_n, state_mlstm_m, cache_swa_k, cache_swa_v,
           state_gla, state_s5_re, state_s5_im, state_ffn_conv, norm_mix, norm_ffn, norm_final,
           w_in_even, b_mlstm_gates, sw_sinks, w_out_even, w_in_odd, w_gla_gate_up, b_gla_gate, g_gla_norm,
           s5_a_re, s5_a_im, s5_log_dt, s5_b_re, s5_b_im, s5_c_re, s5_c_im, s5_d, w_s5_glu, b_s5_glu,
           w_out_odd, w_ffn_up, ffn_conv_w, ffn_conv_b, w_ffn_down):
    W = {'norm_mix': norm_mix.astype(F32), 'norm_ffn': norm_ffn.astype(F32), 'norm_final': norm_final,
         'w_in_even': w_in_even, 'b_mlstm_gates': b_mlstm_gates, 'sw_sinks': sw_sinks, 'w_out_even': w_out_even,
         'w_in_odd': w_in_odd, 'w_gla_gate_up': w_gla_gate_up, 'b_gla_gate': b_gla_gate, 'g_gla_norm': g_gla_norm,
         's5_a_re': s5_a_re, 's5_a_im': s5_a_im, 's5_log_dt': s5_log_dt, 's5_b_re': s5_b_re, 's5_b_im': s5_b_im,
         's5_c_re': s5_c_re, 's5_c_im': s5_c_im, 's5_d': s5_d, 'w_s5_glu': w_s5_glu, 'b_s5_glu': b_s5_glu,
         'w_out_odd': w_out_odd, 'w_ffn_up': w_ffn_up, 'ffn_conv_w': ffn_conv_w, 'ffn_conv_b': ffn_conv_b,
         'w_ffn_down': w_ffn_down}
    P = _prep_weights(W)
    bp, tp = x_prompt.shape[:2]
    st_prompt = {'ml_C': jnp.zeros((bp, ML_HEADS, ML_DK, ML_DV), F32),
                 'ml_n': jnp.zeros((bp, ML_HEADS, ML_DK), F32),
                 'ml_m': jnp.zeros((bp, ML_HEADS), F32),
                 'gla': jnp.zeros((bp, GLA_HEADS, GLA_DK, GLA_DV), F32),
                 's5_re': jnp.zeros((bp, S5_GROUPS, S5_STATE), F32),
                 's5_im': jnp.zeros((bp, S5_GROUPS, S5_STATE), F32),
                 'conv': jnp.zeros((2, bp, CONV_W - 1, 2 * D_FF), F32)}
    st_sample = {'ml_C': state_mlstm_C.astype(F32), 'ml_n': state_mlstm_n.astype(F32),
                 'ml_m': state_mlstm_m.astype(F32), 'kbuf': cache_swa_k.astype(F32),
                 'vbuf': cache_swa_v.astype(F32), 'gla': state_gla.astype(F32),
                 's5_re': state_s5_re, 's5_im': state_s5_im,
                 'conv_rows': jnp.swapaxes(state_ffn_conv.astype(F32), 1, 2)}
    past_len = 16384
    y_p, np_ = _trunk(x_prompt.astype(F32), jnp.arange(tp), st_prompt, W, P, True)
    y_s, ns_ = _trunk(x_sample.astype(F32), past_len + jnp.arange(x_sample.shape[1]), st_sample, W, P, False)
    return (y_p, y_s,
            np_['ml_C'], ns_['ml_C'], np_['ml_n'], ns_['ml_n'], np_['ml_m'], ns_['ml_m'],
            np_['kbuf'], ns_['kbuf'], np_['vbuf'], ns_['vbuf'], np_['gla'], ns_['gla'],
            np_['s5_re'], ns_['s5_re'], np_['s5_im'], ns_['s5_im'], np_['conv'], ns_['conv'])
```

```python
import functools
import math

import numpy as np
import jax
import jax.numpy as jnp
from jax import lax
from jax.experimental import pallas as pl
from jax.experimental.pallas import tpu as pltpu

F32 = jnp.float32
BF16 = jnp.bfloat16
HI = lax.Precision.HIGHEST
NT = (((1,), (1,)), ((), ()))
TN = (((0,), (0,)), ((), ()))

D_MODEL = 1024
ML_HEADS, ML_DK, ML_DV = 4, 128, 128
SW_HEADS, SW_KV_HEADS, SW_HD, WINDOW = 8, 2, 64, 128
SW_GQ = SW_HEADS // SW_KV_HEADS
ROPE_THETA = 10000.0
GLA_HEADS, GLA_DK, GLA_DV, GLA_RANK, GLA_TAU = 4, 64, 128, 16, 16.0
S5_WIDTH, S5_GROUP, S5_STATE = 512, 16, 64
S5_GROUPS = S5_WIDTH // S5_GROUP
S5_LANES = S5_GROUPS * S5_STATE
D_FF = 2816
FFN_TF = 2816
CONV_W = 3
CHUNK = 64
EPS = 1e-6

LANE = 128
SUBLANE = 8
VMEM_LIMIT = 56 * 1024 * 1024

EVEN_QA, EVEN_KD, EVEN_VD, EVEN_PAD = 2048, 2560, 2816, 3200
ODD_PAD = 2176


def _params(sem):
    return pltpu.CompilerParams(dimension_semantics=sem, vmem_limit_bytes=VMEM_LIMIT)


def _norm_matmul_kernel(x_ref, g_ref, w_ref, o_ref):
    x = x_ref[...]
    ms = jnp.mean(x * x, axis=-1, keepdims=True)
    hn = (x * lax.rsqrt(ms + EPS) * g_ref[...]).astype(BF16)
    o_ref[...] = jnp.dot(hn, w_ref[...], preferred_element_type=F32)


def _norm_matmul(x, g, w, tm):
    n, d = x.shape
    c = w.shape[1]
    return pl.pallas_call(
        _norm_matmul_kernel,
        grid=(n // tm,),
        in_specs=[pl.BlockSpec((tm, d), lambda i: (i, 0)),
                  pl.BlockSpec((1, d), lambda i: (0, 0)),
                  pl.BlockSpec((d, c), lambda i: (0, 0))],
        out_specs=pl.BlockSpec((tm, c), lambda i: (i, 0)),
        out_shape=jax.ShapeDtypeStruct((n, c), F32),
        compiler_params=_params(("parallel",)),
        name="norm_matmul",
    )(x, g.reshape(1, d), w)


def _pieces(x, n):
    out, r = [], x
    for _ in range(n):
        p = r.astype(BF16).astype(F32)
        out.append(p)
        r = r - p
    return out


def _dot_sel(m_bf16, x, dims=None, pieces=3):
    ps = _pieces(x, pieces)
    axis = 1 if dims is None else 0
    w = x.shape[axis]
    if w % LANE:
        f = (lambda p: jnp.dot(m_bf16, p, preferred_element_type=F32)) if dims is None else (
            lambda p: lax.dot_general(m_bf16, p, dims, preferred_element_type=F32))
        return sum(f(p.astype(BF16)) for p in ps)
    stacked = jnp.concatenate(ps, axis=axis).astype(BF16)
    if dims is None:
        r = jnp.dot(m_bf16, stacked, preferred_element_type=F32)
    else:
        r = lax.dot_general(m_bf16, stacked, dims, preferred_element_type=F32)
    return sum(r[:, i * w:(i + 1) * w] for i in range(pieces))


def _sel_right(x, e_bf16, pieces=3):
    rows = x.shape[0]
    r = jnp.dot(jnp.concatenate(_pieces(x, pieces), axis=0).astype(BF16), e_bf16, preferred_element_type=F32)
    return sum(r[i * rows:(i + 1) * rows] for i in range(pieces))


def _mlstm_kernel(q_ref, k_ref, v_ref, og_ref, gz_ref, bias_ref, tri_ref, ones_ref, expc_ref, expv_ref, eyet_ref,
                  seg_ref, segv_ref, hmk_ref, c0_ref, n0_ref, m0_ref,
                  h_ref, co_ref, no_ref, mo_ref,
                  c_s, n_s, m_s, *, L, Bb, nsub, G):
    c = pl.program_id(1)

    @pl.when(c == 0)
    def _():
        c_s[...] = c0_ref[...]
        n_s[...] = n0_ref[...]
        m_s[...] = m0_ref[...]

    for sub in range(nsub):
        for bi in range(Bb):
            _mlstm_one(bi, sub * L, q_ref, k_ref, v_ref, og_ref, gz_ref, bias_ref, tri_ref, ones_ref, expc_ref,
                       expv_ref, eyet_ref, seg_ref, segv_ref, hmk_ref, h_ref, c_s, n_s, m_s, L, G)

    @pl.when(c == pl.num_programs(1) - 1)
    def _():
        co_ref[...] = c_s[...]
        no_ref[...] = n_s[...]
        mo_ref[...] = m_s[...]


def _mlstm_one(bi, r0, q_ref, k_ref, v_ref, og_ref, gz_ref, bias_ref, tri_ref, ones_ref, expc_ref, expv_ref,
               eyet_ref, seg_ref, segv_ref, hmk_ref, h_ref, c_s, n_s, m_s, L, G):
    H, DK = ML_HEADS, ML_DK
    HL = H * L
    S = L // G
    seg = lambda a, g: a[g * S:(g + 1) * S]
    per_row = lambda rows: rows[0] if G == 1 else jnp.concatenate(
        [jnp.broadcast_to(r, (S, r.shape[1])) for r in rows], axis=0)
    lane = lax.broadcasted_iota(jnp.int32, (L, LANE), 1)
    rowi = lax.broadcasted_iota(jnp.int32, (L, LANE), 0) % S
    keep = lane < H
    heads_only = lambda x: jnp.where(keep[:x.shape[0]], x, 0.0)

    rw = pl.ds(r0, L)
    gates = gz_ref[bi, rw] + bias_ref[...]
    lf = jax.nn.log_sigmoid(gates)
    b = pltpu.roll(_dot_sel(tri_ref[...], lf), LANE - H, axis=1)
    vv = gates - b
    cm = vv
    sh = 1
    while sh < S:
        cm = jnp.maximum(cm, jnp.where(rowi >= sh, pltpu.roll(cm, sh, axis=0), -jnp.inf))
        sh *= 2
    m_prev_g = [m_s[bi * G + g] for g in range(G)]
    m_prev = per_row(m_prev_g)
    mt = b + jnp.maximum(m_prev, cm)
    a_inter = jnp.exp(b + m_prev - mt)

    expv = expv_ref[...]
    m_new_g = [seg(mt, g)[S - 1:S, :] for g in range(G)]
    b_end_g = [seg(b, g)[S - 1:S, :] for g in range(G)]
    w_end = heads_only(jnp.exp(per_row(b_end_g) - b + gates - per_row(m_new_g)))
    decay_g = [heads_only(jnp.exp(b_end_g[g] + m_prev_g[g] - m_new_g[g])) for g in range(G)]
    decay_rows = jnp.concatenate(decay_g + [jnp.zeros((-G % SUBLANE, LANE), F32)] * (G % SUBLANE != 0), axis=0)
    spread_k = _sel_right(jnp.concatenate([heads_only(b - mt), heads_only(vv)], axis=0), expc_ref[...])
    uc = spread_k[:L]
    vr = _dot_sel(ones_ref[...], spread_k[L:] * eyet_ref[...])
    spread_v = _sel_right(jnp.concatenate([heads_only(a_inter), w_end, decay_rows], axis=0), expv, pieces=2)
    ac, wc, dcs = spread_v[:L], spread_v[L:2 * L], spread_v[2 * L:2 * L + G]
    ti = lax.broadcasted_iota(jnp.int32, (L, HL), 0)
    si = lax.broadcasted_iota(jnp.int32, (L, HL), 1) % L
    causal = (si <= ti) & (si // S == ti // S)
    w = jnp.exp(jnp.where(causal, uc + vr, -jnp.inf))

    q = q_ref[bi, rw]
    ks = k_ref[bi, rw] * (DK ** -0.5)
    v = v_ref[bi, rw]
    qb = q.astype(BF16)
    if L % (2 * SUBLANE) == 0:
        kst = jnp.concatenate([ks.astype(BF16)] * H, axis=0) * hmk_ref[...]
        vsrc = v.astype(BF16)
    else:
        kst = (jnp.concatenate([ks] * H, axis=0) * hmk_ref[...].astype(F32)).astype(BF16)
        vsrc = v
    zero = jnp.zeros((L, DK), vsrc.dtype)
    vbd = jnp.concatenate(
        [jnp.concatenate([zero] * h + [vsrc[:, h * DK:(h + 1) * DK]] + [zero] * (H - 1 - h), axis=1)
         for h in range(H)], axis=0).astype(BF16)
    s = lax.dot_general(qb, kst, NT, preferred_element_type=F32) * w
    qc = jnp.concatenate(
        [jnp.concatenate([jnp.dot(seg(q, g)[:, h * DK:(h + 1) * DK].astype(BF16), c_s[bi * G + g, h].astype(BF16),
                                  preferred_element_type=F32) for h in range(H)], axis=1)
         for g in range(G)], axis=0)
    num = jnp.dot(s.astype(BF16), vbd, preferred_element_type=F32) + ac * qc
    n_rows = per_row([n_s[bi * G + g] for g in range(G)])
    den = (_sel_right(s, seg_ref[...], pieces=2)
           + a_inter * _sel_right(q * n_rows, segv_ref[...], pieces=2))
    rden = heads_only(1.0 / jnp.maximum(jnp.abs(den), jnp.exp(-mt)))
    hh = num * _sel_right(rden, expv, pieces=2) * jax.nn.sigmoid(og_ref[bi, rw])
    h_ref[bi, rw] = hh.astype(h_ref.dtype)

    kw = ks * wc
    for g in range(G):
        e = bi * G + g
        kwb = seg(kw, g).astype(BF16)
        vb = seg(v, g).astype(BF16)
        dc = dcs[g:g + 1]
        for h in range(H):
            hs = slice(h * DK, (h + 1) * DK)
            c_s[e, h] = dc[:, hs] * c_s[e, h] + lax.dot_general(kwb[:, hs], vb[:, hs], TN, preferred_element_type=F32)
        n_s[e] = dc * n_s[e] + jnp.sum(seg(kw, g), axis=0, keepdims=True)
        m_s[e] = heads_only(m_new_g[g])


def _mlstm_consts(S, G):
    L = S * G
    H, DK = ML_HEADS, ML_DK
    expc = np.zeros((LANE, H * L), np.float32)
    expv = np.zeros((LANE, H * DK), np.float32)
    seg = np.zeros((H * L, LANE), np.float32)
    segv = np.zeros((H * DK, LANE), np.float32)
    hmk = np.zeros((H * L, H * DK), np.float32)
    for h in range(H):
        expc[h, h * L:(h + 1) * L] = 1.0
        expv[h, h * DK:(h + 1) * DK] = 1.0
        seg[h * L:(h + 1) * L, h] = 1.0
        segv[h * DK:(h + 1) * DK, h] = 1.0
        hmk[h * L:(h + 1) * L, h * DK:(h + 1) * DK] = 1.0
    eyet = np.tile(np.eye(L, dtype=np.float32), (1, H))
    bf = lambda a: jnp.asarray(a, BF16)
    tri = np.kron(np.eye(G), np.tril(np.ones((S, S)))).astype(np.float32)
    return (bf(tri), bf(np.ones((L, L), np.float32)), bf(expc), bf(expv),
            jnp.asarray(eyet), bf(seg), bf(segv), bf(hmk))


def _mlstm(z3, bias, c0, n0, m0, S, G, Bb, nsub):
    nseq, t_seq, width = z3.shape
    if G > 1:
        assert t_seq == S and nsub == 1
        z3 = z3.reshape(nseq // G, G * S, width)
    b, t, _ = z3.shape
    L = G * S
    nc = t // (L * nsub)
    hw = ML_HEADS * ML_DK
    consts = _mlstm_consts(S, G)
    gate_blk = (EVEN_PAD - LANE) // LANE
    col = lambda j: (lambda bi, ci: (bi, ci, j))
    const2 = lambda bi, ci: (0, 0)
    state3 = lambda bi, ci: (bi, 0, 0)
    m0p = jnp.pad(m0, ((0, 0), (0, LANE - ML_HEADS))).reshape(nseq, 1, LANE)
    Bs = Bb * G
    h, c_new, n_new, m_new = pl.pallas_call(
        functools.partial(_mlstm_kernel, L=L, Bb=Bb, nsub=nsub, G=G),
        grid=(b // Bb, nc),
        in_specs=[pl.BlockSpec((Bb, nsub * L, hw), col(0)),
                  pl.BlockSpec((Bb, nsub * L, hw), col(1)),
                  pl.BlockSpec((Bb, nsub * L, hw), col(2)),
                  pl.BlockSpec((Bb, nsub * L, hw), col(3)),
                  pl.BlockSpec((Bb, nsub * L, LANE), col(gate_blk)),
                  pl.BlockSpec((1, LANE), const2)]
                 + [pl.BlockSpec(a.shape, const2) for a in consts]
                 + [pl.BlockSpec((Bs, ML_HEADS, ML_DK, ML_DV), lambda bi, ci: (bi, 0, 0, 0)),
                    pl.BlockSpec((Bs, 1, hw), state3),
                    pl.BlockSpec((Bs, 1, LANE), state3)],
        out_specs=[pl.BlockSpec((Bb, nsub * L, hw), col(0)),
                   pl.BlockSpec((Bs, ML_HEADS, ML_DK, ML_DV), lambda bi, ci: (bi, 0, 0, 0)),
                   pl.BlockSpec((Bs, 1, hw), state3),
                   pl.BlockSpec((Bs, 1, LANE), state3)],
        out_shape=[jax.ShapeDtypeStruct((b, t, hw), BF16),
                   jax.ShapeDtypeStruct((nseq, ML_HEADS, ML_DK, ML_DV), F32),
                   jax.ShapeDtypeStruct((nseq, 1, hw), F32),
                   jax.ShapeDtypeStruct((nseq, 1, LANE), F32)],
        scratch_shapes=[pltpu.VMEM((Bs, ML_HEADS, ML_DK, ML_DV), F32),
                        pltpu.VMEM((Bs, 1, hw), F32),
                        pltpu.VMEM((Bs, 1, LANE), F32)],
        compiler_params=_params(("parallel", "arbitrary")),
        name="mlstm",
    )(z3, z3, z3, z3, z3, bias, *consts, c0, n0.reshape(nseq, 1, hw), m0p)
    return h.reshape(nseq, t_seq, hw), c_new, n_new.reshape(nseq, ML_HEADS, ML_DK), m_new[:, 0, :ML_HEADS]


def _rope(x, cos, sin_signed, width):
    half = SW_HD // 2
    lane = lax.broadcasted_iota(jnp.int32, x.shape, 1)
    first = (lane % SW_HD) < half
    partner = jnp.where(first, pltpu.roll(x, width - half, axis=1), pltpu.roll(x, half, axis=1))
    return x * cos + partner * sin_signed


def _undup(xd):
    lane = lax.broadcasted_iota(jnp.int32, xd.shape[:-1] + (LANE,), xd.ndim - 1)
    return jnp.where(lane < SW_HD, xd[..., :LANE], xd[..., LANE:])


def _swa_prompt_kernel(sink_ref, q_ref, kc_ref, kp_ref, vc_ref, vp_ref, cosc_ref, sinc_ref, cosp_ref, sinp_ref,
                       h_ref, ko_ref):
    n = pl.program_id(1)
    W2 = 2 * WINDOW
    cosc, sinc = cosc_ref[...], sinc_ref[...]
    cosp, sinp = cosp_ref[...], sinp_ref[...]
    i = lax.broadcasted_iota(jnp.int32, (WINDOW, W2), 0)
    j = lax.broadcasted_iota(jnp.int32, (WINDOW, W2), 1)
    diff = WINDOW + i - j
    valid = (diff >= 0) & (diff < WINDOW) & ((n > 0) | (j >= WINDOW))
    row = lax.broadcasted_iota(jnp.int32, (2 * W2, LANE), 0)
    lane = lax.broadcasted_iota(jnp.int32, (2 * W2, LANE), 1)
    bd = (row < W2) == (lane < SW_HD)
    ones_bd = bd.astype(BF16)
    npair = SW_GQ // 2
    rows = npair * WINDOW
    valid = jnp.concatenate([valid] * npair, axis=0)
    low = lax.broadcasted_iota(jnp.int32, (rows, LANE), 1) < SW_HD
    prow = lax.broadcasted_iota(jnp.int32, (rows, 1), 0) // WINDOW
    cos4, sin4 = jnp.concatenate([cosc] * 4, axis=1), jnp.concatenate([sinc] * 4, axis=1)
    for bi in range(q_ref.shape[0]):
        q = _rope(q_ref[bi], cos4, sin4, 4 * LANE)
        k_cur = _rope(kc_ref[bi], cos4[:, :2 * LANE], sin4[:, :2 * LANE], 2 * LANE)
        k_prev = _rope(kp_ref[bi], jnp.concatenate([cosp] * 2, axis=1), jnp.concatenate([sinp] * 2, axis=1), 2 * LANE)
        kd = jnp.concatenate([k_prev, k_cur], axis=0)
        vd = jnp.concatenate([vp_ref[bi], vc_ref[bi]], axis=0)
        for kv in range(SW_KV_HEADS):
            kblk = kd[:, kv * LANE:(kv + 1) * LANE]
            vblk = vd[:, kv * LANE:(kv + 1) * LANE]
            kbd = jnp.where(bd, jnp.concatenate([kblk, kblk], axis=0), 0.0).astype(BF16)
            vbd = jnp.where(bd, jnp.concatenate([vblk, vblk], axis=0), 0.0).astype(BF16)
            vo = jnp.concatenate([vbd, ones_bd], axis=1)
            h0 = kv * SW_GQ
            qs = jnp.concatenate([q[:, (h0 + 2 * p) * SW_HD:(h0 + 2 * p + 2) * SW_HD] for p in range(npair)],
                                 axis=0).astype(BF16)
            s = lax.dot_general(qs, kbd, NT, preferred_element_type=F32) * (SW_HD ** -0.5)
            es, sink_terms = [], []
            for hh in range(2):
                sh = jnp.where(valid, s[:, hh * W2:(hh + 1) * W2], -jnp.inf)
                sink = sink_ref[h0 + hh]
                for p in range(1, npair):
                    sink = jnp.where(prow == p, sink_ref[h0 + 2 * p + hh], sink)
                m = jnp.maximum(jnp.max(sh, axis=1, keepdims=True), sink)
                es.append(jnp.exp(sh - m).astype(BF16))
                sink_terms.append(jnp.exp(sink - m))
            r = jnp.dot(jnp.concatenate(es, axis=1), vo, preferred_element_type=F32)
            o = r[:, :LANE] / (r[:, LANE:] + jnp.where(low, sink_terms[0], sink_terms[1]))
            for p in range(npair):
                h_ref[bi, :, (h0 + 2 * p) * SW_HD:(h0 + 2 * p + 2) * SW_HD] = (
                    o[p * WINDOW:(p + 1) * WINDOW].astype(h_ref.dtype))

        @pl.when(n == pl.num_programs(1) - 1)
        def _(bi=bi, k_cur=k_cur):
            ko_ref[bi] = _undup(k_cur)


def _swa_prompt(z3, cos, sin_signed, sinks, Bb):
    b, t, _ = z3.shape
    nb = t // WINDOW
    qw = SW_HEADS * SW_HD
    kw = SW_KV_HEADS * SW_HD
    dw = 2 * kw
    qblk, kblk, vblk = EVEN_QA // qw, EVEN_KD // dw, EVEN_VD // dw
    cur = lambda j: (lambda bi, ni: (bi, ni, j))
    prev = lambda j: (lambda bi, ni: (bi, jnp.maximum(ni - 1, 0), j))
    return pl.pallas_call(
        _swa_prompt_kernel,
        grid=(b // Bb, nb),
        in_specs=[pl.BlockSpec(memory_space=pltpu.SMEM),
                  pl.BlockSpec((Bb, WINDOW, qw), cur(qblk)),
                  pl.BlockSpec((Bb, WINDOW, dw), cur(kblk)),
                  pl.BlockSpec((Bb, WINDOW, dw), prev(kblk)),
                  pl.BlockSpec((Bb, WINDOW, dw), cur(vblk)),
                  pl.BlockSpec((Bb, WINDOW, dw), prev(vblk)),
                  pl.BlockSpec((WINDOW, LANE), lambda bi, ni: (ni, 0)),
                  pl.BlockSpec((WINDOW, LANE), lambda bi, ni: (ni, 0)),
                  pl.BlockSpec((WINDOW, LANE), lambda bi, ni: (jnp.maximum(ni - 1, 0), 0)),
                  pl.BlockSpec((WINDOW, LANE), lambda bi, ni: (jnp.maximum(ni - 1, 0), 0))],
        out_specs=[pl.BlockSpec((Bb, WINDOW, qw), cur(0)),
                   pl.BlockSpec((Bb, WINDOW, kw), lambda bi, ni: (bi, 0, 0))],
        out_shape=[jax.ShapeDtypeStruct((b, t, qw), BF16),
                   jax.ShapeDtypeStruct((b, WINDOW, kw), F32)],
        compiler_params=_params(("parallel", "arbitrary")),
        name="swa_prompt",
    )(sinks, z3, z3, z3, z3, z3, cos, sin_signed, cos, sin_signed)


def _swa_sample_kernel(q_ref, k_ref, v_ref, kbuf_ref, vbuf_ref, cos_ref, sin_ref, sink_ref,
                       h_ref, ko_ref, vo_ref, *, T, Bb):
    qw = SW_HEADS * SW_HD
    kw = SW_KV_HEADS * SW_HD
    cos = cos_ref[...]
    sin = sin_ref[...]
    q = _rope(q_ref[...].reshape(Bb * T, qw), jnp.concatenate([cos] * (qw // LANE), axis=1),
              jnp.concatenate([sin] * (qw // LANE), axis=1), qw).reshape(Bb, T, qw)
    k_new = _rope(_undup(k_ref[...]).reshape(Bb * T, kw), cos, sin, kw).reshape(Bb, T, kw)
    kk = jnp.concatenate([kbuf_ref[...], k_new], axis=1)
    vv = jnp.concatenate([vbuf_ref[...], _undup(v_ref[...])], axis=1)
    ko_ref[...] = kk[:, T:, :]
    vo_ref[...] = vv[:, T:, :]
    rows = SW_GQ * T
    i = lax.broadcasted_iota(jnp.int32, (rows, WINDOW + T), 0) % T
    j = lax.broadcasted_iota(jnp.int32, (rows, WINDOW + T), 1)
    diff = WINDOW + i - j
    valid = (diff >= 0) & (diff < WINDOW)
    for kv in range(SW_KV_HEADS):
        qs = jnp.concatenate([q[:, :, (kv * SW_GQ + g) * SW_HD:(kv * SW_GQ + g + 1) * SW_HD]
                              for g in range(SW_GQ)], axis=1).astype(BF16)
        kh = kk[:, :, kv * SW_HD:(kv + 1) * SW_HD].astype(BF16)
        vh = vv[:, :, kv * SW_HD:(kv + 1) * SW_HD].astype(BF16)
        s = jnp.einsum('bqd,bkd->bqk', qs, kh, preferred_element_type=F32) * (SW_HD ** -0.5)
        s = jnp.where(valid, s, -jnp.inf)
        sink = sink_ref[kv * rows:(kv + 1) * rows, :]
        m = jnp.maximum(jnp.max(s, axis=-1, keepdims=True), sink)
        e = jnp.exp(s - m)
        p = e / (jnp.sum(e, axis=-1, keepdims=True) + jnp.exp(sink - m))
        o = jnp.einsum('bqk,bkd->bqd', p.astype(BF16), vh, preferred_element_type=F32)
        for g in range(SW_GQ):
            hh = kv * SW_GQ + g
            h_ref[:, :, hh * SW_HD:(hh + 1) * SW_HD] = o[:, g * T:(g + 1) * T, :].astype(h_ref.dtype)


def _swa_sample(z3, kbuf, vbuf, cos, sin_signed, sinks, Bb):
    b, t, _ = z3.shape
    qw = SW_HEADS * SW_HD
    kw = SW_KV_HEADS * SW_HD
    dw = 2 * kw
    qblk, kblk, vblk = EVEN_QA // qw, EVEN_KD // dw, EVEN_VD // dw
    col = lambda j: (lambda bi: (bi, 0, j))
    const2 = lambda bi: (0, 0)
    sink_col = jnp.repeat(sinks, t).reshape(SW_HEADS * t, 1)
    return pl.pallas_call(
        functools.partial(_swa_sample_kernel, T=t, Bb=Bb),
        grid=(b // Bb,),
        in_specs=[pl.BlockSpec((Bb, t, qw), col(qblk)),
                  pl.BlockSpec((Bb, t, dw), col(kblk)),
                  pl.BlockSpec((Bb, t, dw), col(vblk)),
                  pl.BlockSpec((Bb, WINDOW, kw), col(0)),
                  pl.BlockSpec((Bb, WINDOW, kw), col(0)),
                  pl.BlockSpec((Bb * t, LANE), const2),
                  pl.BlockSpec((Bb * t, LANE), const2),
                  pl.BlockSpec((SW_HEADS * t, 1), const2)],
        out_specs=[pl.BlockSpec((Bb, t, qw), col(0)),
                   pl.BlockSpec((Bb, WINDOW, kw), col(0)),
                   pl.BlockSpec((Bb, WINDOW, kw), col(0))],
        out_shape=[jax.ShapeDtypeStruct((b, t, qw), BF16),
                   jax.ShapeDtypeStruct((b, WINDOW, kw), F32),
                   jax.ShapeDtypeStruct((b, WINDOW, kw), F32)],
        compiler_params=_params(("parallel",)),
        name="swa_sample",
    )(z3, z3, z3, kbuf, vbuf, jnp.tile(cos, (Bb, 1)), jnp.tile(sin_signed, (Bb, 1)), sink_col)


def _gla_tables(S, G):
    nlev = int(math.log2(S))
    assert 2 ** nlev == S
    L = G * S
    mstack = np.zeros((nlev * L, L), np.float32)
    masks = np.zeros((nlev + 1, L, L), np.float32)
    masks[0] = np.eye(L)
    for l in range(1, nlev + 1):
        n = 2 ** l
        for r in range(L):
            blk, pos = divmod(r, n)
            m = blk * n + n // 2 - 1
            if pos >= n // 2:
                mstack[(l - 1) * L + r, m + 1:r + 1] = 1.0
                masks[l, r, blk * n:blk * n + n // 2] = 1.0
            else:
                mstack[(l - 1) * L + r, r + 1:m + 1] = 1.0
    masks = np.tile(masks, (1, 1, GLA_HEADS))
    tri = np.kron(np.eye(G), np.tril(np.ones((S, S)))).astype(np.float32)
    return nlev, jnp.asarray(tri, BF16), jnp.asarray(mstack, BF16), jnp.asarray(masks)


def _gla_consts(L):
    kw = GLA_HEADS * GLA_DK
    vw = GLA_HEADS * GLA_DV
    hk = np.zeros((GLA_HEADS * L, kw), np.float32)
    for h in range(GLA_HEADS):
        hk[h * L:(h + 1) * L, h * GLA_DK:(h + 1) * GLA_DK] = 1.0
    hs = np.zeros((vw, kw), np.float32)
    for h in range(GLA_HEADS):
        hs[h * GLA_DV:(h + 1) * GLA_DV, h * GLA_DK:(h + 1) * GLA_DK] = 1.0
    return (jnp.asarray(hk, BF16), jnp.asarray(hs), jnp.asarray(np.eye(GLA_DV), BF16), jnp.asarray(np.eye(kw), BF16))


def _gla_kernel(q_ref, k_ref, v_ref, r_ref, glr_ref, wg_ref, bg_ref, gn_ref, tri_ref, mstack_ref,
                masks_ref, hk_ref, hs_ref, eyev_ref, eyek_ref, s0_ref,
                o_ref, so_ref, st_s, *, L, nlev, Bb, nsub, G):
    c = pl.program_id(1)
    H, DK, DV = GLA_HEADS, GLA_DK, GLA_DV

    @pl.when(c == 0)
    def _():
        for bi in range(Bb * G):
            for h in range(H):
                pieces = ([jnp.zeros((h * DK, DV), F32)] if h else []) + [s0_ref[bi, h]]
                pieces += [jnp.zeros(((H - 1 - h) * DK, DV), F32)] if h < H - 1 else []
                padded = jnp.concatenate(pieces, axis=0)
                st_s[bi, h * DV:(h + 1) * DV, :] = _dot_sel(eyev_ref[...], padded, NT)

    for sub in range(nsub):
        for bi in range(Bb):
            _gla_one(bi, sub * L, q_ref, k_ref, v_ref, r_ref, glr_ref, wg_ref, bg_ref, gn_ref, tri_ref, mstack_ref,
                     masks_ref, hk_ref, hs_ref, o_ref, st_s, L, nlev, G)

    @pl.when(c == pl.num_programs(1) - 1)
    def _():
        for bi in range(Bb * G):
            for h in range(H):
                blk = st_s[bi, h * DV:(h + 1) * DV, :]
                so_ref[bi, h] = _dot_sel(eyek_ref[h * DK:(h + 1) * DK, :], blk, NT)


def _gla_one(bi, r0, q_ref, k_ref, v_ref, r_ref, glr_ref, wg_ref, bg_ref, gn_ref, tri_ref, mstack_ref,
             masks_ref, hk_ref, hs_ref, o_ref, st_s, L, nlev, G):
    H, DK, DV = GLA_HEADS, GLA_DK, GLA_DV
    packed_rows = L % (2 * SUBLANE) == 0
    rw = pl.ds(r0, L)
    q = q_ref[bi, rw] * (DK ** -0.5)
    k = k_ref[bi, rw]
    kwid = H * DK
    g_hi, g_lo = _pieces(glr_ref[bi, rw], 2)
    pre2 = jnp.dot(g_hi.astype(BF16), wg_ref[...], preferred_element_type=F32)
    pre = (pre2[:, :kwid] + pre2[:, kwid:] + jnp.dot(g_lo.astype(BF16), wg_ref[:, :kwid], preferred_element_type=F32)
           + bg_ref[...])
    lg = jax.nn.log_sigmoid(pre) * (1.0 / GLA_TAU)
    Gc = _dot_sel(tri_ref[...], lg)
    E = jnp.exp(_dot_sel(mstack_ref[...], lg, pieces=2))
    hk = hk_ref[...]
    tok = lax.broadcasted_iota(jnp.int32, (L, kwid), 0)

    att = None
    for l in range(nlev + 1):
        if l == 0:
            ql, kl = q, k
        else:
            e_l = E[(l - 1) * L:l * L]
            upper = ((tok >> (l - 1)) & 1) == 1
            ql = jnp.where(upper, q * e_l, 0.0)
            kl = jnp.where(upper, 0.0, k * e_l)
        if packed_rows:
            kst = jnp.concatenate([kl.astype(BF16)] * H, axis=0) * hk
        else:
            kst = (jnp.concatenate([kl] * H, axis=0) * hk.astype(F32)).astype(BF16)
        part = lax.dot_general(ql.astype(BF16), kst, NT, preferred_element_type=F32)
        part = jnp.where(masks_ref[l] > 0.5, part, 0.0)
        att = part if att is None else att + part

    v = v_ref[bi, rw]
    vsrc = v.astype(BF16) if packed_rows else v
    zero = jnp.zeros((L, DV), vsrc.dtype)
    vbd = jnp.concatenate(
        [jnp.concatenate([zero] * h + [vsrc[:, h * DV:(h + 1) * DV]] + [zero] * (H - 1 - h), axis=1)
         for h in range(H)], axis=0).astype(BF16)
    S = L // G
    seg = lambda a, g: a[g * S:(g + 1) * S]
    qg = q * jnp.exp(Gc)
    o = jnp.dot(att.astype(BF16), vbd, preferred_element_type=F32) + jnp.concatenate(
        [lax.dot_general(seg(qg, g).astype(BF16), st_s[bi * G + g].astype(BF16), NT, preferred_element_type=F32)
         for g in range(G)], axis=0)
    for h in range(H):
        vs = slice(h * DV, (h + 1) * DV)
        oh = o[:, vs]
        oh = oh * lax.rsqrt(jnp.mean(oh * oh, axis=-1, keepdims=True) + EPS)
        oh = oh * gn_ref[:, vs] * jax.nn.silu(r_ref[bi, rw, vs])
        o_ref[bi, rw, vs] = oh.astype(o_ref.dtype)
    for g in range(G):
        g_seg = seg(Gc, g)
        g_end = g_seg[S - 1:S, :]
        k_end = (seg(k, g) * jnp.exp(g_end - g_seg)).astype(BF16)
        upd = lax.dot_general(seg(v, g).astype(BF16), k_end, TN, preferred_element_type=F32)
        st_s[bi * G + g] = st_s[bi * G + g] * jnp.exp(g_end) + jnp.where(hs_ref[...] > 0.5, upd, 0.0)


def _gla(z3, wg, bg, gn, s0, S, G, Bb, nsub):
    nseq, t_seq, width = z3.shape
    if G > 1:
        assert t_seq == S and nsub == 1
        z3 = z3.reshape(nseq // G, G * S, width)
    b, t, _ = z3.shape
    L = G * S
    nc = t // (L * nsub)
    nlev, tri, mstack, masks = _gla_tables(S, G)
    hk, hs, eyev, eyek = _gla_consts(L)
    qk_w = GLA_HEADS * GLA_DK
    v_w = GLA_HEADS * GLA_DV
    col = lambda j: (lambda bi, ci: (bi, ci, j))
    const2 = lambda bi, ci: (0, 0)
    const3 = lambda bi, ci: (0, 0, 0)
    o, s_new = pl.pallas_call(
        functools.partial(_gla_kernel, L=L, nlev=nlev, Bb=Bb, nsub=nsub, G=G),
        grid=(b // Bb, nc),
        in_specs=[pl.BlockSpec((Bb, nsub * L, qk_w), col(0)),
                  pl.BlockSpec((Bb, nsub * L, qk_w), col(1)),
                  pl.BlockSpec((Bb, nsub * L, v_w), col(1)),
                  pl.BlockSpec((Bb, nsub * L, v_w), col(2)),
                  pl.BlockSpec((Bb, nsub * L, LANE), col(2048 // LANE)),
                  pl.BlockSpec((LANE, 2 * qk_w), const2),
                  pl.BlockSpec((1, qk_w), const2),
                  pl.BlockSpec((1, v_w), const2),
                  pl.BlockSpec((L, L), const2),
                  pl.BlockSpec(mstack.shape, const2),
                  pl.BlockSpec(masks.shape, const3),
                  pl.BlockSpec(hk.shape, const2),
                  pl.BlockSpec(hs.shape, const2),
                  pl.BlockSpec(eyev.shape, const2),
                  pl.BlockSpec(eyek.shape, const2),
                  pl.BlockSpec((Bb * G, GLA_HEADS, GLA_DK, GLA_DV), lambda bi, ci: (bi, 0, 0, 0))],
        out_specs=[pl.BlockSpec((Bb, nsub * L, v_w), col(0)),
                   pl.BlockSpec((Bb * G, GLA_HEADS, GLA_DK, GLA_DV), lambda bi, ci: (bi, 0, 0, 0))],
        out_shape=[jax.ShapeDtypeStruct((b, t, v_w), BF16),
                   jax.ShapeDtypeStruct((nseq, GLA_HEADS, GLA_DK, GLA_DV), F32)],
        scratch_shapes=[pltpu.VMEM((Bb * G, v_w, qk_w), F32)],
        compiler_params=_params(("parallel", "arbitrary")),
        name="gla",
    )(z3, z3, z3, z3, z3, wg, bg, gn, tri, mstack, masks, hk, hs, eyev, eyek, s0)
    return o.reshape(nseq, t_seq, v_w), s_new


def _s5_kernel(u_ref, bbre_ref, bbim_ref, cre_ref, cim_ref, d_ref, wglu_ref, bglu_ref, pre_ref, pim_ref,
               x0re_ref, x0im_ref,
               o_ref, xore_ref, xoim_ref,
               xr_s, xi_s, cr_s, ci_s, *, rows, sequential):
    c = pl.program_id(1)
    lead = rows // SUBLANE if sequential else SUBLANE
    inner = rows // lead
    cw = S5_LANES // 4 if inner == SUBLANE else LANE
    u = jnp.swapaxes(u_ref[0].reshape(inner, lead, S5_WIDTH), 0, 1).reshape(rows, S5_WIDTH)
    ub = u.astype(BF16)
    hw, hl = S5_WIDTH // 2, S5_LANES // 2
    for half in range(2):
        uh = ub[:, half * hw:(half + 1) * hw]
        ls = slice(half * hl, (half + 1) * hl)
        xr_s[:, :, ls] = jnp.dot(uh, bbre_ref[half * hw:(half + 1) * hw, ls],
                                 preferred_element_type=F32).reshape(lead, inner, hl)
        xi_s[:, :, ls] = jnp.dot(uh, bbim_ref[half * hw:(half + 1) * hw, ls],
                                 preferred_element_type=F32).reshape(lead, inner, hl)

    for c0 in range(0, S5_LANES, cw):
        ar = jnp.broadcast_to(pre_ref[0, :, c0:c0 + cw], (inner, cw))
        ai = jnp.broadcast_to(pim_ref[0, :, c0:c0 + cw], (inner, cw))

        def body(j, carry, c0=c0, ar=ar, ai=ai):
            xr, xi = carry
            nr = ar * xr - ai * xi + xr_s[j, :, c0:c0 + cw]
            ni = ar * xi + ai * xr + xi_s[j, :, c0:c0 + cw]
            xr_s[j, :, c0:c0 + cw] = nr
            xi_s[j, :, c0:c0 + cw] = ni
            return nr, ni

        if sequential:
            init = (jnp.zeros((inner, cw), F32), jnp.zeros((inner, cw), F32))
        else:
            init = (x0re_ref[0, :, c0:c0 + cw], x0im_ref[0, :, c0:c0 + cw])
        lax.fori_loop(0, lead, body, init, unroll=True)

    if sequential:
        @pl.when(c == 0)
        def _():
            cr_s[...] = x0re_ref[0]
            ci_s[...] = x0im_ref[0]

        end_r = xr_s[lead - 1]
        end_i = xi_s[lead - 1]
        anr = pre_ref[lead - 1]
        ani = pim_ref[lead - 1]
        ent_r, ent_i = [cr_s[...]], [ci_s[...]]
        for s in range(SUBLANE):
            pr, pi = ent_r[-1], ent_i[-1]
            ent_r.append(end_r[s:s + 1] + anr * pr - ani * pi)
            ent_i.append(end_i[s:s + 1] + anr * pi + ani * pr)
        cr_s[...] = ent_r[SUBLANE]
        ci_s[...] = ent_i[SUBLANE]
        er = jnp.concatenate(ent_r[:SUBLANE], axis=0)[None]
        ei = jnp.concatenate(ent_i[:SUBLANE], axis=0)[None]
        pr3, pi3 = pre_ref[...], pim_ref[...]
        xr = xr_s[...] + pr3 * er - pi3 * ei
        xi = xi_s[...] + pr3 * ei + pi3 * er

        @pl.when(c == pl.num_programs(1) - 1)
        def _():
            xore_ref[0] = cr_s[...]
            xoim_ref[0] = ci_s[...]
    else:
        xr = xr_s[...]
        xi = xi_s[...]
        xore_ref[0] = xr[lead - 1]
        xoim_ref[0] = xi[lead - 1]

    xrb = xr.reshape(rows, S5_LANES).astype(BF16)
    xib = xi.reshape(rows, S5_LANES).astype(BF16)
    y = jnp.concatenate(
        [jnp.dot(xrb[:, h * hl:(h + 1) * hl], cre_ref[h * hl:(h + 1) * hl, h * hw:(h + 1) * hw],
                 preferred_element_type=F32)
         - jnp.dot(xib[:, h * hl:(h + 1) * hl], cim_ref[h * hl:(h + 1) * hl, h * hw:(h + 1) * hw],
                   preferred_element_type=F32) for h in range(2)], axis=1) + d_ref[...] * u
    zz = jax.nn.gelu(y)
    gate = jnp.dot(zz.astype(BF16), wglu_ref[...], preferred_element_type=F32) + bglu_ref[...]
    out = zz * jax.nn.sigmoid(gate)
    o_ref[0] = jnp.swapaxes(out.reshape(lead, inner, S5_WIDTH), 0, 1).reshape(rows, S5_WIDTH).astype(o_ref.dtype)


def _s5(z3, tabs, x0re, x0im, rows, sequential):
    b, t, _ = z3.shape
    nblk = t // rows
    groups = rows // SUBLANE
    ublk = 1536 // S5_WIDTH
    const2 = lambda bi, ci: (0, 0)
    const3 = lambda bi, ci: (0, 0, 0)
    lead = groups if sequential else SUBLANE
    kk = jnp.arange(1, lead + 1, dtype=F32).reshape(lead, 1, 1)
    mag = jnp.exp(kk * tabs['la_re'][None])
    p_re = mag * jnp.cos(kk * tabs['la_im'][None])
    p_im = mag * jnp.sin(kk * tabs['la_im'][None])
    if sequential:
        st_spec = pl.BlockSpec((1, 1, S5_LANES), lambda bi, ci: (bi, 0, 0))
        st_shape = jax.ShapeDtypeStruct((b, 1, S5_LANES), F32)
    else:
        st_spec = pl.BlockSpec((1, groups, S5_LANES), lambda bi, ci: (bi, ci, 0))
        st_shape = jax.ShapeDtypeStruct(x0re.shape, F32)
    return pl.pallas_call(
        functools.partial(_s5_kernel, rows=rows, sequential=sequential),
        grid=(b, nblk),
        in_specs=[pl.BlockSpec((1, rows, S5_WIDTH), lambda bi, ci: (bi, ci, ublk)),
                  pl.BlockSpec((S5_WIDTH, S5_LANES), const2),
                  pl.BlockSpec((S5_WIDTH, S5_LANES), const2),
                  pl.BlockSpec((S5_LANES, S5_WIDTH), const2),
                  pl.BlockSpec((S5_LANES, S5_WIDTH), const2),
                  pl.BlockSpec((1, S5_WIDTH), const2),
                  pl.BlockSpec((S5_WIDTH, S5_WIDTH), const2),
                  pl.BlockSpec((1, S5_WIDTH), const2),
                  pl.BlockSpec((lead, 1, S5_LANES), const3),
                  pl.BlockSpec((lead, 1, S5_LANES), const3),
                  st_spec, st_spec],
        out_specs=[pl.BlockSpec((1, rows, S5_WIDTH), lambda bi, ci: (bi, ci, 0)),
                   st_spec, st_spec],
        out_shape=[jax.ShapeDtypeStruct((b, t, S5_WIDTH), BF16), st_shape, st_shape],
        scratch_shapes=[pltpu.VMEM((lead, rows // lead, S5_LANES), F32),
                        pltpu.VMEM((lead, rows // lead, S5_LANES), F32),
                        pltpu.VMEM((1, S5_LANES), F32),
                        pltpu.VMEM((1, S5_LANES), F32)],
        compiler_params=_params(("parallel", "arbitrary")),
        name="s5",
    )(z3, tabs['bbre'], tabs['bbim'], tabs['cre'], tabs['cim'], tabs['d'], tabs['wglu'], tabs['bglu'],
      p_re, p_im, x0re, x0im)


def _s5_tables(a_re, a_im, log_dt, b_re, b_im, c_re, c_im, d, w_glu, b_glu):
    lam_re = a_re.astype(F32)
    lam_im = a_im.astype(F32)
    dt = jnp.exp(log_dt.astype(F32))[:, None]
    mag = jnp.exp(lam_re * dt)
    ang = lam_im * dt
    ab_re = mag * jnp.cos(ang)
    ab_im = mag * jnp.sin(ang)
    den = lam_re * lam_re + lam_im * lam_im
    co_re = ((ab_re - 1.0) * lam_re + ab_im * lam_im) / den
    co_im = (ab_im * lam_re - (ab_re - 1.0) * lam_im) / den
    b_re = b_re.astype(F32)
    b_im = b_im.astype(F32)
    bb_re = co_re[..., None] * b_re - co_im[..., None] * b_im
    bb_im = co_re[..., None] * b_im + co_im[..., None] * b_re
    eye = jnp.eye(S5_GROUPS, dtype=F32)
    blockdiag_in = lambda bb: jnp.einsum('gph,gk->ghkp', bb, eye).reshape(S5_WIDTH, S5_LANES)
    blockdiag_out = lambda cc: jnp.einsum('ghp,gk->gpkh', cc.astype(F32), eye).reshape(S5_LANES, S5_WIDTH)
    return {
        'bbre': blockdiag_in(bb_re).astype(BF16), 'bbim': blockdiag_in(bb_im).astype(BF16),
        'cre': blockdiag_out(c_re).astype(BF16), 'cim': blockdiag_out(c_im).astype(BF16),
        'd': d.astype(F32).reshape(1, S5_WIDTH),
        'wglu': w_glu.astype(BF16), 'bglu': b_glu.astype(F32).reshape(1, S5_WIDTH),
        'la_re': (lam_re * dt).reshape(1, S5_LANES), 'la_im': ang.reshape(1, S5_LANES),
    }


def _rmsnorm_val(x, g):
    return x * lax.rsqrt(jnp.mean(x * x, axis=-1, keepdims=True) + EPS) * g


def _ffn_kernel(r_ref, a1_ref, a2_ref, w1_ref, w2_ref, g_ref, wup_ref, cw_ref, cb_ref, wd_ref, st_ref, gf_ref,
                o_ref, so_ref, halo_s, *, tm, tf, long_seq, final_norm):
    i = pl.program_id(1)
    n = tm // SUBLANE
    d = r_ref.shape[-1]
    x = (r_ref[0] + jnp.dot(a1_ref[0], w1_ref[...], preferred_element_type=F32)
         + jnp.dot(a2_ref[0], w2_ref[...], preferred_element_type=F32))
    hn = _rmsnorm_val(x, g_ref[...])
    if long_seq:
        lead = n

        @pl.when(i == 0)
        def _():
            halo_s[...] = st_ref[0]
    else:
        lead = SUBLANE
    hnp = jnp.swapaxes(hn.reshape(tm // lead, lead, d), 0, 1).reshape(tm, d).astype(BF16)
    inner = tm // lead
    sub = lax.broadcasted_iota(jnp.int32, (SUBLANE, tf), 0)

    acc = jnp.zeros((tm, d), F32)
    for c in range(D_FF // tf):
        conv = []
        for half in range(2):
            c0 = half * D_FF + c * tf
            u = jnp.dot(hnp, wup_ref[:, c0:c0 + tf], preferred_element_type=F32).reshape(lead, inner, tf)
            if long_seq:
                um1 = jnp.where(sub == 0, halo_s[1:2, c0:c0 + tf], pltpu.roll(u[lead - 1], 1, axis=0))
                um2 = jnp.where(sub == 0, halo_s[0:1, c0:c0 + tf], pltpu.roll(u[lead - 2], 1, axis=0))
                halo_s[0:1, c0:c0 + tf] = u[lead - 2][SUBLANE - 1:SUBLANE]
                halo_s[1:2, c0:c0 + tf] = u[lead - 1][SUBLANE - 1:SUBLANE]
            else:
                um2 = st_ref[0, :, c0:c0 + tf]
                um1 = st_ref[1, :, c0:c0 + tf]
                so_ref[0, :, c0:c0 + tf] = u[lead - 2]
                so_ref[1, :, c0:c0 + tf] = u[lead - 1]
            ext = jnp.concatenate([um2[None], um1[None], u], axis=0)
            cv = cb_ref[:, c0:c0 + tf]
            for j in range(CONV_W):
                cv = cv + ext[j:j + lead] * cw_ref[j:j + 1, c0:c0 + tf]
            conv.append(cv)
        act = (jax.nn.silu(conv[1]) * conv[0]).reshape(tm, tf).astype(BF16)
        acc = acc + jnp.dot(act, wd_ref[c * tf:(c + 1) * tf, :], preferred_element_type=F32)

    out = x + jnp.swapaxes(acc.reshape(lead, inner, d), 0, 1).reshape(tm, d)
    if final_norm:
        out = _rmsnorm_val(out, gf_ref[...])
    o_ref[0] = out

    if long_seq:
        @pl.when(i == pl.num_programs(1) - 1)
        def _():
            so_ref[0] = halo_s[...]


def _ffn(res3, a1, a2, w1, w2, g, wup, cw, cb, wd, state, layer, gfinal, tm, tf, long_seq, final_norm):
    b, t, d = res3.shape
    nt = t // tm
    resident = dict(pipeline_mode=pl.Buffered(1))
    const2 = lambda bi, i: (0, 0)
    row_blk = lambda w: pl.BlockSpec((1, tm, w), lambda bi, i: (bi, i, 0))
    per_layer = lambda shape, **kw: pl.BlockSpec((None,) + shape, lambda bi, i: (layer, 0, 0), **kw)
    if long_seq:
        st_in = st_out = pl.BlockSpec((1, CONV_W - 1, 2 * D_FF), lambda bi, i: (bi, 0, 0))
        st_shape = state.shape
    else:
        nseq = tm // SUBLANE
        st_in = pl.BlockSpec((None, CONV_W - 1, nseq, 2 * D_FF), lambda bi, i: (layer, 0, i, 0))
        st_out = pl.BlockSpec((CONV_W - 1, nseq, 2 * D_FF), lambda bi, i: (0, i, 0))
        st_shape = state.shape[1:]
    return pl.pallas_call(
        functools.partial(_ffn_kernel, tm=tm, tf=tf, long_seq=long_seq, final_norm=final_norm),
        grid=(b, nt),
        in_specs=[row_blk(d), row_blk(a1.shape[-1]), row_blk(a2.shape[-1]),
                  pl.BlockSpec(w1.shape, const2, **resident),
                  pl.BlockSpec(w2.shape, const2, **resident),
                  per_layer((1, d)),
                  per_layer((d, 2 * D_FF), **resident),
                  per_layer((CONV_W, 2 * D_FF)),
                  per_layer((1, 2 * D_FF)),
                  per_layer((D_FF, d), **resident),
                  st_in,
                  pl.BlockSpec((1, d), const2)],
        out_specs=[row_blk(d), st_out],
        out_shape=[jax.ShapeDtypeStruct((b, t, d), F32), jax.ShapeDtypeStruct(st_shape, F32)],
        scratch_shapes=[pltpu.VMEM((CONV_W - 1, 2 * D_FF), F32)],
        compiler_params=_params(("parallel", "arbitrary")),
        name="ffn",
    )(res3, a1, a2, w1, w2, g.reshape(-1, 1, d), wup, cw, cb.reshape(-1, 1, 2 * D_FF), wd, state, gfinal)


def _rope_tables(pos):
    half = SW_HD // 2
    inv = ROPE_THETA ** (-jnp.arange(half, dtype=F32) / half)
    ang = pos.astype(F32)[:, None] * inv[None, :]
    cos = jnp.cos(ang)
    sin = jnp.sin(ang)
    reps = LANE // SW_HD
    cos_t = jnp.tile(jnp.concatenate([cos, cos], axis=1), (1, reps))
    sin_t = jnp.tile(jnp.concatenate([-sin, sin], axis=1), (1, reps))
    return cos_t, sin_t


def _prep_weights(W):
    P = {}
    we = W['w_in_even']
    parts = jnp.split(we, [int(c) for c in np.cumsum((512, 512, 512, 512, 8, 512, 128))], axis=1)
    q_m, k_m, v_m, o_m, g_m, q_a, k_a, v_a = parts
    dup = lambda w: jnp.concatenate([w[:, :SW_HD], w[:, :SW_HD], w[:, SW_HD:], w[:, SW_HD:]], axis=1)
    pad = jnp.zeros((D_MODEL, LANE - 2 * ML_HEADS), we.dtype)
    P['w_in_even'] = jnp.concatenate([q_m, k_m, v_m, o_m, q_a, dup(k_a), dup(v_a), g_m, pad], axis=1).astype(BF16)
    assert P['w_in_even'].shape[1] == EVEN_PAD
    P['b_gates'] = jnp.pad(W['b_mlstm_gates'].astype(F32), (0, LANE - 2 * ML_HEADS)).reshape(1, LANE)
    wo = W['w_in_odd']
    q, k, v, r, glr, u = jnp.split(wo, [int(c) for c in np.cumsum((256, 256, 512, 512, 16))], axis=1)
    pad = jnp.zeros((D_MODEL, ODD_PAD - 2048 - GLA_RANK), wo.dtype)
    P['w_in_odd'] = jnp.concatenate([q, k, v, r, u, glr, pad], axis=1).astype(BF16)
    wg = jnp.pad(W['w_gla_gate_up'].astype(F32), ((0, LANE - GLA_RANK), (0, 0)))
    wg_hi = wg.astype(BF16)
    P['w_gate_up'] = jnp.concatenate([wg_hi, (wg - wg_hi.astype(F32)).astype(BF16)], axis=1)
    P['b_gate'] = W['b_gla_gate'].astype(F32).reshape(1, -1)
    P['g_gla'] = W['g_gla_norm'].astype(F32).reshape(1, -1)
    hm = ML_HEADS * ML_DV
    P['w_out_even'] = (W['w_out_even'][:hm].astype(BF16), W['w_out_even'][hm:].astype(BF16))
    hg = GLA_HEADS * GLA_DV
    P['w_out_odd'] = (W['w_out_odd'][:hg].astype(BF16), W['w_out_odd'][hg:].astype(BF16))
    P['w_ffn_up'] = W['w_ffn_up'].astype(BF16)
    P['w_ffn_down'] = W['w_ffn_down'].astype(BF16)
    P['s5'] = _s5_tables(W['s5_a_re'], W['s5_a_im'], W['s5_log_dt'], W['s5_b_re'], W['s5_b_im'],
                         W['s5_c_re'], W['s5_c_im'], W['s5_d'], W['w_s5_glu'], W['b_s5_glu'])
    return P


def _trunk(x, pos, st, W, P, is_prompt):
    b, t, d = x.shape
    n = b * t
    L = CHUNK if t % CHUNK == 0 else t
    nsub = next((c for c in (8, 4, 2) if t % (c * L) == 0), 1)
    tm = next((c for c in (1024, 512) if n % c == 0), n)
    cos, sin_signed = _rope_tables(pos)
    gfinal = W['norm_final'].astype(F32).reshape(1, d)
    new = {}
    conv_out = []
    h = x.reshape(n, d)

    z = _norm_matmul(h, W['norm_mix'][0], P['w_in_even'], tm)
    z3 = z.reshape(b, t, EVEN_PAD)
    pack = dict(G=1, Bb=2 if b % 2 == 0 else 1, nsub=nsub) if (is_prompt or b % 16) else dict(G=8, Bb=2, nsub=1)
    h_m, new['ml_C'], new['ml_n'], m_new = _mlstm(z3, P['b_gates'], st['ml_C'], st['ml_n'], st['ml_m'], L, **pack)
    new['ml_m'] = m_new.reshape(b, ML_HEADS)
    if is_prompt:
        h_a, kb = _swa_prompt(z3, cos, sin_signed, W['sw_sinks'].astype(F32), Bb=2 if b % 2 == 0 else 1)
        vb = jnp.concatenate([z3[:, t - WINDOW:, EVEN_VD:EVEN_VD + SW_HD],
                              z3[:, t - WINDOW:, EVEN_VD + LANE:EVEN_VD + LANE + SW_HD]], axis=-1)
    else:
        h_a, kb, vb = _swa_sample(z3, st['kbuf'].reshape(b, WINDOW, -1), st['vbuf'].reshape(b, WINDOW, -1),
                                  cos, sin_signed, W['sw_sinks'].astype(F32), Bb=8)
    new['kbuf'] = kb.reshape(b, WINDOW, SW_KV_HEADS, SW_HD)
    new['vbuf'] = vb.reshape(b, WINDOW, SW_KV_HEADS, SW_HD)
    h, cb = _ffn_layer(h, h_m, h_a, P['w_out_even'], 0, st, W, P, b, t, gfinal, is_prompt, final_norm=False)
    conv_out.append(cb)

    z = _norm_matmul(h, W['norm_mix'][1], P['w_in_odd'], tm)
    z3 = z.reshape(b, t, ODD_PAD)
    o_g, new['gla'] = _gla(z3, P['w_gate_up'], P['b_gate'], P['g_gla'], st['gla'], L, **pack)
    x0re = st['s5_re'].astype(F32).reshape(b, 1, S5_LANES)
    x0im = st['s5_im'].astype(F32).reshape(b, 1, S5_LANES)
    if is_prompt:
        o_s, xre, xim = _s5(z3, P['s5'], x0re, x0im, rows=next((c for c in (512, 256) if t % c == 0), t),
                            sequential=True)
    else:
        o_s, xre, xim = _s5(z3.reshape(1, n, ODD_PAD), P['s5'], x0re.reshape(1, b, S5_LANES),
                            x0im.reshape(1, b, S5_LANES), rows=256 if n % 256 == 0 else n, sequential=False)
    new['s5_re'] = xre.reshape(b, S5_GROUPS, S5_STATE)
    new['s5_im'] = xim.reshape(b, S5_GROUPS, S5_STATE)
    h, cb = _ffn_layer(h, o_g, o_s, P['w_out_odd'], 1, st, W, P, b, t, gfinal, is_prompt, final_norm=True)
    conv_out.append(cb)
    new['conv'] = jnp.stack(conv_out)
    return h.reshape(b, t, d), new


def _ffn_layer(h, a1, a2, w_out, layer, st, W, P, b, t, gfinal, is_prompt, final_norm):
    n, d = h.shape
    rows = t if is_prompt else n
    tm = next((c for c in (512, 256) if rows % c == 0), rows)
    args = (*w_out, W['norm_ffn'], P['w_ffn_up'], W['ffn_conv_w'].astype(F32), W['ffn_conv_b'].astype(F32),
            P['w_ffn_down'])
    if is_prompt:
        shape3 = lambda a: a.reshape(b, t, -1)
        out, new_state = _ffn(shape3(h), shape3(a1), shape3(a2), *args, st['conv'][layer], layer, gfinal, tm,
                              FFN_TF, True, final_norm)
    else:
        assert t == SUBLANE
        shape3 = lambda a: a.reshape(1, n, -1)
        out, new_state = _ffn(shape3(h), shape3(a1), shape3(a2), *args, st['conv_rows'], layer, gfinal, tm, FFN_TF,
                              False, final_norm)
        new_state = jnp.swapaxes(new_state, 0, 1)
    return out.reshape(n, d), new_state


def kernel(x_prompt, x_sample, state_mlstm_C, state_mlstm_n, state_mlstm_m, cache_swa_k, cache_swa_v,
           state_gla, state_s5_re, state_s5_im, state_ffn_conv, norm_mix, norm_ffn, norm_final,
           w_in_even, b_mlstm_gates, sw_sinks, w_out_even, w_in_odd, w_gla_gate_up, b_gla_gate, g_gla_norm,
           s5_a_re, s5_a_im, s5_log_dt, s5_b_re, s5_b_im, s5_c_re, s5_c_im, s5_d, w_s5_glu, b_s5_glu,
           w_out_odd, w_ffn_up, ffn_conv_w, ffn_conv_b, w_ffn_down):
    W = {'norm_mix': norm_mix.astype(F32), 'norm_ffn': norm_ffn.astype(F32), 'norm_final': norm_final,
         'w_in_even': w_in_even, 'b_mlstm_gates': b_mlstm_gates, 'sw_sinks': sw_sinks, 'w_out_even': w_out_even,
         'w_in_odd': w_in_odd, 'w_gla_gate_up': w_gla_gate_up, 'b_gla_gate': b_gla_gate, 'g_gla_norm': g_gla_norm,
         's5_a_re': s5_a_re, 's5_a_im': s5_a_im, 's5_log_dt': s5_log_dt, 's5_b_re': s5_b_re, 's5_b_im': s5_b_im,
         's5_c_re': s5_c_re, 's5_c_im': s5_c_im, 's5_d': s5_d, 'w_s5_glu': w_s5_glu, 'b_s5_glu': b_s5_glu,
         'w_out_odd': w_out_odd, 'w_ffn_up': w_ffn_up, 'ffn_conv_w': ffn_conv_w, 'ffn_conv_b': ffn_conv_b,
         'w_ffn_down': w_ffn_down}
    P = _prep_weights(W)
    bp, tp = x_prompt.shape[:2]
    st_prompt = {'ml_C': jnp.zeros((bp, ML_HEADS, ML_DK, ML_DV), F32),
                 'ml_n': jnp.zeros((bp, ML_HEADS, ML_DK), F32),
                 'ml_m': jnp.zeros((bp, ML_HEADS), F32),
                 'gla': jnp.zeros((bp, GLA_HEADS, GLA_DK, GLA_DV), F32),
                 's5_re': jnp.zeros((bp, S5_GROUPS, S5_STATE), F32),
                 's5_im': jnp.zeros((bp, S5_GROUPS, S5_STATE), F32),
                 'conv': jnp.zeros((2, bp, CONV_W - 1, 2 * D_FF), F32)}
    st_sample = {'ml_C': state_mlstm_C.astype(F32), 'ml_n': state_mlstm_n.astype(F32),
                 'ml_m': state_mlstm_m.astype(F32), 'kbuf': cache_swa_k.astype(F32),
                 'vbuf': cache_swa_v.astype(F32), 'gla': state_gla.astype(F32),
                 's5_re': state_s5_re, 's5_im': state_s5_im,
                 'conv_rows': jnp.swapaxes(state_ffn_conv.astype(F32), 1, 2)}
    past_len = 16384
    y_p, np_ = _trunk(x_prompt.astype(F32), jnp.arange(tp), st_prompt, W, P, True)
    y_s, ns_ = _trunk(x_sample.astype(F32), past_len + jnp.arange(x_sample.shape[1]), st_sample, W, P, False)
    return (y_p, y_s,
            np_['ml_C'], ns_['ml_C'], np_['ml_n'], ns_['ml_n'], np_['ml_m'], ns_['ml_m'],
            np_['kbuf'], ns_['kbuf'], np_['vbuf'], ns_['vbuf'], np_['gla'], ns_['gla'],
            np_['s5_re'], ns_['s5_re'], np_['s5_im'], ns_['s5_im'], np_['conv'], ns_['conv'])
```

```python
import functools
import math

import numpy as np
import jax
import jax.numpy as jnp
from jax import lax
from jax.experimental import pallas as pl
from jax.experimental.pallas import tpu as pltpu

F32 = jnp.float32
BF16 = jnp.bfloat16
HI = lax.Precision.HIGHEST
NT = (((1,), (1,)), ((), ()))
TN = (((0,), (0,)), ((), ()))

D_MODEL = 1024
ML_HEADS, ML_DK, ML_DV = 4, 128, 128
SW_HEADS, SW_KV_HEADS, SW_HD, WINDOW = 8, 2, 64, 128
SW_GQ = SW_HEADS // SW_KV_HEADS
ROPE_THETA = 10000.0
GLA_HEADS, GLA_DK, GLA_DV, GLA_RANK, GLA_TAU = 4, 64, 128, 16, 16.0
S5_WIDTH, S5_GROUP, S5_STATE = 512, 16, 64
S5_GROUPS = S5_WIDTH // S5_GROUP
S5_LANES = S5_GROUPS * S5_STATE
D_FF = 2816
FFN_TF = 2816
CONV_W = 3
CHUNK = 64
EPS = 1e-6

LANE = 128
SUBLANE = 8
VMEM_LIMIT = 56 * 1024 * 1024

EVEN_QA, EVEN_KD, EVEN_VD, EVEN_PAD = 2048, 2560, 2816, 3200
ODD_PAD = 2176


def _params(sem):
    return pltpu.CompilerParams(dimension_semantics=sem, vmem_limit_bytes=VMEM_LIMIT)


def _norm_matmul_kernel(x_ref, g_ref, w_ref, o_ref):
    x = x_ref[...]
    ms = jnp.mean(x * x, axis=-1, keepdims=True)
    hn = (x * lax.rsqrt(ms + EPS) * g_ref[...]).astype(BF16)
    o_ref[...] = jnp.dot(hn, w_ref[...], preferred_element_type=F32)


def _norm_matmul(x, g, w, tm):
    n, d = x.shape
    c = w.shape[1]
    return pl.pallas_call(
        _norm_matmul_kernel,
        grid=(n // tm,),
        in_specs=[pl.BlockSpec((tm, d), lambda i: (i, 0)),
                  pl.BlockSpec((1, d), lambda i: (0, 0)),
                  pl.BlockSpec((d, c), lambda i: (0, 0))],
        out_specs=pl.BlockSpec((tm, c), lambda i: (i, 0)),
        out_shape=jax.ShapeDtypeStruct((n, c), F32),
        compiler_params=_params(("parallel",)),
        name="norm_matmul",
    )(x, g.reshape(1, d), w)


def _pieces(x, n):
    out, r = [], x
    for _ in range(n):
        p = r.astype(BF16).astype(F32)
        out.append(p)
        r = r - p
    return out


def _dot_sel(m_bf16, x, dims=None, pieces=3):
    ps = _pieces(x, pieces)
    axis = 1 if dims is None else 0
    w = x.shape[axis]
    if w % LANE:
        f = (lambda p: jnp.dot(m_bf16, p, preferred_element_type=F32)) if dims is None else (
            lambda p: lax.dot_general(m_bf16, p, dims, preferred_element_type=F32))
        return sum(f(p.astype(BF16)) for p in ps)
    stacked = jnp.concatenate(ps, axis=axis).astype(BF16)
    if dims is None:
        r = jnp.dot(m_bf16, stacked, preferred_element_type=F32)
    else:
        r = lax.dot_general(m_bf16, stacked, dims, preferred_element_type=F32)
    return sum(r[:, i * w:(i + 1) * w] for i in range(pieces))


def _sel_right(x, e_bf16, pieces=3):
    rows = x.shape[0]
    r = jnp.dot(jnp.concatenate(_pieces(x, pieces), axis=0).astype(BF16), e_bf16, preferred_element_type=F32)
    return sum(r[i * rows:(i + 1) * rows] for i in range(pieces))


def _mlstm_kernel(q_ref, k_ref, v_ref, og_ref, gz_ref, bias_ref, tri_ref, ones_ref, expc_ref, expv_ref, eyet_ref,
                  seg_ref, segv_ref, hmk_ref, c0_ref, n0_ref, m0_ref,
                  h_ref, co_ref, no_ref, mo_ref,
                  c_s, n_s, m_s, *, L, Bb, nsub, G):
    c = pl.program_id(1)

    @pl.when(c == 0)
    def _():
        c_s[...] = c0_ref[...]
        n_s[...] = n0_ref[...]
        m_s[...] = m0_ref[...]

    for sub in range(nsub):
        for bi in range(Bb):
            _mlstm_one(bi, sub * L, q_ref, k_ref, v_ref, og_ref, gz_ref, bias_ref, tri_ref, ones_ref, expc_ref,
                       expv_ref, eyet_ref, seg_ref, segv_ref, hmk_ref, h_ref, c_s, n_s, m_s, L, G)

    @pl.when(c == pl.num_programs(1) - 1)
    def _():
        co_ref[...] = c_s[...]
        no_ref[...] = n_s[...]
        mo_ref[...] = m_s[...]


def _mlstm_one(bi, r0, q_ref, k_ref, v_ref, og_ref, gz_ref, bias_ref, tri_ref, ones_ref, expc_ref, expv_ref,
               eyet_ref, seg_ref, segv_ref, hmk_ref, h_ref, c_s, n_s, m_s, L, G):
    H, DK = ML_HEADS, ML_DK
    HL = H * L
    S = L // G
    seg = lambda a, g: a[g * S:(g + 1) * S]
    per_row = lambda rows: rows[0] if G == 1 else jnp.concatenate(
        [jnp.broadcast_to(r, (S, r.shape[1])) for r in rows], axis=0)
    lane = lax.broadcasted_iota(jnp.int32, (L, LANE), 1)
    rowi = lax.broadcasted_iota(jnp.int32, (L, LANE), 0) % S
    keep = lane < H
    heads_only = lambda x: jnp.where(keep[:x.shape[0]], x, 0.0)

    rw = pl.ds(r0, L)
    gates = gz_ref[bi, rw] + bias_ref[...]
    lf = jax.nn.log_sigmoid(gates)
    b = pltpu.roll(_dot_sel(tri_ref[...], lf), LANE - H, axis=1)
    vv = gates - b
    cm = vv
    sh = 1
    while sh < S:
        cm = jnp.maximum(cm, jnp.where(rowi >= sh, pltpu.roll(cm, sh, axis=0), -jnp.inf))
        sh *= 2
    m_prev_g = [m_s[bi * G + g] for g in range(G)]
    m_prev = per_row(m_prev_g)
    mt = b + jnp.maximum(m_prev, cm)
    a_inter = jnp.exp(b + m_prev - mt)

    expv = expv_ref[...]
    m_new_g = [seg(mt, g)[S - 1:S, :] for g in range(G)]
    b_end_g = [seg(b, g)[S - 1:S, :] for g in range(G)]
    w_end = heads_only(jnp.exp(per_row(b_end_g) - b + gates - per_row(m_new_g)))
    decay_g = [heads_only(jnp.exp(b_end_g[g] + m_prev_g[g] - m_new_g[g])) for g in range(G)]
    decay_rows = jnp.concatenate(decay_g + [jnp.zeros((-G % SUBLANE, LANE), F32)] * (G % SUBLANE != 0), axis=0)
    spread_k = _sel_right(jnp.concatenate([heads_only(b - mt), heads_only(vv)], axis=0), expc_ref[...])
    uc = spread_k[:L]
    vr = _dot_sel(ones_ref[...], spread_k[L:] * eyet_ref[...])
    spread_v = _sel_right(jnp.concatenate([heads_only(a_inter), w_end, decay_rows], axis=0), expv, pieces=2)
    ac, wc, dcs = spread_v[:L], spread_v[L:2 * L], spread_v[2 * L:2 * L + G]
    ti = lax.broadcasted_iota(jnp.int32, (L, HL), 0)
    si = lax.broadcasted_iota(jnp.int32, (L, HL), 1) % L
    causal = (si <= ti) & (si // S == ti // S)
    w = jnp.exp(jnp.where(causal, uc + vr, -jnp.inf))

    q = q_ref[bi, rw]
    ks = k_ref[bi, rw] * (DK ** -0.5)
    v = v_ref[bi, rw]
    qb = q.astype(BF16)
    if L % (2 * SUBLANE) == 0:
        kst = jnp.concatenate([ks.astype(BF16)] * H, axis=0) * hmk_ref[...]
        vsrc = v.astype(BF16)
    else:
        kst = (jnp.concatenate([ks] * H, axis=0) * hmk_ref[...].astype(F32)).astype(BF16)
        vsrc = v
    zero = jnp.zeros((L, DK), vsrc.dtype)
    vbd = jnp.concatenate(
        [jnp.concatenate([zero] * h + [vsrc[:, h * DK:(h + 1) * DK]] + [zero] * (H - 1 - h), axis=1)
         for h in range(H)], axis=0).astype(BF16)
    s = lax.dot_general(qb, kst, NT, preferred_element_type=F32) * w
    qc = jnp.concatenate(
        [jnp.concatenate([jnp.dot(seg(q, g)[:, h * DK:(h + 1) * DK].astype(BF16), c_s[bi * G + g, h].astype(BF16),
                                  preferred_element_type=F32) for h in range(H)], axis=1)
         for g in range(G)], axis=0)
    num = jnp.dot(s.astype(BF16), vbd, preferred_element_type=F32) + ac * qc
    n_rows = per_row([n_s[bi * G + g] for g in range(G)])
    den = (_sel_right(s, seg_ref[...], pieces=2)
           + a_inter * _sel_right(q * n_rows, segv_ref[...], pieces=2))
    rden = heads_only(1.0 / jnp.maximum(jnp.abs(den), jnp.exp(-mt)))
    hh = num * _sel_right(rden, expv, pieces=2) * jax.nn.sigmoid(og_ref[bi, rw])
    h_ref[bi, rw] = hh.astype(h_ref.dtype)

    kw = ks * wc
    for g in range(G):
        e = bi * G + g
        kwb = seg(kw, g).astype(BF16)
        vb = seg(v, g).astype(BF16)
        dc = dcs[g:g + 1]
        for h in range(H):
            hs = slice(h * DK, (h + 1) * DK)
            c_s[e, h] = dc[:, hs] * c_s[e, h] + lax.dot_general(kwb[:, hs], vb[:, hs], TN, preferred_element_type=F32)
        n_s[e] = dc * n_s[e] + jnp.sum(seg(kw, g), axis=0, keepdims=True)
        m_s[e] = heads_only(m_new_g[g])


def _mlstm_consts(S, G):
    L = S * G
    H, DK = ML_HEADS, ML_DK
    expc = np.zeros((LANE, H * L), np.float32)
    expv = np.zeros((LANE, H * DK), np.float32)
    seg = np.zeros((H * L, LANE), np.float32)
    segv = np.zeros((H * DK, LANE), np.float32)
    hmk = np.zeros((H * L, H * DK), np.float32)
    for h in range(H):
        expc[h, h * L:(h + 1) * L] = 1.0
        expv[h, h * DK:(h + 1) * DK] = 1.0
        seg[h * L:(h + 1) * L, h] = 1.0
        segv[h * DK:(h + 1) * DK, h] = 1.0
        hmk[h * L:(h + 1) * L, h * DK:(h + 1) * DK] = 1.0
    eyet = np.tile(np.eye(L, dtype=np.float32), (1, H))
    bf = lambda a: jnp.asarray(a, BF16)
    tri = np.kron(np.eye(G), np.tril(np.ones((S, S)))).astype(np.float32)
    return (bf(tri), bf(np.ones((L, L), np.float32)), bf(expc), bf(expv),
            jnp.asarray(eyet), bf(seg), bf(segv), bf(hmk))


def _mlstm(z3, bias, c0, n0, m0, S, G, Bb, nsub):
    nseq, t_seq, width = z3.shape
    if G > 1:
        assert t_seq == S and nsub == 1
        z3 = z3.reshape(nseq // G, G * S, width)
    b, t, _ = z3.shape
    L = G * S
    nc = t // (L * nsub)
    hw = ML_HEADS * ML_DK
    consts = _mlstm_consts(S, G)
    gate_blk = (EVEN_PAD - LANE) // LANE
    col = lambda j: (lambda bi, ci: (bi, ci, j))
    const2 = lambda bi, ci: (0, 0)
    state3 = lambda bi, ci: (bi, 0, 0)
    m0p = jnp.pad(m0, ((0, 0), (0, LANE - ML_HEADS))).reshape(nseq, 1, LANE)
    Bs = Bb * G
    h, c_new, n_new, m_new = pl.pallas_call(
        functools.partial(_mlstm_kernel, L=L, Bb=Bb, nsub=nsub, G=G),
        grid=(b // Bb, nc),
        in_specs=[pl.BlockSpec((Bb, nsub * L, hw), col(0)),
                  pl.BlockSpec((Bb, nsub * L, hw), col(1)),
                  pl.BlockSpec((Bb, nsub * L, hw), col(2)),
                  pl.BlockSpec((Bb, nsub * L, hw), col(3)),
                  pl.BlockSpec((Bb, nsub * L, LANE), col(gate_blk)),
                  pl.BlockSpec((1, LANE), const2)]
                 + [pl.BlockSpec(a.shape, const2) for a in consts]
                 + [pl.BlockSpec((Bs, ML_HEADS, ML_DK, ML_DV), lambda bi, ci: (bi, 0, 0, 0)),
                    pl.BlockSpec((Bs, 1, hw), state3),
                    pl.BlockSpec((Bs, 1, LANE), state3)],
        out_specs=[pl.BlockSpec((Bb, nsub * L, hw), col(0)),
                   pl.BlockSpec((Bs, ML_HEADS, ML_DK, ML_DV), lambda bi, ci: (bi, 0, 0, 0)),
                   pl.BlockSpec((Bs, 1, hw), state3),
                   pl.BlockSpec((Bs, 1, LANE), state3)],
        out_shape=[jax.ShapeDtypeStruct((b, t, hw), BF16),
                   jax.ShapeDtypeStruct((nseq, ML_HEADS, ML_DK, ML_DV), F32),
                   jax.ShapeDtypeStruct((nseq, 1, hw), F32),
                   jax.ShapeDtypeStruct((nseq, 1, LANE), F32)],
        scratch_shapes=[pltpu.VMEM((Bs, ML_HEADS, ML_DK, ML_DV), F32),
                        pltpu.VMEM((Bs, 1, hw), F32),
                        pltpu.VMEM((Bs, 1, LANE), F32)],
        compiler_params=_params(("parallel", "arbitrary")),
        name="mlstm",
    )(z3, z3, z3, z3, z3, bias, *consts, c0, n0.reshape(nseq, 1, hw), m0p)
    return h.reshape(nseq, t_seq, hw), c_new, n_new.reshape(nseq, ML_HEADS, ML_DK), m_new[:, 0, :ML_HEADS]


def _rope(x, cos, sin_signed, width):
    half = SW_HD // 2
    lane = lax.broadcasted_iota(jnp.int32, x.shape, 1)
    first = (lane % SW_HD) < half
    partner = jnp.where(first, pltpu.roll(x, width - half, axis=1), pltpu.roll(x, half, axis=1))
    return x * cos + partner * sin_signed


def _undup(xd):
    lane = lax.broadcasted_iota(jnp.int32, xd.shape[:-1] + (LANE,), xd.ndim - 1)
    return jnp.where(lane < SW_HD, xd[..., :LANE], xd[..., LANE:])


def _swa_prompt_kernel(sink_ref, q_ref, kc_ref, kp_ref, vc_ref, vp_ref, cosc_ref, sinc_ref, cosp_ref, sinp_ref,
                       h_ref, ko_ref):
    n = pl.program_id(1)
    W2 = 2 * WINDOW
    cosc, sinc = cosc_ref[...], sinc_ref[...]
    cosp, sinp = cosp_ref[...], sinp_ref[...]
    i = lax.broadcasted_iota(jnp.int32, (WINDOW, W2), 0)
    j = lax.broadcasted_iota(jnp.int32, (WINDOW, W2), 1)
    diff = WINDOW + i - j
    valid = (diff >= 0) & (diff < WINDOW) & ((n > 0) | (j >= WINDOW))
    row = lax.broadcasted_iota(jnp.int32, (2 * W2, LANE), 0)
    lane = lax.broadcasted_iota(jnp.int32, (2 * W2, LANE), 1)
    bd = (row < W2) == (lane < SW_HD)
    ones_bd = bd.astype(BF16)
    npair = SW_GQ // 2
    rows = npair * WINDOW
    valid = jnp.concatenate([valid] * npair, axis=0)
    low = lax.broadcasted_iota(jnp.int32, (rows, LANE), 1) < SW_HD
    prow = lax.broadcasted_iota(jnp.int32, (rows, 1), 0) // WINDOW
    cos4, sin4 = jnp.concatenate([cosc] * 4, axis=1), jnp.concatenate([sinc] * 4, axis=1)
    for bi in range(q_ref.shape[0]):
        q = _rope(q_ref[bi], cos4, sin4, 4 * LANE)
        k_cur = _rope(kc_ref[bi], cos4[:, :2 * LANE], sin4[:, :2 * LANE], 2 * LANE)
        k_prev = _rope(kp_ref[bi], jnp.concatenate([cosp] * 2, axis=1), jnp.concatenate([sinp] * 2, axis=1), 2 * LANE)
        kd = jnp.concatenate([k_prev, k_cur], axis=0)
        vd = jnp.concatenate([vp_ref[bi], vc_ref[bi]], axis=0)
        for kv in range(SW_KV_HEADS):
            kblk = kd[:, kv * LANE:(kv + 1) * LANE]
            vblk = vd[:, kv * LANE:(kv + 1) * LANE]
            kbd = jnp.where(bd, jnp.concatenate([kblk, kblk], axis=0), 0.0).astype(BF16)
            vbd = jnp.where(bd, jnp.concatenate([vblk, vblk], axis=0), 0.0).astype(BF16)
            vo = jnp.concatenate([vbd, ones_bd], axis=1)
            h0 = kv * SW_GQ
            qs = jnp.concatenate([q[:, (h0 + 2 * p) * SW_HD:(h0 + 2 * p + 2) * SW_HD] for p in range(npair)],
                                 axis=0).astype(BF16)
            s = lax.dot_general(qs, kbd, NT, preferred_element_type=F32) * (SW_HD ** -0.5)
            es, sink_terms = [], []
            for hh in range(2):
                sh = jnp.where(valid, s[:, hh * W2:(hh + 1) * W2], -jnp.inf)
                sink = sink_ref[h0 + hh]
                for p in range(1, npair):
                    sink = jnp.where(prow == p, sink_ref[h0 + 2 * p + hh], sink)
                m = jnp.maximum(jnp.max(sh, axis=1, keepdims=True), sink)
                es.append(jnp.exp(sh - m).astype(BF16))
                sink_terms.append(jnp.exp(sink - m))
            r = jnp.dot(jnp.concatenate(es, axis=1), vo, preferred_element_type=F32)
            o = r[:, :LANE] / (r[:, LANE:] + jnp.where(low, sink_terms[0], sink_terms[1]))
            for p in range(npair):
                h_ref[bi, :, (h0 + 2 * p) * SW_HD:(h0 + 2 * p + 2) * SW_HD] = (
                    o[p * WINDOW:(p + 1) * WINDOW].astype(h_ref.dtype))

        @pl.when(n == pl.num_programs(1) - 1)
        def _(bi=bi, k_cur=k_cur):
            ko_ref[bi] = _undup(k_cur)


def _swa_prompt(z3, cos, sin_signed, sinks, Bb):
    b, t, _ = z3.shape
    nb = t // WINDOW
    qw = SW_HEADS * SW_HD
    kw = SW_KV_HEADS * SW_HD
    dw = 2 * kw
    qblk, kblk, vblk = EVEN_QA // qw, EVEN_KD // dw, EVEN_VD // dw
    cur = lambda j: (lambda bi, ni: (bi, ni, j))
    prev = lambda j: (lambda bi, ni: (bi, jnp.maximum(ni - 1, 0), j))
    return pl.pallas_call(
        _swa_prompt_kernel,
        grid=(b // Bb, nb),
        in_specs=[pl.BlockSpec(memory_space=pltpu.SMEM),
                  pl.BlockSpec((Bb, WINDOW, qw), cur(qblk)),
                  pl.BlockSpec((Bb, WINDOW, dw), cur(kblk)),
                  pl.BlockSpec((Bb, WINDOW, dw), prev(kblk)),
                  pl.BlockSpec((Bb, WINDOW, dw), cur(vblk)),
                  pl.BlockSpec((Bb, WINDOW, dw), prev(vblk)),
                  pl.BlockSpec((WINDOW, LANE), lambda bi, ni: (ni, 0)),
                  pl.BlockSpec((WINDOW, LANE), lambda bi, ni: (ni, 0)),
                  pl.BlockSpec((WINDOW, LANE), lambda bi, ni: (jnp.maximum(ni - 1, 0), 0)),
                  pl.BlockSpec((WINDOW, LANE), lambda bi, ni: (jnp.maximum(ni - 1, 0), 0))],
        out_specs=[pl.BlockSpec((Bb, WINDOW, qw), cur(0)),
                   pl.BlockSpec((Bb, WINDOW, kw), lambda bi, ni: (bi, 0, 0))],
        out_shape=[jax.ShapeDtypeStruct((b, t, qw), BF16),
                   jax.ShapeDtypeStruct((b, WINDOW, kw), F32)],
        compiler_params=_params(("parallel", "arbitrary")),
        name="swa_prompt",
    )(sinks, z3, z3, z3, z3, z3, cos, sin_signed, cos, sin_signed)


def _swa_sample_kernel(q_ref, k_ref, v_ref, kbuf_ref, vbuf_ref, cos_ref, sin_ref, sink_ref,
                       h_ref, ko_ref, vo_ref, *, T, Bb):
    qw = SW_HEADS * SW_HD
    kw = SW_KV_HEADS * SW_HD
    cos = cos_ref[...]
    sin = sin_ref[...]
    q = _rope(q_ref[...].reshape(Bb * T, qw), jnp.concatenate([cos] * (qw // LANE), axis=1),
              jnp.concatenate([sin] * (qw // LANE), axis=1), qw).reshape(Bb, T, qw)
    k_new = _rope(_undup(k_ref[...]).reshape(Bb * T, kw), cos, sin, kw).reshape(Bb, T, kw)
    kk = jnp.concatenate([kbuf_ref[...], k_new], axis=1)
    vv = jnp.concatenate([vbuf_ref[...], _undup(v_ref[...])], axis=1)
    ko_ref[...] = kk[:, T:, :]
    vo_ref[...] = vv[:, T:, :]
    rows = SW_GQ * T
    i = lax.broadcasted_iota(jnp.int32, (rows, WINDOW + T), 0) % T
    j = lax.broadcasted_iota(jnp.int32, (rows, WINDOW + T), 1)
    diff = WINDOW + i - j
    valid = (diff >= 0) & (diff < WINDOW)
    for kv in range(SW_KV_HEADS):
        qs = jnp.concatenate([q[:, :, (kv * SW_GQ + g) * SW_HD:(kv * SW_GQ + g + 1) * SW_HD]
                              for g in range(SW_GQ)], axis=1).astype(BF16)
        kh = kk[:, :, kv * SW_HD:(kv + 1) * SW_HD].astype(BF16)
        vh = vv[:, :, kv * SW_HD:(kv + 1) * SW_HD].astype(BF16)
        s = jnp.einsum('bqd,bkd->bqk', qs, kh, preferred_element_type=F32) * (SW_HD ** -0.5)
        s = jnp.where(valid, s, -jnp.inf)
        sink = sink_ref[kv * rows:(kv + 1) * rows, :]
        m = jnp.maximum(jnp.max(s, axis=-1, keepdims=True), sink)
        e = jnp.exp(s - m)
        p = e / (jnp.sum(e, axis=-1, keepdims=True) + jnp.exp(sink - m))
        o = jnp.einsum('bqk,bkd->bqd', p.astype(BF16), vh, preferred_element_type=F32)
        for g in range(SW_GQ):
            hh = kv * SW_GQ + g
            h_ref[:, :, hh * SW_HD:(hh + 1) * SW_HD] = o[:, g * T:(g + 1) * T, :].astype(h_ref.dtype)


def _swa_sample(z3, kbuf, vbuf, cos, sin_signed, sinks, Bb):
    b, t, _ = z3.shape
    qw = SW_HEADS * SW_HD
    kw = SW_KV_HEADS * SW_HD
    dw = 2 * kw
    qblk, kblk, vblk = EVEN_QA // qw, EVEN_KD // dw, EVEN_VD // dw
    col = lambda j: (lambda bi: (bi, 0, j))
    const2 = lambda bi: (0, 0)
    sink_col = jnp.repeat(sinks, t).reshape(SW_HEADS * t, 1)
    return pl.pallas_call(
        functools.partial(_swa_sample_kernel, T=t, Bb=Bb),
        grid=(b // Bb,),
        in_specs=[pl.BlockSpec((Bb, t, qw), col(qblk)),
                  pl.BlockSpec((Bb, t, dw), col(kblk)),
                  pl.BlockSpec((Bb, t, dw), col(vblk)),
                  pl.BlockSpec((Bb, WINDOW, kw), col(0)),
                  pl.BlockSpec((Bb, WINDOW, kw), col(0)),
                  pl.BlockSpec((Bb * t, LANE), const2),
                  pl.BlockSpec((Bb * t, LANE), const2),
                  pl.BlockSpec((SW_HEADS * t, 1), const2)],
        out_specs=[pl.BlockSpec((Bb, t, qw), col(0)),
                   pl.BlockSpec((Bb, WINDOW, kw), col(0)),
                   pl.BlockSpec((Bb, WINDOW, kw), col(0))],
        out_shape=[jax.ShapeDtypeStruct((b, t, qw), BF16),
                   jax.ShapeDtypeStruct((b, WINDOW, kw), F32),
                   jax.ShapeDtypeStruct((b, WINDOW, kw), F32)],
        compiler_params=_params(("parallel",)),
        name="swa_sample",
    )(z3, z3, z3, kbuf, vbuf, jnp.tile(cos, (Bb, 1)), jnp.tile(sin_signed, (Bb, 1)), sink_col)


def _gla_tables(S, G):
    nlev = int(math.log2(S))
    assert 2 ** nlev == S
    L = G * S
    mstack = np.zeros((nlev * L, L), np.float32)
    masks = np.zeros((nlev + 1, L, L), np.float32)
    masks[0] = np.eye(L)
    for l in range(1, nlev + 1):
        n = 2 ** l
        for r in range(L):
            blk, pos = divmod(r, n)
            m = blk * n + n // 2 - 1
            if pos >= n // 2:
                mstack[(l - 1) * L + r, m + 1:r + 1] = 1.0
                masks[l, r, blk * n:blk * n + n // 2] = 1.0
            else:
                mstack[(l - 1) * L + r, r + 1:m + 1] = 1.0
    masks = np.tile(masks, (1, 1, GLA_HEADS))
    tri = np.kron(np.eye(G), np.tril(np.ones((S, S)))).astype(np.float32)
    return nlev, jnp.asarray(tri, BF16), jnp.asarray(mstack, BF16), jnp.asarray(masks)


def _gla_consts(L):
    kw = GLA_HEADS * GLA_DK
    vw = GLA_HEADS * GLA_DV
    hk = np.zeros((GLA_HEADS * L, kw), np.float32)
    for h in range(GLA_HEADS):
        hk[h * L:(h + 1) * L, h * GLA_DK:(h + 1) * GLA_DK] = 1.0
    hs = np.zeros((vw, kw), np.float32)
    for h in range(GLA_HEADS):
        hs[h * GLA_DV:(h + 1) * GLA_DV, h * GLA_DK:(h + 1) * GLA_DK] = 1.0
    return (jnp.asarray(hk, BF16), jnp.asarray(hs), jnp.asarray(np.eye(GLA_DV), BF16), jnp.asarray(np.eye(kw), BF16))


def _gla_kernel(q_ref, k_ref, v_ref, r_ref, glr_ref, wg_ref, bg_ref, gn_ref, tri_ref, mstack_ref,
                masks_ref, hk_ref, hs_ref, eyev_ref, eyek_ref, s0_ref,
                o_ref, so_ref, st_s, *, L, nlev, Bb, nsub, G):
    c = pl.program_id(1)
    H, DK, DV = GLA_HEADS, GLA_DK, GLA_DV

    @pl.when(c == 0)
    def _():
        for bi in range(Bb * G):
            for h in range(H):
                pieces = ([jnp.zeros((h * DK, DV), F32)] if h else []) + [s0_ref[bi, h]]
                pieces += [jnp.zeros(((H - 1 - h) * DK, DV), F32)] if h < H - 1 else []
                padded = jnp.concatenate(pieces, axis=0)
                st_s[bi, h * DV:(h + 1) * DV, :] = _dot_sel(eyev_ref[...], padded, NT)

    for sub in range(nsub):
        for bi in range(Bb):
            _gla_one(bi, sub * L, q_ref, k_ref, v_ref, r_ref, glr_ref, wg_ref, bg_ref, gn_ref, tri_ref, mstack_ref,
                     masks_ref, hk_ref, hs_ref, o_ref, st_s, L, nlev, G)

    @pl.when(c == pl.num_programs(1) - 1)
    def _():
        for bi in range(Bb * G):
            for h in range(H):
                blk = st_s[bi, h * DV:(h + 1) * DV, :]
                so_ref[bi, h] = _dot_sel(eyek_ref[h * DK:(h + 1) * DK, :], blk, NT)


def _gla_one(bi, r0, q_ref, k_ref, v_ref, r_ref, glr_ref, wg_ref, bg_ref, gn_ref, tri_ref, mstack_ref,
             masks_ref, hk_ref, hs_ref, o_ref, st_s, L, nlev, G):
    H, DK, DV = GLA_HEADS, GLA_DK, GLA_DV
    packed_rows = L % (2 * SUBLANE) == 0
    rw = pl.ds(r0, L)
    q = q_ref[bi, rw] * (DK ** -0.5)
    k = k_ref[bi, rw]
    kwid = H * DK
    g_hi, g_lo = _pieces(glr_ref[bi, rw], 2)
    pre2 = jnp.dot(g_hi.astype(BF16), wg_ref[...], preferred_element_type=F32)
    pre = (pre2[:, :kwid] + pre2[:, kwid:] + jnp.dot(g_lo.astype(BF16), wg_ref[:, :kwid], preferred_element_type=F32)
           + bg_ref[...])
    lg = jax.nn.log_sigmoid(pre) * (1.0 / GLA_TAU)
    Gc = _dot_sel(tri_ref[...], lg)
    E = jnp.exp(_dot_sel(mstack_ref[...], lg, pieces=2))
    hk = hk_ref[...]
    tok = lax.broadcasted_iota(jnp.int32, (L, kwid), 0)

    att = None
    for l in range(nlev + 1):
        if l == 0:
            ql, kl = q, k
        else:
            e_l = E[(l - 1) * L:l * L]
            upper = ((tok >> (l - 1)) & 1) == 1
            ql = jnp.where(upper, q * e_l, 0.0)
            kl = jnp.where(upper, 0.0, k * e_l)
        if packed_rows:
            kst = jnp.concatenate([kl.astype(BF16)] * H, axis=0) * hk
        else:
            kst = (jnp.concatenate([kl] * H, axis=0) * hk.astype(F32)).astype(BF16)
        part = lax.dot_general(ql.astype(BF16), kst, NT, preferred_element_type=F32)
        part = jnp.where(masks_ref[l] > 0.5, part, 0.0)
        att = part if att is None else att + part

    v = v_ref[bi, rw]
    vsrc = v.astype(BF16) if packed_rows else v
    zero = jnp.zeros((L, DV), vsrc.dtype)
    vbd = jnp.concatenate(
        [jnp.concatenate([zero] * h + [vsrc[:, h * DV:(h + 1) * DV]] + [zero] * (H - 1 - h), axis=1)
         for h in range(H)], axis=0).astype(BF16)
    S = L // G
    seg = lambda a, g: a[g * S:(g + 1) * S]
    qg = q * jnp.exp(Gc)
    o = jnp.dot(att.astype(BF16), vbd, preferred_element_type=F32) + jnp.concatenate(
        [lax.dot_general(seg(qg, g).astype(BF16), st_s[bi * G + g].astype(BF16), NT, preferred_element_type=F32)
         for g in range(G)], axis=0)
    for h in range(H):
        vs = slice(h * DV, (h + 1) * DV)
        oh = o[:, vs]
        oh = oh * lax.rsqrt(jnp.mean(oh * oh, axis=-1, keepdims=True) + EPS)
        oh = oh * gn_ref[:, vs] * jax.nn.silu(r_ref[bi, rw, vs])
        o_ref[bi, rw, vs] = oh.astype(o_ref.dtype)
    for g in range(G):
        g_seg = seg(Gc, g)
        g_end = g_seg[S - 1:S, :]
        k_end = (seg(k, g) * jnp.exp(g_end - g_seg)).astype(BF16)
        upd = lax.dot_general(seg(v, g).astype(BF16), k_end, TN, preferred_element_type=F32)
        st_s[bi * G + g] = st_s[bi * G + g] * jnp.exp(g_end) + jnp.where(hs_ref[...] > 0.5, upd, 0.0)


def _gla(z3, wg, bg, gn, s0, S, G, Bb, nsub):
    nseq, t_seq, width = z3.shape
    if G > 1:
        assert t_seq == S and nsub == 1
        z3 = z3.reshape(nseq // G, G * S, width)
    b, t, _ = z3.shape
    L = G * S
    nc = t // (L * nsub)
    nlev, tri, mstack, masks = _gla_tables(S, G)
    hk, hs, eyev, eyek = _gla_consts(L)
    qk_w = GLA_HEADS * GLA_DK
    v_w = GLA_HEADS * GLA_DV
    col = lambda j: (lambda bi, ci: (bi, ci, j))
    const2 = lambda bi, ci: (0, 0)
    const3 = lambda bi, ci: (0, 0, 0)
    o, s_new = pl.pallas_call(
        functools.partial(_gla_kernel, L=L, nlev=nlev, Bb=Bb, nsub=nsub, G=G),
        grid=(b // Bb, nc),
        in_specs=[pl.BlockSpec((Bb, nsub * L, qk_w), col(0)),
                  pl.BlockSpec((Bb, nsub * L, qk_w), col(1)),
                  pl.BlockSpec((Bb, nsub * L, v_w), col(1)),
                  pl.BlockSpec((Bb, nsub * L, v_w), col(2)),
                  pl.BlockSpec((Bb, nsub * L, LANE), col(2048 // LANE)),
                  pl.BlockSpec((LANE, 2 * qk_w), const2),
                  pl.BlockSpec((1, qk_w), const2),
                  pl.BlockSpec((1, v_w), const2),
                  pl.BlockSpec((L, L), const2),
                  pl.BlockSpec(mstack.shape, const2),
                  pl.BlockSpec(masks.shape, const3),
                  pl.BlockSpec(hk.shape, const2),
                  pl.BlockSpec(hs.shape, const2),
                  pl.BlockSpec(eyev.shape, const2),
                  pl.BlockSpec(eyek.shape, const2),
                  pl.BlockSpec((Bb * G, GLA_HEADS, GLA_DK, GLA_DV), lambda bi, ci: (bi, 0, 0, 0))],
        out_specs=[pl.BlockSpec((Bb, nsub * L, v_w), col(0)),
                   pl.BlockSpec((Bb * G, GLA_HEADS, GLA_DK, GLA_DV), lambda bi, ci: (bi, 0, 0, 0))],
        out_shape=[jax.ShapeDtypeStruct((b, t, v_w), BF16),
                   jax.ShapeDtypeStruct((nseq, GLA_HEADS, GLA_DK, GLA_DV), F32)],
        scratch_shapes=[pltpu.VMEM((Bb * G, v_w, qk_w), F32)],
        compiler_params=_params(("parallel", "arbitrary")),
        name="gla",
    )(z3, z3, z3, z3, z3, wg, bg, gn, tri, mstack, masks, hk, hs, eyev, eyek, s0)
    return o.reshape(nseq, t_seq, v_w), s_new


def _s5_kernel(u_ref, bbre_ref, bbim_ref, cre_ref, cim_ref, d_ref, wglu_ref, bglu_ref, are_ref, aim_ref,
               x0re_ref, x0im_ref,
               o_ref, xore_ref, xoim_ref,
               xr_s, xi_s, cr_s, ci_s, pre_ref, pim_ref, *, rows, sequential):
    c = pl.program_id(1)
    lead = rows // SUBLANE if sequential else SUBLANE
    inner = rows // lead
    cw = S5_LANES // 4 if inner == SUBLANE else LANE
    u = jnp.swapaxes(u_ref[0].reshape(inner, lead, S5_WIDTH), 0, 1).reshape(rows, S5_WIDTH)
    ub = u.astype(BF16)

    if sequential:
        @pl.when(c == 0)
        def _():
            pre_ref[0] = are_ref[...]
            pim_ref[0] = aim_ref[...]
            n = 1
            while n < lead:
                nr, ni = pre_ref[n - 1], pim_ref[n - 1]
                sr, si = pre_ref[0:n], pim_ref[0:n]
                pre_ref[n:2 * n] = sr * nr - si * ni
                pim_ref[n:2 * n] = sr * ni + si * nr
                n *= 2

    hw, hl = S5_WIDTH // 2, S5_LANES // 2
    for half in range(2):
        uh = ub[:, half * hw:(half + 1) * hw]
        ls = slice(half * hl, (half + 1) * hl)
        xr_s[:, :, ls] = jnp.dot(uh, bbre_ref[half], preferred_element_type=F32).reshape(lead, inner, hl)
        xi_s[:, :, ls] = jnp.dot(uh, bbim_ref[half], preferred_element_type=F32).reshape(lead, inner, hl)

    for c0 in range(0, S5_LANES, cw):
        ar = jnp.broadcast_to(are_ref[:, c0:c0 + cw], (inner, cw))
        ai = jnp.broadcast_to(aim_ref[:, c0:c0 + cw], (inner, cw))

        def body(j, carry, c0=c0, ar=ar, ai=ai):
            xr, xi = carry
            nr = ar * xr - ai * xi + xr_s[j, :, c0:c0 + cw]
            ni = ar * xi + ai * xr + xi_s[j, :, c0:c0 + cw]
            xr_s[j, :, c0:c0 + cw] = nr
            xi_s[j, :, c0:c0 + cw] = ni
            return nr, ni

        if sequential:
            init = (jnp.zeros((inner, cw), F32), jnp.zeros((inner, cw), F32))
        else:
            init = (x0re_ref[0, :, c0:c0 + cw], x0im_ref[0, :, c0:c0 + cw])
        lax.fori_loop(0, lead, body, init, unroll=True)

    if sequential:
        @pl.when(c == 0)
        def _():
            cr_s[...] = x0re_ref[0]
            ci_s[...] = x0im_ref[0]

        end_r = xr_s[lead - 1]
        end_i = xi_s[lead - 1]
        anr = pre_ref[lead - 1]
        ani = pim_ref[lead - 1]
        ent_r, ent_i = [cr_s[...]], [ci_s[...]]
        for s in range(SUBLANE):
            pr, pi = ent_r[-1], ent_i[-1]
            ent_r.append(end_r[s:s + 1] + anr * pr - ani * pi)
            ent_i.append(end_i[s:s + 1] + anr * pi + ani * pr)
        cr_s[...] = ent_r[SUBLANE]
        ci_s[...] = ent_i[SUBLANE]
        er = jnp.concatenate(ent_r[:SUBLANE], axis=0)[None]
        ei = jnp.concatenate(ent_i[:SUBLANE], axis=0)[None]
        pr3, pi3 = pre_ref[...], pim_ref[...]
        xr = xr_s[...] + pr3 * er - pi3 * ei
        xi = xi_s[...] + pr3 * ei + pi3 * er

        @pl.when(c == pl.num_programs(1) - 1)
        def _():
            xore_ref[0] = cr_s[...]
            xoim_ref[0] = ci_s[...]
    else:
        xr = xr_s[...]
        xi = xi_s[...]
        xore_ref[0] = xr[lead - 1]
        xoim_ref[0] = xi[lead - 1]

    xrb = xr.reshape(rows, S5_LANES).astype(BF16)
    xib = xi.reshape(rows, S5_LANES).astype(BF16)
    y = jnp.concatenate(
        [jnp.dot(xrb[:, h * hl:(h + 1) * hl], cre_ref[h], preferred_element_type=F32)
         - jnp.dot(xib[:, h * hl:(h + 1) * hl], cim_ref[h], preferred_element_type=F32) for h in range(2)],
        axis=1) + d_ref[...] * u
    zz = jax.nn.gelu(y)
    gate = jnp.dot(zz.astype(BF16), wglu_ref[...], preferred_element_type=F32) + bglu_ref[...]
    out = zz * jax.nn.sigmoid(gate)
    o_ref[0] = jnp.swapaxes(out.reshape(lead, inner, S5_WIDTH), 0, 1).reshape(rows, S5_WIDTH).astype(o_ref.dtype)


def _s5(z3, tabs, x0re, x0im, rows, sequential):
    b, t, _ = z3.shape
    nblk = t // rows
    groups = rows // SUBLANE
    ublk = 1536 // S5_WIDTH
    const2 = lambda bi, ci: (0, 0)
    const3 = lambda bi, ci: (0, 0, 0)
    lead = groups if sequential else SUBLANE
    hw, hl = S5_WIDTH // 2, S5_LANES // 2
    if sequential:
        st_spec = pl.BlockSpec((1, 1, S5_LANES), lambda bi, ci: (bi, 0, 0))
        st_shape = jax.ShapeDtypeStruct((b, 1, S5_LANES), F32)
    else:
        st_spec = pl.BlockSpec((1, groups, S5_LANES), lambda bi, ci: (bi, ci, 0))
        st_shape = jax.ShapeDtypeStruct(x0re.shape, F32)
    return pl.pallas_call(
        functools.partial(_s5_kernel, rows=rows, sequential=sequential),
        grid=(b, nblk),
        in_specs=[pl.BlockSpec((1, rows, S5_WIDTH), lambda bi, ci: (bi, ci, ublk)),
                  pl.BlockSpec((2, hw, hl), const3),
                  pl.BlockSpec((2, hw, hl), const3),
                  pl.BlockSpec((2, hl, hw), const3),
                  pl.BlockSpec((2, hl, hw), const3),
                  pl.BlockSpec((1, S5_WIDTH), const2),
                  pl.BlockSpec((S5_WIDTH, S5_WIDTH), const2),
                  pl.BlockSpec((1, S5_WIDTH), const2),
                  pl.BlockSpec((1, S5_LANES), const2),
                  pl.BlockSpec((1, S5_LANES), const2),
                  st_spec, st_spec],
        out_specs=[pl.BlockSpec((1, rows, S5_WIDTH), lambda bi, ci: (bi, ci, 0)),
                   st_spec, st_spec],
        out_shape=[jax.ShapeDtypeStruct((b, t, S5_WIDTH), BF16), st_shape, st_shape],
        scratch_shapes=[pltpu.VMEM((lead, rows // lead, S5_LANES), F32),
                        pltpu.VMEM((lead, rows // lead, S5_LANES), F32),
                        pltpu.VMEM((1, S5_LANES), F32),
                        pltpu.VMEM((1, S5_LANES), F32),
                        pltpu.VMEM((lead if sequential else 1, 1, S5_LANES), F32),
                        pltpu.VMEM((lead if sequential else 1, 1, S5_LANES), F32)],
        compiler_params=_params(("parallel", "arbitrary")),
        name="s5",
    )(z3, tabs['bbre'], tabs['bbim'], tabs['cre'], tabs['cim'], tabs['d'], tabs['wglu'], tabs['bglu'],
      tabs['are'], tabs['aim'], x0re, x0im)


def _s5_tables(a_re, a_im, log_dt, b_re, b_im, c_re, c_im, d, w_glu, b_glu):
    lam_re = a_re.astype(F32)
    lam_im = a_im.astype(F32)
    dt = jnp.exp(log_dt.astype(F32))[:, None]
    mag = jnp.exp(lam_re * dt)
    ang = lam_im * dt
    ab_re = mag * jnp.cos(ang)
    ab_im = mag * jnp.sin(ang)
    den = lam_re * lam_re + lam_im * lam_im
    co_re = ((ab_re - 1.0) * lam_re + ab_im * lam_im) / den
    co_im = (ab_im * lam_re - (ab_re - 1.0) * lam_im) / den
    b_re = b_re.astype(F32)
    b_im = b_im.astype(F32)
    bb_re = co_re[..., None] * b_re - co_im[..., None] * b_im
    bb_im = co_re[..., None] * b_im + co_im[..., None] * b_re
    gh = S5_GROUPS // 2
    eye = jnp.eye(gh, dtype=F32)
    hw, hl = S5_WIDTH // 2, S5_LANES // 2
    halves = lambda a: a.reshape((2, gh) + a.shape[1:])
    blockdiag_in = lambda bb: jnp.einsum('xgph,gk->xghkp', halves(bb), eye).reshape(2, hw, hl).astype(BF16)
    blockdiag_out = lambda cc: jnp.einsum('xghp,gk->xgpkh', halves(cc.astype(F32)), eye).reshape(2, hl, hw).astype(BF16)
    return {
        'bbre': blockdiag_in(bb_re), 'bbim': blockdiag_in(bb_im),
        'cre': blockdiag_out(c_re), 'cim': blockdiag_out(c_im),
        'd': d.astype(F32).reshape(1, S5_WIDTH),
        'wglu': w_glu.astype(BF16), 'bglu': b_glu.astype(F32).reshape(1, S5_WIDTH),
        'are': ab_re.reshape(1, S5_LANES), 'aim': ab_im.reshape(1, S5_LANES),
    }


def _rmsnorm_val(x, g):
    return x * lax.rsqrt(jnp.mean(x * x, axis=-1, keepdims=True) + EPS) * g


def _ffn_kernel(r_ref, a1_ref, a2_ref, w1_ref, w2_ref, g_ref, wup_ref, cw_ref, cb_ref, wd_ref, st_ref, gf_ref,
                o_ref, so_ref, halo_s, *, tm, tf, long_seq, final_norm):
    i = pl.program_id(1)
    n = tm // SUBLANE
    d = r_ref.shape[-1]
    x = (r_ref[0] + jnp.dot(a1_ref[0], w1_ref[...], preferred_element_type=F32)
         + jnp.dot(a2_ref[0], w2_ref[...], preferred_element_type=F32))
    hn = _rmsnorm_val(x, g_ref[...])
    if long_seq:
        lead = n

        @pl.when(i == 0)
        def _():
            halo_s[...] = st_ref[0]
    else:
        lead = SUBLANE
    hnp = jnp.swapaxes(hn.reshape(tm // lead, lead, d), 0, 1).reshape(tm, d).astype(BF16)
    inner = tm // lead
    sub = lax.broadcasted_iota(jnp.int32, (SUBLANE, tf), 0)

    acc = jnp.zeros((tm, d), F32)
    for c in range(D_FF // tf):
        conv = []
        for half in range(2):
            c0 = half * D_FF + c * tf
            u = jnp.dot(hnp, wup_ref[:, c0:c0 + tf], preferred_element_type=F32).reshape(lead, inner, tf)
            if long_seq:
                um1 = jnp.where(sub == 0, halo_s[1:2, c0:c0 + tf], pltpu.roll(u[lead - 1], 1, axis=0))
                um2 = jnp.where(sub == 0, halo_s[0:1, c0:c0 + tf], pltpu.roll(u[lead - 2], 1, axis=0))
                halo_s[0:1, c0:c0 + tf] = u[lead - 2][SUBLANE - 1:SUBLANE]
                halo_s[1:2, c0:c0 + tf] = u[lead - 1][SUBLANE - 1:SUBLANE]
            else:
                um2 = st_ref[0, :, c0:c0 + tf]
                um1 = st_ref[1, :, c0:c0 + tf]
                so_ref[0, :, c0:c0 + tf] = u[lead - 2]
                so_ref[1, :, c0:c0 + tf] = u[lead - 1]
            ext = jnp.concatenate([um2[None], um1[None], u], axis=0)
            cv = cb_ref[:, c0:c0 + tf]
            for j in range(CONV_W):
                cv = cv + ext[j:j + lead] * cw_ref[j:j + 1, c0:c0 + tf]
            conv.append(cv)
        act = (jax.nn.silu(conv[1]) * conv[0]).reshape(tm, tf).astype(BF16)
        acc = acc + jnp.dot(act, wd_ref[c * tf:(c + 1) * tf, :], preferred_element_type=F32)

    out = x + jnp.swapaxes(acc.reshape(lead, inner, d), 0, 1).reshape(tm, d)
    if final_norm:
        out = _rmsnorm_val(out, gf_ref[...])
    o_ref[0] = out

    if long_seq:
        @pl.when(i == pl.num_programs(1) - 1)
        def _():
            so_ref[0] = halo_s[...]


def _ffn(res3, a1, a2, w1, w2, g, wup, cw, cb, wd, state, layer, gfinal, tm, tf, long_seq, final_norm):
    b, t, d = res3.shape
    nt = t // tm
    resident = dict(pipeline_mode=pl.Buffered(1))
    const2 = lambda bi, i: (0, 0)
    row_blk = lambda w: pl.BlockSpec((1, tm, w), lambda bi, i: (bi, i, 0))
    per_layer = lambda shape, **kw: pl.BlockSpec((None,) + shape, lambda bi, i: (layer, 0, 0), **kw)
    if long_seq:
        st_in = st_out = pl.BlockSpec((1, CONV_W - 1, 2 * D_FF), lambda bi, i: (bi, 0, 0))
        st_shape = state.shape
    else:
        nseq = tm // SUBLANE
        st_in = pl.BlockSpec((None, CONV_W - 1, nseq, 2 * D_FF), lambda bi, i: (layer, 0, i, 0))
        st_out = pl.BlockSpec((CONV_W - 1, nseq, 2 * D_FF), lambda bi, i: (0, i, 0))
        st_shape = state.shape[1:]
    return pl.pallas_call(
        functools.partial(_ffn_kernel, tm=tm, tf=tf, long_seq=long_seq, final_norm=final_norm),
        grid=(b, nt),
        in_specs=[row_blk(d), row_blk(a1.shape[-1]), row_blk(a2.shape[-1]),
                  pl.BlockSpec(w1.shape, const2, **resident),
                  pl.BlockSpec(w2.shape, const2, **resident),
                  per_layer((1, d)),
                  per_layer((d, 2 * D_FF), **resident),
                  per_layer((CONV_W, 2 * D_FF)),
                  per_layer((1, 2 * D_FF)),
                  per_layer((D_FF, d), **resident),
                  st_in,
                  pl.BlockSpec((1, d), const2)],
        out_specs=[row_blk(d), st_out],
        out_shape=[jax.ShapeDtypeStruct((b, t, d), F32), jax.ShapeDtypeStruct(st_shape, F32)],
        scratch_shapes=[pltpu.VMEM((CONV_W - 1, 2 * D_FF), F32)],
        compiler_params=_params(("parallel", "arbitrary")),
        name="ffn",
    )(res3, a1, a2, w1, w2, g.reshape(-1, 1, d), wup, cw, cb.reshape(-1, 1, 2 * D_FF), wd, state, gfinal)


def _rope_tables(pos):
    half = SW_HD // 2
    inv = ROPE_THETA ** (-jnp.arange(half, dtype=F32) / half)
    ang = pos.astype(F32)[:, None] * inv[None, :]
    cos = jnp.cos(ang)
    sin = jnp.sin(ang)
    reps = LANE // SW_HD
    cos_t = jnp.tile(jnp.concatenate([cos, cos], axis=1), (1, reps))
    sin_t = jnp.tile(jnp.concatenate([-sin, sin], axis=1), (1, reps))
    return cos_t, sin_t


def _prep_weights(W):
    P = {}
    we = W['w_in_even']
    nh = 4 * ML_HEADS * ML_DK
    head, g_m, tail = we[:, :nh], we[:, nh:nh + 2 * ML_HEADS], we[:, nh + 2 * ML_HEADS:]
    q_a, k_a, v_a = tail[:, :512], tail[:, 512:640], tail[:, 640:768]
    dup = lambda w: jnp.concatenate([w[:, :SW_HD], w[:, :SW_HD], w[:, SW_HD:], w[:, SW_HD:]], axis=1)
    pad = jnp.zeros((D_MODEL, LANE - 2 * ML_HEADS), we.dtype)
    P['w_in_even'] = jnp.concatenate([head, q_a, dup(k_a), dup(v_a), g_m, pad], axis=1).astype(BF16)
    assert P['w_in_even'].shape[1] == EVEN_PAD
    P['b_gates'] = jnp.pad(W['b_mlstm_gates'].astype(F32), (0, LANE - 2 * ML_HEADS)).reshape(1, LANE)
    wo = W['w_in_odd']
    nh = 2 * GLA_HEADS * GLA_DK + 2 * GLA_HEADS * GLA_DV
    head, glr, u = wo[:, :nh], wo[:, nh:nh + GLA_RANK], wo[:, nh + GLA_RANK:]
    pad = jnp.zeros((D_MODEL, ODD_PAD - 2048 - GLA_RANK), wo.dtype)
    P['w_in_odd'] = jnp.concatenate([head, u, glr, pad], axis=1).astype(BF16)
    wg = jnp.pad(W['w_gla_gate_up'].astype(F32), ((0, LANE - GLA_RANK), (0, 0)))
    wg_hi = wg.astype(BF16)
    P['w_gate_up'] = jnp.concatenate([wg_hi, (wg - wg_hi.astype(F32)).astype(BF16)], axis=1)
    P['b_gate'] = W['b_gla_gate'].astype(F32).reshape(1, -1)
    P['g_gla'] = W['g_gla_norm'].astype(F32).reshape(1, -1)
    hm = ML_HEADS * ML_DV
    P['w_out_even'] = (W['w_out_even'][:hm].astype(BF16), W['w_out_even'][hm:].astype(BF16))
    hg = GLA_HEADS * GLA_DV
    P['w_out_odd'] = (W['w_out_odd'][:hg].astype(BF16), W['w_out_odd'][hg:].astype(BF16))
    P['w_ffn_up'] = W['w_ffn_up'].astype(BF16)
    P['w_ffn_down'] = W['w_ffn_down'].astype(BF16)
    P['s5'] = _s5_tables(W['s5_a_re'], W['s5_a_im'], W['s5_log_dt'], W['s5_b_re'], W['s5_b_im'],
                         W['s5_c_re'], W['s5_c_im'], W['s5_d'], W['w_s5_glu'], W['b_s5_glu'])
    return P


def _trunk(x, pos, st, W, P, is_prompt):
    b, t, d = x.shape
    n = b * t
    L = CHUNK if t % CHUNK == 0 else t
    nsub = next((c for c in (8, 4, 2) if t % (c * L) == 0), 1)
    tm = next((c for c in (1024, 512) if n % c == 0), n)
    cos, sin_signed = _rope_tables(pos)
    gfinal = W['norm_final'].astype(F32).reshape(1, d)
    new = {}
    conv_out = []
    h = x.reshape(n, d)

    z = _norm_matmul(h, W['norm_mix'][0], P['w_in_even'], tm)
    z3 = z.reshape(b, t, EVEN_PAD)
    pack = dict(G=1, Bb=2 if b % 2 == 0 else 1, nsub=nsub) if (is_prompt or b % 16) else dict(G=8, Bb=2, nsub=1)
    h_m, new['ml_C'], new['ml_n'], m_new = _mlstm(z3, P['b_gates'], st['ml_C'], st['ml_n'], st['ml_m'], L, **pack)
    new['ml_m'] = m_new.reshape(b, ML_HEADS)
    if is_prompt:
        h_a, kb = _swa_prompt(z3, cos, sin_signed, W['sw_sinks'].astype(F32), Bb=2 if b % 2 == 0 else 1)
        vb = jnp.concatenate([z3[:, t - WINDOW:, EVEN_VD:EVEN_VD + SW_HD],
                              z3[:, t - WINDOW:, EVEN_VD + LANE:EVEN_VD + LANE + SW_HD]], axis=-1)
    else:
        h_a, kb, vb = _swa_sample(z3, st['kbuf'].reshape(b, WINDOW, -1), st['vbuf'].reshape(b, WINDOW, -1),
                                  cos, sin_signed, W['sw_sinks'].astype(F32), Bb=8)
    new['kbuf'] = kb.reshape(b, WINDOW, SW_KV_HEADS, SW_HD)
    new['vbuf'] = vb.reshape(b, WINDOW, SW_KV_HEADS, SW_HD)
    h, cb = _ffn_layer(h, h_m, h_a, P['w_out_even'], 0, st, W, P, b, t, gfinal, is_prompt, final_norm=False)
    conv_out.append(cb)

    z = _norm_matmul(h, W['norm_mix'][1], P['w_in_odd'], tm)
    z3 = z.reshape(b, t, ODD_PAD)
    o_g, new['gla'] = _gla(z3, P['w_gate_up'], P['b_gate'], P['g_gla'], st['gla'], L, **pack)
    x0re = st['s5_re'].astype(F32).reshape(b, 1, S5_LANES)
    x0im = st['s5_im'].astype(F32).reshape(b, 1, S5_LANES)
    if is_prompt:
        o_s, xre, xim = _s5(z3, P['s5'], x0re, x0im, rows=next((c for c in (512, 256) if t % c == 0), t),
                            sequential=True)
    else:
        o_s, xre, xim = _s5(z3.reshape(1, n, ODD_PAD), P['s5'], x0re.reshape(1, b, S5_LANES),
                            x0im.reshape(1, b, S5_LANES), rows=256 if n % 256 == 0 else n, sequential=False)
    new['s5_re'] = xre.reshape(b, S5_GROUPS, S5_STATE)
    new['s5_im'] = xim.reshape(b, S5_GROUPS, S5_STATE)
    h, cb = _ffn_layer(h, o_g, o_s, P['w_out_odd'], 1, st, W, P, b, t, gfinal, is_prompt, final_norm=True)
    conv_out.append(cb)
    new['conv'] = jnp.stack(conv_out)
    return h.reshape(b, t, d), new


def _ffn_layer(h, a1, a2, w_out, layer, st, W, P, b, t, gfinal, is_prompt, final_norm):
    n, d = h.shape
    rows = t if is_prompt else n
    tm = next((c for c in (512, 256) if rows % c == 0), rows)
    args = (*w_out, W['norm_ffn'], P['w_ffn_up'], W['ffn_conv_w'].astype(F32), W['ffn_conv_b'].astype(F32),
            P['w_ffn_down'])
    if is_prompt:
        shape3 = lambda a: a.reshape(b, t, -1)
        out, new_state = _ffn(shape3(h), shape3(a1), shape3(a2), *args, st['conv'][layer], layer, gfinal, tm,
                              FFN_TF, True, final_norm)
    else:
        assert t == SUBLANE
        shape3 = lambda a: a.reshape(1, n, -1)
        out, new_state = _ffn(shape3(h), shape3(a1), shape3(a2), *args, st['conv_rows'], layer, gfinal, tm, FFN_TF,
                              False, final_norm)
        new_state = jnp.swapaxes(new_state, 0, 1)
    return out.reshape(n, d), new_state


def kernel(x_prompt, x_sample, state_mlstm_C, state_mlstm_n, state_mlstm_m, cache_swa_k, cache_swa_v,
           state_gla, state_s5_re, state_s5_im, state_ffn_conv, norm_mix, norm_ffn, norm_final,
           w_in_even, b_mlstm_gates, sw_sinks, w_out_even, w_in_odd, w_gla_gate_up, b_gla_gate, g_gla_norm,
           s5_a_re, s5_a_im, s5_log_dt, s5_b_re, s5_b_im, s5_c_re, s5_c_im, s5_d, w_s5_glu, b_s5_glu,
           w_out_odd, w_ffn_up, ffn_conv_w, ffn_conv_b, w_ffn_down):
    W = {'norm_mix': norm_mix.astype(F32), 'norm_ffn': norm_ffn.astype(F32), 'norm_final': norm_final,
         'w_in_even': w_in_even, 'b_mlstm_gates': b_mlstm_gates, 'sw_sinks': sw_sinks, 'w_out_even': w_out_even,
         'w_in_odd': w_in_odd, 'w_gla_gate_up': w_gla_gate_up, 'b_gla_gate': b_gla_gate, 'g_gla_norm': g_gla_norm,
         's5_a_re': s5_a_re, 's5_a_im': s5_a_im, 's5_log_dt': s5_log_dt, 's5_b_re': s5_b_re, 's5_b_im': s5_b_im,
         's5_c_re': s5_c_re, 's5_c_im': s5_c_im, 's5_d': s5_d, 'w_s5_glu': w_s5_glu, 'b_s5_glu': b_s5_glu,
         'w_out_odd': w_out_odd, 'w_ffn_up': w_ffn_up, 'ffn_conv_w': ffn_conv_w, 'ffn_conv_b': ffn_conv_b,
         'w_ffn_down': w_ffn_down}
    P = _prep_weights(W)
    bp, tp = x_prompt.shape[:2]
    st_prompt = {'ml_C': jnp.zeros((bp, ML_HEADS, ML_DK, ML_DV), F32),
                 'ml_n': jnp.zeros((bp, ML_HEADS, ML_DK), F32),
                 'ml_m': jnp.zeros((bp, ML_HEADS), F32),
                 'gla': jnp.zeros((bp, GLA_HEADS, GLA_DK, GLA_DV), F32),
                 's5_re': jnp.zeros((bp, S5_GROUPS, S5_STATE), F32),
                 's5_im': jnp.zeros((bp, S5_GROUPS, S5_STATE), F32),
                 'conv': jnp.zeros((2, bp, CONV_W - 1, 2 * D_FF), F32)}
    st_sample = {'ml_C': state_mlstm_C.astype(F32), 'ml_n': state_mlstm_n.astype(F32),
                 'ml_m': state_mlstm_m.astype(F32), 'kbuf': cache_swa_k.astype(F32),
                 'vbuf': cache_swa_v.astype(F32), 'gla': state_gla.astype(F32),
                 's5_re': state_s5_re, 's5_im': state_s5_im,
                 'conv_rows': jnp.swapaxes(state_ffn_conv.astype(F32), 1, 2)}
    past_len = 16384
    y_p, np_ = _trunk(x_prompt.astype(F32), jnp.arange(tp), st_prompt, W, P, True)
    y_s, ns_ = _trunk(x_sample.astype(F32), past_len + jnp.arange(x_sample.shape[1]), st_sample, W, P, False)
    return (y_p, y_s,
            np_['ml_C'], ns_['ml_C'], np_['ml_n'], ns_['ml_n'], np_['ml_m'], ns_['ml_m'],
            np_['kbuf'], ns_['kbuf'], np_['vbuf'], ns_['vbuf'], np_['gla'], ns_['gla'],
            np_['s5_re'], ns_['s5_re'], np_['s5_im'], ns_['s5_im'], np_['conv'], ns_['conv'])
```

```python
import functools
import math

import numpy as np
import jax
import jax.numpy as jnp
from jax import lax
from jax.experimental import pallas as pl
from jax.experimental.pallas import tpu as pltpu

F32 = jnp.float32
BF16 = jnp.bfloat16
HI = lax.Precision.HIGHEST
NT = (((1,), (1,)), ((), ()))
TN = (((0,), (0,)), ((), ()))

D_MODEL = 1024
ML_HEADS, ML_DK, ML_DV = 4, 128, 128
SW_HEADS, SW_KV_HEADS, SW_HD, WINDOW = 8, 2, 64, 128
SW_GQ = SW_HEADS // SW_KV_HEADS
ROPE_THETA = 10000.0
GLA_HEADS, GLA_DK, GLA_DV, GLA_RANK, GLA_TAU = 4, 64, 128, 16, 16.0
S5_WIDTH, S5_GROUP, S5_STATE = 512, 16, 64
S5_GROUPS = S5_WIDTH // S5_GROUP
S5_LANES = S5_GROUPS * S5_STATE
D_FF = 2816
FFN_TF = 2816
CONV_W = 3
CHUNK = 64
EPS = 1e-6

LANE = 128
SUBLANE = 8
VMEM_LIMIT = 56 * 1024 * 1024

EVEN_QA, EVEN_KD, EVEN_VD, EVEN_PAD = 2048, 2560, 2816, 3200
ODD_PAD = 2176


def _params(sem):
    return pltpu.CompilerParams(dimension_semantics=sem, vmem_limit_bytes=VMEM_LIMIT)


def _norm_matmul_kernel(x_ref, g_ref, w_ref, o_ref):
    x = x_ref[...]
    ms = jnp.mean(x * x, axis=-1, keepdims=True)
    hn = (x * lax.rsqrt(ms + EPS) * g_ref[...]).astype(BF16)
    o_ref[...] = jnp.dot(hn, w_ref[...], preferred_element_type=F32)


def _norm_matmul(x, g, w, tm):
    n, d = x.shape
    c = w.shape[1]
    return pl.pallas_call(
        _norm_matmul_kernel,
        grid=(n // tm,),
        in_specs=[pl.BlockSpec((tm, d), lambda i: (i, 0)),
                  pl.BlockSpec((1, d), lambda i: (0, 0)),
                  pl.BlockSpec((d, c), lambda i: (0, 0))],
        out_specs=pl.BlockSpec((tm, c), lambda i: (i, 0)),
        out_shape=jax.ShapeDtypeStruct((n, c), F32),
        compiler_params=_params(("parallel",)),
        name="norm_matmul",
    )(x, g.reshape(1, d), w)


def _pieces(x, n):
    out, r = [], x
    for _ in range(n):
        p = r.astype(BF16).astype(F32)
        out.append(p)
        r = r - p
    return out


def _dot_sel(m_bf16, x, dims=None, pieces=3):
    ps = _pieces(x, pieces)
    axis = 1 if dims is None else 0
    w = x.shape[axis]
    if w % LANE:
        f = (lambda p: jnp.dot(m_bf16, p, preferred_element_type=F32)) if dims is None else (
            lambda p: lax.dot_general(m_bf16, p, dims, preferred_element_type=F32))
        return sum(f(p.astype(BF16)) for p in ps)
    stacked = jnp.concatenate(ps, axis=axis).astype(BF16)
    if dims is None:
        r = jnp.dot(m_bf16, stacked, preferred_element_type=F32)
    else:
        r = lax.dot_general(m_bf16, stacked, dims, preferred_element_type=F32)
    return sum(r[:, i * w:(i + 1) * w] for i in range(pieces))


def _sel_right(x, e_bf16, pieces=3):
    rows = x.shape[0]
    r = jnp.dot(jnp.concatenate(_pieces(x, pieces), axis=0).astype(BF16), e_bf16, preferred_element_type=F32)
    return sum(r[i * rows:(i + 1) * rows] for i in range(pieces))


def _mlstm_kernel(q_ref, k_ref, v_ref, og_ref, gz_ref, bias_ref, tri_ref, ones_ref, expc_ref, expv_ref, eyet_ref,
                  seg_ref, segv_ref, hmk_ref, c0_ref, n0_ref, m0_ref,
                  h_ref, co_ref, no_ref, mo_ref,
                  c_s, n_s, m_s, *, L, Bb, nsub, G):
    c = pl.program_id(1)

    @pl.when(c == 0)
    def _():
        c_s[...] = c0_ref[...]
        n_s[...] = n0_ref[...]
        m_s[...] = m0_ref[...]

    for sub in range(nsub):
        for bi in range(Bb):
            _mlstm_one(bi, sub * L, q_ref, k_ref, v_ref, og_ref, gz_ref, bias_ref, tri_ref, ones_ref, expc_ref,
                       expv_ref, eyet_ref, seg_ref, segv_ref, hmk_ref, h_ref, c_s, n_s, m_s, L, G)

    @pl.when(c == pl.num_programs(1) - 1)
    def _():
        co_ref[...] = c_s[...]
        no_ref[...] = n_s[...]
        mo_ref[...] = m_s[...]


def _mlstm_one(bi, r0, q_ref, k_ref, v_ref, og_ref, gz_ref, bias_ref, tri_ref, ones_ref, expc_ref, expv_ref,
               eyet_ref, seg_ref, segv_ref, hmk_ref, h_ref, c_s, n_s, m_s, L, G):
    H, DK = ML_HEADS, ML_DK
    HL = H * L
    S = L // G
    seg = lambda a, g: a[g * S:(g + 1) * S]
    per_row = lambda rows: rows[0] if G == 1 else jnp.concatenate(
        [jnp.broadcast_to(r, (S, r.shape[1])) for r in rows], axis=0)
    lane = lax.broadcasted_iota(jnp.int32, (L, LANE), 1)
    rowi = lax.broadcasted_iota(jnp.int32, (L, LANE), 0) % S
    keep = lane < H
    heads_only = lambda x: jnp.where(keep[:x.shape[0]], x, 0.0)

    rw = pl.ds(r0, L)
    gates = gz_ref[bi, rw] + bias_ref[...]
    lf = jax.nn.log_sigmoid(gates)
    b = pltpu.roll(_dot_sel(tri_ref[...], lf), LANE - H, axis=1)
    vv = gates - b
    cm = vv
    sh = 1
    while sh < S:
        cm = jnp.maximum(cm, jnp.where(rowi >= sh, pltpu.roll(cm, sh, axis=0), -jnp.inf))
        sh *= 2
    m_prev_g = [m_s[bi * G + g] for g in range(G)]
    m_prev = per_row(m_prev_g)
    mt = b + jnp.maximum(m_prev, cm)
    a_inter = jnp.exp(b + m_prev - mt)

    expv = expv_ref[...]
    m_new_g = [seg(mt, g)[S - 1:S, :] for g in range(G)]
    b_end_g = [seg(b, g)[S - 1:S, :] for g in range(G)]
    w_end = heads_only(jnp.exp(per_row(b_end_g) - b + gates - per_row(m_new_g)))
    decay_g = [heads_only(jnp.exp(b_end_g[g] + m_prev_g[g] - m_new_g[g])) for g in range(G)]
    decay_rows = jnp.concatenate(decay_g + [jnp.zeros((-G % SUBLANE, LANE), F32)] * (G % SUBLANE != 0), axis=0)
    spread_k = _sel_right(jnp.concatenate([heads_only(b - mt), heads_only(vv)], axis=0), expc_ref[...])
    uc = spread_k[:L]
    vr = _dot_sel(ones_ref[...], spread_k[L:] * eyet_ref[...])
    spread_v = _sel_right(jnp.concatenate([heads_only(a_inter), w_end, decay_rows], axis=0), expv, pieces=2)
    ac, wc, dcs = spread_v[:L], spread_v[L:2 * L], spread_v[2 * L:2 * L + G]
    ti = lax.broadcasted_iota(jnp.int32, (L, HL), 0)
    si = lax.broadcasted_iota(jnp.int32, (L, HL), 1) % L
    causal = (si <= ti) & (si // S == ti // S)
    w = jnp.exp(jnp.where(causal, uc + vr, -jnp.inf))

    q = q_ref[bi, rw]
    ks = k_ref[bi, rw] * (DK ** -0.5)
    v = v_ref[bi, rw]
    qb = q.astype(BF16)
    if L % (2 * SUBLANE) == 0:
        kst = jnp.concatenate([ks.astype(BF16)] * H, axis=0) * hmk_ref[...]
        vsrc = v.astype(BF16)
    else:
        kst = (jnp.concatenate([ks] * H, axis=0) * hmk_ref[...].astype(F32)).astype(BF16)
        vsrc = v
    zero = jnp.zeros((L, DK), vsrc.dtype)
    vbd = jnp.concatenate(
        [jnp.concatenate([zero] * h + [vsrc[:, h * DK:(h + 1) * DK]] + [zero] * (H - 1 - h), axis=1)
         for h in range(H)], axis=0).astype(BF16)
    s = lax.dot_general(qb, kst, NT, preferred_element_type=F32) * w
    qc = jnp.concatenate(
        [jnp.concatenate([jnp.dot(seg(q, g)[:, h * DK:(h + 1) * DK].astype(BF16), c_s[bi * G + g, h].astype(BF16),
                                  preferred_element_type=F32) for h in range(H)], axis=1)
         for g in range(G)], axis=0)
    num = jnp.dot(s.astype(BF16), vbd, preferred_element_type=F32) + ac * qc
    n_rows = per_row([n_s[bi * G + g] for g in range(G)])
    den = (_sel_right(s, seg_ref[...], pieces=2)
           + a_inter * _sel_right(q * n_rows, segv_ref[...], pieces=2))
    rden = heads_only(1.0 / jnp.maximum(jnp.abs(den), jnp.exp(-mt)))
    hh = num * _sel_right(rden, expv, pieces=2) * jax.nn.sigmoid(og_ref[bi, rw])
    h_ref[bi, rw] = hh.astype(h_ref.dtype)

    kw = ks * wc
    for g in range(G):
        e = bi * G + g
        kwb = seg(kw, g).astype(BF16)
        vb = seg(v, g).astype(BF16)
        dc = dcs[g:g + 1]
        for h in range(H):
            hs = slice(h * DK, (h + 1) * DK)
            c_s[e, h] = dc[:, hs] * c_s[e, h] + lax.dot_general(kwb[:, hs], vb[:, hs], TN, preferred_element_type=F32)
        n_s[e] = dc * n_s[e] + jnp.sum(seg(kw, g), axis=0, keepdims=True)
        m_s[e] = heads_only(m_new_g[g])


def _mlstm_consts(S, G):
    L = S * G
    H, DK = ML_HEADS, ML_DK
    expc = np.zeros((LANE, H * L), np.float32)
    expv = np.zeros((LANE, H * DK), np.float32)
    seg = np.zeros((H * L, LANE), np.float32)
    segv = np.zeros((H * DK, LANE), np.float32)
    hmk = np.zeros((H * L, H * DK), np.float32)
    for h in range(H):
        expc[h, h * L:(h + 1) * L] = 1.0
        expv[h, h * DK:(h + 1) * DK] = 1.0
        seg[h * L:(h + 1) * L, h] = 1.0
        segv[h * DK:(h + 1) * DK, h] = 1.0
        hmk[h * L:(h + 1) * L, h * DK:(h + 1) * DK] = 1.0
    eyet = np.tile(np.eye(L, dtype=np.float32), (1, H))
    bf = lambda a: jnp.asarray(a, BF16)
    tri = np.kron(np.eye(G), np.tril(np.ones((S, S)))).astype(np.float32)
    return (bf(tri), bf(np.ones((L, L), np.float32)), bf(expc), bf(expv),
            jnp.asarray(eyet), bf(seg), bf(segv), bf(hmk))


def _mlstm(z3, bias, c0, n0, m0, S, G, Bb, nsub):
    nseq, t_seq, width = z3.shape
    if G > 1:
        assert t_seq == S and nsub == 1
        z3 = z3.reshape(nseq // G, G * S, width)
    b, t, _ = z3.shape
    L = G * S
    nc = t // (L * nsub)
    hw = ML_HEADS * ML_DK
    consts = _mlstm_consts(S, G)
    gate_blk = (EVEN_PAD - LANE) // LANE
    col = lambda j: (lambda bi, ci: (bi, ci, j))
    const2 = lambda bi, ci: (0, 0)
    state3 = lambda bi, ci: (bi, 0, 0)
    m0p = jnp.pad(m0, ((0, 0), (0, LANE - ML_HEADS))).reshape(nseq, 1, LANE)
    Bs = Bb * G
    h, c_new, n_new, m_new = pl.pallas_call(
        functools.partial(_mlstm_kernel, L=L, Bb=Bb, nsub=nsub, G=G),
        grid=(b // Bb, nc),
        in_specs=[pl.BlockSpec((Bb, nsub * L, hw), col(0)),
                  pl.BlockSpec((Bb, nsub * L, hw), col(1)),
                  pl.BlockSpec((Bb, nsub * L, hw), col(2)),
                  pl.BlockSpec((Bb, nsub * L, hw), col(3)),
                  pl.BlockSpec((Bb, nsub * L, LANE), col(gate_blk)),
                  pl.BlockSpec((1, LANE), const2)]
                 + [pl.BlockSpec(a.shape, const2) for a in consts]
                 + [pl.BlockSpec((Bs, ML_HEADS, ML_DK, ML_DV), lambda bi, ci: (bi, 0, 0, 0)),
                    pl.BlockSpec((Bs, 1, hw), state3),
                    pl.BlockSpec((Bs, 1, LANE), state3)],
        out_specs=[pl.BlockSpec((Bb, nsub * L, hw), col(0)),
                   pl.BlockSpec((Bs, ML_HEADS, ML_DK, ML_DV), lambda bi, ci: (bi, 0, 0, 0)),
                   pl.BlockSpec((Bs, 1, hw), state3),
                   pl.BlockSpec((Bs, 1, LANE), state3)],
        out_shape=[jax.ShapeDtypeStruct((b, t, hw), BF16),
                   jax.ShapeDtypeStruct((nseq, ML_HEADS, ML_DK, ML_DV), F32),
                   jax.ShapeDtypeStruct((nseq, 1, hw), F32),
                   jax.ShapeDtypeStruct((nseq, 1, LANE), F32)],
        scratch_shapes=[pltpu.VMEM((Bs, ML_HEADS, ML_DK, ML_DV), F32),
                        pltpu.VMEM((Bs, 1, hw), F32),
                        pltpu.VMEM((Bs, 1, LANE), F32)],
        compiler_params=_params(("parallel", "arbitrary")),
        name="mlstm",
    )(z3, z3, z3, z3, z3, bias, *consts, c0, n0.reshape(nseq, 1, hw), m0p)
    return h.reshape(nseq, t_seq, hw), c_new, n_new.reshape(nseq, ML_HEADS, ML_DK), m_new[:, 0, :ML_HEADS]


def _rope(x, cos, sin_signed, width):
    half = SW_HD // 2
    lane = lax.broadcasted_iota(jnp.int32, x.shape, 1)
    first = (lane % SW_HD) < half
    partner = jnp.where(first, pltpu.roll(x, width - half, axis=1), pltpu.roll(x, half, axis=1))
    return x * cos + partner * sin_signed


def _undup(xd):
    lane = lax.broadcasted_iota(jnp.int32, xd.shape[:-1] + (LANE,), xd.ndim - 1)
    return jnp.where(lane < SW_HD, xd[..., :LANE], xd[..., LANE:])


def _swa_prompt_kernel(sink_ref, q_ref, kc_ref, kp_ref, vc_ref, vp_ref, cosc_ref, sinc_ref, cosp_ref, sinp_ref,
                       h_ref, ko_ref):
    n = pl.program_id(1)
    W2 = 2 * WINDOW
    cosc, sinc = cosc_ref[...], sinc_ref[...]
    cosp, sinp = cosp_ref[...], sinp_ref[...]
    i = lax.broadcasted_iota(jnp.int32, (WINDOW, W2), 0)
    j = lax.broadcasted_iota(jnp.int32, (WINDOW, W2), 1)
    diff = WINDOW + i - j
    valid = (diff >= 0) & (diff < WINDOW) & ((n > 0) | (j >= WINDOW))
    row = lax.broadcasted_iota(jnp.int32, (2 * W2, LANE), 0)
    lane = lax.broadcasted_iota(jnp.int32, (2 * W2, LANE), 1)
    bd = (row < W2) == (lane < SW_HD)
    ones_bd = bd.astype(BF16)
    npair = SW_GQ // 2
    rows = npair * WINDOW
    valid = jnp.concatenate([valid] * npair, axis=0)
    low = lax.broadcasted_iota(jnp.int32, (rows, LANE), 1) < SW_HD
    prow = lax.broadcasted_iota(jnp.int32, (rows, 1), 0) // WINDOW
    cos4, sin4 = jnp.concatenate([cosc] * 4, axis=1), jnp.concatenate([sinc] * 4, axis=1)
    for bi in range(q_ref.shape[0]):
        q = _rope(q_ref[bi], cos4, sin4, 4 * LANE)
        k_cur = _rope(kc_ref[bi], cos4[:, :2 * LANE], sin4[:, :2 * LANE], 2 * LANE)
        k_prev = _rope(kp_ref[bi], jnp.concatenate([cosp] * 2, axis=1), jnp.concatenate([sinp] * 2, axis=1), 2 * LANE)
        kd = jnp.concatenate([k_prev, k_cur], axis=0)
        vd = jnp.concatenate([vp_ref[bi], vc_ref[bi]], axis=0)
        for kv in range(SW_KV_HEADS):
            kblk = kd[:, kv * LANE:(kv + 1) * LANE]
            vblk = vd[:, kv * LANE:(kv + 1) * LANE]
            kbd = jnp.where(bd, jnp.concatenate([kblk, kblk], axis=0), 0.0).astype(BF16)
            vbd = jnp.where(bd, jnp.concatenate([vblk, vblk], axis=0), 0.0).astype(BF16)
            vo = jnp.concatenate([vbd, ones_bd], axis=1)
            h0 = kv * SW_GQ
            qs = jnp.concatenate([q[:, (h0 + 2 * p) * SW_HD:(h0 + 2 * p + 2) * SW_HD] for p in range(npair)],
                                 axis=0).astype(BF16)
            s = lax.dot_general(qs, kbd, NT, preferred_element_type=F32) * (SW_HD ** -0.5)
            es, sink_terms = [], []
            for hh in range(2):
                sh = jnp.where(valid, s[:, hh * W2:(hh + 1) * W2], -jnp.inf)
                sink = sink_ref[h0 + hh]
                for p in range(1, npair):
                    sink = jnp.where(prow == p, sink_ref[h0 + 2 * p + hh], sink)
                m = jnp.maximum(jnp.max(sh, axis=1, keepdims=True), sink)
                es.append(jnp.exp(sh - m).astype(BF16))
                sink_terms.append(jnp.exp(sink - m))
            r = jnp.dot(jnp.concatenate(es, axis=1), vo, preferred_element_type=F32)
            o = r[:, :LANE] / (r[:, LANE:] + jnp.where(low, sink_terms[0], sink_terms[1]))
            for p in range(npair):
                h_ref[bi, :, (h0 + 2 * p) * SW_HD:(h0 + 2 * p + 2) * SW_HD] = (
                    o[p * WINDOW:(p + 1) * WINDOW].astype(h_ref.dtype))

        @pl.when(n == pl.num_programs(1) - 1)
        def _(bi=bi, k_cur=k_cur):
            ko_ref[bi] = _undup(k_cur)


def _swa_prompt(z3, cos, sin_signed, sinks, Bb):
    b, t, _ = z3.shape
    nb = t // WINDOW
    qw = SW_HEADS * SW_HD
    kw = SW_KV_HEADS * SW_HD
    dw = 2 * kw
    qblk, kblk, vblk = EVEN_QA // qw, EVEN_KD // dw, EVEN_VD // dw
    cur = lambda j: (lambda bi, ni: (bi, ni, j))
    prev = lambda j: (lambda bi, ni: (bi, jnp.maximum(ni - 1, 0), j))
    return pl.pallas_call(
        _swa_prompt_kernel,
        grid=(b // Bb, nb),
        in_specs=[pl.BlockSpec(memory_space=pltpu.SMEM),
                  pl.BlockSpec((Bb, WINDOW, qw), cur(qblk)),
                  pl.BlockSpec((Bb, WINDOW, dw), cur(kblk)),
                  pl.BlockSpec((Bb, WINDOW, dw), prev(kblk)),
                  pl.BlockSpec((Bb, WINDOW, dw), cur(vblk)),
                  pl.BlockSpec((Bb, WINDOW, dw), prev(vblk)),
                  pl.BlockSpec((WINDOW, LANE), lambda bi, ni: (ni, 0)),
                  pl.BlockSpec((WINDOW, LANE), lambda bi, ni: (ni, 0)),
                  pl.BlockSpec((WINDOW, LANE), lambda bi, ni: (jnp.maximum(ni - 1, 0), 0)),
                  pl.BlockSpec((WINDOW, LANE), lambda bi, ni: (jnp.maximum(ni - 1, 0), 0))],
        out_specs=[pl.BlockSpec((Bb, WINDOW, qw), cur(0)),
                   pl.BlockSpec((Bb, WINDOW, kw), lambda bi, ni: (bi, 0, 0))],
        out_shape=[jax.ShapeDtypeStruct((b, t, qw), BF16),
                   jax.ShapeDtypeStruct((b, WINDOW, kw), F32)],
        compiler_params=_params(("parallel", "arbitrary")),
        name="swa_prompt",
    )(sinks, z3, z3, z3, z3, z3, cos, sin_signed, cos, sin_signed)


def _swa_sample_kernel(q_ref, k_ref, v_ref, kbuf_ref, vbuf_ref, cos_ref, sin_ref, sink_ref,
                       h_ref, ko_ref, vo_ref, *, T, Bb):
    qw = SW_HEADS * SW_HD
    kw = SW_KV_HEADS * SW_HD
    cos = cos_ref[...]
    sin = sin_ref[...]
    q = _rope(q_ref[...].reshape(Bb * T, qw), jnp.concatenate([cos] * (qw // LANE), axis=1),
              jnp.concatenate([sin] * (qw // LANE), axis=1), qw).reshape(Bb, T, qw)
    k_new = _rope(_undup(k_ref[...]).reshape(Bb * T, kw), cos, sin, kw).reshape(Bb, T, kw)
    kk = jnp.concatenate([kbuf_ref[...], k_new], axis=1)
    vv = jnp.concatenate([vbuf_ref[...], _undup(v_ref[...])], axis=1)
    ko_ref[...] = kk[:, T:, :]
    vo_ref[...] = vv[:, T:, :]
    rows = SW_GQ * T
    i = lax.broadcasted_iota(jnp.int32, (rows, WINDOW + T), 0) % T
    j = lax.broadcasted_iota(jnp.int32, (rows, WINDOW + T), 1)
    diff = WINDOW + i - j
    valid = (diff >= 0) & (diff < WINDOW)
    for kv in range(SW_KV_HEADS):
        qs = jnp.concatenate([q[:, :, (kv * SW_GQ + g) * SW_HD:(kv * SW_GQ + g + 1) * SW_HD]
                              for g in range(SW_GQ)], axis=1).astype(BF16)
        kh = kk[:, :, kv * SW_HD:(kv + 1) * SW_HD].astype(BF16)
        vh = vv[:, :, kv * SW_HD:(kv + 1) * SW_HD].astype(BF16)
        s = jnp.einsum('bqd,bkd->bqk', qs, kh, preferred_element_type=F32) * (SW_HD ** -0.5)
        s = jnp.where(valid, s, -jnp.inf)
        sink = sink_ref[kv * rows:(kv + 1) * rows, :]
        m = jnp.maximum(jnp.max(s, axis=-1, keepdims=True), sink)
        e = jnp.exp(s - m)
        p = e / (jnp.sum(e, axis=-1, keepdims=True) + jnp.exp(sink - m))
        o = jnp.einsum('bqk,bkd->bqd', p.astype(BF16), vh, preferred_element_type=F32)
        for g in range(SW_GQ):
            hh = kv * SW_GQ + g
            h_ref[:, :, hh * SW_HD:(hh + 1) * SW_HD] = o[:, g * T:(g + 1) * T, :].astype(h_ref.dtype)


def _swa_sample(z3, kbuf, vbuf, cos, sin_signed, sinks, Bb):
    b, t, _ = z3.shape
    qw = SW_HEADS * SW_HD
    kw = SW_KV_HEADS * SW_HD
    dw = 2 * kw
    qblk, kblk, vblk = EVEN_QA // qw, EVEN_KD // dw, EVEN_VD // dw
    col = lambda j: (lambda bi: (bi, 0, j))
    const2 = lambda bi: (0, 0)
    sink_col = jnp.repeat(sinks, t).reshape(SW_HEADS * t, 1)
    return pl.pallas_call(
        functools.partial(_swa_sample_kernel, T=t, Bb=Bb),
        grid=(b // Bb,),
        in_specs=[pl.BlockSpec((Bb, t, qw), col(qblk)),
                  pl.BlockSpec((Bb, t, dw), col(kblk)),
                  pl.BlockSpec((Bb, t, dw), col(vblk)),
                  pl.BlockSpec((Bb, WINDOW, kw), col(0)),
                  pl.BlockSpec((Bb, WINDOW, kw), col(0)),
                  pl.BlockSpec((Bb * t, LANE), const2),
                  pl.BlockSpec((Bb * t, LANE), const2),
                  pl.BlockSpec((SW_HEADS * t, 1), const2)],
        out_specs=[pl.BlockSpec((Bb, t, qw), col(0)),
                   pl.BlockSpec((Bb, WINDOW, kw), col(0)),
                   pl.BlockSpec((Bb, WINDOW, kw), col(0))],
        out_shape=[jax.ShapeDtypeStruct((b, t, qw), BF16),
                   jax.ShapeDtypeStruct((b, WINDOW, kw), F32),
                   jax.ShapeDtypeStruct((b, WINDOW, kw), F32)],
        compiler_params=_params(("parallel",)),
        name="swa_sample",
    )(z3, z3, z3, kbuf, vbuf, jnp.tile(cos, (Bb, 1)), jnp.tile(sin_signed, (Bb, 1)), sink_col)


def _gla_tables(S, G):
    nlev = int(math.log2(S))
    assert 2 ** nlev == S
    L = G * S
    mstack = np.zeros((nlev * L, L), np.float32)
    masks = np.zeros((nlev + 1, L, L), np.float32)
    masks[0] = np.eye(L)
    for l in range(1, nlev + 1):
        n = 2 ** l
        for r in range(L):
            blk, pos = divmod(r, n)
            m = blk * n + n // 2 - 1
            if pos >= n // 2:
                mstack[(l - 1) * L + r, m + 1:r + 1] = 1.0
                masks[l, r, blk * n:blk * n + n // 2] = 1.0
            else:
                mstack[(l - 1) * L + r, r + 1:m + 1] = 1.0
    masks = np.tile(masks, (1, 1, GLA_HEADS))
    tri = np.kron(np.eye(G), np.tril(np.ones((S, S)))).astype(np.float32)
    return nlev, jnp.asarray(tri, BF16), jnp.asarray(mstack, BF16), jnp.asarray(masks)


def _gla_consts(L):
    kw = GLA_HEADS * GLA_DK
    vw = GLA_HEADS * GLA_DV
    hk = np.zeros((GLA_HEADS * L, kw), np.float32)
    for h in range(GLA_HEADS):
        hk[h * L:(h + 1) * L, h * GLA_DK:(h + 1) * GLA_DK] = 1.0
    hs = np.zeros((vw, kw), np.float32)
    for h in range(GLA_HEADS):
        hs[h * GLA_DV:(h + 1) * GLA_DV, h * GLA_DK:(h + 1) * GLA_DK] = 1.0
    return jnp.asarray(hk, BF16), jnp.asarray(hs)


def _gla_kernel(q_ref, k_ref, v_ref, r_ref, glr_ref, wg_ref, bg_ref, gn_ref, tri_ref, mstack_ref,
                masks_ref, hk_ref, hs_ref, s0_ref,
                o_ref, so_ref, st_s, *, L, nlev, Bb, nsub, G):
    c = pl.program_id(1)
    H, DK, DV = GLA_HEADS, GLA_DK, GLA_DV

    @pl.when(c == 0)
    def _():
        for bi in range(Bb * G):
            for h in range(H):
                pieces = ([jnp.zeros((h * DK, DV), F32)] if h else []) + [s0_ref[bi, h]]
                pieces += [jnp.zeros(((H - 1 - h) * DK, DV), F32)] if h < H - 1 else []
                padded = jnp.concatenate(pieces, axis=0)
                st_s[bi, h * DV:(h + 1) * DV, :] = padded.T

    for sub in range(nsub):
        for bi in range(Bb):
            _gla_one(bi, sub * L, q_ref, k_ref, v_ref, r_ref, glr_ref, wg_ref, bg_ref, gn_ref, tri_ref, mstack_ref,
                     masks_ref, hk_ref, hs_ref, o_ref, st_s, L, nlev, G)

    @pl.when(c == pl.num_programs(1) - 1)
    def _():
        for bi in range(Bb * G):
            for h in range(H):
                blk = st_s[bi, h * DV:(h + 1) * DV, :]
                so_ref[bi, h] = blk.T[h * DK:(h + 1) * DK]


def _gla_one(bi, r0, q_ref, k_ref, v_ref, r_ref, glr_ref, wg_ref, bg_ref, gn_ref, tri_ref, mstack_ref,
             masks_ref, hk_ref, hs_ref, o_ref, st_s, L, nlev, G):
    H, DK, DV = GLA_HEADS, GLA_DK, GLA_DV
    packed_rows = L % (2 * SUBLANE) == 0
    rw = pl.ds(r0, L)
    q = q_ref[bi, rw] * (DK ** -0.5)
    k = k_ref[bi, rw]
    kwid = H * DK
    g_hi, g_lo = _pieces(glr_ref[bi, rw], 2)
    pre2 = jnp.dot(g_hi.astype(BF16), wg_ref[...], preferred_element_type=F32)
    pre = (pre2[:, :kwid] + pre2[:, kwid:] + jnp.dot(g_lo.astype(BF16), wg_ref[:, :kwid], preferred_element_type=F32)
           + bg_ref[...])
    lg = jax.nn.log_sigmoid(pre) * (1.0 / GLA_TAU)
    Gc = _dot_sel(tri_ref[...], lg)
    E = jnp.exp(_dot_sel(mstack_ref[...], lg, pieces=2))
    hk = hk_ref[...]
    tok = lax.broadcasted_iota(jnp.int32, (L, kwid), 0)

    att = None
    for l in range(nlev + 1):
        if l == 0:
            ql, kl = q, k
        else:
            e_l = E[(l - 1) * L:l * L]
            upper = ((tok >> (l - 1)) & 1) == 1
            ql = jnp.where(upper, q * e_l, 0.0)
            kl = jnp.where(upper, 0.0, k * e_l)
        if packed_rows:
            kst = jnp.concatenate([kl.astype(BF16)] * H, axis=0) * hk
        else:
            kst = (jnp.concatenate([kl] * H, axis=0) * hk.astype(F32)).astype(BF16)
        part = lax.dot_general(ql.astype(BF16), kst, NT, preferred_element_type=F32)
        part = jnp.where(masks_ref[l] > 0.5, part, 0.0)
        att = part if att is None else att + part

    v = v_ref[bi, rw]
    vsrc = v.astype(BF16) if packed_rows else v
    zero = jnp.zeros((L, DV), vsrc.dtype)
    vbd = jnp.concatenate(
        [jnp.concatenate([zero] * h + [vsrc[:, h * DV:(h + 1) * DV]] + [zero] * (H - 1 - h), axis=1)
         for h in range(H)], axis=0).astype(BF16)
    S = L // G
    seg = lambda a, g: a[g * S:(g + 1) * S]
    qg = q * jnp.exp(Gc)
    o = jnp.dot(att.astype(BF16), vbd, preferred_element_type=F32) + jnp.concatenate(
        [lax.dot_general(seg(qg, g).astype(BF16), st_s[bi * G + g].astype(BF16), NT, preferred_element_type=F32)
         for g in range(G)], axis=0)
    for h in range(H):
        vs = slice(h * DV, (h + 1) * DV)
        oh = o[:, vs]
        oh = oh * lax.rsqrt(jnp.mean(oh * oh, axis=-1, keepdims=True) + EPS)
        oh = oh * gn_ref[:, vs] * jax.nn.silu(r_ref[bi, rw, vs])
        o_ref[bi, rw, vs] = oh.astype(o_ref.dtype)
    for g in range(G):
        g_seg = seg(Gc, g)
        g_end = g_seg[S - 1:S, :]
        k_end = (seg(k, g) * jnp.exp(g_end - g_seg)).astype(BF16)
        upd = lax.dot_general(seg(v, g).astype(BF16), k_end, TN, preferred_element_type=F32)
        st_s[bi * G + g] = st_s[bi * G + g] * jnp.exp(g_end) + jnp.where(hs_ref[...] > 0.5, upd, 0.0)


def _gla(z3, wg, bg, gn, s0, S, G, Bb, nsub):
    nseq, t_seq, width = z3.shape
    if G > 1:
        assert t_seq == S and nsub == 1
        z3 = z3.reshape(nseq // G, G * S, width)
    b, t, _ = z3.shape
    L = G * S
    nc = t // (L * nsub)
    nlev, tri, mstack, masks = _gla_tables(S, G)
    hk, hs = _gla_consts(L)
    qk_w = GLA_HEADS * GLA_DK
    v_w = GLA_HEADS * GLA_DV
    col = lambda j: (lambda bi, ci: (bi, ci, j))
    const2 = lambda bi, ci: (0, 0)
    const3 = lambda bi, ci: (0, 0, 0)
    o, s_new = pl.pallas_call(
        functools.partial(_gla_kernel, L=L, nlev=nlev, Bb=Bb, nsub=nsub, G=G),
        grid=(b // Bb, nc),
        in_specs=[pl.BlockSpec((Bb, nsub * L, qk_w), col(0)),
                  pl.BlockSpec((Bb, nsub * L, qk_w), col(1)),
                  pl.BlockSpec((Bb, nsub * L, v_w), col(1)),
                  pl.BlockSpec((Bb, nsub * L, v_w), col(2)),
                  pl.BlockSpec((Bb, nsub * L, LANE), col(2048 // LANE)),
                  pl.BlockSpec((LANE, 2 * qk_w), const2),
                  pl.BlockSpec((1, qk_w), const2),
                  pl.BlockSpec((1, v_w), const2),
                  pl.BlockSpec((L, L), const2),
                  pl.BlockSpec(mstack.shape, const2),
                  pl.BlockSpec(masks.shape, const3),
                  pl.BlockSpec(hk.shape, const2),
                  pl.BlockSpec(hs.shape, const2),
                  pl.BlockSpec((Bb * G, GLA_HEADS, GLA_DK, GLA_DV), lambda bi, ci: (bi, 0, 0, 0))],
        out_specs=[pl.BlockSpec((Bb, nsub * L, v_w), col(0)),
                   pl.BlockSpec((Bb * G, GLA_HEADS, GLA_DK, GLA_DV), lambda bi, ci: (bi, 0, 0, 0))],
        out_shape=[jax.ShapeDtypeStruct((b, t, v_w), BF16),
                   jax.ShapeDtypeStruct((nseq, GLA_HEADS, GLA_DK, GLA_DV), F32)],
        scratch_shapes=[pltpu.VMEM((Bb * G, v_w, qk_w), F32)],
        compiler_params=_params(("parallel", "arbitrary")),
        name="gla",
    )(z3, z3, z3, z3, z3, wg, bg, gn, tri, mstack, masks, hk, hs, s0)
    return o.reshape(nseq, t_seq, v_w), s_new


def _s5_kernel(u_ref, bbre_ref, bbim_ref, cre_ref, cim_ref, d_ref, wglu_ref, bglu_ref, are_ref, aim_ref,
               x0re_ref, x0im_ref,
               o_ref, xore_ref, xoim_ref,
               xr_s, xi_s, cr_s, ci_s, pre_ref, pim_ref, *, rows, sequential):
    c = pl.program_id(1)
    lead = rows // SUBLANE if sequential else SUBLANE
    inner = rows // lead
    cw = S5_LANES // 4 if inner == SUBLANE else LANE
    u = jnp.swapaxes(u_ref[0].reshape(inner, lead, S5_WIDTH), 0, 1).reshape(rows, S5_WIDTH)
    ub = u.astype(BF16)

    if sequential:
        @pl.when(c == 0)
        def _():
            pre_ref[0] = are_ref[...]
            pim_ref[0] = aim_ref[...]
            n = 1
            while n < lead:
                nr, ni = pre_ref[n - 1], pim_ref[n - 1]
                sr, si = pre_ref[0:n], pim_ref[0:n]
                pre_ref[n:2 * n] = sr * nr - si * ni
                pim_ref[n:2 * n] = sr * ni + si * nr
                n *= 2

    hw, hl = S5_WIDTH // 2, S5_LANES // 2
    for half in range(2):
        uh = ub[:, half * hw:(half + 1) * hw]
        ls = slice(half * hl, (half + 1) * hl)
        xr_s[:, :, ls] = jnp.dot(uh, bbre_ref[half], preferred_element_type=F32).reshape(lead, inner, hl)
        xi_s[:, :, ls] = jnp.dot(uh, bbim_ref[half], preferred_element_type=F32).reshape(lead, inner, hl)

    for c0 in range(0, S5_LANES, cw):
        ar = jnp.broadcast_to(are_ref[:, c0:c0 + cw], (inner, cw))
        ai = jnp.broadcast_to(aim_ref[:, c0:c0 + cw], (inner, cw))

        def body(j, carry, c0=c0, ar=ar, ai=ai):
            xr, xi = carry
            nr = ar * xr - ai * xi + xr_s[j, :, c0:c0 + cw]
            ni = ar * xi + ai * xr + xi_s[j, :, c0:c0 + cw]
            xr_s[j, :, c0:c0 + cw] = nr
            xi_s[j, :, c0:c0 + cw] = ni
            return nr, ni

        if sequential:
            init = (jnp.zeros((inner, cw), F32), jnp.zeros((inner, cw), F32))
        else:
            init = (x0re_ref[0, :, c0:c0 + cw], x0im_ref[0, :, c0:c0 + cw])
        lax.fori_loop(0, lead, body, init, unroll=True)

    if sequential:
        @pl.when(c == 0)
        def _():
            cr_s[...] = x0re_ref[0]
            ci_s[...] = x0im_ref[0]

        end_r = xr_s[lead - 1]
        end_i = xi_s[lead - 1]
        anr = pre_ref[lead - 1]
        ani = pim_ref[lead - 1]
        ent_r, ent_i = [cr_s[...]], [ci_s[...]]
        for s in range(SUBLANE):
            pr, pi = ent_r[-1], ent_i[-1]
            ent_r.append(end_r[s:s + 1] + anr * pr - ani * pi)
            ent_i.append(end_i[s:s + 1] + anr * pi + ani * pr)
        cr_s[...] = ent_r[SUBLANE]
        ci_s[...] = ent_i[SUBLANE]
        er = jnp.concatenate(ent_r[:SUBLANE], axis=0)[None]
        ei = jnp.concatenate(ent_i[:SUBLANE], axis=0)[None]
        pr3, pi3 = pre_ref[...], pim_ref[...]
        xr = xr_s[...] + pr3 * er - pi3 * ei
        xi = xi_s[...] + pr3 * ei + pi3 * er

        @pl.when(c == pl.num_programs(1) - 1)
        def _():
            xore_ref[0] = cr_s[...]
            xoim_ref[0] = ci_s[...]
    else:
        xr = xr_s[...]
        xi = xi_s[...]
        xore_ref[0] = xr[lead - 1]
        xoim_ref[0] = xi[lead - 1]

    xrb = xr.reshape(rows, S5_LANES).astype(BF16)
    xib = xi.reshape(rows, S5_LANES).astype(BF16)
    y = jnp.concatenate(
        [jnp.dot(xrb[:, h * hl:(h + 1) * hl], cre_ref[h], preferred_element_type=F32)
         - jnp.dot(xib[:, h * hl:(h + 1) * hl], cim_ref[h], preferred_element_type=F32) for h in range(2)],
        axis=1) + d_ref[...] * u
    zz = jax.nn.gelu(y)
    gate = jnp.dot(zz.astype(BF16), wglu_ref[...], preferred_element_type=F32) + bglu_ref[...]
    out = zz * jax.nn.sigmoid(gate)
    o_ref[0] = jnp.swapaxes(out.reshape(lead, inner, S5_WIDTH), 0, 1).reshape(rows, S5_WIDTH).astype(o_ref.dtype)


def _s5(z3, tabs, x0re, x0im, rows, sequential):
    b, t, _ = z3.shape
    nblk = t // rows
    groups = rows // SUBLANE
    ublk = 1536 // S5_WIDTH
    const2 = lambda bi, ci: (0, 0)
    const3 = lambda bi, ci: (0, 0, 0)
    lead = groups if sequential else SUBLANE
    hw, hl = S5_WIDTH // 2, S5_LANES // 2
    if sequential:
        st_spec = pl.BlockSpec((1, 1, S5_LANES), lambda bi, ci: (bi, 0, 0))
        st_shape = jax.ShapeDtypeStruct((b, 1, S5_LANES), F32)
    else:
        st_spec = pl.BlockSpec((1, groups, S5_LANES), lambda bi, ci: (bi, ci, 0))
        st_shape = jax.ShapeDtypeStruct(x0re.shape, F32)
    return pl.pallas_call(
        functools.partial(_s5_kernel, rows=rows, sequential=sequential),
        grid=(b, nblk),
        in_specs=[pl.BlockSpec((1, rows, S5_WIDTH), lambda bi, ci: (bi, ci, ublk)),
                  pl.BlockSpec((2, hw, hl), const3),
                  pl.BlockSpec((2, hw, hl), const3),
                  pl.BlockSpec((2, hl, hw), const3),
                  pl.BlockSpec((2, hl, hw), const3),
                  pl.BlockSpec((1, S5_WIDTH), const2),
                  pl.BlockSpec((S5_WIDTH, S5_WIDTH), const2),
                  pl.BlockSpec((1, S5_WIDTH), const2),
                  pl.BlockSpec((1, S5_LANES), const2),
                  pl.BlockSpec((1, S5_LANES), const2),
                  st_spec, st_spec],
        out_specs=[pl.BlockSpec((1, rows, S5_WIDTH), lambda bi, ci: (bi, ci, 0)),
                   st_spec, st_spec],
        out_shape=[jax.ShapeDtypeStruct((b, t, S5_WIDTH), BF16), st_shape, st_shape],
        scratch_shapes=[pltpu.VMEM((lead, rows // lead, S5_LANES), F32),
                        pltpu.VMEM((lead, rows // lead, S5_LANES), F32),
                        pltpu.VMEM((1, S5_LANES), F32),
                        pltpu.VMEM((1, S5_LANES), F32),
                        pltpu.VMEM((lead if sequential else 1, 1, S5_LANES), F32),
                        pltpu.VMEM((lead if sequential else 1, 1, S5_LANES), F32)],
        compiler_params=_params(("parallel", "arbitrary")),
        name="s5",
    )(z3, tabs['bbre'], tabs['bbim'], tabs['cre'], tabs['cim'], tabs['d'], tabs['wglu'], tabs['bglu'],
      tabs['are'], tabs['aim'], x0re, x0im)


def _s5_tables(a_re, a_im, log_dt, b_re, b_im, c_re, c_im, d, w_glu, b_glu):
    lam_re = a_re.astype(F32)
    lam_im = a_im.astype(F32)
    dt = jnp.exp(log_dt.astype(F32))[:, None]
    mag = jnp.exp(lam_re * dt)
    ang = lam_im * dt
    ab_re = mag * jnp.cos(ang)
    ab_im = mag * jnp.sin(ang)
    den = lam_re * lam_re + lam_im * lam_im
    co_re = ((ab_re - 1.0) * lam_re + ab_im * lam_im) / den
    co_im = (ab_im * lam_re - (ab_re - 1.0) * lam_im) / den
    b_re = b_re.astype(F32)
    b_im = b_im.astype(F32)
    bb_re = co_re[..., None] * b_re - co_im[..., None] * b_im
    bb_im = co_re[..., None] * b_im + co_im[..., None] * b_re
    gh = S5_GROUPS // 2
    eye = jnp.eye(gh, dtype=F32)
    hw, hl = S5_WIDTH // 2, S5_LANES // 2
    halves = lambda a: a.reshape((2, gh) + a.shape[1:])
    blockdiag_in = lambda bb: jnp.einsum('xgph,gk->xghkp', halves(bb), eye).reshape(2, hw, hl).astype(BF16)
    blockdiag_out = lambda cc: jnp.einsum('xghp,gk->xgpkh', halves(cc.astype(F32)), eye).reshape(2, hl, hw).astype(BF16)
    return {
        'bbre': blockdiag_in(bb_re), 'bbim': blockdiag_in(bb_im),
        'cre': blockdiag_out(c_re), 'cim': blockdiag_out(c_im),
        'd': d.astype(F32).reshape(1, S5_WIDTH),
        'wglu': w_glu.astype(BF16), 'bglu': b_glu.astype(F32).reshape(1, S5_WIDTH),
        'are': ab_re.reshape(1, S5_LANES), 'aim': ab_im.reshape(1, S5_LANES),
    }


def _rmsnorm_val(x, g):
    return x * lax.rsqrt(jnp.mean(x * x, axis=-1, keepdims=True) + EPS) * g


def _ffn_kernel(r_ref, a1_ref, a2_ref, w1_ref, w2_ref, g_ref, wup_ref, cw_ref, cb_ref, wd_ref, st_ref, gf_ref,
                o_ref, so_ref, halo_s, *, tm, tf, long_seq, final_norm):
    i = pl.program_id(1)
    n = tm // SUBLANE
    d = r_ref.shape[-1]
    x = (r_ref[0] + jnp.dot(a1_ref[0], w1_ref[...], preferred_element_type=F32)
         + jnp.dot(a2_ref[0], w2_ref[...], preferred_element_type=F32))
    hn = _rmsnorm_val(x, g_ref[...])
    if long_seq:
        lead = n

        @pl.when(i == 0)
        def _():
            halo_s[...] = st_ref[0]
    else:
        lead = SUBLANE
    hnp = jnp.swapaxes(hn.reshape(tm // lead, lead, d), 0, 1).reshape(tm, d).astype(BF16)
    inner = tm // lead
    sub = lax.broadcasted_iota(jnp.int32, (SUBLANE, tf), 0)

    acc = jnp.zeros((tm, d), F32)
    for c in range(D_FF // tf):
        conv = []
        for half in range(2):
            c0 = half * D_FF + c * tf
            u = jnp.dot(hnp, wup_ref[:, c0:c0 + tf], preferred_element_type=F32).reshape(lead, inner, tf)
            if long_seq:
                um1 = jnp.where(sub == 0, halo_s[1:2, c0:c0 + tf], pltpu.roll(u[lead - 1], 1, axis=0))
                um2 = jnp.where(sub == 0, halo_s[0:1, c0:c0 + tf], pltpu.roll(u[lead - 2], 1, axis=0))
                halo_s[0:1, c0:c0 + tf] = u[lead - 2][SUBLANE - 1:SUBLANE]
                halo_s[1:2, c0:c0 + tf] = u[lead - 1][SUBLANE - 1:SUBLANE]
            else:
                um2 = st_ref[0, :, c0:c0 + tf]
                um1 = st_ref[1, :, c0:c0 + tf]
                so_ref[0, :, c0:c0 + tf] = u[lead - 2]
                so_ref[1, :, c0:c0 + tf] = u[lead - 1]
            ext = jnp.concatenate([um2[None], um1[None], u], axis=0)
            cv = cb_ref[:, c0:c0 + tf]
            for j in range(CONV_W):
                cv = cv + ext[j:j + lead] * cw_ref[j:j + 1, c0:c0 + tf]
            conv.append(cv)
        act = (jax.nn.silu(conv[1]) * conv[0]).reshape(tm, tf).astype(BF16)
        acc = acc + jnp.dot(act, wd_ref[c * tf:(c + 1) * tf, :], preferred_element_type=F32)

    out = x + jnp.swapaxes(acc.reshape(lead, inner, d), 0, 1).reshape(tm, d)
    if final_norm:
        out = _rmsnorm_val(out, gf_ref[...])
    o_ref[0] = out

    if long_seq:
        @pl.when(i == pl.num_programs(1) - 1)
        def _():
            so_ref[0] = halo_s[...]


def _ffn(res3, a1, a2, w1, w2, g, wup, cw, cb, wd, state, layer, gfinal, tm, tf, long_seq, final_norm):
    b, t, d = res3.shape
    nt = t // tm
    resident = dict(pipeline_mode=pl.Buffered(1))
    const2 = lambda bi, i: (0, 0)
    row_blk = lambda w: pl.BlockSpec((1, tm, w), lambda bi, i: (bi, i, 0))
    per_layer = lambda shape, **kw: pl.BlockSpec((None,) + shape, lambda bi, i: (layer, 0, 0), **kw)
    if long_seq:
        st_in = st_out = pl.BlockSpec((1, CONV_W - 1, 2 * D_FF), lambda bi, i: (bi, 0, 0))
        st_shape = state.shape
    else:
        nseq = tm // SUBLANE
        st_in = pl.BlockSpec((None, CONV_W - 1, nseq, 2 * D_FF), lambda bi, i: (layer, 0, i, 0))
        st_out = pl.BlockSpec((CONV_W - 1, nseq, 2 * D_FF), lambda bi, i: (0, i, 0))
        st_shape = state.shape[1:]
    return pl.pallas_call(
        functools.partial(_ffn_kernel, tm=tm, tf=tf, long_seq=long_seq, final_norm=final_norm),
        grid=(b, nt),
        in_specs=[row_blk(d), row_blk(a1.shape[-1]), row_blk(a2.shape[-1]),
                  pl.BlockSpec(w1.shape, const2, **resident),
                  pl.BlockSpec(w2.shape, const2, **resident),
                  per_layer((1, d)),
                  per_layer((d, 2 * D_FF), **resident),
                  per_layer((CONV_W, 2 * D_FF)),
                  per_layer((1, 2 * D_FF)),
                  per_layer((D_FF, d), **resident),
                  st_in,
                  pl.BlockSpec((1, d), const2)],
        out_specs=[row_blk(d), st_out],
        out_shape=[jax.ShapeDtypeStruct((b, t, d), F32), jax.ShapeDtypeStruct(st_shape, F32)],
        scratch_shapes=[pltpu.VMEM((CONV_W - 1, 2 * D_FF), F32)],
        compiler_params=_params(("parallel", "arbitrary")),
        name="ffn",
    )(res3, a1, a2, w1, w2, g.reshape(-1, 1, d), wup, cw, cb.reshape(-1, 1, 2 * D_FF), wd, state, gfinal)


def _rope_tables(pos):
    half = SW_HD // 2
    inv = ROPE_THETA ** (-jnp.arange(half, dtype=F32) / half)
    ang = pos.astype(F32)[:, None] * inv[None, :]
    cos = jnp.cos(ang)
    sin = jnp.sin(ang)
    reps = LANE // SW_HD
    cos_t = jnp.tile(jnp.concatenate([cos, cos], axis=1), (1, reps))
    sin_t = jnp.tile(jnp.concatenate([-sin, sin], axis=1), (1, reps))
    return cos_t, sin_t


def _prep_weights(W):
    P = {}
    we = W['w_in_even']
    nh = 4 * ML_HEADS * ML_DK
    head, g_m, tail = we[:, :nh], we[:, nh:nh + 2 * ML_HEADS], we[:, nh + 2 * ML_HEADS:]
    q_a, k_a, v_a = tail[:, :512], tail[:, 512:640], tail[:, 640:768]
    dup = lambda w: jnp.concatenate([w[:, :SW_HD], w[:, :SW_HD], w[:, SW_HD:], w[:, SW_HD:]], axis=1)
    pad = jnp.zeros((D_MODEL, LANE - 2 * ML_HEADS), we.dtype)
    P['w_in_even'] = jnp.concatenate([head, q_a, dup(k_a), dup(v_a), g_m, pad], axis=1).astype(BF16)
    assert P['w_in_even'].shape[1] == EVEN_PAD
    P['b_gates'] = jnp.pad(W['b_mlstm_gates'].astype(F32), (0, LANE - 2 * ML_HEADS)).reshape(1, LANE)
    wo = W['w_in_odd']
    nh = 2 * GLA_HEADS * GLA_DK + 2 * GLA_HEADS * GLA_DV
    head, glr, u = wo[:, :nh], wo[:, nh:nh + GLA_RANK], wo[:, nh + GLA_RANK:]
    pad = jnp.zeros((D_MODEL, ODD_PAD - 2048 - GLA_RANK), wo.dtype)
    P['w_in_odd'] = jnp.concatenate([head, u, glr, pad], axis=1).astype(BF16)
    wg = jnp.pad(W['w_gla_gate_up'].astype(F32), ((0, LANE - GLA_RANK), (0, 0)))
    wg_hi = wg.astype(BF16)
    P['w_gate_up'] = jnp.concatenate([wg_hi, (wg - wg_hi.astype(F32)).astype(BF16)], axis=1)
    P['b_gate'] = W['b_gla_gate'].astype(F32).reshape(1, -1)
    P['g_gla'] = W['g_gla_norm'].astype(F32).reshape(1, -1)
    hm = ML_HEADS * ML_DV
    P['w_out_even'] = (W['w_out_even'][:hm].astype(BF16), W['w_out_even'][hm:].astype(BF16))
    hg = GLA_HEADS * GLA_DV
    P['w_out_odd'] = (W['w_out_odd'][:hg].astype(BF16), W['w_out_odd'][hg:].astype(BF16))
    P['w_ffn_up'] = W['w_ffn_up'].astype(BF16)
    P['w_ffn_down'] = W['w_ffn_down'].astype(BF16)
    P['s5'] = _s5_tables(W['s5_a_re'], W['s5_a_im'], W['s5_log_dt'], W['s5_b_re'], W['s5_b_im'],
                         W['s5_c_re'], W['s5_c_im'], W['s5_d'], W['w_s5_glu'], W['b_s5_glu'])
    return P


def _trunk(x, pos, st, W, P, is_prompt):
    b, t, d = x.shape
    n = b * t
    L = CHUNK if t % CHUNK == 0 else t
    nsub = next((c for c in (8, 4, 2) if t % (c * L) == 0), 1)
    tm = next((c for c in (1024, 512) if n % c == 0), n)
    cos, sin_signed = _rope_tables(pos)
    gfinal = W['norm_final'].astype(F32).reshape(1, d)
    new = {}
    conv_out = []
    h = x.reshape(n, d)

    z = _norm_matmul(h, W['norm_mix'][0], P['w_in_even'], tm)
    z3 = z.reshape(b, t, EVEN_PAD)
    pack = dict(G=1, Bb=2 if b % 2 == 0 else 1, nsub=nsub) if (is_prompt or b % 16) else dict(G=8, Bb=2, nsub=1)
    h_m, new['ml_C'], new['ml_n'], m_new = _mlstm(z3, P['b_gates'], st['ml_C'], st['ml_n'], st['ml_m'], L, **pack)
    new['ml_m'] = m_new.reshape(b, ML_HEADS)
    if is_prompt:
        h_a, kb = _swa_prompt(z3, cos, sin_signed, W['sw_sinks'].astype(F32), Bb=2 if b % 2 == 0 else 1)
        vb = jnp.concatenate([z3[:, t - WINDOW:, EVEN_VD:EVEN_VD + SW_HD],
                              z3[:, t - WINDOW:, EVEN_VD + LANE:EVEN_VD + LANE + SW_HD]], axis=-1)
    else:
        h_a, kb, vb = _swa_sample(z3, st['kbuf'].reshape(b, WINDOW, -1), st['vbuf'].reshape(b, WINDOW, -1),
                                  cos, sin_signed, W['sw_sinks'].astype(F32), Bb=8)
    new['kbuf'] = kb.reshape(b, WINDOW, SW_KV_HEADS, SW_HD)
    new['vbuf'] = vb.reshape(b, WINDOW, SW_KV_HEADS, SW_HD)
    h, cb = _ffn_layer(h, h_m, h_a, P['w_out_even'], 0, st, W, P, b, t, gfinal, is_prompt, final_norm=False)
    conv_out.append(cb)

    z = _norm_matmul(h, W['norm_mix'][1], P['w_in_odd'], tm)
    z3 = z.reshape(b, t, ODD_PAD)
    o_g, new['gla'] = _gla(z3, P['w_gate_up'], P['b_gate'], P['g_gla'], st['gla'], L, **pack)
    x0re = st['s5_re'].astype(F32).reshape(b, 1, S5_LANES)
    x0im = st['s5_im'].astype(F32).reshape(b, 1, S5_LANES)
    if is_prompt:
        o_s, xre, xim = _s5(z3, P['s5'], x0re, x0im, rows=next((c for c in (512, 256) if t % c == 0), t),
                            sequential=True)
    else:
        o_s, xre, xim = _s5(z3.reshape(1, n, ODD_PAD), P['s5'], x0re.reshape(1, b, S5_LANES),
                            x0im.reshape(1, b, S5_LANES), rows=256 if n % 256 == 0 else n, sequential=False)
    new['s5_re'] = xre.reshape(b, S5_GROUPS, S5_STATE)
    new['s5_im'] = xim.reshape(b, S5_GROUPS, S5_STATE)
    h, cb = _ffn_layer(h, o_g, o_s, P['w_out_odd'], 1, st, W, P, b, t, gfinal, is_prompt, final_norm=True)
    conv_out.append(cb)
    new['conv'] = jnp.stack(conv_out)
    return h.reshape(b, t, d), new


def _ffn_layer(h, a1, a2, w_out, layer, st, W, P, b, t, gfinal, is_prompt, final_norm):
    n, d = h.shape
    rows = t if is_prompt else n
    tm = next((c for c in (512, 256) if rows % c == 0), rows)
    args = (*w_out, W['norm_ffn'], P['w_ffn_up'], W['ffn_conv_w'].astype(F32), W['ffn_conv_b'].astype(F32),
            P['w_ffn_down'])
    if is_prompt:
        shape3 = lambda a: a.reshape(b, t, -1)
        out, new_state = _ffn(shape3(h), shape3(a1), shape3(a2), *args, st['conv'][layer], layer, gfinal, tm,
                              FFN_TF, True, final_norm)
    else:
        assert t == SUBLANE
        shape3 = lambda a: a.reshape(1, n, -1)
        out, new_state = _ffn(shape3(h), shape3(a1), shape3(a2), *args, st['conv_rows'], layer, gfinal, tm, FFN_TF,
                              False, final_norm)
        new_state = jnp.swapaxes(new_state, 0, 1)
    return out.reshape(n, d), new_state


def kernel(x_prompt, x_sample, state_mlstm_C, state_mlstm_n, state_mlstm_m, cache_swa_k, cache_swa_v,
           state_gla, state_s5_re, state_s5_im, state_ffn_conv, norm_mix, norm_ffn, norm_final,
           w_in_even, b_mlstm_gates, sw_sinks, w_out_even, w_in_odd, w_gla_gate_up, b_gla_gate, g_gla_norm,
           s5_a_re, s5_a_im, s5_log_dt, s5_b_re, s5_b_im, s5_c_re, s5_c_im, s5_d, w_s5_glu, b_s5_glu,
           w_out_odd, w_ffn_up, ffn_conv_w, ffn_conv_b, w_ffn_down):
    W = {'norm_mix': norm_mix.astype(F32), 'norm_ffn': norm_ffn.astype(F32), 'norm_final': norm_final,
         'w_in_even': w_in_even, 'b_mlstm_gates': b_mlstm_gates, 'sw_sinks': sw_sinks, 'w_out_even': w_out_even,
         'w_in_odd': w_in_odd, 'w_gla_gate_up': w_gla_gate_up, 'b_gla_gate': b_gla_gate, 'g_gla_norm': g_gla_norm,
         's5_a_re': s5_a_re, 's5_a_im': s5_a_im, 's5_log_dt': s5_log_dt, 's5_b_re': s5_b_re, 's5_b_im': s5_b_im,
         's5_c_re': s5_c_re, 's5_c_im': s5_c_im, 's5_d': s5_d, 'w_s5_glu': w_s5_glu, 'b_s5_glu': b_s5_glu,
         'w_out_odd': w_out_odd, 'w_ffn_up': w_ffn_up, 'ffn_conv_w': ffn_conv_w, 'ffn_conv_b': ffn_conv_b,
         'w_ffn_down': w_ffn_down}
    P = _prep_weights(W)
    bp, tp = x_prompt.shape[:2]
    st_prompt = {'ml_C': jnp.zeros((bp, ML_HEADS, ML_DK, ML_DV), F32),
                 'ml_n': jnp.zeros((bp, ML_HEADS, ML_DK), F32),
                 'ml_m': jnp.zeros((bp, ML_HEADS), F32),
                 'gla': jnp.zeros((bp, GLA_HEADS, GLA_DK, GLA_DV), F32),
                 's5_re': jnp.zeros((bp, S5_GROUPS, S5_STATE), F32),
                 's5_im': jnp.zeros((bp, S5_GROUPS, S5_STATE), F32),
                 'conv': jnp.zeros((2, bp, CONV_W - 1, 2 * D_FF), F32)}
    st_sample = {'ml_C': state_mlstm_C.astype(F32), 'ml_n': state_mlstm_n.astype(F32),
                 'ml_m': state_mlstm_m.astype(F32), 'kbuf': cache_swa_k.astype(F32),
                 'vbuf': cache_swa_v.astype(F32), 'gla': state_gla.astype(F32),
                 's5_re': state_s5_re, 's5_im': state_s5_im,
                 'conv_rows': jnp.swapaxes(state_ffn_conv.astype(F32), 1, 2)}
    past_len = 16384
    y_p, np_ = _trunk(x_prompt.astype(F32), jnp.arange(tp), st_prompt, W, P, True)
    y_s, ns_ = _trunk(x_sample.astype(F32), past_len + jnp.arange(x_sample.shape[1]), st_sample, W, P, False)
    return (y_p, y_s,
            np_['ml_C'], ns_['ml_C'], np_['ml_n'], ns_['ml_n'], np_['ml_m'], ns_['ml_m'],
            np_['kbuf'], ns_['kbuf'], np_['vbuf'], ns_['vbuf'], np_['gla'], ns_['gla'],
            np_['s5_re'], ns_['s5_re'], np_['s5_im'], ns_['s5_im'], np_['conv'], ns_['conv'])
```

```python
import functools
import math

import numpy as np
import jax
import jax.numpy as jnp
from jax import lax
from jax.experimental import pallas as pl
from jax.experimental.pallas import tpu as pltpu

F32 = jnp.float32
BF16 = jnp.bfloat16
NT = (((1,), (1,)), ((), ()))
TN = (((0,), (0,)), ((), ()))

D_MODEL = 1024
PAST_LEN = 16384
ML_HEADS, ML_DK, ML_DV = 4, 128, 128
SW_HEADS, SW_KV_HEADS, SW_HD, WINDOW = 8, 2, 64, 128
SW_GQ = SW_HEADS // SW_KV_HEADS
ROPE_THETA = 10000.0
GLA_HEADS, GLA_DK, GLA_DV, GLA_RANK, GLA_TAU = 4, 64, 128, 16, 16.0
S5_WIDTH, S5_GROUP, S5_STATE = 512, 16, 64
S5_GROUPS = S5_WIDTH // S5_GROUP
S5_LANES = S5_GROUPS * S5_STATE
D_FF = 2816
FFN_TF = 2816
CONV_W = 3
CHUNK = 64
EPS = 1e-6

LANE = 128
SUBLANE = 8
VMEM_LIMIT = 56 * 1024 * 1024

EVEN_QA, EVEN_KD, EVEN_VD, EVEN_PAD = 2048, 2560, 2816, 3200
ODD_K, ODD_V, ODD_R, ODD_U, ODD_GATE, ODD_PAD = 256, 512, 1024, 1536, 2048, 2176


def _params(sem):
    return pltpu.CompilerParams(dimension_semantics=sem, vmem_limit_bytes=VMEM_LIMIT)


def _norm_matmul_kernel(x_ref, g_ref, w_ref, o_ref):
    x = x_ref[...]
    ms = jnp.mean(x * x, axis=-1, keepdims=True)
    hn = (x * lax.rsqrt(ms + EPS) * g_ref[...]).astype(BF16)
    o_ref[...] = jnp.dot(hn, w_ref[...], preferred_element_type=F32)


def _norm_matmul(x, g, w, tm):
    n, d = x.shape
    c = w.shape[1]
    return pl.pallas_call(
        _norm_matmul_kernel,
        grid=(n // tm,),
        in_specs=[pl.BlockSpec((tm, d), lambda i: (i, 0)),
                  pl.BlockSpec((1, d), lambda i: (0, 0)),
                  pl.BlockSpec((d, c), lambda i: (0, 0))],
        out_specs=pl.BlockSpec((tm, c), lambda i: (i, 0)),
        out_shape=jax.ShapeDtypeStruct((n, c), F32),
        compiler_params=_params(("parallel",)),
        name="norm_matmul",
    )(x, g.reshape(1, d), w)


def _pieces(x, n):
    out, r = [], x
    for _ in range(n):
        p = r.astype(BF16).astype(F32)
        out.append(p)
        r = r - p
    return out


def _dot_sel(m_bf16, x, dims=None, pieces=3):
    ps = _pieces(x, pieces)
    axis = 1 if dims is None else 0
    w = x.shape[axis]
    if w % LANE:
        f = (lambda p: jnp.dot(m_bf16, p, preferred_element_type=F32)) if dims is None else (
            lambda p: lax.dot_general(m_bf16, p, dims, preferred_element_type=F32))
        return sum(f(p.astype(BF16)) for p in ps)
    stacked = jnp.concatenate(ps, axis=axis).astype(BF16)
    if dims is None:
        r = jnp.dot(m_bf16, stacked, preferred_element_type=F32)
    else:
        r = lax.dot_general(m_bf16, stacked, dims, preferred_element_type=F32)
    return sum(r[:, i * w:(i + 1) * w] for i in range(pieces))


def _sel_right(x, e_bf16, pieces=3):
    rows = x.shape[0]
    r = jnp.dot(jnp.concatenate(_pieces(x, pieces), axis=0).astype(BF16), e_bf16, preferred_element_type=F32)
    return sum(r[i * rows:(i + 1) * rows] for i in range(pieces))


def _mlstm_kernel(q_ref, k_ref, v_ref, og_ref, gz_ref, bias_ref, tri_ref, ones_ref, expc_ref, expv_ref, eyet_ref,
                  seg_ref, segv_ref, hmk_ref, c0_ref, n0_ref, m0_ref,
                  h_ref, co_ref, no_ref, mo_ref,
                  c_s, n_s, m_s, *, L, Bb, nsub, G):
    c = pl.program_id(1)

    @pl.when(c == 0)
    def _():
        c_s[...] = c0_ref[...]
        n_s[...] = n0_ref[...]
        m_s[...] = m0_ref[...]

    for sub in range(nsub):
        for bi in range(Bb):
            _mlstm_one(bi, sub * L, q_ref, k_ref, v_ref, og_ref, gz_ref, bias_ref, tri_ref, ones_ref, expc_ref,
                       expv_ref, eyet_ref, seg_ref, segv_ref, hmk_ref, h_ref, c_s, n_s, m_s, L, G)

    @pl.when(c == pl.num_programs(1) - 1)
    def _():
        co_ref[...] = c_s[...]
        no_ref[...] = n_s[...]
        mo_ref[...] = m_s[...]


def _mlstm_one(bi, r0, q_ref, k_ref, v_ref, og_ref, gz_ref, bias_ref, tri_ref, ones_ref, expc_ref, expv_ref,
               eyet_ref, seg_ref, segv_ref, hmk_ref, h_ref, c_s, n_s, m_s, L, G):
    H, DK = ML_HEADS, ML_DK
    HL = H * L
    S = L // G
    seg = lambda a, g: a[g * S:(g + 1) * S]
    per_row = lambda rows: rows[0] if G == 1 else jnp.concatenate(
        [jnp.broadcast_to(r, (S, r.shape[1])) for r in rows], axis=0)
    lane = lax.broadcasted_iota(jnp.int32, (L, LANE), 1)
    rowi = lax.broadcasted_iota(jnp.int32, (L, LANE), 0) % S
    keep = lane < H
    heads_only = lambda x: jnp.where(keep[:x.shape[0]], x, 0.0)

    rw = pl.ds(r0, L)
    gates = gz_ref[bi, rw] + bias_ref[...]
    lf = jax.nn.log_sigmoid(gates)
    b = pltpu.roll(_dot_sel(tri_ref[...], lf), LANE - H, axis=1)
    vv = gates - b
    cm = vv
    sh = 1
    while sh < S:
        cm = jnp.maximum(cm, jnp.where(rowi >= sh, pltpu.roll(cm, sh, axis=0), -jnp.inf))
        sh *= 2
    m_prev_g = [m_s[bi * G + g] for g in range(G)]
    m_prev = per_row(m_prev_g)
    mt = b + jnp.maximum(m_prev, cm)
    a_inter = jnp.exp(b + m_prev - mt)

    expv = expv_ref[...]
    m_new_g = [seg(mt, g)[S - 1:S, :] for g in range(G)]
    b_end_g = [seg(b, g)[S - 1:S, :] for g in range(G)]
    w_end = heads_only(jnp.exp(per_row(b_end_g) - b + gates - per_row(m_new_g)))
    decay_g = [heads_only(jnp.exp(b_end_g[g] + m_prev_g[g] - m_new_g[g])) for g in range(G)]
    decay_rows = jnp.concatenate(decay_g + [jnp.zeros((-G % SUBLANE, LANE), F32)] * (G % SUBLANE != 0), axis=0)
    spread_k = _sel_right(jnp.concatenate([heads_only(b - mt), heads_only(vv)], axis=0), expc_ref[...])
    uc = spread_k[:L]
    vr = _dot_sel(ones_ref[...], spread_k[L:] * eyet_ref[...])
    spread_v = _sel_right(jnp.concatenate([heads_only(a_inter), w_end, decay_rows], axis=0), expv, pieces=2)
    ac, wc, dcs = spread_v[:L], spread_v[L:2 * L], spread_v[2 * L:2 * L + G]
    ti = lax.broadcasted_iota(jnp.int32, (L, HL), 0)
    si = lax.broadcasted_iota(jnp.int32, (L, HL), 1) % L
    causal = (si <= ti) & (si // S == ti // S)
    w = jnp.exp(jnp.where(causal, uc + vr, -jnp.inf))

    q = q_ref[bi, rw]
    ks = k_ref[bi, rw] * (DK ** -0.5)
    v = v_ref[bi, rw]
    qb = q.astype(BF16)
    if L % (2 * SUBLANE) == 0:
        kst = jnp.concatenate([ks.astype(BF16)] * H, axis=0) * hmk_ref[...]
        vsrc = v.astype(BF16)
    else:
        kst = (jnp.concatenate([ks] * H, axis=0) * hmk_ref[...].astype(F32)).astype(BF16)
        vsrc = v
    zero = jnp.zeros((L, DK), vsrc.dtype)
    vbd = jnp.concatenate(
        [jnp.concatenate([zero] * h + [vsrc[:, h * DK:(h + 1) * DK]] + [zero] * (H - 1 - h), axis=1)
         for h in range(H)], axis=0).astype(BF16)
    s = lax.dot_general(qb, kst, NT, preferred_element_type=F32) * w
    qc = jnp.concatenate(
        [jnp.concatenate([jnp.dot(seg(q, g)[:, h * DK:(h + 1) * DK].astype(BF16), c_s[bi * G + g, h].astype(BF16),
                                  preferred_element_type=F32) for h in range(H)], axis=1)
         for g in range(G)], axis=0)
    num = jnp.dot(s.astype(BF16), vbd, preferred_element_type=F32) + ac * qc
    n_rows = per_row([n_s[bi * G + g] for g in range(G)])
    den = (_sel_right(s, seg_ref[...], pieces=2)
           + a_inter * _sel_right(q * n_rows, segv_ref[...], pieces=2))
    rden = heads_only(1.0 / jnp.maximum(jnp.abs(den), jnp.exp(-mt)))
    hh = num * _sel_right(rden, expv, pieces=2) * jax.nn.sigmoid(og_ref[bi, rw])
    h_ref[bi, rw] = hh.astype(h_ref.dtype)

    kw = ks * wc
    for g in range(G):
        e = bi * G + g
        kwb = seg(kw, g).astype(BF16)
        vb = seg(v, g).astype(BF16)
        dc = dcs[g:g + 1]
        for h in range(H):
            hs = slice(h * DK, (h + 1) * DK)
            c_s[e, h] = dc[:, hs] * c_s[e, h] + lax.dot_general(kwb[:, hs], vb[:, hs], TN, preferred_element_type=F32)
        n_s[e] = dc * n_s[e] + jnp.sum(seg(kw, g), axis=0, keepdims=True)
        m_s[e] = heads_only(m_new_g[g])


def _mlstm_consts(S, G):
    L = S * G
    H, DK = ML_HEADS, ML_DK
    expc = np.zeros((LANE, H * L), np.float32)
    expv = np.zeros((LANE, H * DK), np.float32)
    seg = np.zeros((H * L, LANE), np.float32)
    segv = np.zeros((H * DK, LANE), np.float32)
    hmk = np.zeros((H * L, H * DK), np.float32)
    for h in range(H):
        expc[h, h * L:(h + 1) * L] = 1.0
        expv[h, h * DK:(h + 1) * DK] = 1.0
        seg[h * L:(h + 1) * L, h] = 1.0
        segv[h * DK:(h + 1) * DK, h] = 1.0
        hmk[h * L:(h + 1) * L, h * DK:(h + 1) * DK] = 1.0
    eyet = np.tile(np.eye(L, dtype=np.float32), (1, H))
    bf = lambda a: jnp.asarray(a, BF16)
    tri = np.kron(np.eye(G), np.tril(np.ones((S, S)))).astype(np.float32)
    return (bf(tri), bf(np.ones((L, L), np.float32)), bf(expc), bf(expv),
            jnp.asarray(eyet), bf(seg), bf(segv), bf(hmk))


def _mlstm(z3, bias, c0, n0, m0, S, G, Bb, nsub):
    nseq, t_seq, width = z3.shape
    if G > 1:
        assert t_seq == S and nsub == 1
        z3 = z3.reshape(nseq // G, G * S, width)
    b, t, _ = z3.shape
    L = G * S
    nc = t // (L * nsub)
    hw = ML_HEADS * ML_DK
    consts = _mlstm_consts(S, G)
    gate_blk = (EVEN_PAD - LANE) // LANE
    col = lambda j: (lambda bi, ci: (bi, ci, j))
    const2 = lambda bi, ci: (0, 0)
    state3 = lambda bi, ci: (bi, 0, 0)
    m0p = jnp.pad(m0, ((0, 0), (0, LANE - ML_HEADS))).reshape(nseq, 1, LANE)
    Bs = Bb * G
    h, c_new, n_new, m_new = pl.pallas_call(
        functools.partial(_mlstm_kernel, L=L, Bb=Bb, nsub=nsub, G=G),
        grid=(b // Bb, nc),
        in_specs=[pl.BlockSpec((Bb, nsub * L, hw), col(0)),
                  pl.BlockSpec((Bb, nsub * L, hw), col(1)),
                  pl.BlockSpec((Bb, nsub * L, hw), col(2)),
                  pl.BlockSpec((Bb, nsub * L, hw), col(3)),
                  pl.BlockSpec((Bb, nsub * L, LANE), col(gate_blk)),
                  pl.BlockSpec((1, LANE), const2)]
                 + [pl.BlockSpec(a.shape, const2) for a in consts]
                 + [pl.BlockSpec((Bs, ML_HEADS, ML_DK, ML_DV), lambda bi, ci: (bi, 0, 0, 0)),
                    pl.BlockSpec((Bs, 1, hw), state3),
                    pl.BlockSpec((Bs, 1, LANE), state3)],
        out_specs=[pl.BlockSpec((Bb, nsub * L, hw), col(0)),
                   pl.BlockSpec((Bs, ML_HEADS, ML_DK, ML_DV), lambda bi, ci: (bi, 0, 0, 0)),
                   pl.BlockSpec((Bs, 1, hw), state3),
                   pl.BlockSpec((Bs, 1, LANE), state3)],
        out_shape=[jax.ShapeDtypeStruct((b, t, hw), BF16),
                   jax.ShapeDtypeStruct((nseq, ML_HEADS, ML_DK, ML_DV), F32),
                   jax.ShapeDtypeStruct((nseq, 1, hw), F32),
                   jax.ShapeDtypeStruct((nseq, 1, LANE), F32)],
        scratch_shapes=[pltpu.VMEM((Bs, ML_HEADS, ML_DK, ML_DV), F32),
                        pltpu.VMEM((Bs, 1, hw), F32),
                        pltpu.VMEM((Bs, 1, LANE), F32)],
        compiler_params=_params(("parallel", "arbitrary")),
        name="mlstm",
    )(z3, z3, z3, z3, z3, bias, *consts, c0, n0.reshape(nseq, 1, hw), m0p)
    return h.reshape(nseq, t_seq, hw), c_new, n_new.reshape(nseq, ML_HEADS, ML_DK), m_new[:, 0, :ML_HEADS]


def _rope(x, cos, sin_signed, width):
    half = SW_HD // 2
    lane = lax.broadcasted_iota(jnp.int32, x.shape, 1)
    first = (lane % SW_HD) < half
    partner = jnp.where(first, pltpu.roll(x, width - half, axis=1), pltpu.roll(x, half, axis=1))
    return x * cos + partner * sin_signed


def _undup(xd):
    lane = lax.broadcasted_iota(jnp.int32, xd.shape[:-1] + (LANE,), xd.ndim - 1)
    return jnp.where(lane < SW_HD, xd[..., :LANE], xd[..., LANE:])


def _swa_prompt_kernel(sink_ref, q_ref, kc_ref, vc_ref, vp_ref, cosc_ref, sinc_ref, h_ref, ko_ref, kprev_s):
    n = pl.program_id(1)
    W2 = 2 * WINDOW
    cosc, sinc = cosc_ref[...], sinc_ref[...]

    @pl.when(n == 0)
    def _():
        kprev_s[...] = jnp.zeros_like(kprev_s)

    i = lax.broadcasted_iota(jnp.int32, (WINDOW, W2), 0)
    j = lax.broadcasted_iota(jnp.int32, (WINDOW, W2), 1)
    diff = WINDOW + i - j
    valid = (diff >= 0) & (diff < WINDOW) & ((n > 0) | (j >= WINDOW))
    row = lax.broadcasted_iota(jnp.int32, (2 * W2, LANE), 0)
    lane = lax.broadcasted_iota(jnp.int32, (2 * W2, LANE), 1)
    bd = (row < W2) == (lane < SW_HD)
    ones_bd = bd.astype(BF16)
    npair = SW_GQ // 2
    rows = npair * WINDOW
    valid = jnp.concatenate([valid] * npair, axis=0)
    low = lax.broadcasted_iota(jnp.int32, (rows, LANE), 1) < SW_HD
    prow = lax.broadcasted_iota(jnp.int32, (rows, 1), 0) // WINDOW
    cos4, sin4 = jnp.concatenate([cosc] * 4, axis=1), jnp.concatenate([sinc] * 4, axis=1)
    for bi in range(q_ref.shape[0]):
        q = _rope(q_ref[bi], cos4, sin4, 4 * LANE)
        k_cur = _rope(kc_ref[bi], cos4[:, :2 * LANE], sin4[:, :2 * LANE], 2 * LANE)
        kd = jnp.concatenate([kprev_s[bi], k_cur], axis=0)
        kprev_s[bi] = k_cur
        vd = jnp.concatenate([vp_ref[bi], vc_ref[bi]], axis=0)
        for kv in range(SW_KV_HEADS):
            kblk = kd[:, kv * LANE:(kv + 1) * LANE]
            vblk = vd[:, kv * LANE:(kv + 1) * LANE]
            kbd = jnp.where(bd, jnp.concatenate([kblk, kblk], axis=0), 0.0).astype(BF16)
            vbd = jnp.where(bd, jnp.concatenate([vblk, vblk], axis=0), 0.0).astype(BF16)
            vo = jnp.concatenate([vbd, ones_bd], axis=1)
            h0 = kv * SW_GQ
            qs = jnp.concatenate([q[:, (h0 + 2 * p) * SW_HD:(h0 + 2 * p + 2) * SW_HD] for p in range(npair)],
                                 axis=0).astype(BF16)
            s = lax.dot_general(qs, kbd, NT, preferred_element_type=F32) * (SW_HD ** -0.5)
            es, sink_terms = [], []
            for hh in range(2):
                sh = jnp.where(valid, s[:, hh * W2:(hh + 1) * W2], -jnp.inf)
                sink = sink_ref[h0 + hh]
                for p in range(1, npair):
                    sink = jnp.where(prow == p, sink_ref[h0 + 2 * p + hh], sink)
                m = jnp.maximum(jnp.max(sh, axis=1, keepdims=True), sink)
                es.append(jnp.exp(sh - m).astype(BF16))
                sink_terms.append(jnp.exp(sink - m))
            r = jnp.dot(jnp.concatenate(es, axis=1), vo, preferred_element_type=F32)
            o = r[:, :LANE] / (r[:, LANE:] + jnp.where(low, sink_terms[0], sink_terms[1]))
            for p in range(npair):
                h_ref[bi, :, (h0 + 2 * p) * SW_HD:(h0 + 2 * p + 2) * SW_HD] = (
                    o[p * WINDOW:(p + 1) * WINDOW].astype(h_ref.dtype))

        @pl.when(n == pl.num_programs(1) - 1)
        def _(bi=bi, k_cur=k_cur):
            ko_ref[bi] = _undup(k_cur)


def _swa_prompt(z3, cos, sin_signed, sinks, Bb):
    b, t, _ = z3.shape
    nb = t // WINDOW
    qw = SW_HEADS * SW_HD
    kw = SW_KV_HEADS * SW_HD
    dw = 2 * kw
    qblk, kblk, vblk = EVEN_QA // qw, EVEN_KD // dw, EVEN_VD // dw
    cur = lambda j: (lambda bi, ni: (bi, ni, j))
    prev = lambda j: (lambda bi, ni: (bi, jnp.maximum(ni - 1, 0), j))
    return pl.pallas_call(
        _swa_prompt_kernel,
        grid=(b // Bb, nb),
        in_specs=[pl.BlockSpec(memory_space=pltpu.SMEM),
                  pl.BlockSpec((Bb, WINDOW, qw), cur(qblk)),
                  pl.BlockSpec((Bb, WINDOW, dw), cur(kblk)),
                  pl.BlockSpec((Bb, WINDOW, dw), cur(vblk)),
                  pl.BlockSpec((Bb, WINDOW, dw), prev(vblk)),
                  pl.BlockSpec((WINDOW, LANE), lambda bi, ni: (ni, 0)),
                  pl.BlockSpec((WINDOW, LANE), lambda bi, ni: (ni, 0))],
        out_specs=[pl.BlockSpec((Bb, WINDOW, qw), cur(0)),
                   pl.BlockSpec((Bb, WINDOW, kw), lambda bi, ni: (bi, 0, 0))],
        out_shape=[jax.ShapeDtypeStruct((b, t, qw), BF16),
                   jax.ShapeDtypeStruct((b, WINDOW, kw), F32)],
        scratch_shapes=[pltpu.VMEM((Bb, WINDOW, dw), F32)],
        compiler_params=_params(("parallel", "arbitrary")),
        name="swa_prompt",
    )(sinks, z3, z3, z3, z3, cos, sin_signed)


def _swa_sample_kernel(q_ref, k_ref, v_ref, kbuf_ref, vbuf_ref, cos_ref, sin_ref, sink_ref,
                       h_ref, ko_ref, vo_ref, *, T, Bb):
    qw = SW_HEADS * SW_HD
    kw = SW_KV_HEADS * SW_HD
    cos = cos_ref[...]
    sin = sin_ref[...]
    q = _rope(q_ref[...].reshape(Bb * T, qw), jnp.concatenate([cos] * (qw // LANE), axis=1),
              jnp.concatenate([sin] * (qw // LANE), axis=1), qw).reshape(Bb, T, qw)
    k_new = _rope(_undup(k_ref[...]).reshape(Bb * T, kw), cos, sin, kw).reshape(Bb, T, kw)
    kk = jnp.concatenate([kbuf_ref[...], k_new], axis=1)
    vv = jnp.concatenate([vbuf_ref[...], _undup(v_ref[...])], axis=1)
    ko_ref[...] = kk[:, T:, :]
    vo_ref[...] = vv[:, T:, :]
    rows = SW_GQ * T
    i = lax.broadcasted_iota(jnp.int32, (rows, WINDOW + T), 0) % T
    j = lax.broadcasted_iota(jnp.int32, (rows, WINDOW + T), 1)
    diff = WINDOW + i - j
    valid = (diff >= 0) & (diff < WINDOW)
    for kv in range(SW_KV_HEADS):
        qs = jnp.concatenate([q[:, :, (kv * SW_GQ + g) * SW_HD:(kv * SW_GQ + g + 1) * SW_HD]
                              for g in range(SW_GQ)], axis=1).astype(BF16)
        kh = kk[:, :, kv * SW_HD:(kv + 1) * SW_HD].astype(BF16)
        vh = vv[:, :, kv * SW_HD:(kv + 1) * SW_HD].astype(BF16)
        s = jnp.einsum('bqd,bkd->bqk', qs, kh, preferred_element_type=F32) * (SW_HD ** -0.5)
        s = jnp.where(valid, s, -jnp.inf)
        sink = sink_ref[kv * rows:(kv + 1) * rows, :]
        m = jnp.maximum(jnp.max(s, axis=-1, keepdims=True), sink)
        e = jnp.exp(s - m)
        p = e / (jnp.sum(e, axis=-1, keepdims=True) + jnp.exp(sink - m))
        o = jnp.einsum('bqk,bkd->bqd', p.astype(BF16), vh, preferred_element_type=F32)
        for g in range(SW_GQ):
            hh = kv * SW_GQ + g
            h_ref[:, :, hh * SW_HD:(hh + 1) * SW_HD] = o[:, g * T:(g + 1) * T, :].astype(h_ref.dtype)


def _swa_sample(z3, kbuf, vbuf, cos, sin_signed, sinks, Bb):
    b, t, _ = z3.shape
    qw = SW_HEADS * SW_HD
    kw = SW_KV_HEADS * SW_HD
    dw = 2 * kw
    qblk, kblk, vblk = EVEN_QA // qw, EVEN_KD // dw, EVEN_VD // dw
    col = lambda j: (lambda bi: (bi, 0, j))
    const2 = lambda bi: (0, 0)
    sink_col = jnp.repeat(sinks, t).reshape(SW_HEADS * t, 1)
    return pl.pallas_call(
        functools.partial(_swa_sample_kernel, T=t, Bb=Bb),
        grid=(b // Bb,),
        in_specs=[pl.BlockSpec((Bb, t, qw), col(qblk)),
                  pl.BlockSpec((Bb, t, dw), col(kblk)),
                  pl.BlockSpec((Bb, t, dw), col(vblk)),
                  pl.BlockSpec((Bb, WINDOW, kw), col(0)),
                  pl.BlockSpec((Bb, WINDOW, kw), col(0)),
                  pl.BlockSpec((Bb * t, LANE), const2),
                  pl.BlockSpec((Bb * t, LANE), const2),
                  pl.BlockSpec((SW_HEADS * t, 1), const2)],
        out_specs=[pl.BlockSpec((Bb, t, qw), col(0)),
                   pl.BlockSpec((Bb, WINDOW, kw), col(0)),
                   pl.BlockSpec((Bb, WINDOW, kw), col(0))],
        out_shape=[jax.ShapeDtypeStruct((b, t, qw), BF16),
                   jax.ShapeDtypeStruct((b, WINDOW, kw), F32),
                   jax.ShapeDtypeStruct((b, WINDOW, kw), F32)],
        compiler_params=_params(("parallel",)),
        name="swa_sample",
    )(z3, z3, z3, kbuf, vbuf, jnp.tile(cos, (Bb, 1)), jnp.tile(sin_signed, (Bb, 1)), sink_col)


def _gla_tables(S, G):
    nlev = int(math.log2(S))
    assert 2 ** nlev == S
    L = G * S
    mstack = np.zeros((nlev * L, L), np.float32)
    masks = np.zeros((nlev + 1, L, L), np.float32)
    masks[0] = np.eye(L)
    for l in range(1, nlev + 1):
        n = 2 ** l
        for r in range(L):
            blk, pos = divmod(r, n)
            m = blk * n + n // 2 - 1
            if pos >= n // 2:
                mstack[(l - 1) * L + r, m + 1:r + 1] = 1.0
                masks[l, r, blk * n:blk * n + n // 2] = 1.0
            else:
                mstack[(l - 1) * L + r, r + 1:m + 1] = 1.0
    masks = np.tile(masks, (1, 1, GLA_HEADS))
    tri = np.kron(np.eye(G), np.tril(np.ones((S, S)))).astype(np.float32)
    return nlev, jnp.asarray(tri, BF16), jnp.asarray(mstack, BF16), jnp.asarray(masks)


def _gla_consts(L):
    kw = GLA_HEADS * GLA_DK
    vw = GLA_HEADS * GLA_DV
    hk = np.zeros((GLA_HEADS * L, kw), np.float32)
    for h in range(GLA_HEADS):
        hk[h * L:(h + 1) * L, h * GLA_DK:(h + 1) * GLA_DK] = 1.0
    hs = np.zeros((vw, kw), np.float32)
    for h in range(GLA_HEADS):
        hs[h * GLA_DV:(h + 1) * GLA_DV, h * GLA_DK:(h + 1) * GLA_DK] = 1.0
    return jnp.asarray(hk, BF16), jnp.asarray(hs)


def _gla_kernel(q_ref, k_ref, v_ref, r_ref, glr_ref, wg_ref, bg_ref, gn_ref, tri_ref, mstack_ref,
                masks_ref, hk_ref, hs_ref, s0_ref,
                o_ref, so_ref, st_s, *, L, nlev, Bb, nsub, G):
    c = pl.program_id(1)
    H, DK, DV = GLA_HEADS, GLA_DK, GLA_DV

    @pl.when(c == 0)
    def _():
        for bi in range(Bb * G):
            for h in range(H):
                pieces = ([jnp.zeros((h * DK, DV), F32)] if h else []) + [s0_ref[bi, h]]
                pieces += [jnp.zeros(((H - 1 - h) * DK, DV), F32)] if h < H - 1 else []
                padded = jnp.concatenate(pieces, axis=0)
                st_s[bi, h * DV:(h + 1) * DV, :] = padded.T

    for sub in range(nsub):
        for bi in range(Bb):
            _gla_one(bi, sub * L, q_ref, k_ref, v_ref, r_ref, glr_ref, wg_ref, bg_ref, gn_ref, tri_ref, mstack_ref,
                     masks_ref, hk_ref, hs_ref, o_ref, st_s, L, nlev, G)

    @pl.when(c == pl.num_programs(1) - 1)
    def _():
        for bi in range(Bb * G):
            for h in range(H):
                blk = st_s[bi, h * DV:(h + 1) * DV, :]
                so_ref[bi, h] = blk.T[h * DK:(h + 1) * DK]


def _gla_one(bi, r0, q_ref, k_ref, v_ref, r_ref, glr_ref, wg_ref, bg_ref, gn_ref, tri_ref, mstack_ref,
             masks_ref, hk_ref, hs_ref, o_ref, st_s, L, nlev, G):
    H, DK, DV = GLA_HEADS, GLA_DK, GLA_DV
    packed_rows = L % (2 * SUBLANE) == 0
    rw = pl.ds(r0, L)
    q = q_ref[bi, rw] * (DK ** -0.5)
    k = k_ref[bi, rw]
    kwid = H * DK
    g_hi, g_lo = _pieces(glr_ref[bi, rw], 2)
    pre2 = jnp.dot(g_hi.astype(BF16), wg_ref[...], preferred_element_type=F32)
    pre = (pre2[:, :kwid] + pre2[:, kwid:] + jnp.dot(g_lo.astype(BF16), wg_ref[:, :kwid], preferred_element_type=F32)
           + bg_ref[...])
    lg = jax.nn.log_sigmoid(pre) * (1.0 / GLA_TAU)
    Gc = _dot_sel(tri_ref[...], lg)
    E = jnp.exp(_dot_sel(mstack_ref[...], lg, pieces=2))
    hk = hk_ref[...]
    tok = lax.broadcasted_iota(jnp.int32, (L, kwid), 0)

    att = None
    for l in range(nlev + 1):
        if l == 0:
            ql, kl = q, k
        else:
            e_l = E[(l - 1) * L:l * L]
            upper = ((tok >> (l - 1)) & 1) == 1
            ql = jnp.where(upper, q * e_l, 0.0)
            kl = jnp.where(upper, 0.0, k * e_l)
        if packed_rows:
            kst = jnp.concatenate([kl.astype(BF16)] * H, axis=0) * hk
        else:
            kst = (jnp.concatenate([kl] * H, axis=0) * hk.astype(F32)).astype(BF16)
        part = lax.dot_general(ql.astype(BF16), kst, NT, preferred_element_type=F32)
        part = jnp.where(masks_ref[l] > 0.5, part, 0.0)
        att = part if att is None else att + part

    v = v_ref[bi, rw]
    vsrc = v.astype(BF16) if packed_rows else v
    zero = jnp.zeros((L, DV), vsrc.dtype)
    vbd = jnp.concatenate(
        [jnp.concatenate([zero] * h + [vsrc[:, h * DV:(h + 1) * DV]] + [zero] * (H - 1 - h), axis=1)
         for h in range(H)], axis=0).astype(BF16)
    S = L // G
    seg = lambda a, g: a[g * S:(g + 1) * S]
    qg = q * jnp.exp(Gc)
    o = jnp.dot(att.astype(BF16), vbd, preferred_element_type=F32) + jnp.concatenate(
        [lax.dot_general(seg(qg, g).astype(BF16), st_s[bi * G + g].astype(BF16), NT, preferred_element_type=F32)
         for g in range(G)], axis=0)
    for h in range(H):
        vs = slice(h * DV, (h + 1) * DV)
        oh = o[:, vs]
        oh = oh * lax.rsqrt(jnp.mean(oh * oh, axis=-1, keepdims=True) + EPS)
        oh = oh * gn_ref[:, vs] * jax.nn.silu(r_ref[bi, rw, vs])
        o_ref[bi, rw, vs] = oh.astype(o_ref.dtype)
    for g in range(G):
        g_seg = seg(Gc, g)
        g_end = g_seg[S - 1:S, :]
        k_end = (seg(k, g) * jnp.exp(g_end - g_seg)).astype(BF16)
        upd = lax.dot_general(seg(v, g).astype(BF16), k_end, TN, preferred_element_type=F32)
        st_s[bi * G + g] = st_s[bi * G + g] * jnp.exp(g_end) + jnp.where(hs_ref[...] > 0.5, upd, 0.0)


def _gla(z3, wg, bg, gn, s0, S, G, Bb, nsub):
    nseq, t_seq, width = z3.shape
    if G > 1:
        assert t_seq == S and nsub == 1
        z3 = z3.reshape(nseq // G, G * S, width)
    b, t, _ = z3.shape
    L = G * S
    nc = t // (L * nsub)
    nlev, tri, mstack, masks = _gla_tables(S, G)
    hk, hs = _gla_consts(L)
    qk_w = GLA_HEADS * GLA_DK
    v_w = GLA_HEADS * GLA_DV
    col = lambda j: (lambda bi, ci: (bi, ci, j))
    const2 = lambda bi, ci: (0, 0)
    const3 = lambda bi, ci: (0, 0, 0)
    o, s_new = pl.pallas_call(
        functools.partial(_gla_kernel, L=L, nlev=nlev, Bb=Bb, nsub=nsub, G=G),
        grid=(b // Bb, nc),
        in_specs=[pl.BlockSpec((Bb, nsub * L, qk_w), col(0)),
                  pl.BlockSpec((Bb, nsub * L, qk_w), col(ODD_K // qk_w)),
                  pl.BlockSpec((Bb, nsub * L, v_w), col(ODD_V // v_w)),
                  pl.BlockSpec((Bb, nsub * L, v_w), col(ODD_R // v_w)),
                  pl.BlockSpec((Bb, nsub * L, LANE), col(ODD_GATE // LANE)),
                  pl.BlockSpec((LANE, 2 * qk_w), const2),
                  pl.BlockSpec((1, qk_w), const2),
                  pl.BlockSpec((1, v_w), const2),
                  pl.BlockSpec((L, L), const2),
                  pl.BlockSpec(mstack.shape, const2),
                  pl.BlockSpec(masks.shape, const3),
                  pl.BlockSpec(hk.shape, const2),
                  pl.BlockSpec(hs.shape, const2),
                  pl.BlockSpec((Bb * G, GLA_HEADS, GLA_DK, GLA_DV), lambda bi, ci: (bi, 0, 0, 0))],
        out_specs=[pl.BlockSpec((Bb, nsub * L, v_w), col(0)),
                   pl.BlockSpec((Bb * G, GLA_HEADS, GLA_DK, GLA_DV), lambda bi, ci: (bi, 0, 0, 0))],
        out_shape=[jax.ShapeDtypeStruct((b, t, v_w), BF16),
                   jax.ShapeDtypeStruct((nseq, GLA_HEADS, GLA_DK, GLA_DV), F32)],
        scratch_shapes=[pltpu.VMEM((Bb * G, v_w, qk_w), F32)],
        compiler_params=_params(("parallel", "arbitrary")),
        name="gla",
    )(z3, z3, z3, z3, z3, wg, bg, gn, tri, mstack, masks, hk, hs, s0)
    return o.reshape(nseq, t_seq, v_w), s_new


def _s5_kernel(u_ref, bbre_ref, bbim_ref, cre_ref, cim_ref, d_ref, wglu_ref, bglu_ref, are_ref, aim_ref,
               x0re_ref, x0im_ref,
               o_ref, xore_ref, xoim_ref,
               xr_s, xi_s, cr_s, ci_s, pre_ref, pim_ref, *, rows, sequential):
    c = pl.program_id(1)
    lead = rows // SUBLANE if sequential else SUBLANE
    inner = rows // lead
    cw = S5_LANES // 4 if inner == SUBLANE else LANE
    u = jnp.swapaxes(u_ref[0].reshape(inner, lead, S5_WIDTH), 0, 1).reshape(rows, S5_WIDTH)
    ub = u.astype(BF16)

    if sequential:
        @pl.when(c == 0)
        def _():
            pre_ref[0] = are_ref[...]
            pim_ref[0] = aim_ref[...]
            n = 1
            while n < lead:
                nr, ni = pre_ref[n - 1], pim_ref[n - 1]
                sr, si = pre_ref[0:n], pim_ref[0:n]
                pre_ref[n:2 * n] = sr * nr - si * ni
                pim_ref[n:2 * n] = sr * ni + si * nr
                n *= 2

    hw, hl = S5_WIDTH // 2, S5_LANES // 2
    for half in range(2):
        uh = ub[:, half * hw:(half + 1) * hw]
        ls = slice(half * hl, (half + 1) * hl)
        xr_s[:, :, ls] = jnp.dot(uh, bbre_ref[half], preferred_element_type=F32).reshape(lead, inner, hl)
        xi_s[:, :, ls] = jnp.dot(uh, bbim_ref[half], preferred_element_type=F32).reshape(lead, inner, hl)

    for c0 in range(0, S5_LANES, cw):
        ar = jnp.broadcast_to(are_ref[:, c0:c0 + cw], (inner, cw))
        ai = jnp.broadcast_to(aim_ref[:, c0:c0 + cw], (inner, cw))

        def body(j, carry, c0=c0, ar=ar, ai=ai):
            xr, xi = carry
            nr = ar * xr - ai * xi + xr_s[j, :, c0:c0 + cw]
            ni = ar * xi + ai * xr + xi_s[j, :, c0:c0 + cw]
            xr_s[j, :, c0:c0 + cw] = nr
            xi_s[j, :, c0:c0 + cw] = ni
            return nr, ni

        if sequential:
            init = (jnp.zeros((inner, cw), F32), jnp.zeros((inner, cw), F32))
        else:
            init = (x0re_ref[0, :, c0:c0 + cw], x0im_ref[0, :, c0:c0 + cw])
        lax.fori_loop(0, lead, body, init, unroll=True)

    if sequential:
        @pl.when(c == 0)
        def _():
            cr_s[...] = x0re_ref[0]
            ci_s[...] = x0im_ref[0]

        end_r = xr_s[lead - 1]
        end_i = xi_s[lead - 1]
        anr = pre_ref[lead - 1]
        ani = pim_ref[lead - 1]
        ent_r, ent_i = [cr_s[...]], [ci_s[...]]
        for s in range(SUBLANE):
            pr, pi = ent_r[-1], ent_i[-1]
            ent_r.append(end_r[s:s + 1] + anr * pr - ani * pi)
            ent_i.append(end_i[s:s + 1] + anr * pi + ani * pr)
        cr_s[...] = ent_r[SUBLANE]
        ci_s[...] = ent_i[SUBLANE]
        er = jnp.concatenate(ent_r[:SUBLANE], axis=0)[None]
        ei = jnp.concatenate(ent_i[:SUBLANE], axis=0)[None]
        pr3, pi3 = pre_ref[...], pim_ref[...]
        xr = xr_s[...] + pr3 * er - pi3 * ei
        xi = xi_s[...] + pr3 * ei + pi3 * er

        @pl.when(c == pl.num_programs(1) - 1)
        def _():
            xore_ref[0] = cr_s[...]
            xoim_ref[0] = ci_s[...]
    else:
        xr = xr_s[...]
        xi = xi_s[...]
        xore_ref[0] = xr[lead - 1]
        xoim_ref[0] = xi[lead - 1]

    xrb = xr.reshape(rows, S5_LANES).astype(BF16)
    xib = xi.reshape(rows, S5_LANES).astype(BF16)
    y = jnp.concatenate(
        [jnp.dot(xrb[:, h * hl:(h + 1) * hl], cre_ref[h], preferred_element_type=F32)
         - jnp.dot(xib[:, h * hl:(h + 1) * hl], cim_ref[h], preferred_element_type=F32) for h in range(2)],
        axis=1) + d_ref[...] * u
    zz = jax.nn.gelu(y)
    gate = jnp.dot(zz.astype(BF16), wglu_ref[...], preferred_element_type=F32) + bglu_ref[...]
    out = zz * jax.nn.sigmoid(gate)
    o_ref[0] = jnp.swapaxes(out.reshape(lead, inner, S5_WIDTH), 0, 1).reshape(rows, S5_WIDTH).astype(o_ref.dtype)


def _s5(z3, tabs, x0re, x0im, rows, sequential):
    b, t, _ = z3.shape
    nblk = t // rows
    groups = rows // SUBLANE
    ublk = ODD_U // S5_WIDTH
    const2 = lambda bi, ci: (0, 0)
    const3 = lambda bi, ci: (0, 0, 0)
    lead = groups if sequential else SUBLANE
    hw, hl = S5_WIDTH // 2, S5_LANES // 2
    if sequential:
        st_spec = pl.BlockSpec((1, 1, S5_LANES), lambda bi, ci: (bi, 0, 0))
        st_shape = jax.ShapeDtypeStruct((b, 1, S5_LANES), F32)
    else:
        st_spec = pl.BlockSpec((1, groups, S5_LANES), lambda bi, ci: (bi, ci, 0))
        st_shape = jax.ShapeDtypeStruct(x0re.shape, F32)
    return pl.pallas_call(
        functools.partial(_s5_kernel, rows=rows, sequential=sequential),
        grid=(b, nblk),
        in_specs=[pl.BlockSpec((1, rows, S5_WIDTH), lambda bi, ci: (bi, ci, ublk)),
                  pl.BlockSpec((2, hw, hl), const3),
                  pl.BlockSpec((2, hw, hl), const3),
                  pl.BlockSpec((2, hl, hw), const3),
                  pl.BlockSpec((2, hl, hw), const3),
                  pl.BlockSpec((1, S5_WIDTH), const2),
                  pl.BlockSpec((S5_WIDTH, S5_WIDTH), const2),
                  pl.BlockSpec((1, S5_WIDTH), const2),
                  pl.BlockSpec((1, S5_LANES), const2),
                  pl.BlockSpec((1, S5_LANES), const2),
                  st_spec, st_spec],
        out_specs=[pl.BlockSpec((1, rows, S5_WIDTH), lambda bi, ci: (bi, ci, 0)),
                   st_spec, st_spec],
        out_shape=[jax.ShapeDtypeStruct((b, t, S5_WIDTH), BF16), st_shape, st_shape],
        scratch_shapes=[pltpu.VMEM((lead, rows // lead, S5_LANES), F32),
                        pltpu.VMEM((lead, rows // lead, S5_LANES), F32),
                        pltpu.VMEM((1, S5_LANES), F32),
                        pltpu.VMEM((1, S5_LANES), F32),
                        pltpu.VMEM((lead if sequential else 1, 1, S5_LANES), F32),
                        pltpu.VMEM((lead if sequential else 1, 1, S5_LANES), F32)],
        compiler_params=_params(("parallel", "arbitrary")),
        name="s5",
    )(z3, tabs['bbre'], tabs['bbim'], tabs['cre'], tabs['cim'], tabs['d'], tabs['wglu'], tabs['bglu'],
      tabs['are'], tabs['aim'], x0re, x0im)


def _s5_tables(a_re, a_im, log_dt, b_re, b_im, c_re, c_im, d, w_glu, b_glu):
    lam_re = a_re.astype(F32)
    lam_im = a_im.astype(F32)
    dt = jnp.exp(log_dt.astype(F32))[:, None]
    mag = jnp.exp(lam_re * dt)
    ang = lam_im * dt
    ab_re = mag * jnp.cos(ang)
    ab_im = mag * jnp.sin(ang)
    den = lam_re * lam_re + lam_im * lam_im
    co_re = ((ab_re - 1.0) * lam_re + ab_im * lam_im) / den
    co_im = (ab_im * lam_re - (ab_re - 1.0) * lam_im) / den
    b_re = b_re.astype(F32)
    b_im = b_im.astype(F32)
    bb_re = co_re[..., None] * b_re - co_im[..., None] * b_im
    bb_im = co_re[..., None] * b_im + co_im[..., None] * b_re
    gh = S5_GROUPS // 2
    eye = jnp.eye(gh, dtype=F32)
    hw, hl = S5_WIDTH // 2, S5_LANES // 2
    halves = lambda a: a.reshape((2, gh) + a.shape[1:])
    blockdiag_in = lambda bb: jnp.einsum('xgph,gk->xghkp', halves(bb), eye).reshape(2, hw, hl).astype(BF16)
    blockdiag_out = lambda cc: jnp.einsum('xghp,gk->xgpkh', halves(cc.astype(F32)), eye).reshape(2, hl, hw).astype(BF16)
    return {
        'bbre': blockdiag_in(bb_re), 'bbim': blockdiag_in(bb_im),
        'cre': blockdiag_out(c_re), 'cim': blockdiag_out(c_im),
        'd': d.astype(F32).reshape(1, S5_WIDTH),
        'wglu': w_glu.astype(BF16), 'bglu': b_glu.astype(F32).reshape(1, S5_WIDTH),
        'are': ab_re.reshape(1, S5_LANES), 'aim': ab_im.reshape(1, S5_LANES),
    }


def _rmsnorm_val(x, g):
    return x * lax.rsqrt(jnp.mean(x * x, axis=-1, keepdims=True) + EPS) * g


def _ffn_kernel(r_ref, a1_ref, a2_ref, w1_ref, w2_ref, g_ref, wup_ref, cw_ref, cb_ref, wd_ref, st_ref, gf_ref,
                o_ref, so_ref, halo_s, *, tm, tf, long_seq, final_norm):
    i = pl.program_id(1)
    n = tm // SUBLANE
    d = r_ref.shape[-1]
    x = (r_ref[0] + jnp.dot(a1_ref[0], w1_ref[...], preferred_element_type=F32)
         + jnp.dot(a2_ref[0], w2_ref[...], preferred_element_type=F32))
    hn = _rmsnorm_val(x, g_ref[...])
    if long_seq:
        lead = n

        @pl.when(i == 0)
        def _():
            halo_s[...] = st_ref[0]
    else:
        lead = SUBLANE
    hnp = jnp.swapaxes(hn.reshape(tm // lead, lead, d), 0, 1).reshape(tm, d).astype(BF16)
    inner = tm // lead
    sub = lax.broadcasted_iota(jnp.int32, (SUBLANE, tf), 0)

    acc = jnp.zeros((tm, d), F32)
    for c in range(D_FF // tf):
        conv = []
        for half in range(2):
            c0 = half * D_FF + c * tf
            u = jnp.dot(hnp, wup_ref[:, c0:c0 + tf], preferred_element_type=F32).reshape(lead, inner, tf)
            if long_seq:
                um1 = jnp.where(sub == 0, halo_s[1:2, c0:c0 + tf], pltpu.roll(u[lead - 1], 1, axis=0))
                um2 = jnp.where(sub == 0, halo_s[0:1, c0:c0 + tf], pltpu.roll(u[lead - 2], 1, axis=0))
                halo_s[0:1, c0:c0 + tf] = u[lead - 2][SUBLANE - 1:SUBLANE]
                halo_s[1:2, c0:c0 + tf] = u[lead - 1][SUBLANE - 1:SUBLANE]
            else:
                um2 = st_ref[0, :, c0:c0 + tf]
                um1 = st_ref[1, :, c0:c0 + tf]
                so_ref[0, :, c0:c0 + tf] = u[lead - 2]
                so_ref[1, :, c0:c0 + tf] = u[lead - 1]
            ext = jnp.concatenate([um2[None], um1[None], u], axis=0)
            cv = cb_ref[:, c0:c0 + tf]
            for j in range(CONV_W):
                cv = cv + ext[j:j + lead] * cw_ref[j:j + 1, c0:c0 + tf]
            conv.append(cv)
        act = (jax.nn.silu(conv[1]) * conv[0]).reshape(tm, tf).astype(BF16)
        acc = acc + jnp.dot(act, wd_ref[c * tf:(c + 1) * tf, :], preferred_element_type=F32)

    out = x + jnp.swapaxes(acc.reshape(lead, inner, d), 0, 1).reshape(tm, d)
    if final_norm:
        out = _rmsnorm_val(out, gf_ref[...])
    o_ref[0] = out

    if long_seq:
        @pl.when(i == pl.num_programs(1) - 1)
        def _():
            so_ref[0] = halo_s[...]


def _ffn(res3, a1, a2, w1, w2, g, wup, cw, cb, wd, state, layer, gfinal, tm, tf, long_seq, final_norm):
    b, t, d = res3.shape
    nt = t // tm
    resident = dict(pipeline_mode=pl.Buffered(1))
    const2 = lambda bi, i: (0, 0)
    row_blk = lambda w: pl.BlockSpec((1, tm, w), lambda bi, i: (bi, i, 0))
    per_layer = lambda shape, **kw: pl.BlockSpec((None,) + shape, lambda bi, i: (layer, 0, 0), **kw)
    if long_seq:
        st_in = st_out = pl.BlockSpec((1, CONV_W - 1, 2 * D_FF), lambda bi, i: (bi, 0, 0))
        st_shape = state.shape
    else:
        nseq = tm // SUBLANE
        st_in = pl.BlockSpec((None, CONV_W - 1, nseq, 2 * D_FF), lambda bi, i: (layer, 0, i, 0))
        st_out = pl.BlockSpec((CONV_W - 1, nseq, 2 * D_FF), lambda bi, i: (0, i, 0))
        st_shape = state.shape[1:]
    return pl.pallas_call(
        functools.partial(_ffn_kernel, tm=tm, tf=tf, long_seq=long_seq, final_norm=final_norm),
        grid=(b, nt),
        in_specs=[row_blk(d), row_blk(a1.shape[-1]), row_blk(a2.shape[-1]),
                  pl.BlockSpec(w1.shape, const2, **resident),
                  pl.BlockSpec(w2.shape, const2, **resident),
                  per_layer((1, d)),
                  per_layer((d, 2 * D_FF), **resident),
                  per_layer((CONV_W, 2 * D_FF)),
                  per_layer((1, 2 * D_FF)),
                  per_layer((D_FF, d), **resident),
                  st_in,
                  pl.BlockSpec((1, d), const2)],
        out_specs=[row_blk(d), st_out],
        out_shape=[jax.ShapeDtypeStruct((b, t, d), F32), jax.ShapeDtypeStruct(st_shape, F32)],
        scratch_shapes=[pltpu.VMEM((CONV_W - 1, 2 * D_FF), F32)],
        compiler_params=_params(("parallel", "arbitrary")),
        name="ffn",
    )(res3, a1, a2, w1, w2, g.reshape(-1, 1, d), wup, cw, cb.reshape(-1, 1, 2 * D_FF), wd, state, gfinal)


def _rope_tables(pos):
    half = SW_HD // 2
    inv = ROPE_THETA ** (-jnp.arange(half, dtype=F32) / half)
    ang = pos.astype(F32)[:, None] * inv[None, :]
    cos = jnp.cos(ang)
    sin = jnp.sin(ang)
    reps = LANE // SW_HD
    cos_t = jnp.tile(jnp.concatenate([cos, cos], axis=1), (1, reps))
    sin_t = jnp.tile(jnp.concatenate([-sin, sin], axis=1), (1, reps))
    return cos_t, sin_t


def _prep_weights(W):
    P = {}
    we = W['w_in_even']
    nh = 4 * ML_HEADS * ML_DK
    head, g_m, tail = we[:, :nh], we[:, nh:nh + 2 * ML_HEADS], we[:, nh + 2 * ML_HEADS:]
    q_a, k_a, v_a = tail[:, :512], tail[:, 512:640], tail[:, 640:768]
    dup = lambda w: jnp.concatenate([w[:, :SW_HD], w[:, :SW_HD], w[:, SW_HD:], w[:, SW_HD:]], axis=1)
    pad = jnp.zeros((D_MODEL, LANE - 2 * ML_HEADS), we.dtype)
    P['w_in_even'] = jnp.concatenate([head, q_a, dup(k_a), dup(v_a), g_m, pad], axis=1).astype(BF16)
    assert P['w_in_even'].shape[1] == EVEN_PAD
    P['b_gates'] = jnp.pad(W['b_mlstm_gates'].astype(F32), (0, LANE - 2 * ML_HEADS)).reshape(1, LANE)
    wo = W['w_in_odd']
    nh = 2 * GLA_HEADS * GLA_DK + 2 * GLA_HEADS * GLA_DV
    head, glr, u = wo[:, :nh], wo[:, nh:nh + GLA_RANK], wo[:, nh + GLA_RANK:]
    pad = jnp.zeros((D_MODEL, ODD_PAD - ODD_GATE - GLA_RANK), wo.dtype)
    P['w_in_odd'] = jnp.concatenate([head, u, glr, pad], axis=1).astype(BF16)
    wg = jnp.pad(W['w_gla_gate_up'].astype(F32), ((0, LANE - GLA_RANK), (0, 0)))
    wg_hi = wg.astype(BF16)
    P['w_gate_up'] = jnp.concatenate([wg_hi, (wg - wg_hi.astype(F32)).astype(BF16)], axis=1)
    P['b_gate'] = W['b_gla_gate'].astype(F32).reshape(1, -1)
    P['g_gla'] = W['g_gla_norm'].astype(F32).reshape(1, -1)
    hm = ML_HEADS * ML_DV
    P['w_out_even'] = (W['w_out_even'][:hm].astype(BF16), W['w_out_even'][hm:].astype(BF16))
    hg = GLA_HEADS * GLA_DV
    P['w_out_odd'] = (W['w_out_odd'][:hg].astype(BF16), W['w_out_odd'][hg:].astype(BF16))
    P['w_ffn_up'] = W['w_ffn_up'].astype(BF16)
    P['w_ffn_down'] = W['w_ffn_down'].astype(BF16)
    P['s5'] = _s5_tables(W['s5_a_re'], W['s5_a_im'], W['s5_log_dt'], W['s5_b_re'], W['s5_b_im'],
                         W['s5_c_re'], W['s5_c_im'], W['s5_d'], W['w_s5_glu'], W['b_s5_glu'])
    return P


def _trunk(x, pos, st, W, P, is_prompt):
    b, t, d = x.shape
    n = b * t
    L = CHUNK if t % CHUNK == 0 else t
    nsub = next((c for c in (8, 4, 2) if t % (c * L) == 0), 1)
    tm = next((c for c in (1024, 512) if n % c == 0), n)
    cos, sin_signed = _rope_tables(pos)
    gfinal = W['norm_final'].astype(F32).reshape(1, d)
    new = {}
    conv_out = []
    h = x.reshape(n, d)

    z = _norm_matmul(h, W['norm_mix'][0], P['w_in_even'], tm)
    z3 = z.reshape(b, t, EVEN_PAD)
    pack = dict(G=1, Bb=2 if b % 2 == 0 else 1, nsub=nsub) if (is_prompt or b % 16) else dict(G=8, Bb=2, nsub=1)
    h_m, new['ml_C'], new['ml_n'], m_new = _mlstm(z3, P['b_gates'], st['ml_C'], st['ml_n'], st['ml_m'], L, **pack)
    new['ml_m'] = m_new.reshape(b, ML_HEADS)
    if is_prompt:
        h_a, kb = _swa_prompt(z3, cos, sin_signed, W['sw_sinks'].astype(F32), Bb=2 if b % 2 == 0 else 1)
        vb = jnp.concatenate([z3[:, t - WINDOW:, EVEN_VD:EVEN_VD + SW_HD],
                              z3[:, t - WINDOW:, EVEN_VD + LANE:EVEN_VD + LANE + SW_HD]], axis=-1)
    else:
        h_a, kb, vb = _swa_sample(z3, st['kbuf'].reshape(b, WINDOW, -1), st['vbuf'].reshape(b, WINDOW, -1),
                                  cos, sin_signed, W['sw_sinks'].astype(F32), Bb=8)
    new['kbuf'] = kb.reshape(b, WINDOW, SW_KV_HEADS, SW_HD)
    new['vbuf'] = vb.reshape(b, WINDOW, SW_KV_HEADS, SW_HD)
    h, cb = _ffn_layer(h, h_m, h_a, P['w_out_even'], 0, st, W, P, b, t, gfinal, is_prompt, final_norm=False)
    conv_out.append(cb)

    z = _norm_matmul(h, W['norm_mix'][1], P['w_in_odd'], tm)
    z3 = z.reshape(b, t, ODD_PAD)
    o_g, new['gla'] = _gla(z3, P['w_gate_up'], P['b_gate'], P['g_gla'], st['gla'], L, **pack)
    x0re = st['s5_re'].astype(F32).reshape(b, 1, S5_LANES)
    x0im = st['s5_im'].astype(F32).reshape(b, 1, S5_LANES)
    if is_prompt:
        o_s, xre, xim = _s5(z3, P['s5'], x0re, x0im, rows=next((c for c in (512, 256) if t % c == 0), t),
                            sequential=True)
    else:
        o_s, xre, xim = _s5(z3.reshape(1, n, ODD_PAD), P['s5'], x0re.reshape(1, b, S5_LANES),
                            x0im.reshape(1, b, S5_LANES), rows=256 if n % 256 == 0 else n, sequential=False)
    new['s5_re'] = xre.reshape(b, S5_GROUPS, S5_STATE)
    new['s5_im'] = xim.reshape(b, S5_GROUPS, S5_STATE)
    h, cb = _ffn_layer(h, o_g, o_s, P['w_out_odd'], 1, st, W, P, b, t, gfinal, is_prompt, final_norm=True)
    conv_out.append(cb)
    new['conv'] = jnp.stack(conv_out)
    return h.reshape(b, t, d), new


def _ffn_layer(h, a1, a2, w_out, layer, st, W, P, b, t, gfinal, is_prompt, final_norm):
    n, d = h.shape
    rows = t if is_prompt else n
    tm = next((c for c in (512, 256) if rows % c == 0), rows)
    args = (*w_out, W['norm_ffn'], P['w_ffn_up'], W['ffn_conv_w'].astype(F32), W['ffn_conv_b'].astype(F32),
            P['w_ffn_down'])
    if is_prompt:
        shape3 = lambda a: a.reshape(b, t, -1)
        out, new_state = _ffn(shape3(h), shape3(a1), shape3(a2), *args, st['conv'][layer], layer, gfinal, tm,
                              FFN_TF, True, final_norm)
    else:
        assert t == SUBLANE
        shape3 = lambda a: a.reshape(1, n, -1)
        out, new_state = _ffn(shape3(h), shape3(a1), shape3(a2), *args, st['conv_rows'], layer, gfinal, tm, FFN_TF,
                              False, final_norm)
        new_state = jnp.swapaxes(new_state, 0, 1)
    return out.reshape(n, d), new_state


def kernel(x_prompt, x_sample, state_mlstm_C, state_mlstm_n, state_mlstm_m, cache_swa_k, cache_swa_v,
           state_gla, state_s5_re, state_s5_im, state_ffn_conv, norm_mix, norm_ffn, norm_final,
           w_in_even, b_mlstm_gates, sw_sinks, w_out_even, w_in_odd, w_gla_gate_up, b_gla_gate, g_gla_norm,
           s5_a_re, s5_a_im, s5_log_dt, s5_b_re, s5_b_im, s5_c_re, s5_c_im, s5_d, w_s5_glu, b_s5_glu,
           w_out_odd, w_ffn_up, ffn_conv_w, ffn_conv_b, w_ffn_down):
    W = {'norm_mix': norm_mix.astype(F32), 'norm_ffn': norm_ffn.astype(F32), 'norm_final': norm_final,
         'w_in_even': w_in_even, 'b_mlstm_gates': b_mlstm_gates, 'sw_sinks': sw_sinks, 'w_out_even': w_out_even,
         'w_in_odd': w_in_odd, 'w_gla_gate_up': w_gla_gate_up, 'b_gla_gate': b_gla_gate, 'g_gla_norm': g_gla_norm,
         's5_a_re': s5_a_re, 's5_a_im': s5_a_im, 's5_log_dt': s5_log_dt, 's5_b_re': s5_b_re, 's5_b_im': s5_b_im,
         's5_c_re': s5_c_re, 's5_c_im': s5_c_im, 's5_d': s5_d, 'w_s5_glu': w_s5_glu, 'b_s5_glu': b_s5_glu,
         'w_out_odd': w_out_odd, 'w_ffn_up': w_ffn_up, 'ffn_conv_w': ffn_conv_w, 'ffn_conv_b': ffn_conv_b,
         'w_ffn_down': w_ffn_down}
    P = _prep_weights(W)
    bp, tp = x_prompt.shape[:2]
    st_prompt = {'ml_C': jnp.zeros((bp, ML_HEADS, ML_DK, ML_DV), F32),
                 'ml_n': jnp.zeros((bp, ML_HEADS, ML_DK), F32),
                 'ml_m': jnp.zeros((bp, ML_HEADS), F32),
                 'gla': jnp.zeros((bp, GLA_HEADS, GLA_DK, GLA_DV), F32),
                 's5_re': jnp.zeros((bp, S5_GROUPS, S5_STATE), F32),
                 's5_im': jnp.zeros((bp, S5_GROUPS, S5_STATE), F32),
                 'conv': jnp.zeros((2, bp, CONV_W - 1, 2 * D_FF), F32)}
    st_sample = {'ml_C': state_mlstm_C.astype(F32), 'ml_n': state_mlstm_n.astype(F32),
                 'ml_m': state_mlstm_m.astype(F32), 'kbuf': cache_swa_k.astype(F32),
                 'vbuf': cache_swa_v.astype(F32), 'gla': state_gla.astype(F32),
                 's5_re': state_s5_re, 's5_im': state_s5_im,
                 'conv_rows': jnp.swapaxes(state_ffn_conv.astype(F32), 1, 2)}
    y_p, np_ = _trunk(x_prompt.astype(F32), jnp.arange(tp), st_prompt, W, P, True)
    y_s, ns_ = _trunk(x_sample.astype(F32), PAST_LEN + jnp.arange(x_sample.shape[1]), st_sample, W, P, False)
    return (y_p, y_s,
            np_['ml_C'], ns_['ml_C'], np_['ml_n'], ns_['ml_n'], np_['ml_m'], ns_['ml_m'],
            np_['kbuf'], ns_['kbuf'], np_['vbuf'], ns_['vbuf'], np_['gla'], ns_['gla'],
            np_['s5_re'], ns_['s5_re'], np_['s5_im'], ns_['s5_im'], np_['conv'], ns_['conv'])
```

```python
import functools
import math

import numpy as np
import jax
import jax.numpy as jnp
from jax import lax
from jax.experimental import pallas as pl
from jax.experimental.pallas import tpu as pltpu

F32 = jnp.float32
BF16 = jnp.bfloat16
NT = (((1,), (1,)), ((), ()))
TN = (((0,), (0,)), ((), ()))

D_MODEL = 1024
PAST_LEN = 16384
ML_HEADS, ML_DK, ML_DV = 4, 128, 128
SW_HEADS, SW_KV_HEADS, SW_HD, WINDOW = 8, 2, 64, 128
SW_GQ = SW_HEADS // SW_KV_HEADS
ROPE_THETA = 10000.0
GLA_HEADS, GLA_DK, GLA_DV, GLA_RANK, GLA_TAU = 4, 64, 128, 16, 16.0
S5_WIDTH, S5_GROUP, S5_STATE = 512, 16, 64
S5_GROUPS = S5_WIDTH // S5_GROUP
S5_LANES = S5_GROUPS * S5_STATE
D_FF = 2816
FFN_TF = 2816
CONV_W = 3
CHUNK = 64
EPS = 1e-6

LANE = 128
SUBLANE = 8
VMEM_LIMIT = 56 * 1024 * 1024

EVEN_QA, EVEN_KD, EVEN_VD, EVEN_PAD = 2048, 2560, 2816, 3200
ODD_K, ODD_V, ODD_R, ODD_U, ODD_GATE, ODD_PAD = 256, 512, 1024, 1536, 2048, 2176


def _params(sem):
    return pltpu.CompilerParams(dimension_semantics=sem, vmem_limit_bytes=VMEM_LIMIT)


def _norm_matmul_kernel(x_ref, g_ref, w_ref, o_ref):
    x = x_ref[...]
    ms = jnp.mean(x * x, axis=-1, keepdims=True)
    hn = (x * lax.rsqrt(ms + EPS) * g_ref[...]).astype(BF16)
    o_ref[...] = jnp.dot(hn, w_ref[...], preferred_element_type=F32)


def _norm_matmul(x, g, w, tm):
    n, d = x.shape
    c = w.shape[1]
    return pl.pallas_call(
        _norm_matmul_kernel,
        grid=(n // tm,),
        in_specs=[pl.BlockSpec((tm, d), lambda i: (i, 0)),
                  pl.BlockSpec((1, d), lambda i: (0, 0)),
                  pl.BlockSpec((d, c), lambda i: (0, 0))],
        out_specs=pl.BlockSpec((tm, c), lambda i: (i, 0)),
        out_shape=jax.ShapeDtypeStruct((n, c), F32),
        compiler_params=_params(("parallel",)),
        name="norm_matmul",
    )(x, g.reshape(1, d), w)


def _pieces(x, n):
    out, r = [], x
    for _ in range(n):
        p = r.astype(BF16).astype(F32)
        out.append(p)
        r = r - p
    return out


def _dot_sel(m_bf16, x, dims=None, pieces=3):
    ps = _pieces(x, pieces)
    axis = 1 if dims is None else 0
    w = x.shape[axis]
    if w % LANE:
        f = (lambda p: jnp.dot(m_bf16, p, preferred_element_type=F32)) if dims is None else (
            lambda p: lax.dot_general(m_bf16, p, dims, preferred_element_type=F32))
        return sum(f(p.astype(BF16)) for p in ps)
    stacked = jnp.concatenate(ps, axis=axis).astype(BF16)
    if dims is None:
        r = jnp.dot(m_bf16, stacked, preferred_element_type=F32)
    else:
        r = lax.dot_general(m_bf16, stacked, dims, preferred_element_type=F32)
    return sum(r[:, i * w:(i + 1) * w] for i in range(pieces))


def _sel_right(x, e_bf16, pieces=3):
    rows = x.shape[0]
    r = jnp.dot(jnp.concatenate(_pieces(x, pieces), axis=0).astype(BF16), e_bf16, preferred_element_type=F32)
    return sum(r[i * rows:(i + 1) * rows] for i in range(pieces))


def _mlstm_kernel(q_ref, k_ref, v_ref, og_ref, gz_ref, bias_ref, tri_ref, ones_ref, expc_ref, expv_ref, eyet_ref,
                  seg_ref, segv_ref, hmk_ref, c0_ref, n0_ref, m0_ref,
                  h_ref, co_ref, no_ref, mo_ref,
                  c_s, n_s, m_s, *, L, Bb, nsub, G):
    c = pl.program_id(1)

    @pl.when(c == 0)
    def _():
        c_s[...] = c0_ref[...]
        n_s[...] = n0_ref[...]
        m_s[...] = m0_ref[...]

    for sub in range(nsub):
        for bi in range(Bb):
            _mlstm_one(bi, sub * L, q_ref, k_ref, v_ref, og_ref, gz_ref, bias_ref, tri_ref, ones_ref, expc_ref,
                       expv_ref, eyet_ref, seg_ref, segv_ref, hmk_ref, h_ref, c_s, n_s, m_s, L, G)

    @pl.when(c == pl.num_programs(1) - 1)
    def _():
        co_ref[...] = c_s[...]
        no_ref[...] = n_s[...]
        mo_ref[...] = m_s[...]


def _mlstm_one(bi, r0, q_ref, k_ref, v_ref, og_ref, gz_ref, bias_ref, tri_ref, ones_ref, expc_ref, expv_ref,
               eyet_ref, seg_ref, segv_ref, hmk_ref, h_ref, c_s, n_s, m_s, L, G):
    H, DK = ML_HEADS, ML_DK
    HL = H * L
    S = L // G
    seg = lambda a, g: a[g * S:(g + 1) * S]
    per_row = lambda rows: rows[0] if G == 1 else jnp.concatenate(
        [jnp.broadcast_to(r, (S, r.shape[1])) for r in rows], axis=0)
    lane = lax.broadcasted_iota(jnp.int32, (L, LANE), 1)
    rowi = lax.broadcasted_iota(jnp.int32, (L, LANE), 0) % S
    keep = lane < H
    heads_only = lambda x: jnp.where(keep[:x.shape[0]], x, 0.0)

    rw = pl.ds(r0, L)
    gates = gz_ref[bi, rw] + bias_ref[...]
    lf = jax.nn.log_sigmoid(gates)
    b = pltpu.roll(_dot_sel(tri_ref[...], lf), LANE - H, axis=1)
    vv = gates - b
    cm = vv
    sh = 1
    while sh < S:
        cm = jnp.maximum(cm, jnp.where(rowi >= sh, pltpu.roll(cm, sh, axis=0), -jnp.inf))
        sh *= 2
    m_prev_g = [m_s[bi * G + g] for g in range(G)]
    m_prev = per_row(m_prev_g)
    mt = b + jnp.maximum(m_prev, cm)
    a_inter = jnp.exp(b + m_prev - mt)

    expv = expv_ref[...]
    m_new_g = [seg(mt, g)[S - 1:S, :] for g in range(G)]
    b_end_g = [seg(b, g)[S - 1:S, :] for g in range(G)]
    w_end = heads_only(jnp.exp(per_row(b_end_g) - b + gates - per_row(m_new_g)))
    decay_g = [heads_only(jnp.exp(b_end_g[g] + m_prev_g[g] - m_new_g[g])) for g in range(G)]
    decay_rows = jnp.concatenate(decay_g + [jnp.zeros((-G % SUBLANE, LANE), F32)] * (G % SUBLANE != 0), axis=0)
    spread_k = _sel_right(jnp.concatenate([heads_only(b - mt), heads_only(vv)], axis=0), expc_ref[...])
    uc = spread_k[:L]
    vr = _dot_sel(ones_ref[...], spread_k[L:] * eyet_ref[...])
    spread_v = _sel_right(jnp.concatenate([heads_only(a_inter), w_end, decay_rows], axis=0), expv, pieces=2)
    ac, wc, dcs = spread_v[:L], spread_v[L:2 * L], spread_v[2 * L:2 * L + G]
    ti = lax.broadcasted_iota(jnp.int32, (L, HL), 0)
    si = lax.broadcasted_iota(jnp.int32, (L, HL), 1) % L
    causal = (si <= ti) & (si // S == ti // S)
    w = jnp.exp(jnp.where(causal, uc + vr, -jnp.inf))

    q = q_ref[bi, rw]
    ks = k_ref[bi, rw] * (DK ** -0.5)
    v = v_ref[bi, rw]
    qb = q.astype(BF16)
    if L % (2 * SUBLANE) == 0:
        kst = jnp.concatenate([ks.astype(BF16)] * H, axis=0) * hmk_ref[...]
        vsrc = v.astype(BF16)
    else:
        kst = (jnp.concatenate([ks] * H, axis=0) * hmk_ref[...].astype(F32)).astype(BF16)
        vsrc = v
    zero = jnp.zeros((L, DK), vsrc.dtype)
    vbd = jnp.concatenate(
        [jnp.concatenate([zero] * h + [vsrc[:, h * DK:(h + 1) * DK]] + [zero] * (H - 1 - h), axis=1)
         for h in range(H)], axis=0).astype(BF16)
    s = lax.dot_general(qb, kst, NT, preferred_element_type=F32) * w
    qc = jnp.concatenate(
        [jnp.concatenate([jnp.dot(seg(q, g)[:, h * DK:(h + 1) * DK].astype(BF16), c_s[bi * G + g, h].astype(BF16),
                                  preferred_element_type=F32) for h in range(H)], axis=1)
         for g in range(G)], axis=0)
    num = jnp.dot(s.astype(BF16), vbd, preferred_element_type=F32) + ac * qc
    n_rows = per_row([n_s[bi * G + g] for g in range(G)])
    den = (_sel_right(s, seg_ref[...], pieces=2)
           + a_inter * _sel_right(q * n_rows, segv_ref[...], pieces=2))
    rden = heads_only(1.0 / jnp.maximum(jnp.abs(den), jnp.exp(-mt)))
    hh = num * _sel_right(rden, expv, pieces=2) * jax.nn.sigmoid(og_ref[bi, rw])
    h_ref[bi, rw] = hh.astype(h_ref.dtype)

    kw = ks * wc
    for g in range(G):
        e = bi * G + g
        kwb = seg(kw, g).astype(BF16)
        vb = seg(v, g).astype(BF16)
        dc = dcs[g:g + 1]
        for h in range(H):
            hs = slice(h * DK, (h + 1) * DK)
            c_s[e, h] = dc[:, hs] * c_s[e, h] + lax.dot_general(kwb[:, hs], vb[:, hs], TN, preferred_element_type=F32)
        n_s[e] = dc * n_s[e] + jnp.sum(seg(kw, g), axis=0, keepdims=True)
        m_s[e] = heads_only(m_new_g[g])


def _mlstm_consts(S, G):
    L = S * G
    H, DK = ML_HEADS, ML_DK
    expc = np.zeros((LANE, H * L), np.float32)
    expv = np.zeros((LANE, H * DK), np.float32)
    seg = np.zeros((H * L, LANE), np.float32)
    segv = np.zeros((H * DK, LANE), np.float32)
    hmk = np.zeros((H * L, H * DK), np.float32)
    for h in range(H):
        expc[h, h * L:(h + 1) * L] = 1.0
        expv[h, h * DK:(h + 1) * DK] = 1.0
        seg[h * L:(h + 1) * L, h] = 1.0
        segv[h * DK:(h + 1) * DK, h] = 1.0
        hmk[h * L:(h + 1) * L, h * DK:(h + 1) * DK] = 1.0
    eyet = np.tile(np.eye(L, dtype=np.float32), (1, H))
    bf = lambda a: jnp.asarray(a, BF16)
    tri = np.kron(np.eye(G), np.tril(np.ones((S, S)))).astype(np.float32)
    return (bf(tri), bf(np.ones((L, L), np.float32)), bf(expc), bf(expv),
            jnp.asarray(eyet), bf(seg), bf(segv), bf(hmk))


def _mlstm(z3, bias, c0, n0, m0, S, G, Bb, nsub):
    nseq, t_seq, width = z3.shape
    if G > 1:
        assert t_seq == S and nsub == 1
        z3 = z3.reshape(nseq // G, G * S, width)
    b, t, _ = z3.shape
    L = G * S
    nc = t // (L * nsub)
    hw = ML_HEADS * ML_DK
    consts = _mlstm_consts(S, G)
    gate_blk = (EVEN_PAD - LANE) // LANE
    col = lambda j: (lambda bi, ci: (bi, ci, j))
    const2 = lambda bi, ci: (0, 0)
    state3 = lambda bi, ci: (bi, 0, 0)
    m0p = jnp.pad(m0, ((0, 0), (0, LANE - ML_HEADS))).reshape(nseq, 1, LANE)
    Bs = Bb * G
    h, c_new, n_new, m_new = pl.pallas_call(
        functools.partial(_mlstm_kernel, L=L, Bb=Bb, nsub=nsub, G=G),
        grid=(b // Bb, nc),
        in_specs=[pl.BlockSpec((Bb, nsub * L, hw), col(0)),
                  pl.BlockSpec((Bb, nsub * L, hw), col(1)),
                  pl.BlockSpec((Bb, nsub * L, hw), col(2)),
                  pl.BlockSpec((Bb, nsub * L, hw), col(3)),
                  pl.BlockSpec((Bb, nsub * L, LANE), col(gate_blk)),
                  pl.BlockSpec((1, LANE), const2)]
                 + [pl.BlockSpec(a.shape, const2) for a in consts]
                 + [pl.BlockSpec((Bs, ML_HEADS, ML_DK, ML_DV), lambda bi, ci: (bi, 0, 0, 0)),
                    pl.BlockSpec((Bs, 1, hw), state3),
                    pl.BlockSpec((Bs, 1, LANE), state3)],
        out_specs=[pl.BlockSpec((Bb, nsub * L, hw), col(0)),
                   pl.BlockSpec((Bs, ML_HEADS, ML_DK, ML_DV), lambda bi, ci: (bi, 0, 0, 0)),
                   pl.BlockSpec((Bs, 1, hw), state3),
                   pl.BlockSpec((Bs, 1, LANE), state3)],
        out_shape=[jax.ShapeDtypeStruct((b, t, hw), BF16),
                   jax.ShapeDtypeStruct((nseq, ML_HEADS, ML_DK, ML_DV), F32),
                   jax.ShapeDtypeStruct((nseq, 1, hw), F32),
                   jax.ShapeDtypeStruct((nseq, 1, LANE), F32)],
        scratch_shapes=[pltpu.VMEM((Bs, ML_HEADS, ML_DK, ML_DV), F32),
                        pltpu.VMEM((Bs, 1, hw), F32),
                        pltpu.VMEM((Bs, 1, LANE), F32)],
        compiler_params=_params(("parallel", "arbitrary")),
        name="mlstm",
    )(z3, z3, z3, z3, z3, bias, *consts, c0, n0.reshape(nseq, 1, hw), m0p)
    return h.reshape(nseq, t_seq, hw), c_new, n_new.reshape(nseq, ML_HEADS, ML_DK), m_new[:, 0, :ML_HEADS]


def _rope(x, cos, sin_signed, width):
    half = SW_HD // 2
    lane = lax.broadcasted_iota(jnp.int32, x.shape, 1)
    first = (lane % SW_HD) < half
    partner = jnp.where(first, pltpu.roll(x, width - half, axis=1), pltpu.roll(x, half, axis=1))
    return x * cos + partner * sin_signed


def _undup(xd):
    lane = lax.broadcasted_iota(jnp.int32, xd.shape[:-1] + (LANE,), xd.ndim - 1)
    return jnp.where(lane < SW_HD, xd[..., :LANE], xd[..., LANE:])


def _swa_prompt_kernel(sink_ref, q_ref, kc_ref, vc_ref, vp_ref, cosc_ref, sinc_ref, h_ref, ko_ref, kprev_s):
    n = pl.program_id(1)
    W2 = 2 * WINDOW
    cosc, sinc = cosc_ref[...], sinc_ref[...]

    @pl.when(n == 0)
    def _():
        kprev_s[...] = jnp.zeros_like(kprev_s)

    i = lax.broadcasted_iota(jnp.int32, (WINDOW, W2), 0)
    j = lax.broadcasted_iota(jnp.int32, (WINDOW, W2), 1)
    diff = WINDOW + i - j
    valid = (diff >= 0) & (diff < WINDOW) & ((n > 0) | (j >= WINDOW))
    row = lax.broadcasted_iota(jnp.int32, (2 * W2, LANE), 0)
    lane = lax.broadcasted_iota(jnp.int32, (2 * W2, LANE), 1)
    bd = (row < W2) == (lane < SW_HD)
    ones_bd = bd.astype(BF16)
    npair = SW_GQ // 2
    rows = npair * WINDOW
    valid = jnp.concatenate([valid] * npair, axis=0)
    low = lax.broadcasted_iota(jnp.int32, (rows, LANE), 1) < SW_HD
    prow = lax.broadcasted_iota(jnp.int32, (rows, 1), 0) // WINDOW
    cos4, sin4 = jnp.concatenate([cosc] * 4, axis=1), jnp.concatenate([sinc] * 4, axis=1)
    for bi in range(q_ref.shape[0]):
        q = _rope(q_ref[bi], cos4, sin4, 4 * LANE)
        k_cur = _rope(kc_ref[bi], cos4[:, :2 * LANE], sin4[:, :2 * LANE], 2 * LANE)
        kd = jnp.concatenate([kprev_s[bi], k_cur], axis=0)
        kprev_s[bi] = k_cur
        vd = jnp.concatenate([vp_ref[bi], vc_ref[bi]], axis=0)
        for kv in range(SW_KV_HEADS):
            kblk = kd[:, kv * LANE:(kv + 1) * LANE]
            vblk = vd[:, kv * LANE:(kv + 1) * LANE]
            kbd = jnp.where(bd, jnp.concatenate([kblk, kblk], axis=0), 0.0).astype(BF16)
            vbd = jnp.where(bd, jnp.concatenate([vblk, vblk], axis=0), 0.0).astype(BF16)
            vo = jnp.concatenate([vbd, ones_bd], axis=1)
            h0 = kv * SW_GQ
            qs = jnp.concatenate([q[:, (h0 + 2 * p) * SW_HD:(h0 + 2 * p + 2) * SW_HD] for p in range(npair)],
                                 axis=0).astype(BF16)
            s = lax.dot_general(qs, kbd, NT, preferred_element_type=F32) * (SW_HD ** -0.5)
            es, sink_terms = [], []
            for hh in range(2):
                sh = jnp.where(valid, s[:, hh * W2:(hh + 1) * W2], -jnp.inf)
                sink = sink_ref[h0 + hh]
                for p in range(1, npair):
                    sink = jnp.where(prow == p, sink_ref[h0 + 2 * p + hh], sink)
                m = jnp.maximum(jnp.max(sh, axis=1, keepdims=True), sink)
                es.append(jnp.exp(sh - m).astype(BF16))
                sink_terms.append(jnp.exp(sink - m))
            r = jnp.dot(jnp.concatenate(es, axis=1), vo, preferred_element_type=F32)
            o = r[:, :LANE] / (r[:, LANE:] + jnp.where(low, sink_terms[0], sink_terms[1]))
            for p in range(npair):
                h_ref[bi, :, (h0 + 2 * p) * SW_HD:(h0 + 2 * p + 2) * SW_HD] = (
                    o[p * WINDOW:(p + 1) * WINDOW].astype(h_ref.dtype))

        @pl.when(n == pl.num_programs(1) - 1)
        def _(bi=bi, k_cur=k_cur):
            ko_ref[bi] = _undup(k_cur)


def _swa_prompt(z3, cos, sin_signed, sinks, Bb):
    b, t, _ = z3.shape
    nb = t // WINDOW
    qw = SW_HEADS * SW_HD
    kw = SW_KV_HEADS * SW_HD
    dw = 2 * kw
    qblk, kblk, vblk = EVEN_QA // qw, EVEN_KD // dw, EVEN_VD // dw
    cur = lambda j: (lambda bi, ni: (bi, ni, j))
    prev = lambda j: (lambda bi, ni: (bi, jnp.maximum(ni - 1, 0), j))
    return pl.pallas_call(
        _swa_prompt_kernel,
        grid=(b // Bb, nb),
        in_specs=[pl.BlockSpec(memory_space=pltpu.SMEM),
                  pl.BlockSpec((Bb, WINDOW, qw), cur(qblk)),
                  pl.BlockSpec((Bb, WINDOW, dw), cur(kblk)),
                  pl.BlockSpec((Bb, WINDOW, dw), cur(vblk)),
                  pl.BlockSpec((Bb, WINDOW, dw), prev(vblk)),
                  pl.BlockSpec((WINDOW, LANE), lambda bi, ni: (ni, 0)),
                  pl.BlockSpec((WINDOW, LANE), lambda bi, ni: (ni, 0))],
        out_specs=[pl.BlockSpec((Bb, WINDOW, qw), cur(0)),
                   pl.BlockSpec((Bb, WINDOW, kw), lambda bi, ni: (bi, 0, 0))],
        out_shape=[jax.ShapeDtypeStruct((b, t, qw), BF16),
                   jax.ShapeDtypeStruct((b, WINDOW, kw), F32)],
        scratch_shapes=[pltpu.VMEM((Bb, WINDOW, dw), F32)],
        compiler_params=_params(("parallel", "arbitrary")),
        name="swa_prompt",
    )(sinks, z3, z3, z3, z3, cos, sin_signed)


def _swa_sample_kernel(q_ref, k_ref, v_ref, kbuf_ref, vbuf_ref, cos_ref, sin_ref, sink_ref,
                       h_ref, ko_ref, vo_ref, *, T, Bb):
    qw = SW_HEADS * SW_HD
    kw = SW_KV_HEADS * SW_HD
    cos = cos_ref[...]
    sin = sin_ref[...]
    q = _rope(q_ref[...].reshape(Bb * T, qw), jnp.concatenate([cos] * (qw // LANE), axis=1),
              jnp.concatenate([sin] * (qw // LANE), axis=1), qw).reshape(Bb, T, qw)
    k_new = _rope(_undup(k_ref[...]).reshape(Bb * T, kw), cos, sin, kw).reshape(Bb, T, kw)
    kk = jnp.concatenate([kbuf_ref[...], k_new], axis=1)
    vv = jnp.concatenate([vbuf_ref[...], _undup(v_ref[...])], axis=1)
    ko_ref[...] = kk[:, T:, :]
    vo_ref[...] = vv[:, T:, :]
    rows = SW_GQ * T
    i = lax.broadcasted_iota(jnp.int32, (rows, WINDOW + T), 0) % T
    j = lax.broadcasted_iota(jnp.int32, (rows, WINDOW + T), 1)
    diff = WINDOW + i - j
    valid = (diff >= 0) & (diff < WINDOW)
    for kv in range(SW_KV_HEADS):
        qs = jnp.concatenate([q[:, :, (kv * SW_GQ + g) * SW_HD:(kv * SW_GQ + g + 1) * SW_HD]
                              for g in range(SW_GQ)], axis=1).astype(BF16)
        kh = kk[:, :, kv * SW_HD:(kv + 1) * SW_HD].astype(BF16)
        vh = vv[:, :, kv * SW_HD:(kv + 1) * SW_HD].astype(BF16)
        s = jnp.einsum('bqd,bkd->bqk', qs, kh, preferred_element_type=F32) * (SW_HD ** -0.5)
        s = jnp.where(valid, s, -jnp.inf)
        sink = sink_ref[kv * rows:(kv + 1) * rows, :]
        m = jnp.maximum(jnp.max(s, axis=-1, keepdims=True), sink)
        e = jnp.exp(s - m)
        p = e / (jnp.sum(e, axis=-1, keepdims=True) + jnp.exp(sink - m))
        o = jnp.einsum('bqk,bkd->bqd', p.astype(BF16), vh, preferred_element_type=F32)
        for g in range(SW_GQ):
            hh = kv * SW_GQ + g
            h_ref[:, :, hh * SW_HD:(hh + 1) * SW_HD] = o[:, g * T:(g + 1) * T, :].astype(h_ref.dtype)


def _swa_sample(z3, kbuf, vbuf, cos, sin_signed, sinks, Bb):
    b, t, _ = z3.shape
    qw = SW_HEADS * SW_HD
    kw = SW_KV_HEADS * SW_HD
    dw = 2 * kw
    qblk, kblk, vblk = EVEN_QA // qw, EVEN_KD // dw, EVEN_VD // dw
    col = lambda j: (lambda bi: (bi, 0, j))
    const2 = lambda bi: (0, 0)
    sink_col = jnp.repeat(sinks, t).reshape(SW_HEADS * t, 1)
    return pl.pallas_call(
        functools.partial(_swa_sample_kernel, T=t, Bb=Bb),
        grid=(b // Bb,),
        in_specs=[pl.BlockSpec((Bb, t, qw), col(qblk)),
                  pl.BlockSpec((Bb, t, dw), col(kblk)),
                  pl.BlockSpec((Bb, t, dw), col(vblk)),
                  pl.BlockSpec((Bb, WINDOW, kw), col(0)),
                  pl.BlockSpec((Bb, WINDOW, kw), col(0)),
                  pl.BlockSpec((Bb * t, LANE), const2),
                  pl.BlockSpec((Bb * t, LANE), const2),
                  pl.BlockSpec((SW_HEADS * t, 1), const2)],
        out_specs=[pl.BlockSpec((Bb, t, qw), col(0)),
                   pl.BlockSpec((Bb, WINDOW, kw), col(0)),
                   pl.BlockSpec((Bb, WINDOW, kw), col(0))],
        out_shape=[jax.ShapeDtypeStruct((b, t, qw), BF16),
                   jax.ShapeDtypeStruct((b, WINDOW, kw), F32),
                   jax.ShapeDtypeStruct((b, WINDOW, kw), F32)],
        compiler_params=_params(("parallel",)),
        name="swa_sample",
    )(z3, z3, z3, kbuf, vbuf, jnp.tile(cos, (Bb, 1)), jnp.tile(sin_signed, (Bb, 1)), sink_col)


def _gla_tables(S, G):
    nlev = int(math.log2(S))
    assert 2 ** nlev == S
    L = G * S
    mstack = np.zeros((nlev * L, L), np.float32)
    masks = np.zeros((nlev + 1, L, L), np.float32)
    masks[0] = np.eye(L)
    for l in range(1, nlev + 1):
        n = 2 ** l
        for r in range(L):
            blk, pos = divmod(r, n)
            m = blk * n + n // 2 - 1
            if pos >= n // 2:
                mstack[(l - 1) * L + r, m + 1:r + 1] = 1.0
                masks[l, r, blk * n:blk * n + n // 2] = 1.0
            else:
                mstack[(l - 1) * L + r, r + 1:m + 1] = 1.0
    masks = np.tile(masks, (1, 1, GLA_HEADS))
    tri = np.kron(np.eye(G), np.tril(np.ones((S, S)))).astype(np.float32)
    return nlev, jnp.asarray(tri, BF16), jnp.asarray(mstack, BF16), jnp.asarray(masks)


def _gla_consts(L):
    kw = GLA_HEADS * GLA_DK
    vw = GLA_HEADS * GLA_DV
    hk = np.zeros((GLA_HEADS * L, kw), np.float32)
    for h in range(GLA_HEADS):
        hk[h * L:(h + 1) * L, h * GLA_DK:(h + 1) * GLA_DK] = 1.0
    hs = np.zeros((vw, kw), np.float32)
    for h in range(GLA_HEADS):
        hs[h * GLA_DV:(h + 1) * GLA_DV, h * GLA_DK:(h + 1) * GLA_DK] = 1.0
    return jnp.asarray(hk, BF16), jnp.asarray(hs)


def _gla_kernel(q_ref, k_ref, v_ref, r_ref, glr_ref, wg_ref, bg_ref, gn_ref, tri_ref, mstack_ref,
                masks_ref, hk_ref, hs_ref, s0_ref,
                o_ref, so_ref, st_s, *, L, nlev, Bb, nsub, G):
    c = pl.program_id(1)
    H, DK, DV = GLA_HEADS, GLA_DK, GLA_DV

    @pl.when(c == 0)
    def _():
        for bi in range(Bb * G):
            for h in range(H):
                pieces = ([jnp.zeros((h * DK, DV), F32)] if h else []) + [s0_ref[bi, h]]
                pieces += [jnp.zeros(((H - 1 - h) * DK, DV), F32)] if h < H - 1 else []
                padded = jnp.concatenate(pieces, axis=0)
                st_s[bi, h * DV:(h + 1) * DV, :] = padded.T

    for sub in range(nsub):
        for bi in range(Bb):
            _gla_one(bi, sub * L, q_ref, k_ref, v_ref, r_ref, glr_ref, wg_ref, bg_ref, gn_ref, tri_ref, mstack_ref,
                     masks_ref, hk_ref, hs_ref, o_ref, st_s, L, nlev, G)

    @pl.when(c == pl.num_programs(1) - 1)
    def _():
        for bi in range(Bb * G):
            for h in range(H):
                blk = st_s[bi, h * DV:(h + 1) * DV, :]
                so_ref[bi, h] = blk.T[h * DK:(h + 1) * DK]


def _gla_one(bi, r0, q_ref, k_ref, v_ref, r_ref, glr_ref, wg_ref, bg_ref, gn_ref, tri_ref, mstack_ref,
             masks_ref, hk_ref, hs_ref, o_ref, st_s, L, nlev, G):
    H, DK, DV = GLA_HEADS, GLA_DK, GLA_DV
    packed_rows = L % (2 * SUBLANE) == 0
    rw = pl.ds(r0, L)
    q = q_ref[bi, rw] * (DK ** -0.5)
    k = k_ref[bi, rw]
    kwid = H * DK
    g_hi, g_lo = _pieces(glr_ref[bi, rw], 2)
    pre2 = jnp.dot(g_hi.astype(BF16), wg_ref[...], preferred_element_type=F32)
    pre = (pre2[:, :kwid] + pre2[:, kwid:] + jnp.dot(g_lo.astype(BF16), wg_ref[:, :kwid], preferred_element_type=F32)
           + bg_ref[...])
    lg = jax.nn.log_sigmoid(pre) * (1.0 / GLA_TAU)
    Gc = _dot_sel(tri_ref[...], lg)
    E = jnp.exp(_dot_sel(mstack_ref[...], lg, pieces=2))
    hk = hk_ref[...]
    tok = lax.broadcasted_iota(jnp.int32, (L, kwid), 0)

    att = None
    for l in range(nlev + 1):
        if l == 0:
            ql, kl = q, k
        else:
            e_l = E[(l - 1) * L:l * L]
            upper = ((tok >> (l - 1)) & 1) == 1
            ql = jnp.where(upper, q * e_l, 0.0)
            kl = jnp.where(upper, 0.0, k * e_l)
        if packed_rows:
            kst = jnp.concatenate([kl.astype(BF16)] * H, axis=0) * hk
        else:
            kst = (jnp.concatenate([kl] * H, axis=0) * hk.astype(F32)).astype(BF16)
        part = lax.dot_general(ql.astype(BF16), kst, NT, preferred_element_type=F32)
        part = jnp.where(masks_ref[l] > 0.5, part, 0.0)
        att = part if att is None else att + part

    v = v_ref[bi, rw]
    vsrc = v.astype(BF16) if packed_rows else v
    zero = jnp.zeros((L, DV), vsrc.dtype)
    vbd = jnp.concatenate(
        [jnp.concatenate([zero] * h + [vsrc[:, h * DV:(h + 1) * DV]] + [zero] * (H - 1 - h), axis=1)
         for h in range(H)], axis=0).astype(BF16)
    S = L // G
    seg = lambda a, g: a[g * S:(g + 1) * S]
    qg = q * jnp.exp(Gc)
    o = jnp.dot(att.astype(BF16), vbd, preferred_element_type=F32) + jnp.concatenate(
        [lax.dot_general(seg(qg, g).astype(BF16), st_s[bi * G + g].astype(BF16), NT, preferred_element_type=F32)
         for g in range(G)], axis=0)
    for h in range(H):
        vs = slice(h * DV, (h + 1) * DV)
        oh = o[:, vs]
        oh = oh * lax.rsqrt(jnp.mean(oh * oh, axis=-1, keepdims=True) + EPS)
        oh = oh * gn_ref[:, vs] * jax.nn.silu(r_ref[bi, rw, vs])
        o_ref[bi, rw, vs] = oh.astype(o_ref.dtype)
    for g in range(G):
        g_seg = seg(Gc, g)
        g_end = g_seg[S - 1:S, :]
        k_end = (seg(k, g) * jnp.exp(g_end - g_seg)).astype(BF16)
        upd = lax.dot_general(seg(v, g).astype(BF16), k_end, TN, preferred_element_type=F32)
        st_s[bi * G + g] = st_s[bi * G + g] * jnp.exp(g_end) + jnp.where(hs_ref[...] > 0.5, upd, 0.0)


def _gla(z3, wg, bg, gn, s0, S, G, Bb, nsub):
    nseq, t_seq, width = z3.shape
    if G > 1:
        assert t_seq == S and nsub == 1
        z3 = z3.reshape(nseq // G, G * S, width)
    b, t, _ = z3.shape
    L = G * S
    nc = t // (L * nsub)
    nlev, tri, mstack, masks = _gla_tables(S, G)
    hk, hs = _gla_consts(L)
    qk_w = GLA_HEADS * GLA_DK
    v_w = GLA_HEADS * GLA_DV
    col = lambda j: (lambda bi, ci: (bi, ci, j))
    const2 = lambda bi, ci: (0, 0)
    const3 = lambda bi, ci: (0, 0, 0)
    o, s_new = pl.pallas_call(
        functools.partial(_gla_kernel, L=L, nlev=nlev, Bb=Bb, nsub=nsub, G=G),
        grid=(b // Bb, nc),
        in_specs=[pl.BlockSpec((Bb, nsub * L, qk_w), col(0)),
                  pl.BlockSpec((Bb, nsub * L, qk_w), col(ODD_K // qk_w)),
                  pl.BlockSpec((Bb, nsub * L, v_w), col(ODD_V // v_w)),
                  pl.BlockSpec((Bb, nsub * L, v_w), col(ODD_R // v_w)),
                  pl.BlockSpec((Bb, nsub * L, LANE), col(ODD_GATE // LANE)),
                  pl.BlockSpec((LANE, 2 * qk_w), const2),
                  pl.BlockSpec((1, qk_w), const2),
                  pl.BlockSpec((1, v_w), const2),
                  pl.BlockSpec((L, L), const2),
                  pl.BlockSpec(mstack.shape, const2),
                  pl.BlockSpec(masks.shape, const3),
                  pl.BlockSpec(hk.shape, const2),
                  pl.BlockSpec(hs.shape, const2),
                  pl.BlockSpec((Bb * G, GLA_HEADS, GLA_DK, GLA_DV), lambda bi, ci: (bi, 0, 0, 0))],
        out_specs=[pl.BlockSpec((Bb, nsub * L, v_w), col(0)),
                   pl.BlockSpec((Bb * G, GLA_HEADS, GLA_DK, GLA_DV), lambda bi, ci: (bi, 0, 0, 0))],
        out_shape=[jax.ShapeDtypeStruct((b, t, v_w), BF16),
                   jax.ShapeDtypeStruct((nseq, GLA_HEADS, GLA_DK, GLA_DV), F32)],
        scratch_shapes=[pltpu.VMEM((Bb * G, v_w, qk_w), F32)],
        compiler_params=_params(("parallel", "arbitrary")),
        name="gla",
    )(z3, z3, z3, z3, z3, wg, bg, gn, tri, mstack, masks, hk, hs, s0)
    return o.reshape(nseq, t_seq, v_w), s_new


def _s5_kernel(u_ref, bbre_ref, bbim_ref, cre_ref, cim_ref, d_ref, wglu_ref, bglu_ref, are_ref, aim_ref,
               x0re_ref, x0im_ref,
               o_ref, xore_ref, xoim_ref,
               xr_s, xi_s, cr_s, ci_s, pre_ref, pim_ref, *, rows, sequential):
    c = pl.program_id(1)
    lead = rows // SUBLANE if sequential else SUBLANE
    inner = rows // lead
    cw = S5_LANES // 4 if inner == SUBLANE else LANE
    u = jnp.swapaxes(u_ref[0].reshape(inner, lead, S5_WIDTH), 0, 1).reshape(rows, S5_WIDTH)
    ub = u.astype(BF16)

    if sequential:
        @pl.when(c == 0)
        def _():
            pre_ref[0] = are_ref[...]
            pim_ref[0] = aim_ref[...]
            n = 1
            while n < lead:
                nr, ni = pre_ref[n - 1], pim_ref[n - 1]
                sr, si = pre_ref[0:n], pim_ref[0:n]
                pre_ref[n:2 * n] = sr * nr - si * ni
                pim_ref[n:2 * n] = sr * ni + si * nr
                n *= 2

    hw, hl = S5_WIDTH // 2, S5_LANES // 2
    for half in range(2):
        uh = ub[:, half * hw:(half + 1) * hw]
        ls = slice(half * hl, (half + 1) * hl)
        xr_s[:, :, ls] = jnp.dot(uh, bbre_ref[half], preferred_element_type=F32).reshape(lead, inner, hl)
        xi_s[:, :, ls] = jnp.dot(uh, bbim_ref[half], preferred_element_type=F32).reshape(lead, inner, hl)

    for c0 in range(0, S5_LANES, cw):
        ar = jnp.broadcast_to(are_ref[:, c0:c0 + cw], (inner, cw))
        ai = jnp.broadcast_to(aim_ref[:, c0:c0 + cw], (inner, cw))

        def body(j, carry, c0=c0, ar=ar, ai=ai):
            xr, xi = carry
            nr = ar * xr - ai * xi + xr_s[j, :, c0:c0 + cw]
            ni = ar * xi + ai * xr + xi_s[j, :, c0:c0 + cw]
            xr_s[j, :, c0:c0 + cw] = nr
            xi_s[j, :, c0:c0 + cw] = ni
            return nr, ni

        if sequential:
            init = (jnp.zeros((inner, cw), F32), jnp.zeros((inner, cw), F32))
        else:
            init = (x0re_ref[0, :, c0:c0 + cw], x0im_ref[0, :, c0:c0 + cw])
        lax.fori_loop(0, lead, body, init, unroll=True)

    if sequential:
        @pl.when(c == 0)
        def _():
            cr_s[...] = x0re_ref[0]
            ci_s[...] = x0im_ref[0]

        end_r = xr_s[lead - 1]
        end_i = xi_s[lead - 1]
        anr = pre_ref[lead - 1]
        ani = pim_ref[lead - 1]
        ent_r, ent_i = [cr_s[...]], [ci_s[...]]
        for s in range(SUBLANE):
            pr, pi = ent_r[-1], ent_i[-1]
            ent_r.append(end_r[s:s + 1] + anr * pr - ani * pi)
            ent_i.append(end_i[s:s + 1] + anr * pi + ani * pr)
        cr_s[...] = ent_r[SUBLANE]
        ci_s[...] = ent_i[SUBLANE]
        er = jnp.concatenate(ent_r[:SUBLANE], axis=0)[None]
        ei = jnp.concatenate(ent_i[:SUBLANE], axis=0)[None]
        pr3, pi3 = pre_ref[...], pim_ref[...]
        xr = xr_s[...] + pr3 * er - pi3 * ei
        xi = xi_s[...] + pr3 * ei + pi3 * er

        @pl.when(c == pl.num_programs(1) - 1)
        def _():
            xore_ref[0] = cr_s[...]
            xoim_ref[0] = ci_s[...]
    else:
        xr = xr_s[...]
        xi = xi_s[...]
        xore_ref[0] = xr[lead - 1]
        xoim_ref[0] = xi[lead - 1]

    xrb = xr.reshape(rows, S5_LANES).astype(BF16)
    xib = xi.reshape(rows, S5_LANES).astype(BF16)
    y = jnp.concatenate(
        [jnp.dot(xrb[:, h * hl:(h + 1) * hl], cre_ref[h], preferred_element_type=F32)
         - jnp.dot(xib[:, h * hl:(h + 1) * hl], cim_ref[h], preferred_element_type=F32) for h in range(2)],
        axis=1) + d_ref[...] * u
    zz = jax.nn.gelu(y)
    gate = jnp.dot(zz.astype(BF16), wglu_ref[...], preferred_element_type=F32) + bglu_ref[...]
    out = zz * jax.nn.sigmoid(gate)
    o_ref[0] = jnp.swapaxes(out.reshape(lead, inner, S5_WIDTH), 0, 1).reshape(rows, S5_WIDTH).astype(o_ref.dtype)


def _s5(z3, tabs, x0re, x0im, rows, sequential):
    b, t, _ = z3.shape
    nblk = t // rows
    groups = rows // SUBLANE
    ublk = ODD_U // S5_WIDTH
    const2 = lambda bi, ci: (0, 0)
    const3 = lambda bi, ci: (0, 0, 0)
    lead = groups if sequential else SUBLANE
    hw, hl = S5_WIDTH // 2, S5_LANES // 2
    if sequential:
        st_spec = pl.BlockSpec((1, 1, S5_LANES), lambda bi, ci: (bi, 0, 0))
        st_shape = jax.ShapeDtypeStruct((b, 1, S5_LANES), F32)
    else:
        st_spec = pl.BlockSpec((1, groups, S5_LANES), lambda bi, ci: (bi, ci, 0))
        st_shape = jax.ShapeDtypeStruct(x0re.shape, F32)
    return pl.pallas_call(
        functools.partial(_s5_kernel, rows=rows, sequential=sequential),
        grid=(b, nblk),
        in_specs=[pl.BlockSpec((1, rows, S5_WIDTH), lambda bi, ci: (bi, ci, ublk)),
                  pl.BlockSpec((2, hw, hl), const3),
                  pl.BlockSpec((2, hw, hl), const3),
                  pl.BlockSpec((2, hl, hw), const3),
                  pl.BlockSpec((2, hl, hw), const3),
                  pl.BlockSpec((1, S5_WIDTH), const2),
                  pl.BlockSpec((S5_WIDTH, S5_WIDTH), const2),
                  pl.BlockSpec((1, S5_WIDTH), const2),
                  pl.BlockSpec((1, S5_LANES), const2),
                  pl.BlockSpec((1, S5_LANES), const2),
                  st_spec, st_spec],
        out_specs=[pl.BlockSpec((1, rows, S5_WIDTH), lambda bi, ci: (bi, ci, 0)),
                   st_spec, st_spec],
        out_shape=[jax.ShapeDtypeStruct((b, t, S5_WIDTH), BF16), st_shape, st_shape],
        scratch_shapes=[pltpu.VMEM((lead, rows // lead, S5_LANES), F32),
                        pltpu.VMEM((lead, rows // lead, S5_LANES), F32),
                        pltpu.VMEM((1, S5_LANES), F32),
                        pltpu.VMEM((1, S5_LANES), F32),
                        pltpu.VMEM((lead if sequential else 1, 1, S5_LANES), F32),
                        pltpu.VMEM((lead if sequential else 1, 1, S5_LANES), F32)],
        compiler_params=_params(("parallel", "arbitrary")),
        name="s5",
    )(z3, tabs['bbre'], tabs['bbim'], tabs['cre'], tabs['cim'], tabs['d'], tabs['wglu'], tabs['bglu'],
      tabs['are'], tabs['aim'], x0re, x0im)


def _s5_tables(a_re, a_im, log_dt, b_re, b_im, c_re, c_im, d, w_glu, b_glu):
    lam_re = a_re.astype(F32)
    lam_im = a_im.astype(F32)
    dt = jnp.exp(log_dt.astype(F32))[:, None]
    mag = jnp.exp(lam_re * dt)
    ang = lam_im * dt
    ab_re = mag * jnp.cos(ang)
    ab_im = mag * jnp.sin(ang)
    den = lam_re * lam_re + lam_im * lam_im
    co_re = ((ab_re - 1.0) * lam_re + ab_im * lam_im) / den
    co_im = (ab_im * lam_re - (ab_re - 1.0) * lam_im) / den
    b_re = b_re.astype(F32)
    b_im = b_im.astype(F32)
    bb_re = co_re[..., None] * b_re - co_im[..., None] * b_im
    bb_im = co_re[..., None] * b_im + co_im[..., None] * b_re
    gh = S5_GROUPS // 2
    eye = jnp.eye(gh, dtype=F32)
    hw, hl = S5_WIDTH // 2, S5_LANES // 2
    halves = lambda a: a.reshape((2, gh) + a.shape[1:])
    blockdiag_in = lambda bb: jnp.einsum('xgph,gk->xghkp', halves(bb), eye).reshape(2, hw, hl).astype(BF16)
    blockdiag_out = lambda cc: jnp.einsum('xghp,gk->xgpkh', halves(cc.astype(F32)), eye).reshape(2, hl, hw).astype(BF16)
    return {
        'bbre': blockdiag_in(bb_re), 'bbim': blockdiag_in(bb_im),
        'cre': blockdiag_out(c_re), 'cim': blockdiag_out(c_im),
        'd': d.astype(F32).reshape(1, S5_WIDTH),
        'wglu': w_glu.astype(BF16), 'bglu': b_glu.astype(F32).reshape(1, S5_WIDTH),
        'are': ab_re.reshape(1, S5_LANES), 'aim': ab_im.reshape(1, S5_LANES),
    }


def _rmsnorm_val(x, g):
    return x * lax.rsqrt(jnp.mean(x * x, axis=-1, keepdims=True) + EPS) * g


def _ffn_kernel(r_ref, a1_ref, a2_ref, w1_ref, w2_ref, g_ref, wup_ref, cw_ref, cb_ref, wd_ref, st_ref, gf_ref,
                o_ref, so_ref, halo_s, *, tm, tf, long_seq, final_norm):
    i = pl.program_id(1)
    n = tm // SUBLANE
    d = r_ref.shape[-1]
    x = (r_ref[0] + jnp.dot(a1_ref[0], w1_ref[...], preferred_element_type=F32)
         + jnp.dot(a2_ref[0], w2_ref[...], preferred_element_type=F32))
    hn = _rmsnorm_val(x, g_ref[...])
    if long_seq:
        lead = n

        @pl.when(i == 0)
        def _():
            halo_s[...] = st_ref[0]
    else:
        lead = SUBLANE
    hnp = jnp.swapaxes(hn.reshape(tm // lead, lead, d), 0, 1).reshape(tm, d).astype(BF16)
    inner = tm // lead
    sub = lax.broadcasted_iota(jnp.int32, (SUBLANE, tf), 0)

    acc = jnp.zeros((tm, d), F32)
    for c in range(D_FF // tf):
        conv = []
        for half in range(2):
            c0 = half * D_FF + c * tf
            u = jnp.dot(hnp, wup_ref[:, c0:c0 + tf], preferred_element_type=F32).reshape(lead, inner, tf)
            if long_seq:
                um1 = jnp.where(sub == 0, halo_s[1:2, c0:c0 + tf], pltpu.roll(u[lead - 1], 1, axis=0))
                um2 = jnp.where(sub == 0, halo_s[0:1, c0:c0 + tf], pltpu.roll(u[lead - 2], 1, axis=0))
                halo_s[0:1, c0:c0 + tf] = u[lead - 2][SUBLANE - 1:SUBLANE]
                halo_s[1:2, c0:c0 + tf] = u[lead - 1][SUBLANE - 1:SUBLANE]
            else:
                um2 = st_ref[0, :, c0:c0 + tf]
                um1 = st_ref[1, :, c0:c0 + tf]
                so_ref[0, :, c0:c0 + tf] = u[lead - 2]
                so_ref[1, :, c0:c0 + tf] = u[lead - 1]
            ext = jnp.concatenate([um2[None], um1[None], u], axis=0)
            cv = cb_ref[:, c0:c0 + tf]
            for j in range(CONV_W):
                cv = cv + ext[j:j + lead] * cw_ref[j:j + 1, c0:c0 + tf]
            conv.append(cv)
        act = (jax.nn.silu(conv[1]) * conv[0]).reshape(tm, tf).astype(BF16)
        acc = acc + jnp.dot(act, wd_ref[c * tf:(c + 1) * tf, :], preferred_element_type=F32)

    out = x + jnp.swapaxes(acc.reshape(lead, inner, d), 0, 1).reshape(tm, d)
    if final_norm:
        out = _rmsnorm_val(out, gf_ref[...])
    o_ref[0] = out

    if long_seq:
        @pl.when(i == pl.num_programs(1) - 1)
        def _():
            so_ref[0] = halo_s[...]


def _ffn(res3, a1, a2, w1, w2, g, wup, cw, cb, wd, state, layer, gfinal, tm, tf, long_seq, final_norm):
    b, t, d = res3.shape
    nt = t // tm
    resident = dict(pipeline_mode=pl.Buffered(1))
    const2 = lambda bi, i: (0, 0)
    row_blk = lambda w: pl.BlockSpec((1, tm, w), lambda bi, i: (bi, i, 0))
    per_layer = lambda shape, **kw: pl.BlockSpec((None,) + shape, lambda bi, i: (layer, 0, 0), **kw)
    if long_seq:
        st_in = st_out = pl.BlockSpec((1, CONV_W - 1, 2 * D_FF), lambda bi, i: (bi, 0, 0))
        st_shape = state.shape
    else:
        nseq = tm // SUBLANE
        st_in = pl.BlockSpec((None, CONV_W - 1, nseq, 2 * D_FF), lambda bi, i: (layer, 0, i, 0))
        st_out = pl.BlockSpec((CONV_W - 1, nseq, 2 * D_FF), lambda bi, i: (0, i, 0))
        st_shape = state.shape[1:]
    return pl.pallas_call(
        functools.partial(_ffn_kernel, tm=tm, tf=tf, long_seq=long_seq, final_norm=final_norm),
        grid=(b, nt),
        in_specs=[row_blk(d), row_blk(a1.shape[-1]), row_blk(a2.shape[-1]),
                  pl.BlockSpec(w1.shape, const2, **resident),
                  pl.BlockSpec(w2.shape, const2, **resident),
                  per_layer((1, d)),
                  per_layer((d, 2 * D_FF), **resident),
                  per_layer((CONV_W, 2 * D_FF)),
                  per_layer((1, 2 * D_FF)),
                  per_layer((D_FF, d), **resident),
                  st_in,
                  pl.BlockSpec((1, d), const2)],
        out_specs=[row_blk(d), st_out],
        out_shape=[jax.ShapeDtypeStruct((b, t, d), F32), jax.ShapeDtypeStruct(st_shape, F32)],
        scratch_shapes=[pltpu.VMEM((CONV_W - 1, 2 * D_FF), F32)],
        compiler_params=_params(("parallel", "arbitrary")),
        name="ffn",
    )(res3, a1, a2, w1, w2, g.reshape(-1, 1, d), wup, cw, cb.reshape(-1, 1, 2 * D_FF), wd, state, gfinal)


def _rope_tables(pos):
    half = SW_HD // 2
    inv = ROPE_THETA ** (-jnp.arange(half, dtype=F32) / half)
    ang = pos.astype(F32)[:, None] * inv[None, :]
    cos = jnp.cos(ang)
    sin = jnp.sin(ang)
    reps = LANE // SW_HD
    cos_t = jnp.tile(jnp.concatenate([cos, cos], axis=1), (1, reps))
    sin_t = jnp.tile(jnp.concatenate([-sin, sin], axis=1), (1, reps))
    return cos_t, sin_t


def _prep_weights(W):
    P = {}
    we = W['w_in_even']
    nh = 4 * ML_HEADS * ML_DK
    head, g_m, tail = we[:, :nh], we[:, nh:nh + 2 * ML_HEADS], we[:, nh + 2 * ML_HEADS:]
    q_a, k_a, v_a = tail[:, :512], tail[:, 512:640], tail[:, 640:768]
    dup = lambda w: jnp.concatenate([w[:, :SW_HD], w[:, :SW_HD], w[:, SW_HD:], w[:, SW_HD:]], axis=1)
    pad = jnp.zeros((D_MODEL, LANE - 2 * ML_HEADS), we.dtype)
    P['w_in_even'] = jnp.concatenate([head, q_a, dup(k_a), dup(v_a), g_m, pad], axis=1).astype(BF16)
    assert P['w_in_even'].shape[1] == EVEN_PAD
    P['b_gates'] = jnp.pad(W['b_mlstm_gates'].astype(F32), (0, LANE - 2 * ML_HEADS)).reshape(1, LANE)
    wo = W['w_in_odd']
    nh = 2 * GLA_HEADS * GLA_DK + 2 * GLA_HEADS * GLA_DV
    head, glr, u = wo[:, :nh], wo[:, nh:nh + GLA_RANK], wo[:, nh + GLA_RANK:]
    pad = jnp.zeros((D_MODEL, ODD_PAD - ODD_GATE - GLA_RANK), wo.dtype)
    P['w_in_odd'] = jnp.concatenate([head, u, glr, pad], axis=1).astype(BF16)
    wg = jnp.pad(W['w_gla_gate_up'].astype(F32), ((0, LANE - GLA_RANK), (0, 0)))
    wg_hi = wg.astype(BF16)
    P['w_gate_up'] = jnp.concatenate([wg_hi, (wg - wg_hi.astype(F32)).astype(BF16)], axis=1)
    P['b_gate'] = W['b_gla_gate'].astype(F32).reshape(1, -1)
    P['g_gla'] = W['g_gla_norm'].astype(F32).reshape(1, -1)
    hm = ML_HEADS * ML_DV
    P['w_out_even'] = (W['w_out_even'][:hm].astype(BF16), W['w_out_even'][hm:].astype(BF16))
    hg = GLA_HEADS * GLA_DV
    P['w_out_odd'] = (W['w_out_odd'][:hg].astype(BF16), W['w_out_odd'][hg:].astype(BF16))
    P['w_ffn_up'] = W['w_ffn_up'].astype(BF16)
    P['w_ffn_down'] = W['w_ffn_down'].astype(BF16)
    P['s5'] = _s5_tables(W['s5_a_re'], W['s5_a_im'], W['s5_log_dt'], W['s5_b_re'], W['s5_b_im'],
                         W['s5_c_re'], W['s5_c_im'], W['s5_d'], W['w_s5_glu'], W['b_s5_glu'])
    return P


def _trunk(x, pos, st, W, P, is_prompt):
    b, t, d = x.shape
    n = b * t
    L = CHUNK if t % CHUNK == 0 else t
    nsub = next((c for c in (8, 4, 2) if t % (c * L) == 0), 1)
    tm = next((c for c in (1024, 512, 256) if n % c == 0 and n // c >= 4), n)
    cos, sin_signed = _rope_tables(pos)
    gfinal = W['norm_final'].astype(F32).reshape(1, d)
    new = {}
    conv_out = []
    h = x.reshape(n, d)

    z = _norm_matmul(h, W['norm_mix'][0], P['w_in_even'], tm)
    z3 = z.reshape(b, t, EVEN_PAD)
    if is_prompt or b % 16:
        pack = dict(G=1, Bb=2 if b % 2 == 0 else 1, nsub=nsub)
    else:
        pack = dict(G=8, Bb=4 if b % 32 == 0 else 2, nsub=1)
    h_m, new['ml_C'], new['ml_n'], m_new = _mlstm(z3, P['b_gates'], st['ml_C'], st['ml_n'], st['ml_m'], L, **pack)
    new['ml_m'] = m_new.reshape(b, ML_HEADS)
    if is_prompt:
        h_a, kb = _swa_prompt(z3, cos, sin_signed, W['sw_sinks'].astype(F32), Bb=2 if b % 2 == 0 else 1)
        vb = jnp.concatenate([z3[:, t - WINDOW:, EVEN_VD:EVEN_VD + SW_HD],
                              z3[:, t - WINDOW:, EVEN_VD + LANE:EVEN_VD + LANE + SW_HD]], axis=-1)
    else:
        h_a, kb, vb = _swa_sample(z3, st['kbuf'].reshape(b, WINDOW, -1), st['vbuf'].reshape(b, WINDOW, -1),
                                  cos, sin_signed, W['sw_sinks'].astype(F32), Bb=16 if b % 16 == 0 else 8)
    new['kbuf'] = kb.reshape(b, WINDOW, SW_KV_HEADS, SW_HD)
    new['vbuf'] = vb.reshape(b, WINDOW, SW_KV_HEADS, SW_HD)
    h, cb = _ffn_layer(h, h_m, h_a, P['w_out_even'], 0, st, W, P, b, t, gfinal, is_prompt, final_norm=False)
    conv_out.append(cb)

    z = _norm_matmul(h, W['norm_mix'][1], P['w_in_odd'], tm)
    z3 = z.reshape(b, t, ODD_PAD)
    o_g, new['gla'] = _gla(z3, P['w_gate_up'], P['b_gate'], P['g_gla'], st['gla'], L, **pack)
    x0re = st['s5_re'].astype(F32).reshape(b, 1, S5_LANES)
    x0im = st['s5_im'].astype(F32).reshape(b, 1, S5_LANES)
    if is_prompt:
        o_s, xre, xim = _s5(z3, P['s5'], x0re, x0im, rows=next((c for c in (512, 256) if t % c == 0), t),
                            sequential=True)
    else:
        o_s, xre, xim = _s5(z3.reshape(1, n, ODD_PAD), P['s5'], x0re.reshape(1, b, S5_LANES),
                            x0im.reshape(1, b, S5_LANES), rows=256 if n % 256 == 0 else n, sequential=False)
    new['s5_re'] = xre.reshape(b, S5_GROUPS, S5_STATE)
    new['s5_im'] = xim.reshape(b, S5_GROUPS, S5_STATE)
    h, cb = _ffn_layer(h, o_g, o_s, P['w_out_odd'], 1, st, W, P, b, t, gfinal, is_prompt, final_norm=True)
    conv_out.append(cb)
    new['conv'] = jnp.stack(conv_out)
    return h.reshape(b, t, d), new


def _ffn_layer(h, a1, a2, w_out, layer, st, W, P, b, t, gfinal, is_prompt, final_norm):
    n, d = h.shape
    rows = t if is_prompt else n
    tm = next((c for c in (512, 256) if rows % c == 0), rows)
    args = (*w_out, W['norm_ffn'], P['w_ffn_up'], W['ffn_conv_w'].astype(F32), W['ffn_conv_b'].astype(F32),
            P['w_ffn_down'])
    if is_prompt:
        shape3 = lambda a: a.reshape(b, t, -1)
        out, new_state = _ffn(shape3(h), shape3(a1), shape3(a2), *args, st['conv'][layer], layer, gfinal, tm,
                              FFN_TF, True, final_norm)
    else:
        assert t == SUBLANE
        shape3 = lambda a: a.reshape(1, n, -1)
        out, new_state = _ffn(shape3(h), shape3(a1), shape3(a2), *args, st['conv_rows'], layer, gfinal, tm, FFN_TF,
                              False, final_norm)
        new_state = jnp.swapaxes(new_state, 0, 1)
    return out.reshape(n, d), new_state


def kernel(x_prompt, x_sample, state_mlstm_C, state_mlstm_n, state_mlstm_m, cache_swa_k, cache_swa_v,
           state_gla, state_s5_re, state_s5_im, state_ffn_conv, norm_mix, norm_ffn, norm_final,
           w_in_even, b_mlstm_gates, sw_sinks, w_out_even, w_in_odd, w_gla_gate_up, b_gla_gate, g_gla_norm,
           s5_a_re, s5_a_im, s5_log_dt, s5_b_re, s5_b_im, s5_c_re, s5_c_im, s5_d, w_s5_glu, b_s5_glu,
           w_out_odd, w_ffn_up, ffn_conv_w, ffn_conv_b, w_ffn_down):
    W = {'norm_mix': norm_mix.astype(F32), 'norm_ffn': norm_ffn.astype(F32), 'norm_final': norm_final,
         'w_in_even': w_in_even, 'b_mlstm_gates': b_mlstm_gates, 'sw_sinks': sw_sinks, 'w_out_even': w_out_even,
         'w_in_odd': w_in_odd, 'w_gla_gate_up': w_gla_gate_up, 'b_gla_gate': b_gla_gate, 'g_gla_norm': g_gla_norm,
         's5_a_re': s5_a_re, 's5_a_im': s5_a_im, 's5_log_dt': s5_log_dt, 's5_b_re': s5_b_re, 's5_b_im': s5_b_im,
         's5_c_re': s5_c_re, 's5_c_im': s5_c_im, 's5_d': s5_d, 'w_s5_glu': w_s5_glu, 'b_s5_glu': b_s5_glu,
         'w_out_odd': w_out_odd, 'w_ffn_up': w_ffn_up, 'ffn_conv_w': ffn_conv_w, 'ffn_conv_b': ffn_conv_b,
         'w_ffn_down': w_ffn_down}
    P = _prep_weights(W)
    bp, tp = x_prompt.shape[:2]
    st_prompt = {'ml_C': jnp.zeros((bp, ML_HEADS, ML_DK, ML_DV), F32),
                 'ml_n': jnp.zeros((bp, ML_HEADS, ML_DK), F32),
                 'ml_m': jnp.zeros((bp, ML_HEADS), F32),
                 'gla': jnp.zeros((bp, GLA_HEADS, GLA_DK, GLA_DV), F32),
                 's5_re': jnp.zeros((bp, S5_GROUPS, S5_STATE), F32),
                 's5_im': jnp.zeros((bp, S5_GROUPS, S5_STATE), F32),
                 'conv': jnp.zeros((2, bp, CONV_W - 1, 2 * D_FF), F32)}
    st_sample = {'ml_C': state_mlstm_C.astype(F32), 'ml_n': state_mlstm_n.astype(F32),
                 'ml_m': state_mlstm_m.astype(F32), 'kbuf': cache_swa_k.astype(F32),
                 'vbuf': cache_swa_v.astype(F32), 'gla': state_gla.astype(F32),
                 's5_re': state_s5_re, 's5_im': state_s5_im,
                 'conv_rows': jnp.swapaxes(state_ffn_conv.astype(F32), 1, 2)}
    y_p, np_ = _trunk(x_prompt.astype(F32), jnp.arange(tp), st_prompt, W, P, True)
    y_s, ns_ = _trunk(x_sample.astype(F32), PAST_LEN + jnp.arange(x_sample.shape[1]), st_sample, W, P, False)
    return (y_p, y_s,
            np_['ml_C'], ns_['ml_C'], np_['ml_n'], ns_['ml_n'], np_['ml_m'], ns_['ml_m'],
            np_['kbuf'], ns_['kbuf'], np_['vbuf'], ns_['vbuf'], np_['gla'], ns_['gla'],
            np_['s5_re'], ns_['s5_re'], np_['s5_im'], ns_['s5_im'], np_['conv'], ns_['conv'])
```

```python
import functools
import math

import numpy as np
import jax
import jax.numpy as jnp
from jax import lax
from jax.experimental import pallas as pl
from jax.experimental.pallas import tpu as pltpu

F32 = jnp.float32
BF16 = jnp.bfloat16
NT = (((1,), (1,)), ((), ()))
TN = (((0,), (0,)), ((), ()))

D_MODEL = 1024
PAST_LEN = 16384
ML_HEADS, ML_DK, ML_DV = 4, 128, 128
SW_HEADS, SW_KV_HEADS, SW_HD, WINDOW = 8, 2, 64, 128
SW_GQ = SW_HEADS // SW_KV_HEADS
ROPE_THETA = 10000.0
GLA_HEADS, GLA_DK, GLA_DV, GLA_RANK, GLA_TAU = 4, 64, 128, 16, 16.0
S5_WIDTH, S5_GROUP, S5_STATE = 512, 16, 64
S5_GROUPS = S5_WIDTH // S5_GROUP
S5_LANES = S5_GROUPS * S5_STATE
D_FF = 2816
FFN_TF = 2816
CONV_W = 3
CHUNK = 64
EPS = 1e-6

LANE = 128
SUBLANE = 8
VMEM_LIMIT = 56 * 1024 * 1024

EVEN_QA, EVEN_KD, EVEN_VD, EVEN_PAD = 2048, 2560, 2816, 3200
ODD_K, ODD_V, ODD_R, ODD_U, ODD_GATE, ODD_PAD = 256, 512, 1024, 1536, 2048, 2176


def _params(sem):
    return pltpu.CompilerParams(dimension_semantics=sem, vmem_limit_bytes=VMEM_LIMIT)


def _norm_matmul_kernel(x_ref, g_ref, w_ref, o_ref):
    x = x_ref[...]
    ms = jnp.mean(x * x, axis=-1, keepdims=True)
    hn = (x * lax.rsqrt(ms + EPS) * g_ref[...]).astype(BF16)
    o_ref[...] = jnp.dot(hn, w_ref[...], preferred_element_type=F32)


def _norm_matmul(x, g, w, tm):
    n, d = x.shape
    c = w.shape[1]
    return pl.pallas_call(
        _norm_matmul_kernel,
        grid=(n // tm,),
        in_specs=[pl.BlockSpec((tm, d), lambda i: (i, 0)),
                  pl.BlockSpec((1, d), lambda i: (0, 0)),
                  pl.BlockSpec((d, c), lambda i: (0, 0))],
        out_specs=pl.BlockSpec((tm, c), lambda i: (i, 0)),
        out_shape=jax.ShapeDtypeStruct((n, c), F32),
        compiler_params=_params(("parallel",)),
        name="norm_matmul",
    )(x, g.reshape(1, d), w)


def _pieces(x, n):
    out, r = [], x
    for _ in range(n):
        p = r.astype(BF16).astype(F32)
        out.append(p)
        r = r - p
    return out


def _dot_sel(m_bf16, x, dims=None, pieces=3):
    ps = _pieces(x, pieces)
    axis = 1 if dims is None else 0
    w = x.shape[axis]
    if w % LANE:
        f = (lambda p: jnp.dot(m_bf16, p, preferred_element_type=F32)) if dims is None else (
            lambda p: lax.dot_general(m_bf16, p, dims, preferred_element_type=F32))
        return sum(f(p.astype(BF16)) for p in ps)
    stacked = jnp.concatenate(ps, axis=axis).astype(BF16)
    if dims is None:
        r = jnp.dot(m_bf16, stacked, preferred_element_type=F32)
    else:
        r = lax.dot_general(m_bf16, stacked, dims, preferred_element_type=F32)
    return sum(r[:, i * w:(i + 1) * w] for i in range(pieces))


def _sel_right(x, e_bf16, pieces=3):
    rows = x.shape[0]
    r = jnp.dot(jnp.concatenate(_pieces(x, pieces), axis=0).astype(BF16), e_bf16, preferred_element_type=F32)
    return sum(r[i * rows:(i + 1) * rows] for i in range(pieces))


def _mlstm_kernel(q_ref, k_ref, v_ref, og_ref, gz_ref, bias_ref, tri_ref, ones_ref, expc_ref, expv_ref, eyet_ref,
                  seg_ref, segv_ref, hmk_ref, c0_ref, n0_ref, m0_ref,
                  h_ref, co_ref, no_ref, mo_ref,
                  c_s, n_s, m_s, *, L, Bb, nsub, G):
    c = pl.program_id(1)

    @pl.when(c == 0)
    def _():
        c_s[...] = c0_ref[...]
        n_s[...] = n0_ref[...]
        m_s[...] = m0_ref[...]

    for sub in range(nsub):
        for bi in range(Bb):
            _mlstm_one(bi, sub * L, q_ref, k_ref, v_ref, og_ref, gz_ref, bias_ref, tri_ref, ones_ref, expc_ref,
                       expv_ref, eyet_ref, seg_ref, segv_ref, hmk_ref, h_ref, c_s, n_s, m_s, L, G)

    @pl.when(c == pl.num_programs(1) - 1)
    def _():
        co_ref[...] = c_s[...]
        no_ref[...] = n_s[...]
        mo_ref[...] = m_s[...]


def _mlstm_one(bi, r0, q_ref, k_ref, v_ref, og_ref, gz_ref, bias_ref, tri_ref, ones_ref, expc_ref, expv_ref,
               eyet_ref, seg_ref, segv_ref, hmk_ref, h_ref, c_s, n_s, m_s, L, G):
    H, DK = ML_HEADS, ML_DK
    HL = H * L
    S = L // G
    seg = lambda a, g: a[g * S:(g + 1) * S]
    per_row = lambda rows: rows[0] if G == 1 else jnp.concatenate(
        [jnp.broadcast_to(r, (S, r.shape[1])) for r in rows], axis=0)
    lane = lax.broadcasted_iota(jnp.int32, (L, LANE), 1)
    rowi = lax.broadcasted_iota(jnp.int32, (L, LANE), 0) % S
    keep = lane < H
    heads_only = lambda x: jnp.where(keep[:x.shape[0]], x, 0.0)

    rw = pl.ds(r0, L)
    gates = gz_ref[bi, rw] + bias_ref[...]
    lf = jax.nn.log_sigmoid(gates)
    b = pltpu.roll(_dot_sel(tri_ref[...], lf), LANE - H, axis=1)
    vv = gates - b
    cm = vv
    sh = 1
    while sh < S:
        cm = jnp.maximum(cm, jnp.where(rowi >= sh, pltpu.roll(cm, sh, axis=0), -jnp.inf))
        sh *= 2
    m_prev_g = [m_s[bi * G + g] for g in range(G)]
    m_prev = per_row(m_prev_g)
    mt = b + jnp.maximum(m_prev, cm)
    a_inter = jnp.exp(b + m_prev - mt)

    expv = expv_ref[...]
    m_new_g = [seg(mt, g)[S - 1:S, :] for g in range(G)]
    b_end_g = [seg(b, g)[S - 1:S, :] for g in range(G)]
    w_end = heads_only(jnp.exp(per_row(b_end_g) - b + gates - per_row(m_new_g)))
    decay_g = [heads_only(jnp.exp(b_end_g[g] + m_prev_g[g] - m_new_g[g])) for g in range(G)]
    decay_rows = jnp.concatenate(decay_g + [jnp.zeros((-G % SUBLANE, LANE), F32)] * (G % SUBLANE != 0), axis=0)
    spread_k = _sel_right(jnp.concatenate([heads_only(b - mt), heads_only(vv)], axis=0), expc_ref[...])
    uc = spread_k[:L]
    vr = _dot_sel(ones_ref[...], spread_k[L:] * eyet_ref[...])
    spread_v = _sel_right(jnp.concatenate([heads_only(a_inter), w_end, decay_rows], axis=0), expv, pieces=2)
    ac, wc, dcs = spread_v[:L], spread_v[L:2 * L], spread_v[2 * L:2 * L + G]
    ti = lax.broadcasted_iota(jnp.int32, (L, HL), 0)
    si = lax.broadcasted_iota(jnp.int32, (L, HL), 1) % L
    causal = (si <= ti) & (si // S == ti // S)
    w = jnp.exp(jnp.where(causal, uc + vr, -jnp.inf))

    q = q_ref[bi, rw]
    ks = k_ref[bi, rw] * (DK ** -0.5)
    v = v_ref[bi, rw]
    qb = q.astype(BF16)
    if L % (2 * SUBLANE) == 0:
        kst = jnp.concatenate([ks.astype(BF16)] * H, axis=0) * hmk_ref[...]
        vsrc = v.astype(BF16)
    else:
        kst = (jnp.concatenate([ks] * H, axis=0) * hmk_ref[...].astype(F32)).astype(BF16)
        vsrc = v
    zero = jnp.zeros((L, DK), vsrc.dtype)
    vbd = jnp.concatenate(
        [jnp.concatenate([zero] * h + [vsrc[:, h * DK:(h + 1) * DK]] + [zero] * (H - 1 - h), axis=1)
         for h in range(H)], axis=0).astype(BF16)
    s = lax.dot_general(qb, kst, NT, preferred_element_type=F32) * w
    qc = jnp.concatenate(
        [jnp.concatenate([jnp.dot(seg(q, g)[:, h * DK:(h + 1) * DK].astype(BF16), c_s[bi * G + g, h].astype(BF16),
                                  preferred_element_type=F32) for h in range(H)], axis=1)
         for g in range(G)], axis=0)
    num = jnp.dot(s.astype(BF16), vbd, preferred_element_type=F32) + ac * qc
    n_rows = per_row([n_s[bi * G + g] for g in range(G)])
    den = (_sel_right(s, seg_ref[...], pieces=2)
           + a_inter * _sel_right(q * n_rows, segv_ref[...], pieces=2))
    rden = heads_only(1.0 / jnp.maximum(jnp.abs(den), jnp.exp(-mt)))
    hh = num * _sel_right(rden, expv, pieces=2) * jax.nn.sigmoid(og_ref[bi, rw])
    h_ref[bi, rw] = hh.astype(h_ref.dtype)

    kw = ks * wc
    for g in range(G):
        e = bi * G + g
        kwb = seg(kw, g).astype(BF16)
        vb = seg(v, g).astype(BF16)
        dc = dcs[g:g + 1]
        for h in range(H):
            hs = slice(h * DK, (h + 1) * DK)
            c_s[e, h] = dc[:, hs] * c_s[e, h] + lax.dot_general(kwb[:, hs], vb[:, hs], TN, preferred_element_type=F32)
        n_s[e] = dc * n_s[e] + jnp.sum(seg(kw, g), axis=0, keepdims=True)
        m_s[e] = heads_only(m_new_g[g])


def _mlstm_consts(S, G):
    L = S * G
    H, DK = ML_HEADS, ML_DK
    expc = np.zeros((LANE, H * L), np.float32)
    expv = np.zeros((LANE, H * DK), np.float32)
    seg = np.zeros((H * L, LANE), np.float32)
    segv = np.zeros((H * DK, LANE), np.float32)
    hmk = np.zeros((H * L, H * DK), np.float32)
    for h in range(H):
        expc[h, h * L:(h + 1) * L] = 1.0
        expv[h, h * DK:(h + 1) * DK] = 1.0
        seg[h * L:(h + 1) * L, h] = 1.0
        segv[h * DK:(h + 1) * DK, h] = 1.0
        hmk[h * L:(h + 1) * L, h * DK:(h + 1) * DK] = 1.0
    eyet = np.tile(np.eye(L, dtype=np.float32), (1, H))
    bf = lambda a: jnp.asarray(a, BF16)
    tri = np.kron(np.eye(G), np.tril(np.ones((S, S)))).astype(np.float32)
    return (bf(tri), bf(np.ones((L, L), np.float32)), bf(expc), bf(expv),
            jnp.asarray(eyet), bf(seg), bf(segv), bf(hmk))


def _mlstm(z3, bias, c0, n0, m0, S, G, Bb, nsub):
    nseq, t_seq, width = z3.shape
    if G > 1:
        assert t_seq == S and nsub == 1
        z3 = z3.reshape(nseq // G, G * S, width)
    b, t, _ = z3.shape
    L = G * S
    nc = t // (L * nsub)
    hw = ML_HEADS * ML_DK
    consts = _mlstm_consts(S, G)
    gate_blk = (EVEN_PAD - LANE) // LANE
    col = lambda j: (lambda bi, ci: (bi, ci, j))
    const2 = lambda bi, ci: (0, 0)
    state3 = lambda bi, ci: (bi, 0, 0)
    m0p = jnp.pad(m0, ((0, 0), (0, LANE - ML_HEADS))).reshape(nseq, 1, LANE)
    Bs = Bb * G
    h, c_new, n_new, m_new = pl.pallas_call(
        functools.partial(_mlstm_kernel, L=L, Bb=Bb, nsub=nsub, G=G),
        grid=(b // Bb, nc),
        in_specs=[pl.BlockSpec((Bb, nsub * L, hw), col(0)),
                  pl.BlockSpec((Bb, nsub * L, hw), col(1)),
                  pl.BlockSpec((Bb, nsub * L, hw), col(2)),
                  pl.BlockSpec((Bb, nsub * L, hw), col(3)),
                  pl.BlockSpec((Bb, nsub * L, LANE), col(gate_blk)),
                  pl.BlockSpec((1, LANE), const2)]
                 + [pl.BlockSpec(a.shape, const2) for a in consts]
                 + [pl.BlockSpec((Bs, ML_HEADS, ML_DK, ML_DV), lambda bi, ci: (bi, 0, 0, 0)),
                    pl.BlockSpec((Bs, 1, hw), state3),
                    pl.BlockSpec((Bs, 1, LANE), state3)],
        out_specs=[pl.BlockSpec((Bb, nsub * L, hw), col(0)),
                   pl.BlockSpec((Bs, ML_HEADS, ML_DK, ML_DV), lambda bi, ci: (bi, 0, 0, 0)),
                   pl.BlockSpec((Bs, 1, hw), state3),
                   pl.BlockSpec((Bs, 1, LANE), state3)],
        out_shape=[jax.ShapeDtypeStruct((b, t, hw), BF16),
                   jax.ShapeDtypeStruct((nseq, ML_HEADS, ML_DK, ML_DV), F32),
                   jax.ShapeDtypeStruct((nseq, 1, hw), F32),
                   jax.ShapeDtypeStruct((nseq, 1, LANE), F32)],
        scratch_shapes=[pltpu.VMEM((Bs, ML_HEADS, ML_DK, ML_DV), F32),
                        pltpu.VMEM((Bs, 1, hw), F32),
                        pltpu.VMEM((Bs, 1, LANE), F32)],
        compiler_params=_params(("parallel", "arbitrary")),
        name="mlstm",
    )(z3, z3, z3, z3, z3, bias, *consts, c0, n0.reshape(nseq, 1, hw), m0p)
    return h.reshape(nseq, t_seq, hw), c_new, n_new.reshape(nseq, ML_HEADS, ML_DK), m_new[:, 0, :ML_HEADS]


def _rope(x, cos, sin_signed, width):
    half = SW_HD // 2
    lane = lax.broadcasted_iota(jnp.int32, x.shape, 1)
    first = (lane % SW_HD) < half
    partner = jnp.where(first, pltpu.roll(x, width - half, axis=1), pltpu.roll(x, half, axis=1))
    return x * cos + partner * sin_signed


def _undup(xd):
    lane = lax.broadcasted_iota(jnp.int32, xd.shape[:-1] + (LANE,), xd.ndim - 1)
    return jnp.where(lane < SW_HD, xd[..., :LANE], xd[..., LANE:])


def _swa_prompt_kernel(sink_ref, q_ref, kc_ref, vc_ref, vp_ref, cosc_ref, sinc_ref, h_ref, ko_ref, kprev_s):
    n = pl.program_id(1)
    W2 = 2 * WINDOW
    cosc, sinc = cosc_ref[...], sinc_ref[...]

    @pl.when(n == 0)
    def _():
        kprev_s[...] = jnp.zeros_like(kprev_s)

    i = lax.broadcasted_iota(jnp.int32, (WINDOW, W2), 0)
    j = lax.broadcasted_iota(jnp.int32, (WINDOW, W2), 1)
    diff = WINDOW + i - j
    valid = (diff >= 0) & (diff < WINDOW) & ((n > 0) | (j >= WINDOW))
    row = lax.broadcasted_iota(jnp.int32, (2 * W2, LANE), 0)
    lane = lax.broadcasted_iota(jnp.int32, (2 * W2, LANE), 1)
    bd = (row < W2) == (lane < SW_HD)
    ones_bd = bd.astype(BF16)
    npair = SW_GQ // 2
    rows = npair * WINDOW
    valid = jnp.concatenate([valid] * npair, axis=0)
    low = lax.broadcasted_iota(jnp.int32, (rows, LANE), 1) < SW_HD
    prow = lax.broadcasted_iota(jnp.int32, (rows, 1), 0) // WINDOW
    cos4, sin4 = jnp.concatenate([cosc] * 4, axis=1), jnp.concatenate([sinc] * 4, axis=1)
    for bi in range(q_ref.shape[0]):
        q = _rope(q_ref[bi], cos4, sin4, 4 * LANE)
        k_cur = _rope(kc_ref[bi], cos4[:, :2 * LANE], sin4[:, :2 * LANE], 2 * LANE)
        kd = jnp.concatenate([kprev_s[bi], k_cur], axis=0)
        kprev_s[bi] = k_cur
        vd = jnp.concatenate([vp_ref[bi], vc_ref[bi]], axis=0)
        for kv in range(SW_KV_HEADS):
            kblk = kd[:, kv * LANE:(kv + 1) * LANE]
            vblk = vd[:, kv * LANE:(kv + 1) * LANE]
            kbd = jnp.where(bd, jnp.concatenate([kblk, kblk], axis=0), 0.0).astype(BF16)
            vbd = jnp.where(bd, jnp.concatenate([vblk, vblk], axis=0), 0.0).astype(BF16)
            vo = jnp.concatenate([vbd, ones_bd], axis=1)
            h0 = kv * SW_GQ
            qs = jnp.concatenate([q[:, (h0 + 2 * p) * SW_HD:(h0 + 2 * p + 2) * SW_HD] for p in range(npair)],
                                 axis=0).astype(BF16)
            s = lax.dot_general(qs, kbd, NT, preferred_element_type=F32) * (SW_HD ** -0.5)
            es, sink_terms = [], []
            for hh in range(2):
                sh = jnp.where(valid, s[:, hh * W2:(hh + 1) * W2], -jnp.inf)
                sink = sink_ref[h0 + hh]
                for p in range(1, npair):
                    sink = jnp.where(prow == p, sink_ref[h0 + 2 * p + hh], sink)
                m = jnp.maximum(jnp.max(sh, axis=1, keepdims=True), sink)
                es.append(jnp.exp(sh - m).astype(BF16))
                sink_terms.append(jnp.exp(sink - m))
            r = jnp.dot(jnp.concatenate(es, axis=1), vo, preferred_element_type=F32)
            o = r[:, :LANE] / (r[:, LANE:] + jnp.where(low, sink_terms[0], sink_terms[1]))
            for p in range(npair):
                h_ref[bi, :, (h0 + 2 * p) * SW_HD:(h0 + 2 * p + 2) * SW_HD] = (
                    o[p * WINDOW:(p + 1) * WINDOW].astype(h_ref.dtype))

        @pl.when(n == pl.num_programs(1) - 1)
        def _(bi=bi, k_cur=k_cur):
            ko_ref[bi] = _undup(k_cur)


def _swa_prompt(z3, cos, sin_signed, sinks, Bb):
    b, t, _ = z3.shape
    nb = t // WINDOW
    qw = SW_HEADS * SW_HD
    kw = SW_KV_HEADS * SW_HD
    dw = 2 * kw
    qblk, kblk, vblk = EVEN_QA // qw, EVEN_KD // dw, EVEN_VD // dw
    cur = lambda j: (lambda bi, ni: (bi, ni, j))
    prev = lambda j: (lambda bi, ni: (bi, jnp.maximum(ni - 1, 0), j))
    return pl.pallas_call(
        _swa_prompt_kernel,
        grid=(b // Bb, nb),
        in_specs=[pl.BlockSpec(memory_space=pltpu.SMEM),
                  pl.BlockSpec((Bb, WINDOW, qw), cur(qblk)),
                  pl.BlockSpec((Bb, WINDOW, dw), cur(kblk)),
                  pl.BlockSpec((Bb, WINDOW, dw), cur(vblk)),
                  pl.BlockSpec((Bb, WINDOW, dw), prev(vblk)),
                  pl.BlockSpec((WINDOW, LANE), lambda bi, ni: (ni, 0)),
                  pl.BlockSpec((WINDOW, LANE), lambda bi, ni: (ni, 0))],
        out_specs=[pl.BlockSpec((Bb, WINDOW, qw), cur(0)),
                   pl.BlockSpec((Bb, WINDOW, kw), lambda bi, ni: (bi, 0, 0))],
        out_shape=[jax.ShapeDtypeStruct((b, t, qw), BF16),
                   jax.ShapeDtypeStruct((b, WINDOW, kw), F32)],
        scratch_shapes=[pltpu.VMEM((Bb, WINDOW, dw), F32)],
        compiler_params=_params(("parallel", "arbitrary")),
        name="swa_prompt",
    )(sinks, z3, z3, z3, z3, cos, sin_signed)


def _swa_sample_kernel(q_ref, k_ref, v_ref, kbuf_ref, vbuf_ref, cos_ref, sin_ref, sink_ref,
                       h_ref, ko_ref, vo_ref, *, T, Bb):
    qw = SW_HEADS * SW_HD
    kw = SW_KV_HEADS * SW_HD
    cos = cos_ref[...]
    sin = sin_ref[...]
    q = _rope(q_ref[...].reshape(Bb * T, qw), jnp.concatenate([cos] * (qw // LANE), axis=1),
              jnp.concatenate([sin] * (qw // LANE), axis=1), qw).reshape(Bb, T, qw)
    k_new = _rope(_undup(k_ref[...]).reshape(Bb * T, kw), cos, sin, kw).reshape(Bb, T, kw)
    kk = jnp.concatenate([kbuf_ref[...], k_new], axis=1)
    vv = jnp.concatenate([vbuf_ref[...], _undup(v_ref[...])], axis=1)
    ko_ref[...] = kk[:, T:, :]
    vo_ref[...] = vv[:, T:, :]
    rows = SW_GQ * T
    i = lax.broadcasted_iota(jnp.int32, (rows, WINDOW + T), 0) % T
    j = lax.broadcasted_iota(jnp.int32, (rows, WINDOW + T), 1)
    diff = WINDOW + i - j
    valid = (diff >= 0) & (diff < WINDOW)
    for kv in range(SW_KV_HEADS):
        qs = jnp.concatenate([q[:, :, (kv * SW_GQ + g) * SW_HD:(kv * SW_GQ + g + 1) * SW_HD]
                              for g in range(SW_GQ)], axis=1).astype(BF16)
        kh = kk[:, :, kv * SW_HD:(kv + 1) * SW_HD].astype(BF16)
        vh = vv[:, :, kv * SW_HD:(kv + 1) * SW_HD].astype(BF16)
        s = jnp.einsum('bqd,bkd->bqk', qs, kh, preferred_element_type=F32) * (SW_HD ** -0.5)
        s = jnp.where(valid, s, -jnp.inf)
        sink = sink_ref[kv * rows:(kv + 1) * rows, :]
        m = jnp.maximum(jnp.max(s, axis=-1, keepdims=True), sink)
        e = jnp.exp(s - m)
        p = e / (jnp.sum(e, axis=-1, keepdims=True) + jnp.exp(sink - m))
        o = jnp.einsum('bqk,bkd->bqd', p.astype(BF16), vh, preferred_element_type=F32)
        for g in range(SW_GQ):
            hh = kv * SW_GQ + g
            h_ref[:, :, hh * SW_HD:(hh + 1) * SW_HD] = o[:, g * T:(g + 1) * T, :].astype(h_ref.dtype)


def _swa_sample(z3, kbuf, vbuf, cos, sin_signed, sinks, Bb):
    b, t, _ = z3.shape
    qw = SW_HEADS * SW_HD
    kw = SW_KV_HEADS * SW_HD
    dw = 2 * kw
    qblk, kblk, vblk = EVEN_QA // qw, EVEN_KD // dw, EVEN_VD // dw
    col = lambda j: (lambda bi: (bi, 0, j))
    const2 = lambda bi: (0, 0)
    sink_col = jnp.repeat(sinks, t).reshape(SW_HEADS * t, 1)
    return pl.pallas_call(
        functools.partial(_swa_sample_kernel, T=t, Bb=Bb),
        grid=(b // Bb,),
        in_specs=[pl.BlockSpec((Bb, t, qw), col(qblk)),
                  pl.BlockSpec((Bb, t, dw), col(kblk)),
                  pl.BlockSpec((Bb, t, dw), col(vblk)),
                  pl.BlockSpec((Bb, WINDOW, kw), col(0)),
                  pl.BlockSpec((Bb, WINDOW, kw), col(0)),
                  pl.BlockSpec((Bb * t, LANE), const2),
                  pl.BlockSpec((Bb * t, LANE), const2),
                  pl.BlockSpec((SW_HEADS * t, 1), const2)],
        out_specs=[pl.BlockSpec((Bb, t, qw), col(0)),
                   pl.BlockSpec((Bb, WINDOW, kw), col(0)),
                   pl.BlockSpec((Bb, WINDOW, kw), col(0))],
        out_shape=[jax.ShapeDtypeStruct((b, t, qw), BF16),
                   jax.ShapeDtypeStruct((b, WINDOW, kw), F32),
                   jax.ShapeDtypeStruct((b, WINDOW, kw), F32)],
        compiler_params=_params(("parallel",)),
        name="swa_sample",
    )(z3, z3, z3, kbuf, vbuf, jnp.tile(cos, (Bb, 1)), jnp.tile(sin_signed, (Bb, 1)), sink_col)


def _gla_tables(S, G):
    nlev = int(math.log2(S))
    assert 2 ** nlev == S
    L = G * S
    mstack = np.zeros((nlev * L, L), np.float32)
    masks = np.zeros((nlev + 1, L, L), np.float32)
    masks[0] = np.eye(L)
    for l in range(1, nlev + 1):
        n = 2 ** l
        for r in range(L):
            blk, pos = divmod(r, n)
            m = blk * n + n // 2 - 1
            if pos >= n // 2:
                mstack[(l - 1) * L + r, m + 1:r + 1] = 1.0
                masks[l, r, blk * n:blk * n + n // 2] = 1.0
            else:
                mstack[(l - 1) * L + r, r + 1:m + 1] = 1.0
    masks = np.tile(masks, (1, 1, GLA_HEADS))
    tri = np.kron(np.eye(G), np.tril(np.ones((S, S)))).astype(np.float32)
    return nlev, jnp.asarray(tri, BF16), jnp.asarray(mstack, BF16), jnp.asarray(masks)


def _gla_consts(L):
    kw = GLA_HEADS * GLA_DK
    vw = GLA_HEADS * GLA_DV
    hk = np.zeros((GLA_HEADS * L, kw), np.float32)
    for h in range(GLA_HEADS):
        hk[h * L:(h + 1) * L, h * GLA_DK:(h + 1) * GLA_DK] = 1.0
    hs = np.zeros((vw, kw), np.float32)
    for h in range(GLA_HEADS):
        hs[h * GLA_DV:(h + 1) * GLA_DV, h * GLA_DK:(h + 1) * GLA_DK] = 1.0
    return jnp.asarray(hk, BF16), jnp.asarray(hs)


def _gla_kernel(q_ref, k_ref, v_ref, r_ref, glr_ref, wg_ref, bg_ref, gn_ref, tri_ref, mstack_ref,
                masks_ref, hk_ref, hs_ref, s0_ref,
                o_ref, so_ref, st_s, *, L, nlev, Bb, nsub, G):
    c = pl.program_id(1)
    H, DK, DV = GLA_HEADS, GLA_DK, GLA_DV

    @pl.when(c == 0)
    def _():
        for bi in range(Bb * G):
            for h in range(H):
                pieces = ([jnp.zeros((h * DK, DV), F32)] if h else []) + [s0_ref[bi, h]]
                pieces += [jnp.zeros(((H - 1 - h) * DK, DV), F32)] if h < H - 1 else []
                padded = jnp.concatenate(pieces, axis=0)
                st_s[bi, h * DV:(h + 1) * DV, :] = padded.T

    for sub in range(nsub):
        for bi in range(Bb):
            _gla_one(bi, sub * L, q_ref, k_ref, v_ref, r_ref, glr_ref, wg_ref, bg_ref, gn_ref, tri_ref, mstack_ref,
                     masks_ref, hk_ref, hs_ref, o_ref, st_s, L, nlev, G)

    @pl.when(c == pl.num_programs(1) - 1)
    def _():
        for bi in range(Bb * G):
            for h in range(H):
                blk = st_s[bi, h * DV:(h + 1) * DV, :]
                so_ref[bi, h] = blk.T[h * DK:(h + 1) * DK]


def _gla_one(bi, r0, q_ref, k_ref, v_ref, r_ref, glr_ref, wg_ref, bg_ref, gn_ref, tri_ref, mstack_ref,
             masks_ref, hk_ref, hs_ref, o_ref, st_s, L, nlev, G):
    H, DK, DV = GLA_HEADS, GLA_DK, GLA_DV
    packed_rows = L % (2 * SUBLANE) == 0
    rw = pl.ds(r0, L)
    q = q_ref[bi, rw] * (DK ** -0.5)
    k = k_ref[bi, rw]
    kwid = H * DK
    g_hi, g_lo = _pieces(glr_ref[bi, rw], 2)
    pre2 = jnp.dot(g_hi.astype(BF16), wg_ref[...], preferred_element_type=F32)
    pre = (pre2[:, :kwid] + pre2[:, kwid:] + jnp.dot(g_lo.astype(BF16), wg_ref[:, :kwid], preferred_element_type=F32)
           + bg_ref[...])
    lg = jax.nn.log_sigmoid(pre) * (1.0 / GLA_TAU)
    Gc = _dot_sel(tri_ref[...], lg)
    E = jnp.exp(_dot_sel(mstack_ref[...], lg, pieces=2))
    hk = hk_ref[...]
    tok = lax.broadcasted_iota(jnp.int32, (L, kwid), 0)

    att = None
    for l in range(nlev + 1):
        if l == 0:
            ql, kl = q, k
        else:
            e_l = E[(l - 1) * L:l * L]
            upper = ((tok >> (l - 1)) & 1) == 1
            ql = jnp.where(upper, q * e_l, 0.0)
            kl = jnp.where(upper, 0.0, k * e_l)
        if packed_rows:
            kst = jnp.concatenate([kl.astype(BF16)] * H, axis=0) * hk
        else:
            kst = (jnp.concatenate([kl] * H, axis=0) * hk.astype(F32)).astype(BF16)
        part = lax.dot_general(ql.astype(BF16), kst, NT, preferred_element_type=F32)
        part = jnp.where(masks_ref[l] > 0.5, part, 0.0)
        att = part if att is None else att + part

    v = v_ref[bi, rw]
    vsrc = v.astype(BF16) if packed_rows else v
    zero = jnp.zeros((L, DV), vsrc.dtype)
    vbd = jnp.concatenate(
        [jnp.concatenate([zero] * h + [vsrc[:, h * DV:(h + 1) * DV]] + [zero] * (H - 1 - h), axis=1)
         for h in range(H)], axis=0).astype(BF16)
    S = L // G
    seg = lambda a, g: a[g * S:(g + 1) * S]
    qg = q * jnp.exp(Gc)
    o = jnp.dot(att.astype(BF16), vbd, preferred_element_type=F32) + jnp.concatenate(
        [lax.dot_general(seg(qg, g).astype(BF16), st_s[bi * G + g].astype(BF16), NT, preferred_element_type=F32)
         for g in range(G)], axis=0)
    for h in range(H):
        vs = slice(h * DV, (h + 1) * DV)
        oh = o[:, vs]
        oh = oh * lax.rsqrt(jnp.mean(oh * oh, axis=-1, keepdims=True) + EPS)
        oh = oh * gn_ref[:, vs] * jax.nn.silu(r_ref[bi, rw, vs])
        o_ref[bi, rw, vs] = oh.astype(o_ref.dtype)
    for g in range(G):
        g_seg = seg(Gc, g)
        g_end = g_seg[S - 1:S, :]
        k_end = (seg(k, g) * jnp.exp(g_end - g_seg)).astype(BF16)
        upd = lax.dot_general(seg(v, g).astype(BF16), k_end, TN, preferred_element_type=F32)
        st_s[bi * G + g] = st_s[bi * G + g] * jnp.exp(g_end) + jnp.where(hs_ref[...] > 0.5, upd, 0.0)


def _gla(z3, wg, bg, gn, s0, S, G, Bb, nsub):
    nseq, t_seq, width = z3.shape
    if G > 1:
        assert t_seq == S and nsub == 1
        z3 = z3.reshape(nseq // G, G * S, width)
    b, t, _ = z3.shape
    L = G * S
    nc = t // (L * nsub)
    nlev, tri, mstack, masks = _gla_tables(S, G)
    hk, hs = _gla_consts(L)
    qk_w = GLA_HEADS * GLA_DK
    v_w = GLA_HEADS * GLA_DV
    col = lambda j: (lambda bi, ci: (bi, ci, j))
    const2 = lambda bi, ci: (0, 0)
    const3 = lambda bi, ci: (0, 0, 0)
    o, s_new = pl.pallas_call(
        functools.partial(_gla_kernel, L=L, nlev=nlev, Bb=Bb, nsub=nsub, G=G),
        grid=(b // Bb, nc),
        in_specs=[pl.BlockSpec((Bb, nsub * L, qk_w), col(0)),
                  pl.BlockSpec((Bb, nsub * L, qk_w), col(ODD_K // qk_w)),
                  pl.BlockSpec((Bb, nsub * L, v_w), col(ODD_V // v_w)),
                  pl.BlockSpec((Bb, nsub * L, v_w), col(ODD_R // v_w)),
                  pl.BlockSpec((Bb, nsub * L, LANE), col(ODD_GATE // LANE)),
                  pl.BlockSpec((LANE, 2 * qk_w), const2),
                  pl.BlockSpec((1, qk_w), const2),
                  pl.BlockSpec((1, v_w), const2),
                  pl.BlockSpec((L, L), const2),
                  pl.BlockSpec(mstack.shape, const2),
                  pl.BlockSpec(masks.shape, const3),
                  pl.BlockSpec(hk.shape, const2),
                  pl.BlockSpec(hs.shape, const2),
                  pl.BlockSpec((Bb * G, GLA_HEADS, GLA_DK, GLA_DV), lambda bi, ci: (bi, 0, 0, 0))],
        out_specs=[pl.BlockSpec((Bb, nsub * L, v_w), col(0)),
                   pl.BlockSpec((Bb * G, GLA_HEADS, GLA_DK, GLA_DV), lambda bi, ci: (bi, 0, 0, 0))],
        out_shape=[jax.ShapeDtypeStruct((b, t, v_w), BF16),
                   jax.ShapeDtypeStruct((nseq, GLA_HEADS, GLA_DK, GLA_DV), F32)],
        scratch_shapes=[pltpu.VMEM((Bb * G, v_w, qk_w), F32)],
        compiler_params=_params(("parallel", "arbitrary")),
        name="gla",
    )(z3, z3, z3, z3, z3, wg, bg, gn, tri, mstack, masks, hk, hs, s0)
    return o.reshape(nseq, t_seq, v_w), s_new


def _s5_kernel(u_ref, bbre_ref, bbim_ref, cre_ref, cim_ref, d_ref, wglu_ref, bglu_ref, are_ref, aim_ref,
               x0re_ref, x0im_ref,
               o_ref, xore_ref, xoim_ref,
               xr_s, xi_s, cr_s, ci_s, pre_ref, pim_ref, *, rows, sequential):
    c = pl.program_id(1)
    lead = rows // SUBLANE if sequential else SUBLANE
    inner = rows // lead
    cw = S5_LANES // 4 if inner == SUBLANE else LANE
    u = jnp.swapaxes(u_ref[0].reshape(inner, lead, S5_WIDTH), 0, 1).reshape(rows, S5_WIDTH)
    ub = u.astype(BF16)

    if sequential:
        @pl.when(c == 0)
        def _():
            pre_ref[0] = are_ref[...]
            pim_ref[0] = aim_ref[...]
            n = 1
            while n < lead:
                nr, ni = pre_ref[n - 1], pim_ref[n - 1]
                sr, si = pre_ref[0:n], pim_ref[0:n]
                pre_ref[n:2 * n] = sr * nr - si * ni
                pim_ref[n:2 * n] = sr * ni + si * nr
                n *= 2

    hw, hl = S5_WIDTH // 2, S5_LANES // 2
    for half in range(2):
        uh = ub[:, half * hw:(half + 1) * hw]
        ls = slice(half * hl, (half + 1) * hl)
        xr_s[:, :, ls] = jnp.dot(uh, bbre_ref[half], preferred_element_type=F32).reshape(lead, inner, hl)
        xi_s[:, :, ls] = jnp.dot(uh, bbim_ref[half], preferred_element_type=F32).reshape(lead, inner, hl)

    for c0 in range(0, S5_LANES, cw):
        ar = jnp.broadcast_to(are_ref[:, c0:c0 + cw], (inner, cw))
        ai = jnp.broadcast_to(aim_ref[:, c0:c0 + cw], (inner, cw))

        def body(j, carry, c0=c0, ar=ar, ai=ai):
            xr, xi = carry
            nr = ar * xr - ai * xi + xr_s[j, :, c0:c0 + cw]
            ni = ar * xi + ai * xr + xi_s[j, :, c0:c0 + cw]
            xr_s[j, :, c0:c0 + cw] = nr
            xi_s[j, :, c0:c0 + cw] = ni
            return nr, ni

        if sequential:
            init = (jnp.zeros((inner, cw), F32), jnp.zeros((inner, cw), F32))
        else:
            init = (x0re_ref[0, :, c0:c0 + cw], x0im_ref[0, :, c0:c0 + cw])
        lax.fori_loop(0, lead, body, init, unroll=True)

    if sequential:
        @pl.when(c == 0)
        def _():
            cr_s[...] = x0re_ref[0]
            ci_s[...] = x0im_ref[0]

        end_r = xr_s[lead - 1]
        end_i = xi_s[lead - 1]
        anr = pre_ref[lead - 1]
        ani = pim_ref[lead - 1]
        ent_r, ent_i = [cr_s[...]], [ci_s[...]]
        for s in range(SUBLANE):
            pr, pi = ent_r[-1], ent_i[-1]
            ent_r.append(end_r[s:s + 1] + anr * pr - ani * pi)
            ent_i.append(end_i[s:s + 1] + anr * pi + ani * pr)
        cr_s[...] = ent_r[SUBLANE]
        ci_s[...] = ent_i[SUBLANE]
        er = jnp.concatenate(ent_r[:SUBLANE], axis=0)[None]
        ei = jnp.concatenate(ent_i[:SUBLANE], axis=0)[None]
        pr3, pi3 = pre_ref[...], pim_ref[...]
        xr = xr_s[...] + pr3 * er - pi3 * ei
        xi = xi_s[...] + pr3 * ei + pi3 * er

        @pl.when(c == pl.num_programs(1) - 1)
        def _():
            xore_ref[0] = cr_s[...]
            xoim_ref[0] = ci_s[...]
    else:
        xr = xr_s[...]
        xi = xi_s[...]
        xore_ref[0] = xr[lead - 1]
        xoim_ref[0] = xi[lead - 1]

    xrb = xr.reshape(rows, S5_LANES).astype(BF16)
    xib = xi.reshape(rows, S5_LANES).astype(BF16)
    y = jnp.concatenate(
        [jnp.dot(xrb[:, h * hl:(h + 1) * hl], cre_ref[h], preferred_element_type=F32)
         - jnp.dot(xib[:, h * hl:(h + 1) * hl], cim_ref[h], preferred_element_type=F32) for h in range(2)],
        axis=1) + d_ref[...] * u
    zz = jax.nn.gelu(y)
    gate = jnp.dot(zz.astype(BF16), wglu_ref[...], preferred_element_type=F32) + bglu_ref[...]
    out = zz * jax.nn.sigmoid(gate)
    o_ref[0] = jnp.swapaxes(out.reshape(lead, inner, S5_WIDTH), 0, 1).reshape(rows, S5_WIDTH).astype(o_ref.dtype)


def _s5(z3, tabs, x0re, x0im, rows, sequential):
    b, t, _ = z3.shape
    nblk = t // rows
    groups = rows // SUBLANE
    ublk = ODD_U // S5_WIDTH
    const2 = lambda bi, ci: (0, 0)
    const3 = lambda bi, ci: (0, 0, 0)
    lead = groups if sequential else SUBLANE
    hw, hl = S5_WIDTH // 2, S5_LANES // 2
    if sequential:
        st_spec = pl.BlockSpec((1, 1, S5_LANES), lambda bi, ci: (bi, 0, 0))
        st_shape = jax.ShapeDtypeStruct((b, 1, S5_LANES), F32)
    else:
        st_spec = pl.BlockSpec((1, groups, S5_LANES), lambda bi, ci: (bi, ci, 0))
        st_shape = jax.ShapeDtypeStruct(x0re.shape, F32)
    return pl.pallas_call(
        functools.partial(_s5_kernel, rows=rows, sequential=sequential),
        grid=(b, nblk),
        in_specs=[pl.BlockSpec((1, rows, S5_WIDTH), lambda bi, ci: (bi, ci, ublk)),
                  pl.BlockSpec((2, hw, hl), const3),
                  pl.BlockSpec((2, hw, hl), const3),
                  pl.BlockSpec((2, hl, hw), const3),
                  pl.BlockSpec((2, hl, hw), const3),
                  pl.BlockSpec((1, S5_WIDTH), const2),
                  pl.BlockSpec((S5_WIDTH, S5_WIDTH), const2),
                  pl.BlockSpec((1, S5_WIDTH), const2),
                  pl.BlockSpec((1, S5_LANES), const2),
                  pl.BlockSpec((1, S5_LANES), const2),
                  st_spec, st_spec],
        out_specs=[pl.BlockSpec((1, rows, S5_WIDTH), lambda bi, ci: (bi, ci, 0)),
                   st_spec, st_spec],
        out_shape=[jax.ShapeDtypeStruct((b, t, S5_WIDTH), BF16), st_shape, st_shape],
        scratch_shapes=[pltpu.VMEM((lead, rows // lead, S5_LANES), F32),
                        pltpu.VMEM((lead, rows // lead, S5_LANES), F32),
                        pltpu.VMEM((1, S5_LANES), F32),
                        pltpu.VMEM((1, S5_LANES), F32),
                        pltpu.VMEM((lead if sequential else 1, 1, S5_LANES), F32),
                        pltpu.VMEM((lead if sequential else 1, 1, S5_LANES), F32)],
        compiler_params=_params(("parallel", "arbitrary")),
        name="s5",
    )(z3, tabs['bbre'], tabs['bbim'], tabs['cre'], tabs['cim'], tabs['d'], tabs['wglu'], tabs['bglu'],
      tabs['are'], tabs['aim'], x0re, x0im)


def _s5_tables(a_re, a_im, log_dt, b_re, b_im, c_re, c_im, d, w_glu, b_glu):
    lam_re = a_re.astype(F32)
    lam_im = a_im.astype(F32)
    dt = jnp.exp(log_dt.astype(F32))[:, None]
    mag = jnp.exp(lam_re * dt)
    ang = lam_im * dt
    ab_re = mag * jnp.cos(ang)
    ab_im = mag * jnp.sin(ang)
    den = lam_re * lam_re + lam_im * lam_im
    co_re = ((ab_re - 1.0) * lam_re + ab_im * lam_im) / den
    co_im = (ab_im * lam_re - (ab_re - 1.0) * lam_im) / den
    b_re = b_re.astype(F32)
    b_im = b_im.astype(F32)
    bb_re = co_re[..., None] * b_re - co_im[..., None] * b_im
    bb_im = co_re[..., None] * b_im + co_im[..., None] * b_re
    gh = S5_GROUPS // 2
    eye = jnp.eye(gh, dtype=F32)
    hw, hl = S5_WIDTH // 2, S5_LANES // 2
    halves = lambda a: a.reshape((2, gh) + a.shape[1:])
    blockdiag_in = lambda bb: jnp.einsum('xgph,gk->xghkp', halves(bb), eye).reshape(2, hw, hl).astype(BF16)
    blockdiag_out = lambda cc: jnp.einsum('xghp,gk->xgpkh', halves(cc.astype(F32)), eye).reshape(2, hl, hw).astype(BF16)
    return {
        'bbre': blockdiag_in(bb_re), 'bbim': blockdiag_in(bb_im),
        'cre': blockdiag_out(c_re), 'cim': blockdiag_out(c_im),
        'd': d.astype(F32).reshape(1, S5_WIDTH),
        'wglu': w_glu.astype(BF16), 'bglu': b_glu.astype(F32).reshape(1, S5_WIDTH),
        'are': ab_re.reshape(1, S5_LANES), 'aim': ab_im.reshape(1, S5_LANES),
    }


def _rmsnorm_val(x, g):
    return x * lax.rsqrt(jnp.mean(x * x, axis=-1, keepdims=True) + EPS) * g


def _ffn_kernel(r_ref, a1_ref, a2_ref, w1_ref, w2_ref, g_ref, wup_ref, cw_ref, cb_ref, wd_ref, st_ref, gf_ref,
                o_ref, so_ref, halo_s, *, tm, tf, long_seq, final_norm):
    i = pl.program_id(1)
    n = tm // SUBLANE
    d = r_ref.shape[-1]
    x = (r_ref[0] + jnp.dot(a1_ref[0], w1_ref[...], preferred_element_type=F32)
         + jnp.dot(a2_ref[0], w2_ref[...], preferred_element_type=F32))
    hn = _rmsnorm_val(x, g_ref[...])
    if long_seq:
        lead = n

        @pl.when(i == 0)
        def _():
            halo_s[...] = st_ref[0]
    else:
        lead = SUBLANE
    hnp = jnp.swapaxes(hn.reshape(tm // lead, lead, d), 0, 1).reshape(tm, d).astype(BF16)
    inner = tm // lead
    sub = lax.broadcasted_iota(jnp.int32, (SUBLANE, tf), 0)

    acc = jnp.zeros((tm, d), F32)
    for c in range(D_FF // tf):
        conv = []
        for half in range(2):
            c0 = half * D_FF + c * tf
            u = jnp.dot(hnp, wup_ref[:, c0:c0 + tf], preferred_element_type=F32).reshape(lead, inner, tf)
            if long_seq:
                um1 = jnp.where(sub == 0, halo_s[1:2, c0:c0 + tf], pltpu.roll(u[lead - 1], 1, axis=0))
                um2 = jnp.where(sub == 0, halo_s[0:1, c0:c0 + tf], pltpu.roll(u[lead - 2], 1, axis=0))
                halo_s[0:1, c0:c0 + tf] = u[lead - 2][SUBLANE - 1:SUBLANE]
                halo_s[1:2, c0:c0 + tf] = u[lead - 1][SUBLANE - 1:SUBLANE]
            else:
                um2 = st_ref[0, :, c0:c0 + tf]
                um1 = st_ref[1, :, c0:c0 + tf]
                so_ref[0, :, c0:c0 + tf] = u[lead - 2]
                so_ref[1, :, c0:c0 + tf] = u[lead - 1]
            ext = jnp.concatenate([um2[None], um1[None], u], axis=0)
            cv = cb_ref[:, c0:c0 + tf]
            for j in range(CONV_W):
                cv = cv + ext[j:j + lead] * cw_ref[j:j + 1, c0:c0 + tf]
            conv.append(cv)
        act = (jax.nn.silu(conv[1]) * conv[0]).reshape(tm, tf).astype(BF16)
        acc = acc + jnp.dot(act, wd_ref[c * tf:(c + 1) * tf, :], preferred_element_type=F32)

    out = x + jnp.swapaxes(acc.reshape(lead, inner, d), 0, 1).reshape(tm, d)
    if final_norm:
        out = _rmsnorm_val(out, gf_ref[...])
    o_ref[0] = out

    if long_seq:
        @pl.when(i == pl.num_programs(1) - 1)
        def _():
            so_ref[0] = halo_s[...]


def _ffn(res3, a1, a2, w1, w2, g, wup, cw, cb, wd, state, layer, gfinal, tm, tf, long_seq, final_norm):
    b, t, d = res3.shape
    nt = t // tm
    resident = dict(pipeline_mode=pl.Buffered(1))
    const2 = lambda bi, i: (0, 0)
    row_blk = lambda w: pl.BlockSpec((1, tm, w), lambda bi, i: (bi, i, 0))
    per_layer = lambda shape, **kw: pl.BlockSpec((None,) + shape, lambda bi, i: (layer, 0, 0), **kw)
    if long_seq:
        st_in = st_out = pl.BlockSpec((1, CONV_W - 1, 2 * D_FF), lambda bi, i: (bi, 0, 0))
        st_shape = state.shape
    else:
        nseq = tm // SUBLANE
        st_in = pl.BlockSpec((None, CONV_W - 1, nseq, 2 * D_FF), lambda bi, i: (layer, 0, i, 0))
        st_out = pl.BlockSpec((CONV_W - 1, nseq, 2 * D_FF), lambda bi, i: (0, i, 0))
        st_shape = state.shape[1:]
    return pl.pallas_call(
        functools.partial(_ffn_kernel, tm=tm, tf=tf, long_seq=long_seq, final_norm=final_norm),
        grid=(b, nt),
        in_specs=[row_blk(d), row_blk(a1.shape[-1]), row_blk(a2.shape[-1]),
                  pl.BlockSpec(w1.shape, const2, **resident),
                  pl.BlockSpec(w2.shape, const2, **resident),
                  per_layer((1, d)),
                  per_layer((d, 2 * D_FF), **resident),
                  per_layer((CONV_W, 2 * D_FF)),
                  per_layer((1, 2 * D_FF)),
                  per_layer((D_FF, d), **resident),
                  st_in,
                  pl.BlockSpec((1, d), const2)],
        out_specs=[row_blk(d), st_out],
        out_shape=[jax.ShapeDtypeStruct((b, t, d), F32), jax.ShapeDtypeStruct(st_shape, F32)],
        scratch_shapes=[pltpu.VMEM((CONV_W - 1, 2 * D_FF), F32)],
        compiler_params=_params(("parallel", "arbitrary")),
        name="ffn",
    )(res3, a1, a2, w1, w2, g.reshape(-1, 1, d), wup, cw, cb.reshape(-1, 1, 2 * D_FF), wd, state, gfinal)


def _rope_tables(pos):
    half = SW_HD // 2
    inv = ROPE_THETA ** (-jnp.arange(half, dtype=F32) / half)
    ang = pos.astype(F32)[:, None] * inv[None, :]
    cos = jnp.cos(ang)
    sin = jnp.sin(ang)
    reps = LANE // SW_HD
    cos_t = jnp.tile(jnp.concatenate([cos, cos], axis=1), (1, reps))
    sin_t = jnp.tile(jnp.concatenate([-sin, sin], axis=1), (1, reps))
    return cos_t, sin_t


def _prep_weights(W):
    P = {}
    we = W['w_in_even']
    nh = 4 * ML_HEADS * ML_DK
    head, g_m, tail = we[:, :nh], we[:, nh:nh + 2 * ML_HEADS], we[:, nh + 2 * ML_HEADS:]
    q_a, k_a, v_a = tail[:, :512], tail[:, 512:640], tail[:, 640:768]
    dup = lambda w: jnp.concatenate([w[:, :SW_HD], w[:, :SW_HD], w[:, SW_HD:], w[:, SW_HD:]], axis=1)
    pad = jnp.zeros((D_MODEL, LANE - 2 * ML_HEADS), we.dtype)
    P['w_in_even'] = jnp.concatenate([head, q_a, dup(k_a), dup(v_a), g_m, pad], axis=1).astype(BF16)
    assert P['w_in_even'].shape[1] == EVEN_PAD
    P['b_gates'] = jnp.pad(W['b_mlstm_gates'].astype(F32), (0, LANE - 2 * ML_HEADS)).reshape(1, LANE)
    wo = W['w_in_odd']
    nh = 2 * GLA_HEADS * GLA_DK + 2 * GLA_HEADS * GLA_DV
    head, glr, u = wo[:, :nh], wo[:, nh:nh + GLA_RANK], wo[:, nh + GLA_RANK:]
    pad = jnp.zeros((D_MODEL, ODD_PAD - ODD_GATE - GLA_RANK), wo.dtype)
    P['w_in_odd'] = jnp.concatenate([head, u, glr, pad], axis=1).astype(BF16)
    wg = jnp.pad(W['w_gla_gate_up'].astype(F32), ((0, LANE - GLA_RANK), (0, 0)))
    wg_hi = wg.astype(BF16)
    P['w_gate_up'] = jnp.concatenate([wg_hi, (wg - wg_hi.astype(F32)).astype(BF16)], axis=1)
    P['b_gate'] = W['b_gla_gate'].astype(F32).reshape(1, -1)
    P['g_gla'] = W['g_gla_norm'].astype(F32).reshape(1, -1)
    hm = ML_HEADS * ML_DV
    P['w_out_even'] = (W['w_out_even'][:hm].astype(BF16), W['w_out_even'][hm:].astype(BF16))
    hg = GLA_HEADS * GLA_DV
    P['w_out_odd'] = (W['w_out_odd'][:hg].astype(BF16), W['w_out_odd'][hg:].astype(BF16))
    P['w_ffn_up'] = W['w_ffn_up'].astype(BF16)
    P['w_ffn_down'] = W['w_ffn_down'].astype(BF16)
    P['s5'] = _s5_tables(W['s5_a_re'], W['s5_a_im'], W['s5_log_dt'], W['s5_b_re'], W['s5_b_im'],
                         W['s5_c_re'], W['s5_c_im'], W['s5_d'], W['w_s5_glu'], W['b_s5_glu'])
    return P


def _trunk(x, pos, st, W, P, is_prompt):
    b, t, d = x.shape
    n = b * t
    L = next((c for c in (2 * CHUNK, CHUNK) if t % c == 0), t)
    nsub = next((c for c in (8, 4, 2) if t % (c * L) == 0 and c * L <= 8 * CHUNK), 1)
    tm = next((c for c in (1024, 512) if n % c == 0), n)
    cos, sin_signed = _rope_tables(pos)
    gfinal = W['norm_final'].astype(F32).reshape(1, d)
    new = {}
    conv_out = []
    h = x.reshape(n, d)

    z = _norm_matmul(h, W['norm_mix'][0], P['w_in_even'], tm)
    z3 = z.reshape(b, t, EVEN_PAD)
    pack = dict(G=1, Bb=2 if b % 2 == 0 else 1, nsub=nsub) if (is_prompt or b % 16) else dict(G=8, Bb=2, nsub=1)
    h_m, new['ml_C'], new['ml_n'], m_new = _mlstm(z3, P['b_gates'], st['ml_C'], st['ml_n'], st['ml_m'], L, **pack)
    new['ml_m'] = m_new.reshape(b, ML_HEADS)
    if is_prompt:
        h_a, kb = _swa_prompt(z3, cos, sin_signed, W['sw_sinks'].astype(F32), Bb=2 if b % 2 == 0 else 1)
        vb = jnp.concatenate([z3[:, t - WINDOW:, EVEN_VD:EVEN_VD + SW_HD],
                              z3[:, t - WINDOW:, EVEN_VD + LANE:EVEN_VD + LANE + SW_HD]], axis=-1)
    else:
        h_a, kb, vb = _swa_sample(z3, st['kbuf'].reshape(b, WINDOW, -1), st['vbuf'].reshape(b, WINDOW, -1),
                                  cos, sin_signed, W['sw_sinks'].astype(F32), Bb=8)
    new['kbuf'] = kb.reshape(b, WINDOW, SW_KV_HEADS, SW_HD)
    new['vbuf'] = vb.reshape(b, WINDOW, SW_KV_HEADS, SW_HD)
    h, cb = _ffn_layer(h, h_m, h_a, P['w_out_even'], 0, st, W, P, b, t, gfinal, is_prompt, final_norm=False)
    conv_out.append(cb)

    z = _norm_matmul(h, W['norm_mix'][1], P['w_in_odd'], tm)
    z3 = z.reshape(b, t, ODD_PAD)
    o_g, new['gla'] = _gla(z3, P['w_gate_up'], P['b_gate'], P['g_gla'], st['gla'], L, **pack)
    x0re = st['s5_re'].astype(F32).reshape(b, 1, S5_LANES)
    x0im = st['s5_im'].astype(F32).reshape(b, 1, S5_LANES)
    if is_prompt:
        o_s, xre, xim = _s5(z3, P['s5'], x0re, x0im, rows=next((c for c in (512, 256) if t % c == 0), t),
                            sequential=True)
    else:
        o_s, xre, xim = _s5(z3.reshape(1, n, ODD_PAD), P['s5'], x0re.reshape(1, b, S5_LANES),
                            x0im.reshape(1, b, S5_LANES), rows=256 if n % 256 == 0 else n, sequential=False)
    new['s5_re'] = xre.reshape(b, S5_GROUPS, S5_STATE)
    new['s5_im'] = xim.reshape(b, S5_GROUPS, S5_STATE)
    h, cb = _ffn_layer(h, o_g, o_s, P['w_out_odd'], 1, st, W, P, b, t, gfinal, is_prompt, final_norm=True)
    conv_out.append(cb)
    new['conv'] = jnp.stack(conv_out)
    return h.reshape(b, t, d), new


def _ffn_layer(h, a1, a2, w_out, layer, st, W, P, b, t, gfinal, is_prompt, final_norm):
    n, d = h.shape
    rows = t if is_prompt else n
    tm = next((c for c in (512, 256) if rows % c == 0), rows)
    args = (*w_out, W['norm_ffn'], P['w_ffn_up'], W['ffn_conv_w'].astype(F32), W['ffn_conv_b'].astype(F32),
            P['w_ffn_down'])
    if is_prompt:
        shape3 = lambda a: a.reshape(b, t, -1)
        out, new_state = _ffn(shape3(h), shape3(a1), shape3(a2), *args, st['conv'][layer], layer, gfinal, tm,
                              FFN_TF, True, final_norm)
    else:
        assert t == SUBLANE
        shape3 = lambda a: a.reshape(1, n, -1)
        out, new_state = _ffn(shape3(h), shape3(a1), shape3(a2), *args, st['conv_rows'], layer, gfinal, tm, FFN_TF,
                              False, final_norm)
        new_state = jnp.swapaxes(new_state, 0, 1)
    return out.reshape(n, d), new_state


def kernel(x_prompt, x_sample, state_mlstm_C, state_mlstm_n, state_mlstm_m, cache_swa_k, cache_swa_v,
           state_gla, state_s5_re, state_s5_im, state_ffn_conv, norm_mix, norm_ffn, norm_final,
           w_in_even, b_mlstm_gates, sw_sinks, w_out_even, w_in_odd, w_gla_gate_up, b_gla_gate, g_gla_norm,
           s5_a_re, s5_a_im, s5_log_dt, s5_b_re, s5_b_im, s5_c_re, s5_c_im, s5_d, w_s5_glu, b_s5_glu,
           w_out_odd, w_ffn_up, ffn_conv_w, ffn_conv_b, w_ffn_down):
    W = {'norm_mix': norm_mix.astype(F32), 'norm_ffn': norm_ffn.astype(F32), 'norm_final': norm_final,
         'w_in_even': w_in_even, 'b_mlstm_gates': b_mlstm_gates, 'sw_sinks': sw_sinks, 'w_out_even': w_out_even,
         'w_in_odd': w_in_odd, 'w_gla_gate_up': w_gla_gate_up, 'b_gla_gate': b_gla_gate, 'g_gla_norm': g_gla_norm,
         's5_a_re': s5_a_re, 's5_a_im': s5_a_im, 's5_log_dt': s5_log_dt, 's5_b_re': s5_b_re, 's5_b_im': s5_b_im,
         's5_c_re': s5_c_re, 's5_c_im': s5_c_im, 's5_d': s5_d, 'w_s5_glu': w_s5_glu, 'b_s5_glu': b_s5_glu,
         'w_out_odd': w_out_odd, 'w_ffn_up': w_ffn_up, 'ffn_conv_w': ffn_conv_w, 'ffn_conv_b': ffn_conv_b,
         'w_ffn_down': w_ffn_down}
    P = _prep_weights(W)
    bp, tp = x_prompt.shape[:2]
    st_prompt = {'ml_C': jnp.zeros((bp, ML_HEADS, ML_DK, ML_DV), F32),
                 'ml_n': jnp.zeros((bp, ML_HEADS, ML_DK), F32),
                 'ml_m': jnp.zeros((bp, ML_HEADS), F32),
                 'gla': jnp.zeros((bp, GLA_HEADS, GLA_DK, GLA_DV), F32),
                 's5_re': jnp.zeros((bp, S5_GROUPS, S5_STATE), F32),
                 's5_im': jnp.zeros((bp, S5_GROUPS, S5_STATE), F32),
                 'conv': jnp.zeros((2, bp, CONV_W - 1, 2 * D_FF), F32)}
    st_sample = {'ml_C': state_mlstm_C.astype(F32), 'ml_n': state_mlstm_n.astype(F32),
                 'ml_m': state_mlstm_m.astype(F32), 'kbuf': cache_swa_k.astype(F32),
                 'vbuf': cache_swa_v.astype(F32), 'gla': state_gla.astype(F32),
                 's5_re': state_s5_re, 's5_im': state_s5_im,
                 'conv_rows': jnp.swapaxes(state_ffn_conv.astype(F32), 1, 2)}
    y_p, np_ = _trunk(x_prompt.astype(F32), jnp.arange(tp), st_prompt, W, P, True)
    y_s, ns_ = _trunk(x_sample.astype(F32), PAST_LEN + jnp.arange(x_sample.shape[1]), st_sample, W, P, False)
    return (y_p, y_s,
            np_['ml_C'], ns_['ml_C'], np_['ml_n'], ns_['ml_n'], np_['ml_m'], ns_['ml_m'],
            np_['kbuf'], ns_['kbuf'], np_['vbuf'], ns_['vbuf'], np_['gla'], ns_['gla'],
            np_['s5_re'], ns_['s5_re'], np_['s5_im'], ns_['s5_im'], np_['conv'], ns_['conv'])
```

```python
import functools
import math

import numpy as np
import jax
import jax.numpy as jnp
from jax import lax
from jax.experimental import pallas as pl
from jax.experimental.pallas import tpu as pltpu

F32 = jnp.float32
BF16 = jnp.bfloat16
NT = (((1,), (1,)), ((), ()))
TN = (((0,), (0,)), ((), ()))

D_MODEL = 1024
PAST_LEN = 16384
ML_HEADS, ML_DK, ML_DV = 4, 128, 128
SW_HEADS, SW_KV_HEADS, SW_HD, WINDOW = 8, 2, 64, 128
SW_GQ = SW_HEADS // SW_KV_HEADS
ROPE_THETA = 10000.0
GLA_HEADS, GLA_DK, GLA_DV, GLA_RANK, GLA_TAU = 4, 64, 128, 16, 16.0
S5_WIDTH, S5_GROUP, S5_STATE = 512, 16, 64
S5_GROUPS = S5_WIDTH // S5_GROUP
S5_LANES = S5_GROUPS * S5_STATE
D_FF = 2816
FFN_TF = 2816
CONV_W = 3
CHUNK = 64
EPS = 1e-6

LANE = 128
SUBLANE = 8
VMEM_LIMIT = 56 * 1024 * 1024

EVEN_QA, EVEN_KD, EVEN_VD, EVEN_PAD = 2048, 2560, 2816, 3200
ODD_K, ODD_V, ODD_R, ODD_U, ODD_GATE, ODD_PAD = 256, 512, 1024, 1536, 2048, 2176


def _params(sem):
    return pltpu.CompilerParams(dimension_semantics=sem, vmem_limit_bytes=VMEM_LIMIT)


def _norm_matmul_kernel(x_ref, g_ref, w_ref, o_ref):
    x = x_ref[...]
    ms = jnp.mean(x * x, axis=-1, keepdims=True)
    hn = (x * lax.rsqrt(ms + EPS) * g_ref[...]).astype(BF16)
    o_ref[...] = jnp.dot(hn, w_ref[...], preferred_element_type=F32)


def _norm_matmul(x, g, w, tm):
    n, d = x.shape
    c = w.shape[1]
    return pl.pallas_call(
        _norm_matmul_kernel,
        grid=(n // tm,),
        in_specs=[pl.BlockSpec((tm, d), lambda i: (i, 0)),
                  pl.BlockSpec((1, d), lambda i: (0, 0)),
                  pl.BlockSpec((d, c), lambda i: (0, 0))],
        out_specs=pl.BlockSpec((tm, c), lambda i: (i, 0)),
        out_shape=jax.ShapeDtypeStruct((n, c), F32),
        compiler_params=_params(("parallel",)),
        name="norm_matmul",
    )(x, g.reshape(1, d), w)


def _pieces(x, n):
    out, r = [], x
    for _ in range(n):
        p = r.astype(BF16).astype(F32)
        out.append(p)
        r = r - p
    return out


def _dot_sel(m_bf16, x, dims=None, pieces=3):
    ps = _pieces(x, pieces)
    axis = 1 if dims is None else 0
    w = x.shape[axis]
    if w % LANE:
        f = (lambda p: jnp.dot(m_bf16, p, preferred_element_type=F32)) if dims is None else (
            lambda p: lax.dot_general(m_bf16, p, dims, preferred_element_type=F32))
        return sum(f(p.astype(BF16)) for p in ps)
    stacked = jnp.concatenate(ps, axis=axis).astype(BF16)
    if dims is None:
        r = jnp.dot(m_bf16, stacked, preferred_element_type=F32)
    else:
        r = lax.dot_general(m_bf16, stacked, dims, preferred_element_type=F32)
    return sum(r[:, i * w:(i + 1) * w] for i in range(pieces))


def _sel_right(x, e_bf16, pieces=3):
    rows = x.shape[0]
    r = jnp.dot(jnp.concatenate(_pieces(x, pieces), axis=0).astype(BF16), e_bf16, preferred_element_type=F32)
    return sum(r[i * rows:(i + 1) * rows] for i in range(pieces))


def _mlstm_kernel(q_ref, k_ref, v_ref, og_ref, gz_ref, bias_ref, tri_ref, ones_ref, expc_ref, expv_ref, eyet_ref,
                  seg_ref, segv_ref, hmk_ref, c0_ref, n0_ref, m0_ref,
                  h_ref, co_ref, no_ref, mo_ref,
                  c_s, n_s, m_s, *, L, Bb, nsub, G):
    c = pl.program_id(1)

    @pl.when(c == 0)
    def _():
        c_s[...] = c0_ref[...]
        n_s[...] = n0_ref[...]
        m_s[...] = m0_ref[...]

    for sub in range(nsub):
        for bi in range(Bb):
            _mlstm_one(bi, sub * L, q_ref, k_ref, v_ref, og_ref, gz_ref, bias_ref, tri_ref, ones_ref, expc_ref,
                       expv_ref, eyet_ref, seg_ref, segv_ref, hmk_ref, h_ref, c_s, n_s, m_s, L, G)

    @pl.when(c == pl.num_programs(1) - 1)
    def _():
        co_ref[...] = c_s[...]
        no_ref[...] = n_s[...]
        mo_ref[...] = m_s[...]


def _mlstm_one(bi, r0, q_ref, k_ref, v_ref, og_ref, gz_ref, bias_ref, tri_ref, ones_ref, expc_ref, expv_ref,
               eyet_ref, seg_ref, segv_ref, hmk_ref, h_ref, c_s, n_s, m_s, L, G):
    H, DK = ML_HEADS, ML_DK
    HL = H * L
    S = L // G
    seg = lambda a, g: a[g * S:(g + 1) * S]
    per_row = lambda rows: rows[0] if G == 1 else jnp.concatenate(
        [jnp.broadcast_to(r, (S, r.shape[1])) for r in rows], axis=0)
    lane = lax.broadcasted_iota(jnp.int32, (L, LANE), 1)
    rowi = lax.broadcasted_iota(jnp.int32, (L, LANE), 0) % S
    keep = lane < H
    heads_only = lambda x: jnp.where(keep[:x.shape[0]], x, 0.0)

    rw = pl.ds(r0, L)
    gates = gz_ref[bi, rw] + bias_ref[...]
    lf = jax.nn.log_sigmoid(gates)
    b = pltpu.roll(_dot_sel(tri_ref[...], lf), LANE - H, axis=1)
    vv = gates - b
    cm = vv
    sh = 1
    while sh < S:
        cm = jnp.maximum(cm, jnp.where(rowi >= sh, pltpu.roll(cm, sh, axis=0), -jnp.inf))
        sh *= 2
    m_prev_g = [m_s[bi * G + g] for g in range(G)]
    m_prev = per_row(m_prev_g)
    mt = b + jnp.maximum(m_prev, cm)
    a_inter = jnp.exp(b + m_prev - mt)

    expv = expv_ref[...]
    m_new_g = [seg(mt, g)[S - 1:S, :] for g in range(G)]
    b_end_g = [seg(b, g)[S - 1:S, :] for g in range(G)]
    w_end = heads_only(jnp.exp(per_row(b_end_g) - b + gates - per_row(m_new_g)))
    decay_g = [heads_only(jnp.exp(b_end_g[g] + m_prev_g[g] - m_new_g[g])) for g in range(G)]
    decay_rows = jnp.concatenate(decay_g + [jnp.zeros((-G % SUBLANE, LANE), F32)] * (G % SUBLANE != 0), axis=0)
    spread_k = _sel_right(jnp.concatenate([heads_only(b - mt), heads_only(vv)], axis=0), expc_ref[...])
    uc = spread_k[:L]
    vr = _dot_sel(ones_ref[...], spread_k[L:] * eyet_ref[...])
    spread_v = _sel_right(jnp.concatenate([heads_only(a_inter), w_end, decay_rows], axis=0), expv, pieces=2)
    ac, wc, dcs = spread_v[:L], spread_v[L:2 * L], spread_v[2 * L:2 * L + G]
    ti = lax.broadcasted_iota(jnp.int32, (L, HL), 0)
    si = lax.broadcasted_iota(jnp.int32, (L, HL), 1) % L
    causal = (si <= ti) & (si // S == ti // S)
    w = jnp.exp(jnp.where(causal, uc + vr, -jnp.inf))

    q = q_ref[bi, rw]
    ks = k_ref[bi, rw] * (DK ** -0.5)
    v = v_ref[bi, rw]
    qb = q.astype(BF16)
    if L % (2 * SUBLANE) == 0:
        kst = jnp.concatenate([ks.astype(BF16)] * H, axis=0) * hmk_ref[...]
        vsrc = v.astype(BF16)
    else:
        kst = (jnp.concatenate([ks] * H, axis=0) * hmk_ref[...].astype(F32)).astype(BF16)
        vsrc = v
    zero = jnp.zeros((L, DK), vsrc.dtype)
    vbd = jnp.concatenate(
        [jnp.concatenate([zero] * h + [vsrc[:, h * DK:(h + 1) * DK]] + [zero] * (H - 1 - h), axis=1)
         for h in range(H)], axis=0).astype(BF16)
    s = lax.dot_general(qb, kst, NT, preferred_element_type=F32) * w
    qc = jnp.concatenate(
        [jnp.concatenate([jnp.dot(seg(q, g)[:, h * DK:(h + 1) * DK].astype(BF16), c_s[bi * G + g, h].astype(BF16),
                                  preferred_element_type=F32) for h in range(H)], axis=1)
         for g in range(G)], axis=0)
    num = jnp.dot(s.astype(BF16), vbd, preferred_element_type=F32) + ac * qc
    n_rows = per_row([n_s[bi * G + g] for g in range(G)])
    den = (_sel_right(s, seg_ref[...], pieces=2)
           + a_inter * _sel_right(q * n_rows, segv_ref[...], pieces=2))
    rden = heads_only(1.0 / jnp.maximum(jnp.abs(den), jnp.exp(-mt)))
    hh = num * _sel_right(rden, expv, pieces=2) * jax.nn.sigmoid(og_ref[bi, rw])
    h_ref[bi, rw] = hh.astype(h_ref.dtype)

    kw = ks * wc
    for g in range(G):
        e = bi * G + g
        kwb = seg(kw, g).astype(BF16)
        vb = seg(v, g).astype(BF16)
        dc = dcs[g:g + 1]
        for h in range(H):
            hs = slice(h * DK, (h + 1) * DK)
            c_s[e, h] = dc[:, hs] * c_s[e, h] + lax.dot_general(kwb[:, hs], vb[:, hs], TN, preferred_element_type=F32)
        n_s[e] = dc * n_s[e] + jnp.sum(seg(kw, g), axis=0, keepdims=True)
        m_s[e] = heads_only(m_new_g[g])


def _mlstm_consts(S, G):
    L = S * G
    H, DK = ML_HEADS, ML_DK
    expc = np.zeros((LANE, H * L), np.float32)
    expv = np.zeros((LANE, H * DK), np.float32)
    seg = np.zeros((H * L, LANE), np.float32)
    segv = np.zeros((H * DK, LANE), np.float32)
    hmk = np.zeros((H * L, H * DK), np.float32)
    for h in range(H):
        expc[h, h * L:(h + 1) * L] = 1.0
        expv[h, h * DK:(h + 1) * DK] = 1.0
        seg[h * L:(h + 1) * L, h] = 1.0
        segv[h * DK:(h + 1) * DK, h] = 1.0
        hmk[h * L:(h + 1) * L, h * DK:(h + 1) * DK] = 1.0
    eyet = np.tile(np.eye(L, dtype=np.float32), (1, H))
    bf = lambda a: jnp.asarray(a, BF16)
    tri = np.kron(np.eye(G), np.tril(np.ones((S, S)))).astype(np.float32)
    return (bf(tri), bf(np.ones((L, L), np.float32)), bf(expc), bf(expv),
            jnp.asarray(eyet), bf(seg), bf(segv), bf(hmk))


def _mlstm(z3, bias, c0, n0, m0, S, G, Bb, nsub):
    nseq, t_seq, width = z3.shape
    if G > 1:
        assert t_seq == S and nsub == 1
        z3 = z3.reshape(nseq // G, G * S, width)
    b, t, _ = z3.shape
    L = G * S
    nc = t // (L * nsub)
    hw = ML_HEADS * ML_DK
    consts = _mlstm_consts(S, G)
    gate_blk = (EVEN_PAD - LANE) // LANE
    col = lambda j: (lambda bi, ci: (bi, ci, j))
    const2 = lambda bi, ci: (0, 0)
    state3 = lambda bi, ci: (bi, 0, 0)
    m0p = jnp.pad(m0, ((0, 0), (0, LANE - ML_HEADS))).reshape(nseq, 1, LANE)
    Bs = Bb * G
    h, c_new, n_new, m_new = pl.pallas_call(
        functools.partial(_mlstm_kernel, L=L, Bb=Bb, nsub=nsub, G=G),
        grid=(b // Bb, nc),
        in_specs=[pl.BlockSpec((Bb, nsub * L, hw), col(0)),
                  pl.BlockSpec((Bb, nsub * L, hw), col(1)),
                  pl.BlockSpec((Bb, nsub * L, hw), col(2)),
                  pl.BlockSpec((Bb, nsub * L, hw), col(3)),
                  pl.BlockSpec((Bb, nsub * L, LANE), col(gate_blk)),
                  pl.BlockSpec((1, LANE), const2)]
                 + [pl.BlockSpec(a.shape, const2) for a in consts]
                 + [pl.BlockSpec((Bs, ML_HEADS, ML_DK, ML_DV), lambda bi, ci: (bi, 0, 0, 0)),
                    pl.BlockSpec((Bs, 1, hw), state3),
                    pl.BlockSpec((Bs, 1, LANE), state3)],
        out_specs=[pl.BlockSpec((Bb, nsub * L, hw), col(0)),
                   pl.BlockSpec((Bs, ML_HEADS, ML_DK, ML_DV), lambda bi, ci: (bi, 0, 0, 0)),
                   pl.BlockSpec((Bs, 1, hw), state3),
                   pl.BlockSpec((Bs, 1, LANE), state3)],
        out_shape=[jax.ShapeDtypeStruct((b, t, hw), BF16),
                   jax.ShapeDtypeStruct((nseq, ML_HEADS, ML_DK, ML_DV), F32),
                   jax.ShapeDtypeStruct((nseq, 1, hw), F32),
                   jax.ShapeDtypeStruct((nseq, 1, LANE), F32)],
        scratch_shapes=[pltpu.VMEM((Bs, ML_HEADS, ML_DK, ML_DV), F32),
                        pltpu.VMEM((Bs, 1, hw), F32),
                        pltpu.VMEM((Bs, 1, LANE), F32)],
        compiler_params=_params(("parallel", "arbitrary")),
        name="mlstm",
    )(z3, z3, z3, z3, z3, bias, *consts, c0, n0.reshape(nseq, 1, hw), m0p)
    return h.reshape(nseq, t_seq, hw), c_new, n_new.reshape(nseq, ML_HEADS, ML_DK), m_new[:, 0, :ML_HEADS]


def _rope(x, cos, sin_signed, width):
    half = SW_HD // 2
    lane = lax.broadcasted_iota(jnp.int32, x.shape, 1)
    first = (lane % SW_HD) < half
    partner = jnp.where(first, pltpu.roll(x, width - half, axis=1), pltpu.roll(x, half, axis=1))
    return x * cos + partner * sin_signed


def _undup(xd):
    lane = lax.broadcasted_iota(jnp.int32, xd.shape[:-1] + (LANE,), xd.ndim - 1)
    return jnp.where(lane < SW_HD, xd[..., :LANE], xd[..., LANE:])


def _swa_prompt_kernel(sink_ref, q_ref, kc_ref, vc_ref, vp_ref, cosc_ref, sinc_ref, h_ref, ko_ref, kprev_s):
    n = pl.program_id(1)
    W2 = 2 * WINDOW
    cosc, sinc = cosc_ref[...], sinc_ref[...]

    @pl.when(n == 0)
    def _():
        kprev_s[...] = jnp.zeros_like(kprev_s)

    i = lax.broadcasted_iota(jnp.int32, (WINDOW, W2), 0)
    j = lax.broadcasted_iota(jnp.int32, (WINDOW, W2), 1)
    diff = WINDOW + i - j
    valid = (diff >= 0) & (diff < WINDOW) & ((n > 0) | (j >= WINDOW))
    row = lax.broadcasted_iota(jnp.int32, (2 * W2, LANE), 0)
    lane = lax.broadcasted_iota(jnp.int32, (2 * W2, LANE), 1)
    bd = (row < W2) == (lane < SW_HD)
    ones_bd = bd.astype(BF16)
    npair = SW_GQ // 2
    rows = npair * WINDOW
    valid = jnp.concatenate([valid] * npair, axis=0)
    low = lax.broadcasted_iota(jnp.int32, (rows, LANE), 1) < SW_HD
    prow = lax.broadcasted_iota(jnp.int32, (rows, 1), 0) // WINDOW
    cos4, sin4 = jnp.concatenate([cosc] * 4, axis=1), jnp.concatenate([sinc] * 4, axis=1)
    for bi in range(q_ref.shape[0]):
        q = _rope(q_ref[bi], cos4, sin4, 4 * LANE)
        k_cur = _rope(kc_ref[bi], cos4[:, :2 * LANE], sin4[:, :2 * LANE], 2 * LANE)
        kd = jnp.concatenate([kprev_s[bi], k_cur], axis=0)
        kprev_s[bi] = k_cur
        vd = jnp.concatenate([vp_ref[bi], vc_ref[bi]], axis=0)
        for kv in range(SW_KV_HEADS):
            kblk = kd[:, kv * LANE:(kv + 1) * LANE]
            vblk = vd[:, kv * LANE:(kv + 1) * LANE]
            kbd = jnp.where(bd, jnp.concatenate([kblk, kblk], axis=0), 0.0).astype(BF16)
            vbd = jnp.where(bd, jnp.concatenate([vblk, vblk], axis=0), 0.0).astype(BF16)
            vo = jnp.concatenate([vbd, ones_bd], axis=1)
            h0 = kv * SW_GQ
            qs = jnp.concatenate([q[:, (h0 + 2 * p) * SW_HD:(h0 + 2 * p + 2) * SW_HD] for p in range(npair)],
                                 axis=0).astype(BF16)
            s = lax.dot_general(qs, kbd, NT, preferred_element_type=F32) * (SW_HD ** -0.5)
            es, sink_terms = [], []
            for hh in range(2):
                sh = jnp.where(valid, s[:, hh * W2:(hh + 1) * W2], -jnp.inf)
                sink = sink_ref[h0 + hh]
                for p in range(1, npair):
                    sink = jnp.where(prow == p, sink_ref[h0 + 2 * p + hh], sink)
                m = jnp.maximum(jnp.max(sh, axis=1, keepdims=True), sink)
                es.append(jnp.exp(sh - m).astype(BF16))
                sink_terms.append(jnp.exp(sink - m))
            r = jnp.dot(jnp.concatenate(es, axis=1), vo, preferred_element_type=F32)
            o = r[:, :LANE] / (r[:, LANE:] + jnp.where(low, sink_terms[0], sink_terms[1]))
            for p in range(npair):
                h_ref[bi, :, (h0 + 2 * p) * SW_HD:(h0 + 2 * p + 2) * SW_HD] = (
                    o[p * WINDOW:(p + 1) * WINDOW].astype(h_ref.dtype))

        @pl.when(n == pl.num_programs(1) - 1)
        def _(bi=bi, k_cur=k_cur):
            ko_ref[bi] = _undup(k_cur)


def _swa_prompt(z3, cos, sin_signed, sinks, Bb):
    b, t, _ = z3.shape
    nb = t // WINDOW
    qw = SW_HEADS * SW_HD
    kw = SW_KV_HEADS * SW_HD
    dw = 2 * kw
    qblk, kblk, vblk = EVEN_QA // qw, EVEN_KD // dw, EVEN_VD // dw
    cur = lambda j: (lambda bi, ni: (bi, ni, j))
    prev = lambda j: (lambda bi, ni: (bi, jnp.maximum(ni - 1, 0), j))
    return pl.pallas_call(
        _swa_prompt_kernel,
        grid=(b // Bb, nb),
        in_specs=[pl.BlockSpec(memory_space=pltpu.SMEM),
                  pl.BlockSpec((Bb, WINDOW, qw), cur(qblk)),
                  pl.BlockSpec((Bb, WINDOW, dw), cur(kblk)),
                  pl.BlockSpec((Bb, WINDOW, dw), cur(vblk)),
                  pl.BlockSpec((Bb, WINDOW, dw), prev(vblk)),
                  pl.BlockSpec((WINDOW, LANE), lambda bi, ni: (ni, 0)),
                  pl.BlockSpec((WINDOW, LANE), lambda bi, ni: (ni, 0))],
        out_specs=[pl.BlockSpec((Bb, WINDOW, qw), cur(0)),
                   pl.BlockSpec((Bb, WINDOW, kw), lambda bi, ni: (bi, 0, 0))],
        out_shape=[jax.ShapeDtypeStruct((b, t, qw), BF16),
                   jax.ShapeDtypeStruct((b, WINDOW, kw), F32)],
        scratch_shapes=[pltpu.VMEM((Bb, WINDOW, dw), F32)],
        compiler_params=_params(("parallel", "arbitrary")),
        name="swa_prompt",
    )(sinks, z3, z3, z3, z3, cos, sin_signed)


def _swa_sample_kernel(q_ref, k_ref, v_ref, kbuf_ref, vbuf_ref, cos_ref, sin_ref, sink_ref,
                       h_ref, ko_ref, vo_ref, *, T, Bb):
    qw = SW_HEADS * SW_HD
    kw = SW_KV_HEADS * SW_HD
    cos = cos_ref[...]
    sin = sin_ref[...]
    q = _rope(q_ref[...].reshape(Bb * T, qw), jnp.concatenate([cos] * (qw // LANE), axis=1),
              jnp.concatenate([sin] * (qw // LANE), axis=1), qw).reshape(Bb, T, qw)
    k_new = _rope(_undup(k_ref[...]).reshape(Bb * T, kw), cos, sin, kw).reshape(Bb, T, kw)
    kk = jnp.concatenate([kbuf_ref[...], k_new], axis=1)
    vv = jnp.concatenate([vbuf_ref[...], _undup(v_ref[...])], axis=1)
    ko_ref[...] = kk[:, T:, :]
    vo_ref[...] = vv[:, T:, :]
    rows = SW_GQ * T
    i = lax.broadcasted_iota(jnp.int32, (rows, WINDOW + T), 0) % T
    j = lax.broadcasted_iota(jnp.int32, (rows, WINDOW + T), 1)
    diff = WINDOW + i - j
    valid = (diff >= 0) & (diff < WINDOW)
    for kv in range(SW_KV_HEADS):
        qs = jnp.concatenate([q[:, :, (kv * SW_GQ + g) * SW_HD:(kv * SW_GQ + g + 1) * SW_HD]
                              for g in range(SW_GQ)], axis=1).astype(BF16)
        kh = kk[:, :, kv * SW_HD:(kv + 1) * SW_HD].astype(BF16)
        vh = vv[:, :, kv * SW_HD:(kv + 1) * SW_HD].astype(BF16)
        s = jnp.einsum('bqd,bkd->bqk', qs, kh, preferred_element_type=F32) * (SW_HD ** -0.5)
        s = jnp.where(valid, s, -jnp.inf)
        sink = sink_ref[kv * rows:(kv + 1) * rows, :]
        m = jnp.maximum(jnp.max(s, axis=-1, keepdims=True), sink)
        e = jnp.exp(s - m)
        p = e / (jnp.sum(e, axis=-1, keepdims=True) + jnp.exp(sink - m))
        o = jnp.einsum('bqk,bkd->bqd', p.astype(BF16), vh, preferred_element_type=F32)
        for g in range(SW_GQ):
            hh = kv * SW_GQ + g
            h_ref[:, :, hh * SW_HD:(hh + 1) * SW_HD] = o[:, g * T:(g + 1) * T, :].astype(h_ref.dtype)


def _swa_sample(z3, kbuf, vbuf, cos, sin_signed, sinks, Bb):
    b, t, _ = z3.shape
    qw = SW_HEADS * SW_HD
    kw = SW_KV_HEADS * SW_HD
    dw = 2 * kw
    qblk, kblk, vblk = EVEN_QA // qw, EVEN_KD // dw, EVEN_VD // dw
    col = lambda j: (lambda bi: (bi, 0, j))
    const2 = lambda bi: (0, 0)
    sink_col = jnp.repeat(sinks, t).reshape(SW_HEADS * t, 1)
    return pl.pallas_call(
        functools.partial(_swa_sample_kernel, T=t, Bb=Bb),
        grid=(b // Bb,),
        in_specs=[pl.BlockSpec((Bb, t, qw), col(qblk)),
                  pl.BlockSpec((Bb, t, dw), col(kblk)),
                  pl.BlockSpec((Bb, t, dw), col(vblk)),
                  pl.BlockSpec((Bb, WINDOW, kw), col(0)),
                  pl.BlockSpec((Bb, WINDOW, kw), col(0)),
                  pl.BlockSpec((Bb * t, LANE), const2),
                  pl.BlockSpec((Bb * t, LANE), const2),
                  pl.BlockSpec((SW_HEADS * t, 1), const2)],
        out_specs=[pl.BlockSpec((Bb, t, qw), col(0)),
                   pl.BlockSpec((Bb, WINDOW, kw), col(0)),
                   pl.BlockSpec((Bb, WINDOW, kw), col(0))],
        out_shape=[jax.ShapeDtypeStruct((b, t, qw), BF16),
                   jax.ShapeDtypeStruct((b, WINDOW, kw), F32),
                   jax.ShapeDtypeStruct((b, WINDOW, kw), F32)],
        compiler_params=_params(("parallel",)),
        name="swa_sample",
    )(z3, z3, z3, kbuf, vbuf, jnp.tile(cos, (Bb, 1)), jnp.tile(sin_signed, (Bb, 1)), sink_col)


def _gla_tables(S, G):
    nlev = int(math.log2(S))
    assert 2 ** nlev == S
    L = G * S
    mstack = np.zeros((nlev * L, L), np.float32)
    masks = np.zeros((nlev + 1, L, L), np.float32)
    masks[0] = np.eye(L)
    for l in range(1, nlev + 1):
        n = 2 ** l
        for r in range(L):
            blk, pos = divmod(r, n)
            m = blk * n + n // 2 - 1
            if pos >= n // 2:
                mstack[(l - 1) * L + r, m + 1:r + 1] = 1.0
                masks[l, r, blk * n:blk * n + n // 2] = 1.0
            else:
                mstack[(l - 1) * L + r, r + 1:m + 1] = 1.0
    masks = np.tile(masks, (1, 1, GLA_HEADS))
    tri = np.kron(np.eye(G), np.tril(np.ones((S, S)))).astype(np.float32)
    return nlev, jnp.asarray(tri, BF16), jnp.asarray(mstack, BF16), jnp.asarray(masks)


def _gla_consts(L):
    kw = GLA_HEADS * GLA_DK
    vw = GLA_HEADS * GLA_DV
    hk = np.zeros((GLA_HEADS * L, kw), np.float32)
    for h in range(GLA_HEADS):
        hk[h * L:(h + 1) * L, h * GLA_DK:(h + 1) * GLA_DK] = 1.0
    hs = np.zeros((vw, kw), np.float32)
    for h in range(GLA_HEADS):
        hs[h * GLA_DV:(h + 1) * GLA_DV, h * GLA_DK:(h + 1) * GLA_DK] = 1.0
    return jnp.asarray(hk, BF16), jnp.asarray(hs)


def _gla_kernel(q_ref, k_ref, v_ref, r_ref, glr_ref, wg_ref, bg_ref, gn_ref, tri_ref, mstack_ref,
                masks_ref, hk_ref, hs_ref, s0_ref,
                o_ref, so_ref, st_s, *, L, nlev, Bb, nsub, G):
    c = pl.program_id(1)
    H, DK, DV = GLA_HEADS, GLA_DK, GLA_DV

    @pl.when(c == 0)
    def _():
        for bi in range(Bb * G):
            for h in range(H):
                pieces = ([jnp.zeros((h * DK, DV), F32)] if h else []) + [s0_ref[bi, h]]
                pieces += [jnp.zeros(((H - 1 - h) * DK, DV), F32)] if h < H - 1 else []
                padded = jnp.concatenate(pieces, axis=0)
                st_s[bi, h * DV:(h + 1) * DV, :] = padded.T

    for sub in range(nsub):
        for bi in range(Bb):
            _gla_one(bi, sub * L, q_ref, k_ref, v_ref, r_ref, glr_ref, wg_ref, bg_ref, gn_ref, tri_ref, mstack_ref,
                     masks_ref, hk_ref, hs_ref, o_ref, st_s, L, nlev, G)

    @pl.when(c == pl.num_programs(1) - 1)
    def _():
        for bi in range(Bb * G):
            for h in range(H):
                blk = st_s[bi, h * DV:(h + 1) * DV, :]
                so_ref[bi, h] = blk.T[h * DK:(h + 1) * DK]


def _gla_one(bi, r0, q_ref, k_ref, v_ref, r_ref, glr_ref, wg_ref, bg_ref, gn_ref, tri_ref, mstack_ref,
             masks_ref, hk_ref, hs_ref, o_ref, st_s, L, nlev, G):
    H, DK, DV = GLA_HEADS, GLA_DK, GLA_DV
    packed_rows = L % (2 * SUBLANE) == 0
    rw = pl.ds(r0, L)
    q = q_ref[bi, rw] * (DK ** -0.5)
    k = k_ref[bi, rw]
    kwid = H * DK
    g_hi, g_lo = _pieces(glr_ref[bi, rw], 2)
    pre2 = jnp.dot(g_hi.astype(BF16), wg_ref[...], preferred_element_type=F32)
    pre = (pre2[:, :kwid] + pre2[:, kwid:] + jnp.dot(g_lo.astype(BF16), wg_ref[:, :kwid], preferred_element_type=F32)
           + bg_ref[...])
    lg = jax.nn.log_sigmoid(pre) * (1.0 / GLA_TAU)
    Gc = _dot_sel(tri_ref[...], lg)
    E = jnp.exp(_dot_sel(mstack_ref[...], lg, pieces=2))
    hk = hk_ref[...]
    tok = lax.broadcasted_iota(jnp.int32, (L, kwid), 0)

    att = None
    for l in range(nlev + 1):
        if l == 0:
            ql, kl = q, k
        else:
            e_l = E[(l - 1) * L:l * L]
            upper = ((tok >> (l - 1)) & 1) == 1
            ql = jnp.where(upper, q * e_l, 0.0)
            kl = jnp.where(upper, 0.0, k * e_l)
        if packed_rows:
            kst = jnp.concatenate([kl.astype(BF16)] * H, axis=0) * hk
        else:
            kst = (jnp.concatenate([kl] * H, axis=0) * hk.astype(F32)).astype(BF16)
        part = lax.dot_general(ql.astype(BF16), kst, NT, preferred_element_type=F32)
        part = jnp.where(masks_ref[l] > 0.5, part, 0.0)
        att = part if att is None else att + part

    v = v_ref[bi, rw]
    vsrc = v.astype(BF16) if packed_rows else v
    zero = jnp.zeros((L, DV), vsrc.dtype)
    vbd = jnp.concatenate(
        [jnp.concatenate([zero] * h + [vsrc[:, h * DV:(h + 1) * DV]] + [zero] * (H - 1 - h), axis=1)
         for h in range(H)], axis=0).astype(BF16)
    S = L // G
    seg = lambda a, g: a[g * S:(g + 1) * S]
    qg = q * jnp.exp(Gc)
    o = jnp.dot(att.astype(BF16), vbd, preferred_element_type=F32) + jnp.concatenate(
        [lax.dot_general(seg(qg, g).astype(BF16), st_s[bi * G + g].astype(BF16), NT, preferred_element_type=F32)
         for g in range(G)], axis=0)
    for h in range(H):
        vs = slice(h * DV, (h + 1) * DV)
        oh = o[:, vs]
        oh = oh * lax.rsqrt(jnp.mean(oh * oh, axis=-1, keepdims=True) + EPS)
        oh = oh * gn_ref[:, vs] * jax.nn.silu(r_ref[bi, rw, vs])
        o_ref[bi, rw, vs] = oh.astype(o_ref.dtype)
    for g in range(G):
        g_seg = seg(Gc, g)
        g_end = g_seg[S - 1:S, :]
        k_end = (seg(k, g) * jnp.exp(g_end - g_seg)).astype(BF16)
        upd = lax.dot_general(seg(v, g).astype(BF16), k_end, TN, preferred_element_type=F32)
        st_s[bi * G + g] = st_s[bi * G + g] * jnp.exp(g_end) + jnp.where(hs_ref[...] > 0.5, upd, 0.0)


def _gla(z3, wg, bg, gn, s0, S, G, Bb, nsub):
    nseq, t_seq, width = z3.shape
    if G > 1:
        assert t_seq == S and nsub == 1
        z3 = z3.reshape(nseq // G, G * S, width)
    b, t, _ = z3.shape
    L = G * S
    nc = t // (L * nsub)
    nlev, tri, mstack, masks = _gla_tables(S, G)
    hk, hs = _gla_consts(L)
    qk_w = GLA_HEADS * GLA_DK
    v_w = GLA_HEADS * GLA_DV
    col = lambda j: (lambda bi, ci: (bi, ci, j))
    const2 = lambda bi, ci: (0, 0)
    const3 = lambda bi, ci: (0, 0, 0)
    o, s_new = pl.pallas_call(
        functools.partial(_gla_kernel, L=L, nlev=nlev, Bb=Bb, nsub=nsub, G=G),
        grid=(b // Bb, nc),
        in_specs=[pl.BlockSpec((Bb, nsub * L, qk_w), col(0)),
                  pl.BlockSpec((Bb, nsub * L, qk_w), col(ODD_K // qk_w)),
                  pl.BlockSpec((Bb, nsub * L, v_w), col(ODD_V // v_w)),
                  pl.BlockSpec((Bb, nsub * L, v_w), col(ODD_R // v_w)),
                  pl.BlockSpec((Bb, nsub * L, LANE), col(ODD_GATE // LANE)),
                  pl.BlockSpec((LANE, 2 * qk_w), const2),
                  pl.BlockSpec((1, qk_w), const2),
                  pl.BlockSpec((1, v_w), const2),
                  pl.BlockSpec((L, L), const2),
                  pl.BlockSpec(mstack.shape, const2),
                  pl.BlockSpec(masks.shape, const3),
                  pl.BlockSpec(hk.shape, const2),
                  pl.BlockSpec(hs.shape, const2),
                  pl.BlockSpec((Bb * G, GLA_HEADS, GLA_DK, GLA_DV), lambda bi, ci: (bi, 0, 0, 0))],
        out_specs=[pl.BlockSpec((Bb, nsub * L, v_w), col(0)),
                   pl.BlockSpec((Bb * G, GLA_HEADS, GLA_DK, GLA_DV), lambda bi, ci: (bi, 0, 0, 0))],
        out_shape=[jax.ShapeDtypeStruct((b, t, v_w), BF16),
                   jax.ShapeDtypeStruct((nseq, GLA_HEADS, GLA_DK, GLA_DV), F32)],
        scratch_shapes=[pltpu.VMEM((Bb * G, v_w, qk_w), F32)],
        compiler_params=_params(("parallel", "arbitrary")),
        name="gla",
    )(z3, z3, z3, z3, z3, wg, bg, gn, tri, mstack, masks, hk, hs, s0)
    return o.reshape(nseq, t_seq, v_w), s_new


def _s5_kernel(u_ref, bbre_ref, bbim_ref, cre_ref, cim_ref, d_ref, wglu_ref, bglu_ref, are_ref, aim_ref,
               x0re_ref, x0im_ref,
               o_ref, xore_ref, xoim_ref,
               xr_s, xi_s, cr_s, ci_s, pre_ref, pim_ref, *, rows, sequential):
    c = pl.program_id(1)
    lead = rows // SUBLANE if sequential else SUBLANE
    inner = rows // lead
    cw = S5_LANES // 4 if inner == SUBLANE else LANE
    u = jnp.swapaxes(u_ref[0].reshape(inner, lead, S5_WIDTH), 0, 1).reshape(rows, S5_WIDTH)
    ub = u.astype(BF16)

    if sequential:
        @pl.when(c == 0)
        def _():
            pre_ref[0] = are_ref[...]
            pim_ref[0] = aim_ref[...]
            n = 1
            while n < lead:
                nr, ni = pre_ref[n - 1], pim_ref[n - 1]
                sr, si = pre_ref[0:n], pim_ref[0:n]
                pre_ref[n:2 * n] = sr * nr - si * ni
                pim_ref[n:2 * n] = sr * ni + si * nr
                n *= 2

    hw, hl = S5_WIDTH // 2, S5_LANES // 2
    for half in range(2):
        uh = ub[:, half * hw:(half + 1) * hw]
        ls = slice(half * hl, (half + 1) * hl)
        xr_s[:, :, ls] = jnp.dot(uh, bbre_ref[half], preferred_element_type=F32).reshape(lead, inner, hl)
        xi_s[:, :, ls] = jnp.dot(uh, bbim_ref[half], preferred_element_type=F32).reshape(lead, inner, hl)

    for c0 in range(0, S5_LANES, cw):
        ar = jnp.broadcast_to(are_ref[:, c0:c0 + cw], (inner, cw))
        ai = jnp.broadcast_to(aim_ref[:, c0:c0 + cw], (inner, cw))

        def body(j, carry, c0=c0, ar=ar, ai=ai):
            xr, xi = carry
            nr = ar * xr - ai * xi + xr_s[j, :, c0:c0 + cw]
            ni = ar * xi + ai * xr + xi_s[j, :, c0:c0 + cw]
            xr_s[j, :, c0:c0 + cw] = nr
            xi_s[j, :, c0:c0 + cw] = ni
            return nr, ni

        if sequential:
            init = (jnp.zeros((inner, cw), F32), jnp.zeros((inner, cw), F32))
        else:
            init = (x0re_ref[0, :, c0:c0 + cw], x0im_ref[0, :, c0:c0 + cw])
        lax.fori_loop(0, lead, body, init, unroll=True)

    if sequential:
        @pl.when(c == 0)
        def _():
            cr_s[...] = x0re_ref[0]
            ci_s[...] = x0im_ref[0]

        end_r = xr_s[lead - 1]
        end_i = xi_s[lead - 1]
        anr = pre_ref[lead - 1]
        ani = pim_ref[lead - 1]
        ent_r, ent_i = [cr_s[...]], [ci_s[...]]
        for s in range(SUBLANE):
            pr, pi = ent_r[-1], ent_i[-1]
            ent_r.append(end_r[s:s + 1] + anr * pr - ani * pi)
            ent_i.append(end_i[s:s + 1] + anr * pi + ani * pr)
        cr_s[...] = ent_r[SUBLANE]
        ci_s[...] = ent_i[SUBLANE]
        er = jnp.concatenate(ent_r[:SUBLANE], axis=0)[None]
        ei = jnp.concatenate(ent_i[:SUBLANE], axis=0)[None]
        pr3, pi3 = pre_ref[...], pim_ref[...]
        xr = xr_s[...] + pr3 * er - pi3 * ei
        xi = xi_s[...] + pr3 * ei + pi3 * er

        @pl.when(c == pl.num_programs(1) - 1)
        def _():
            xore_ref[0] = cr_s[...]
            xoim_ref[0] = ci_s[...]
    else:
        xr = xr_s[...]
        xi = xi_s[...]
        xore_ref[0] = xr[lead - 1]
        xoim_ref[0] = xi[lead - 1]

    xrb = xr.reshape(rows, S5_LANES).astype(BF16)
    xib = xi.reshape(rows, S5_LANES).astype(BF16)
    y = jnp.concatenate(
        [jnp.dot(xrb[:, h * hl:(h + 1) * hl], cre_ref[h], preferred_element_type=F32)
         - jnp.dot(xib[:, h * hl:(h + 1) * hl], cim_ref[h], preferred_element_type=F32) for h in range(2)],
        axis=1) + d_ref[...] * u
    zz = jax.nn.gelu(y)
    gate = jnp.dot(zz.astype(BF16), wglu_ref[...], preferred_element_type=F32) + bglu_ref[...]
    out = zz * jax.nn.sigmoid(gate)
    o_ref[0] = jnp.swapaxes(out.reshape(lead, inner, S5_WIDTH), 0, 1).reshape(rows, S5_WIDTH).astype(o_ref.dtype)


def _s5(z3, tabs, x0re, x0im, rows, sequential):
    b, t, _ = z3.shape
    nblk = t // rows
    groups = rows // SUBLANE
    ublk = ODD_U // S5_WIDTH
    const2 = lambda bi, ci: (0, 0)
    const3 = lambda bi, ci: (0, 0, 0)
    lead = groups if sequential else SUBLANE
    hw, hl = S5_WIDTH // 2, S5_LANES // 2
    if sequential:
        st_spec = pl.BlockSpec((1, 1, S5_LANES), lambda bi, ci: (bi, 0, 0))
        st_shape = jax.ShapeDtypeStruct((b, 1, S5_LANES), F32)
    else:
        st_spec = pl.BlockSpec((1, groups, S5_LANES), lambda bi, ci: (bi, ci, 0))
        st_shape = jax.ShapeDtypeStruct(x0re.shape, F32)
    return pl.pallas_call(
        functools.partial(_s5_kernel, rows=rows, sequential=sequential),
        grid=(b, nblk),
        in_specs=[pl.BlockSpec((1, rows, S5_WIDTH), lambda bi, ci: (bi, ci, ublk)),
                  pl.BlockSpec((2, hw, hl), const3),
                  pl.BlockSpec((2, hw, hl), const3),
                  pl.BlockSpec((2, hl, hw), const3),
                  pl.BlockSpec((2, hl, hw), const3),
                  pl.BlockSpec((1, S5_WIDTH), const2),
                  pl.BlockSpec((S5_WIDTH, S5_WIDTH), const2),
                  pl.BlockSpec((1, S5_WIDTH), const2),
                  pl.BlockSpec((1, S5_LANES), const2),
                  pl.BlockSpec((1, S5_LANES), const2),
                  st_spec, st_spec],
        out_specs=[pl.BlockSpec((1, rows, S5_WIDTH), lambda bi, ci: (bi, ci, 0)),
                   st_spec, st_spec],
        out_shape=[jax.ShapeDtypeStruct((b, t, S5_WIDTH), BF16), st_shape, st_shape],
        scratch_shapes=[pltpu.VMEM((lead, rows // lead, S5_LANES), F32),
                        pltpu.VMEM((lead, rows // lead, S5_LANES), F32),
                        pltpu.VMEM((1, S5_LANES), F32),
                        pltpu.VMEM((1, S5_LANES), F32),
                        pltpu.VMEM((lead if sequential else 1, 1, S5_LANES), F32),
                        pltpu.VMEM((lead if sequential else 1, 1, S5_LANES), F32)],
        compiler_params=_params(("parallel", "arbitrary")),
        name="s5",
    )(z3, tabs['bbre'], tabs['bbim'], tabs['cre'], tabs['cim'], tabs['d'], tabs['wglu'], tabs['bglu'],
      tabs['are'], tabs['aim'], x0re, x0im)


def _s5_tables(a_re, a_im, log_dt, b_re, b_im, c_re, c_im, d, w_glu, b_glu):
    lam_re = a_re.astype(F32)
    lam_im = a_im.astype(F32)
    dt = jnp.exp(log_dt.astype(F32))[:, None]
    mag = jnp.exp(lam_re * dt)
    ang = lam_im * dt
    ab_re = mag * jnp.cos(ang)
    ab_im = mag * jnp.sin(ang)
    den = lam_re * lam_re + lam_im * lam_im
    co_re = ((ab_re - 1.0) * lam_re + ab_im * lam_im) / den
    co_im = (ab_im * lam_re - (ab_re - 1.0) * lam_im) / den
    b_re = b_re.astype(F32)
    b_im = b_im.astype(F32)
    bb_re = co_re[..., None] * b_re - co_im[..., None] * b_im
    bb_im = co_re[..., None] * b_im + co_im[..., None] * b_re
    gh = S5_GROUPS // 2
    eye = jnp.eye(gh, dtype=F32)
    hw, hl = S5_WIDTH // 2, S5_LANES // 2
    halves = lambda a: a.reshape((2, gh) + a.shape[1:])
    blockdiag_in = lambda bb: jnp.einsum('xgph,gk->xghkp', halves(bb), eye).reshape(2, hw, hl).astype(BF16)
    blockdiag_out = lambda cc: jnp.einsum('xghp,gk->xgpkh', halves(cc.astype(F32)), eye).reshape(2, hl, hw).astype(BF16)
    return {
        'bbre': blockdiag_in(bb_re), 'bbim': blockdiag_in(bb_im),
        'cre': blockdiag_out(c_re), 'cim': blockdiag_out(c_im),
        'd': d.astype(F32).reshape(1, S5_WIDTH),
        'wglu': w_glu.astype(BF16), 'bglu': b_glu.astype(F32).reshape(1, S5_WIDTH),
        'are': ab_re.reshape(1, S5_LANES), 'aim': ab_im.reshape(1, S5_LANES),
    }


def _rmsnorm_val(x, g):
    return x * lax.rsqrt(jnp.mean(x * x, axis=-1, keepdims=True) + EPS) * g


def _ffn_kernel(r_ref, a1_ref, a2_ref, w1_ref, w2_ref, g_ref, wup_ref, cw_ref, cb_ref, wd_ref, st_ref, gf_ref,
                o_ref, so_ref, halo_s, *, tm, tf, long_seq, final_norm):
    i = pl.program_id(1)
    n = tm // SUBLANE
    d = r_ref.shape[-1]
    x = (r_ref[0] + jnp.dot(a1_ref[0], w1_ref[...], preferred_element_type=F32)
         + jnp.dot(a2_ref[0], w2_ref[...], preferred_element_type=F32))
    hn = _rmsnorm_val(x, g_ref[...])
    if long_seq:
        lead = n

        @pl.when(i == 0)
        def _():
            halo_s[...] = st_ref[0]
    else:
        lead = SUBLANE
    hnp = jnp.swapaxes(hn.reshape(tm // lead, lead, d), 0, 1).reshape(tm, d).astype(BF16)
    inner = tm // lead
    sub = lax.broadcasted_iota(jnp.int32, (SUBLANE, tf), 0)

    acc = jnp.zeros((tm, d), F32)
    for c in range(D_FF // tf):
        conv = []
        for half in range(2):
            c0 = half * D_FF + c * tf
            u = jnp.dot(hnp, wup_ref[:, c0:c0 + tf], preferred_element_type=F32).reshape(lead, inner, tf)
            if long_seq:
                um1 = jnp.where(sub == 0, halo_s[1:2, c0:c0 + tf], pltpu.roll(u[lead - 1], 1, axis=0))
                um2 = jnp.where(sub == 0, halo_s[0:1, c0:c0 + tf], pltpu.roll(u[lead - 2], 1, axis=0))
                halo_s[0:1, c0:c0 + tf] = u[lead - 2][SUBLANE - 1:SUBLANE]
                halo_s[1:2, c0:c0 + tf] = u[lead - 1][SUBLANE - 1:SUBLANE]
            else:
                um2 = st_ref[0, :, c0:c0 + tf]
                um1 = st_ref[1, :, c0:c0 + tf]
                so_ref[0, :, c0:c0 + tf] = u[lead - 2]
                so_ref[1, :, c0:c0 + tf] = u[lead - 1]
            ext = jnp.concatenate([um2[None], um1[None], u], axis=0)
            cv = cb_ref[:, c0:c0 + tf]
            for j in range(CONV_W):
                cv = cv + ext[j:j + lead] * cw_ref[j:j + 1, c0:c0 + tf]
            conv.append(cv)
        act = (jax.nn.silu(conv[1]) * conv[0]).reshape(tm, tf).astype(BF16)
        acc = acc + jnp.dot(act, wd_ref[c * tf:(c + 1) * tf, :], preferred_element_type=F32)

    out = x + jnp.swapaxes(acc.reshape(lead, inner, d), 0, 1).reshape(tm, d)
    if final_norm:
        out = _rmsnorm_val(out, gf_ref[...])
    o_ref[0] = out

    if long_seq:
        @pl.when(i == pl.num_programs(1) - 1)
        def _():
            so_ref[0] = halo_s[...]


def _ffn(res3, a1, a2, w1, w2, g, wup, cw, cb, wd, state, layer, gfinal, tm, tf, long_seq, final_norm):
    b, t, d = res3.shape
    nt = t // tm
    resident = dict(pipeline_mode=pl.Buffered(1))
    const2 = lambda bi, i: (0, 0)
    row_blk = lambda w: pl.BlockSpec((1, tm, w), lambda bi, i: (bi, i, 0))
    per_layer = lambda shape, **kw: pl.BlockSpec((None,) + shape, lambda bi, i: (layer, 0, 0), **kw)
    if long_seq:
        st_in = st_out = pl.BlockSpec((1, CONV_W - 1, 2 * D_FF), lambda bi, i: (bi, 0, 0))
        st_shape = state.shape
    else:
        nseq = tm // SUBLANE
        st_in = pl.BlockSpec((None, CONV_W - 1, nseq, 2 * D_FF), lambda bi, i: (layer, 0, i, 0))
        st_out = pl.BlockSpec((CONV_W - 1, nseq, 2 * D_FF), lambda bi, i: (0, i, 0))
        st_shape = state.shape[1:]
    return pl.pallas_call(
        functools.partial(_ffn_kernel, tm=tm, tf=tf, long_seq=long_seq, final_norm=final_norm),
        grid=(b, nt),
        in_specs=[row_blk(d), row_blk(a1.shape[-1]), row_blk(a2.shape[-1]),
                  pl.BlockSpec(w1.shape, const2, **resident),
                  pl.BlockSpec(w2.shape, const2, **resident),
                  per_layer((1, d)),
                  per_layer((d, 2 * D_FF), **resident),
                  per_layer((CONV_W, 2 * D_FF)),
                  per_layer((1, 2 * D_FF)),
                  per_layer((D_FF, d), **resident),
                  st_in,
                  pl.BlockSpec((1, d), const2)],
        out_specs=[row_blk(d), st_out],
        out_shape=[jax.ShapeDtypeStruct((b, t, d), F32), jax.ShapeDtypeStruct(st_shape, F32)],
        scratch_shapes=[pltpu.VMEM((CONV_W - 1, 2 * D_FF), F32)],
        compiler_params=_params(("parallel", "arbitrary")),
        name="ffn",
    )(res3, a1, a2, w1, w2, g.reshape(-1, 1, d), wup, cw, cb.reshape(-1, 1, 2 * D_FF), wd, state, gfinal)


def _rope_tables(pos):
    half = SW_HD // 2
    inv = ROPE_THETA ** (-jnp.arange(half, dtype=F32) / half)
    ang = pos.astype(F32)[:, None] * inv[None, :]
    cos = jnp.cos(ang)
    sin = jnp.sin(ang)
    reps = LANE // SW_HD
    cos_t = jnp.tile(jnp.concatenate([cos, cos], axis=1), (1, reps))
    sin_t = jnp.tile(jnp.concatenate([-sin, sin], axis=1), (1, reps))
    return cos_t, sin_t


def _prep_weights(W):
    P = {}
    we = W['w_in_even']
    nh = 4 * ML_HEADS * ML_DK
    head, g_m, tail = we[:, :nh], we[:, nh:nh + 2 * ML_HEADS], we[:, nh + 2 * ML_HEADS:]
    q_a, k_a, v_a = tail[:, :512], tail[:, 512:640], tail[:, 640:768]
    dup = lambda w: jnp.concatenate([w[:, :SW_HD], w[:, :SW_HD], w[:, SW_HD:], w[:, SW_HD:]], axis=1)
    pad = jnp.zeros((D_MODEL, LANE - 2 * ML_HEADS), we.dtype)
    P['w_in_even'] = jnp.concatenate([head, q_a, dup(k_a), dup(v_a), g_m, pad], axis=1).astype(BF16)
    assert P['w_in_even'].shape[1] == EVEN_PAD
    P['b_gates'] = jnp.pad(W['b_mlstm_gates'].astype(F32), (0, LANE - 2 * ML_HEADS)).reshape(1, LANE)
    wo = W['w_in_odd']
    nh = 2 * GLA_HEADS * GLA_DK + 2 * GLA_HEADS * GLA_DV
    head, glr, u = wo[:, :nh], wo[:, nh:nh + GLA_RANK], wo[:, nh + GLA_RANK:]
    pad = jnp.zeros((D_MODEL, ODD_PAD - ODD_GATE - GLA_RANK), wo.dtype)
    P['w_in_odd'] = jnp.concatenate([head, u, glr, pad], axis=1).astype(BF16)
    wg = jnp.pad(W['w_gla_gate_up'].astype(F32), ((0, LANE - GLA_RANK), (0, 0)))
    wg_hi = wg.astype(BF16)
    P['w_gate_up'] = jnp.concatenate([wg_hi, (wg - wg_hi.astype(F32)).astype(BF16)], axis=1)
    P['b_gate'] = W['b_gla_gate'].astype(F32).reshape(1, -1)
    P['g_gla'] = W['g_gla_norm'].astype(F32).reshape(1, -1)
    hm = ML_HEADS * ML_DV
    P['w_out_even'] = (W['w_out_even'][:hm].astype(BF16), W['w_out_even'][hm:].astype(BF16))
    hg = GLA_HEADS * GLA_DV
    P['w_out_odd'] = (W['w_out_odd'][:hg].astype(BF16), W['w_out_odd'][hg:].astype(BF16))
    P['w_ffn_up'] = W['w_ffn_up'].astype(BF16)
    P['w_ffn_down'] = W['w_ffn_down'].astype(BF16)
    P['s5'] = _s5_tables(W['s5_a_re'], W['s5_a_im'], W['s5_log_dt'], W['s5_b_re'], W['s5_b_im'],
                         W['s5_c_re'], W['s5_c_im'], W['s5_d'], W['w_s5_glu'], W['b_s5_glu'])
    return P


def _trunk(x, pos, st, W, P, is_prompt):
    b, t, d = x.shape
    n = b * t
    L = next((c for c in (2 * CHUNK, CHUNK) if t % c == 0), t)
    nsub = next((c for c in (8, 4, 2) if t % (c * L) == 0 and c * L <= 8 * CHUNK), 1)
    tm = next((c for c in (1024, 512) if n % c == 0), n)
    cos, sin_signed = _rope_tables(pos)
    gfinal = W['norm_final'].astype(F32).reshape(1, d)
    new = {}
    conv_out = []
    h = x.reshape(n, d)

    z = _norm_matmul(h, W['norm_mix'][0], P['w_in_even'], tm)
    z3 = z.reshape(b, t, EVEN_PAD)
    pack = dict(G=1, Bb=2 if b % 2 == 0 else 1, nsub=nsub) if (is_prompt or b % 16) else dict(G=8, Bb=2, nsub=1)
    h_m, new['ml_C'], new['ml_n'], m_new = _mlstm(z3, P['b_gates'], st['ml_C'], st['ml_n'], st['ml_m'], L, **pack)
    new['ml_m'] = m_new.reshape(b, ML_HEADS)
    if is_prompt:
        h_a, kb = _swa_prompt(z3, cos, sin_signed, W['sw_sinks'].astype(F32), Bb=2 if b % 2 == 0 else 1)
        vb = jnp.concatenate([z3[:, t - WINDOW:, EVEN_VD:EVEN_VD + SW_HD],
                              z3[:, t - WINDOW:, EVEN_VD + LANE:EVEN_VD + LANE + SW_HD]], axis=-1)
    else:
        h_a, kb, vb = _swa_sample(z3, st['kbuf'].reshape(b, WINDOW, -1), st['vbuf'].reshape(b, WINDOW, -1),
                                  cos, sin_signed, W['sw_sinks'].astype(F32), Bb=8)
    new['kbuf'] = kb.reshape(b, WINDOW, SW_KV_HEADS, SW_HD)
    new['vbuf'] = vb.reshape(b, WINDOW, SW_KV_HEADS, SW_HD)
    h, cb = _ffn_layer(h, h_m, h_a, P['w_out_even'], 0, st, W, P, b, t, gfinal, is_prompt, final_norm=False)
    conv_out.append(cb)

    z = _norm_matmul(h, W['norm_mix'][1], P['w_in_odd'], tm)
    z3 = z.reshape(b, t, ODD_PAD)
    lg = CHUNK if (pack['G'] == 1 and L == 2 * CHUNK) else L
    o_g, new['gla'] = _gla(z3, P['w_gate_up'], P['b_gate'], P['g_gla'], st['gla'], lg,
                           **dict(pack, nsub=pack['nsub'] * L // lg))
    x0re = st['s5_re'].astype(F32).reshape(b, 1, S5_LANES)
    x0im = st['s5_im'].astype(F32).reshape(b, 1, S5_LANES)
    if is_prompt:
        o_s, xre, xim = _s5(z3, P['s5'], x0re, x0im, rows=next((c for c in (512, 256) if t % c == 0), t),
                            sequential=True)
    else:
        o_s, xre, xim = _s5(z3.reshape(1, n, ODD_PAD), P['s5'], x0re.reshape(1, b, S5_LANES),
                            x0im.reshape(1, b, S5_LANES), rows=256 if n % 256 == 0 else n, sequential=False)
    new['s5_re'] = xre.reshape(b, S5_GROUPS, S5_STATE)
    new['s5_im'] = xim.reshape(b, S5_GROUPS, S5_STATE)
    h, cb = _ffn_layer(h, o_g, o_s, P['w_out_odd'], 1, st, W, P, b, t, gfinal, is_prompt, final_norm=True)
    conv_out.append(cb)
    new['conv'] = jnp.stack(conv_out)
    return h.reshape(b, t, d), new


def _ffn_layer(h, a1, a2, w_out, layer, st, W, P, b, t, gfinal, is_prompt, final_norm):
    n, d = h.shape
    rows = t if is_prompt else n
    tm = next((c for c in (512, 256) if rows % c == 0), rows)
    args = (*w_out, W['norm_ffn'], P['w_ffn_up'], W['ffn_conv_w'].astype(F32), W['ffn_conv_b'].astype(F32),
            P['w_ffn_down'])
    if is_prompt:
        shape3 = lambda a: a.reshape(b, t, -1)
        out, new_state = _ffn(shape3(h), shape3(a1), shape3(a2), *args, st['conv'][layer], layer, gfinal, tm,
                              FFN_TF, True, final_norm)
    else:
        assert t == SUBLANE
        shape3 = lambda a: a.reshape(1, n, -1)
        out, new_state = _ffn(shape3(h), shape3(a1), shape3(a2), *args, st['conv_rows'], layer, gfinal, tm, FFN_TF,
                              False, final_norm)
        new_state = jnp.swapaxes(new_state, 0, 1)
    return out.reshape(n, d), new_state


def kernel(x_prompt, x_sample, state_mlstm_C, state_mlstm_n, state_mlstm_m, cache_swa_k, cache_swa_v,
           state_gla, state_s5_re, state_s5_im, state_ffn_conv, norm_mix, norm_ffn, norm_final,
           w_in_even, b_mlstm_gates, sw_sinks, w_out_even, w_in_odd, w_gla_gate_up, b_gla_gate, g_gla_norm,
           s5_a_re, s5_a_im, s5_log_dt, s5_b_re, s5_b_im, s5_c_re, s5_c_im, s5_d, w_s5_glu, b_s5_glu,
           w_out_odd, w_ffn_up, ffn_conv_w, ffn_conv_b, w_ffn_down):
    W = {'norm_mix': norm_mix.astype(F32), 'norm_ffn': norm_ffn.astype(F32), 'norm_final': norm_final,
         'w_in_even': w_in_even, 'b_mlstm_gates': b_mlstm_gates, 'sw_sinks': sw_sinks, 'w_out_even': w_out_even,
         'w_in_odd': w_in_odd, 'w_gla_gate_up': w_gla_gate_up, 'b_gla_gate': b_gla_gate, 'g_gla_norm': g_gla_norm,
         's5_a_re': s5_a_re, 's5_a_im': s5_a_im, 's5_log_dt': s5_log_dt, 's5_b_re': s5_b_re, 's5_b_im': s5_b_im,
         's5_c_re': s5_c_re, 's5_c_im': s5_c_im, 's5_d': s5_d, 'w_s5_glu': w_s5_glu, 'b_s5_glu': b_s5_glu,
         'w_out_odd': w_out_odd, 'w_ffn_up': w_ffn_up, 'ffn_conv_w': ffn_conv_w, 'ffn_conv_b': ffn_conv_b,
         'w_ffn_down': w_ffn_down}
    P = _prep_weights(W)
    bp, tp = x_prompt.shape[:2]
    st_prompt = {'ml_C': jnp.zeros((bp, ML_HEADS, ML_DK, ML_DV), F32),
                 'ml_n': jnp.zeros((bp, ML_HEADS, ML_DK), F32),
                 'ml_m': jnp.zeros((bp, ML_HEADS), F32),
                 'gla': jnp.zeros((bp, GLA_HEADS, GLA_DK, GLA_DV), F32),
                 's5_re': jnp.zeros((bp, S5_GROUPS, S5_STATE), F32),
                 's5_im': jnp.zeros((bp, S5_GROUPS, S5_STATE), F32),
                 'conv': jnp.zeros((2, bp, CONV_W - 1, 2 * D_FF), F32)}
    st_sample = {'ml_C': state_mlstm_C.astype(F32), 'ml_n': state_mlstm_n.astype(F32),
                 'ml_m': state_mlstm_m.astype(F32), 'kbuf': cache_swa_k.astype(F32),
                 'vbuf': cache_swa_v.astype(F32), 'gla': state_gla.astype(F32),
                 's5_re': state_s5_re, 's5_im': state_s5_im,
                 'conv_rows': jnp.swapaxes(state_ffn_conv.astype(F32), 1, 2)}
    y_p, np_ = _trunk(x_prompt.astype(F32), jnp.arange(tp), st_prompt, W, P, True)
    y_s, ns_ = _trunk(x_sample.astype(F32), PAST_LEN + jnp.arange(x_sample.shape[1]), st_sample, W, P, False)
    return (y_p, y_s,
            np_['ml_C'], ns_['ml_C'], np_['ml_n'], ns_['ml_n'], np_['ml_m'], ns_['ml_m'],
            np_['kbuf'], ns_['kbuf'], np_['vbuf'], ns_['vbuf'], np_['gla'], ns_['gla'],
            np_['s5_re'], ns_['s5_re'], np_['s5_im'], ns_['s5_im'], np_['conv'], ns_['conv'])
```
